```python
import math
import jax
import jax.numpy as jnp
from jax import lax
import numpy as np

D_MODEL = 1024
BATCH = 8
SEQ = 2048
DEPTH = 2

GRID_W = 64
CTX_LEN = 256
GROUP_W = D_MODEL // 4
HEAD_DIM = 64
Q_BLOCK = 128
ROPE_THETA = 10000.0
NORM_EPS = 1e-6
DA_HEADS = GROUP_W // HEAD_DIM
DA_QK = HEAD_DIM // 2
ML_HEADS = GROUP_W // HEAD_DIM
ML_HEAD_DIM = HEAD_DIM
ML_CHUNK = 128
ML_CONV = 3
MLA_HEADS = 4
MLA_NOPE = 64
MLA_ROPE = 32
MLA_V = GROUP_W // MLA_HEADS
MLA_Q_RANK = 192
MLA_KV_RANK = 128
S5_CH = GROUP_W
S5_GROUP = 16
S5_NGROUPS = S5_CH // S5_GROUP
S5_STATE = 64
N_EXPERTS = 32
TOP_K = 4
D_FF = D_MODEL
SWIGLU_ALPHA = 1.702
SWIGLU_LIMIT = 7.0
MOE_BLOCK = 256
DA_COLS = 3 * GROUP_W
ML_COLS = 4 * GROUP_W + 4 * ML_HEADS
MLA_COLS = MLA_Q_RANK + MLA_KV_RANK + MLA_ROPE
S5_COLS = S5_CH
D_IN = DA_COLS + ML_COLS + MLA_COLS + S5_COLS

kernel_name = 'hymba_style_diffusion_hybrid_moe'


def rmsnorm(x, w):
    xf = x.astype(jnp.float32)
    y = xf * lax.rsqrt(jnp.mean(xf * xf, axis=-1, keepdims=True) + NORM_EPS)
    return (y * w.astype(jnp.float32)).astype(x.dtype)


def modulate(x, w, shift, scale):
    return rmsnorm(x, w) * (1.0 + scale) + shift


def axial_rope_tables(rows, cols, rot_dim):
    n_freq = rot_dim // 4
    inv = ROPE_THETA ** (-jnp.arange(n_freq, dtype=jnp.float32) / n_freq)
    ang = jnp.concatenate([rows[:, None] * inv, cols[:, None] * inv], axis=-1)
    return jnp.cos(ang), jnp.sin(ang)


def apply_rope(x, cos, sin):
    shp = (cos.shape[0],) + (1,) * (x.ndim - 3) + (cos.shape[1],)
    cs, sn = cos.reshape(shp), sin.reshape(shp)
    xf = x.astype(jnp.float32)
    half = x.shape[-1] // 2
    x1, x2 = xf[..., :half], xf[..., half:]
    return jnp.concatenate([x1 * cs - x2 * sn, x2 * cs + x1 * sn], axis=-1).astype(x.dtype)


def block_attn(q, k, v, scale, coef):
    b, s, m, h, dk = q.shape
    nb = s // Q_BLOCK
    qb = q.reshape(b, nb, Q_BLOCK, m, h, dk).swapaxes(0, 1)

    def one_block(q_blk):
        logits = jnp.einsum('bqmhd,bkmhd->bmhqk', q_blk, k).astype(jnp.float32) * scale
        probs = jax.nn.softmax(logits, axis=-1)
        w = jnp.einsum('m,bmhqk->bhqk', coef, probs).astype(v.dtype)
        return jnp.einsum('bhqk,bkhd->bqhd', w, v)

    out = lax.map(one_block, qb)
    return out.swapaxes(0, 1).reshape(b, s, h, v.shape[-1])


def diff_attn_mixer(p_ctx, p_lat, cos, sin, lam, subln_w, lambda_init, with_ctx):
    def qk_heads(a):
        return a.reshape(a.shape[0], a.shape[1], DA_HEADS, 2, DA_QK).swapaxes(2, 3)

    def queries(p, rotate):
        q = qk_heads(p[..., :GROUP_W])
        return apply_rope(q, cos, sin) if rotate else q

    def keys_values(p, rotate):
        k = qk_heads(p[..., GROUP_W:2 * GROUP_W])
        if rotate:
            k = apply_rope(k, cos, sin)
        v = p[..., 2 * GROUP_W:].reshape(p.shape[0], p.shape[1], DA_HEADS, 2 * DA_QK)
        return k, v

    lamf = lam.astype(jnp.float32)
    lam_full = jnp.exp(jnp.sum(lamf[0] * lamf[1])) - jnp.exp(jnp.sum(lamf[2] * lamf[3])) + lambda_init
    coef = jnp.stack([jnp.ones_like(lam_full), -lam_full])
    scale = DA_QK ** -0.5

    def finish(o):
        return (rmsnorm(o, subln_w) * (1.0 - lambda_init)).reshape(o.shape[0], o.shape[1], GROUP_W)

    k_ctx, v_ctx = keys_values(p_ctx, False)
    k_lat, v_lat = keys_values(p_lat, True)
    y_lat = finish(block_attn(queries(p_lat, True), jnp.concatenate([k_ctx, k_lat], 1),
                              jnp.concatenate([v_ctx, v_lat], 1), scale, coef))
    y_ctx = finish(block_attn(queries(p_ctx, False), k_ctx, v_ctx, scale, coef)) if with_ctx else None
    return y_ctx, y_lat


def short_conv(x, w, b):
    ch = x.shape[-1]
    y = lax.conv_general_dilated(x, w[:, None, :].astype(x.dtype), window_strides=(1,),
                                 padding=[((ML_CONV - 1) // 2, ML_CONV // 2)],
                                 dimension_numbers=('NWC', 'WIO', 'NWC'), feature_group_count=ch)
    return y + b.astype(x.dtype)


def mlstm_chunk_scan(q, k, v, log_i, log_f, state):
    b, n, h, d = q.shape
    nc = n // ML_CHUNK

    def chunks(a):
        a = a.reshape((b, nc, ML_CHUNK) + a.shape[2:])
        return jnp.moveaxis(a, (1, 3), (0, 2))

    causal = jnp.tril(jnp.ones((ML_CHUNK, ML_CHUNK), dtype=bool))

    def step(carry, inp):
        c_mat, n_vec, m = carry
        qc, kc, vc, ic, fc = inp
        bcum = jnp.cumsum(fc, axis=-1)
        log_w = jnp.where(causal, bcum[..., :, None] - bcum[..., None, :] + ic[..., None, :], -jnp.inf)
        log_inter = bcum + m[..., None]
        m_t = jnp.maximum(log_inter, jnp.max(log_w, axis=-1))
        w = jnp.einsum('bhtd,bhsd->bhts', qc, kc) * jnp.exp(log_w - m_t[..., None])
        w_inter = jnp.exp(log_inter - m_t)
        num = jnp.einsum('bhts,bhsd->bhtd', w, vc) + w_inter[..., None] * jnp.einsum('bhvk,bhtk->bhtv', c_mat, qc)
        den = jnp.sum(w, axis=-1) + w_inter * jnp.einsum('bhtk,bhk->bht', qc, n_vec)
        h_out = num / jnp.maximum(jnp.abs(den), jnp.exp(-m_t))[..., None]
        b_last = bcum[..., -1]
        log_upd = b_last[..., None] - bcum + ic
        m_new = jnp.maximum(b_last + m, jnp.max(log_upd, axis=-1))
        w_upd = jnp.exp(log_upd - m_new[..., None])
        decay = jnp.exp(b_last + m - m_new)
        c_new = decay[..., None, None] * c_mat + jnp.einsum('bhs,bhsv,bhsk->bhvk', w_upd, vc, kc)
        n_new = decay[..., None] * n_vec + jnp.einsum('bhs,bhsk->bhk', w_upd, kc)
        return (c_new, n_new, m_new), h_out

    state, hs = lax.scan(step, state, (chunks(q), chunks(k), chunks(v), chunks(log_i), chunks(log_f)))
    return state, jnp.moveaxis(hs, (0, 2), (1, 3)).reshape(b, n, h, d)


def mlstm_mixer(p_ctx, p_lat, conv_w, conv_b, gate_b, norm_w, with_ctx):
    f32 = jnp.float32

    def prep(p):
        bsz, n = p.shape[:2]
        qk, v, o, g = jnp.split(p, [2 * GROUP_W, 3 * GROUP_W, 4 * GROUP_W], axis=-1)
        qk = jax.nn.silu(short_conv(qk, conv_w, conv_b))
        q, k = jnp.split(qk, 2, axis=-1)
        heads = lambda a: a.astype(f32).reshape(bsz, n, ML_HEADS, ML_HEAD_DIM)
        g = (g + gate_b).astype(f32).reshape(bsz, n, 4, ML_HEADS)
        gates = ((g[:, :, 0], jax.nn.log_sigmoid(g[:, :, 1])), (g[:, :, 2], jax.nn.log_sigmoid(g[:, :, 3])))
        return heads(q), heads(k) * ML_HEAD_DIM ** -0.5, heads(v), o, gates

    qc, kc, vc, oc, gc = prep(p_ctx)
    ql, kl, vl, ol, gl = prep(p_lat)
    bsz = p_lat.shape[0]

    def direction(idx, reverse):
        flip = (lambda a: jnp.flip(a, axis=1)) if reverse else (lambda a: a)
        state0 = (jnp.zeros((bsz, ML_HEADS, ML_HEAD_DIM, ML_HEAD_DIM), f32),
                  jnp.zeros((bsz, ML_HEADS, ML_HEAD_DIM), f32), jnp.zeros((bsz, ML_HEADS), f32))
        (li_c, lf_c), (li_l, lf_l) = gc[idx], gl[idx]
        state, hc = mlstm_chunk_scan(flip(qc), flip(kc), flip(vc), flip(li_c), flip(lf_c), state0)
        _, hl = mlstm_chunk_scan(flip(ql), flip(kl), flip(vl), flip(li_l), flip(lf_l), state)
        return flip(hc), flip(hl)

    hc_f, hl_f = direction(0, False)
    hc_b, hl_b = direction(1, True)

    def finish(h, o):
        bsz_, n = o.shape[:2]
        h = jax.nn.sigmoid(o.astype(f32)).reshape(bsz_, n, ML_HEADS, ML_HEAD_DIM) * h
        return rmsnorm(h, norm_w.reshape(ML_HEADS, ML_HEAD_DIM)).reshape(bsz_, n, GROUP_W).astype(o.dtype)

    y_ctx = finish(hc_f + hc_b, oc) if with_ctx else None
    return y_ctx, finish(hl_f + hl_b, ol)


def mla_mixer(p_ctx, p_lat, cos, sin, q_norm, w_uq, kv_norm, w_ukv, with_ctx):
    def queries(p, rotate):
        bsz, n = p.shape[:2]
        q = (rmsnorm(p[..., :MLA_Q_RANK], q_norm) @ w_uq).reshape(bsz, n, MLA_HEADS, MLA_NOPE + MLA_ROPE)
        if rotate:
            q = jnp.concatenate([q[..., :MLA_NOPE], apply_rope(q[..., MLA_NOPE:], cos, sin)], axis=-1)
        return q[:, :, None]

    def keys_values(p, rotate):
        bsz, n = p.shape[:2]
        kv = (rmsnorm(p[..., MLA_Q_RANK:MLA_Q_RANK + MLA_KV_RANK], kv_norm) @ w_ukv).reshape(
            bsz, n, MLA_HEADS, MLA_NOPE + MLA_V)
        k_rope = p[..., MLA_Q_RANK + MLA_KV_RANK:][:, :, None, :]
        if rotate:
            k_rope = apply_rope(k_rope, cos, sin)
        k = jnp.concatenate([kv[..., :MLA_NOPE], jnp.broadcast_to(k_rope, (bsz, n, MLA_HEADS, MLA_ROPE))], axis=-1)
        return k[:, :, None], kv[..., MLA_NOPE:]

    scale = (MLA_NOPE + MLA_ROPE) ** -0.5
    coef = jnp.ones((1,), jnp.float32)
    flat = lambda o: o.reshape(o.shape[0], o.shape[1], GROUP_W)
    k_ctx, v_ctx = keys_values(p_ctx, False)
    k_lat, v_lat = keys_values(p_lat, True)
    y_lat = flat(block_attn(queries(p_lat, True), jnp.concatenate([k_ctx, k_lat], 1),
                            jnp.concatenate([v_ctx, v_lat], 1), scale, coef))
    y_ctx = flat(block_attn(queries(p_ctx, False), k_ctx, v_ctx, scale, coef)) if with_ctx else None
    return y_ctx, y_lat


def s5_discretize(a_re, a_im, log_step, b_re, b_im):
    a_re = jnp.minimum(a_re, -1e-4)
    dt = jnp.exp(log_step)[:, None]
    mag = jnp.exp(dt * a_re)
    ab_re, ab_im = mag * jnp.cos(dt * a_im), mag * jnp.sin(dt * a_im)
    inv = 1.0 / (a_re * a_re + a_im * a_im)
    f_re = ((ab_re - 1.0) * a_re + ab_im * a_im) * inv
    f_im = (ab_im * a_re - (ab_re - 1.0) * a_im) * inv
    bb_re = f_re[..., None] * b_re - f_im[..., None] * b_im
    bb_im = f_re[..., None] * b_im + f_im[..., None] * b_re
    return ab_re, ab_im, bb_re, bb_im


def complex_affine_combine(e1, e2):
    a1r, a1i, b1r, b1i = e1
    a2r, a2i, b2r, b2i = e2
    return (a1r * a2r - a1i * a2i, a1r * a2i + a1i * a2r,
            a2r * b1r - a2i * b1i + b2r, a2r * b1i + a2i * b1r + b2i)


def s5_scan(u, ab_re, ab_im, bb_re, bb_im, x0_re, x0_im):
    bu_re = jnp.einsum('bngh,gph->bngp', u, bb_re)
    bu_im = jnp.einsum('bngh,gph->bngp', u, bb_im)
    bu_re = bu_re.at[:, 0].add(ab_re * x0_re - ab_im * x0_im)
    bu_im = bu_im.at[:, 0].add(ab_re * x0_im + ab_im * x0_re)
    a_re = jnp.broadcast_to(ab_re, bu_re.shape)
    a_im = jnp.broadcast_to(ab_im, bu_im.shape)
    _, _, x_re, x_im = lax.associative_scan(complex_affine_combine, (a_re, a_im, bu_re, bu_im), axis=1)
    return x_re, x_im


def s5_mixer(p_ctx, p_lat, a_re, a_im, log_step, b_re, b_im, c_re, c_im, d_skip, w_glu, b_glu, with_ctx):
    f32 = jnp.float32
    groups = lambda p: p.astype(f32).reshape(p.shape[0], p.shape[1], S5_NGROUPS, S5_GROUP)
    u_ctx, u_lat = groups(p_ctx), groups(p_lat)
    bsz = p_lat.shape[0]
    d_g = d_skip.astype(f32).reshape(S5_NGROUPS, S5_GROUP)

    def direction(idx, reverse):
        flip = (lambda a: jnp.flip(a, axis=1)) if reverse else (lambda a: a)
        ab_re, ab_im, bb_re, bb_im = s5_discretize(a_re[idx].astype(f32), a_im[idx].astype(f32),
                                                   log_step[idx].astype(f32), b_re[idx].astype(f32),
                                                   b_im[idx].astype(f32))
        zero = jnp.zeros((bsz, S5_NGROUPS, S5_STATE), f32)
        xc_re, xc_im = s5_scan(flip(u_ctx), ab_re, ab_im, bb_re, bb_im, zero, zero)
        xl_re, xl_im = s5_scan(flip(u_lat), ab_re, ab_im, bb_re, bb_im, xc_re[:, -1], xc_im[:, -1])
        cr, ci = c_re[idx].astype(f32), c_im[idx].astype(f32)
        readout = lambda xr, xi: flip(jnp.einsum('ghp,bngp->bngh', cr, xr) - jnp.einsum('ghp,bngp->bngh', ci, xi))
        return (readout(xc_re, xc_im) if with_ctx else None), readout(xl_re, xl_im)

    yc_f, yl_f = direction(0, False)
    yc_b, yl_b = direction(1, True)

    def glu(y, dtype):
        g = jax.nn.gelu(y.reshape(y.shape[0], y.shape[1], S5_CH))
        return (g * jax.nn.sigmoid(g @ w_glu.astype(f32) + b_glu.astype(f32))).astype(dtype)

    y_lat = glu(yl_f + yl_b + u_lat * d_g, p_lat.dtype)
    y_ctx = glu(yc_f + yc_b + u_ctx * d_g, p_ctx.dtype) if with_ctx else None
    return y_ctx, y_lat


def moe_ffn(h, w_router, b_router, w_gate_up, b_gate_up, w_down, b_down):
    n_tok, d = h.shape
    n_assign = n_tok * TOP_K
    logits = (h @ w_router + b_router).astype(jnp.float32)
    top_logit, top_expert = lax.top_k(logits, TOP_K)
    top_gate = jax.nn.softmax(top_logit, axis=-1)
    flat_expert = top_expert.reshape(-1)
    flat_token = jnp.arange(n_assign) // TOP_K
    order = jnp.argsort(flat_expert)
    sorted_expert = flat_expert[order]
    counts = jnp.bincount(flat_expert, length=N_EXPERTS)
    padded = (counts + MOE_BLOCK - 1) // MOE_BLOCK * MOE_BLOCK
    group_start = jnp.cumsum(counts) - counts
    padded_end = jnp.cumsum(padded)
    dest = (padded_end - padded)[sorted_expert] + jnp.arange(n_assign) - group_start[sorted_expert]
    n_blocks = -(-(n_assign + N_EXPERTS * (MOE_BLOCK - 1)) // MOE_BLOCK)
    n_slots = n_blocks * MOE_BLOCK
    slot_token = jnp.full((n_slots,), n_tok, jnp.int32).at[dest].set(flat_token[order])
    slot_gate = jnp.zeros((n_slots,), h.dtype).at[dest].set(top_gate.reshape(-1)[order].astype(h.dtype))
    block_expert = jnp.minimum(jnp.searchsorted(padded_end, jnp.arange(n_blocks) * MOE_BLOCK, side='right'),
                               N_EXPERTS - 1)
    h_pad = jnp.concatenate([h, jnp.zeros((1, d), h.dtype)], axis=0)
    x_blocks = h_pad[slot_token].reshape(n_blocks, MOE_BLOCK, d)

    def expert_block(args):
        xb, e = args
        gu = xb @ w_gate_up[e] + b_gate_up[e]
        gate = jnp.minimum(gu[:, :D_FF], SWIGLU_LIMIT)
        up = jnp.clip(gu[:, D_FF:], -SWIGLU_LIMIT, SWIGLU_LIMIT)
        return ((up + 1.0) * gate * jax.nn.sigmoid(SWIGLU_ALPHA * gate)) @ w_down[e] + b_down[e]

    y = lax.map(expert_block, (x_blocks, block_expert)).reshape(n_slots, d)
    out = jnp.zeros((n_tok + 1, d), h.dtype).at[slot_token].add(y * slot_gate[:, None])
    return out[:n_tok]


def setup_inputs(seed: int = 0) -> dict:
    key = jax.random.key(seed)
    keys = jax.random.split(key, 64)
    counter = iter(range(64))
    f32 = jnp.float32
    L = DEPTH

    def nrm(shape, scale):
        return scale * jax.random.normal(keys[next(counter)], shape, f32)

    x = nrm((BATCH, SEQ, D_MODEL), 1.0)
    c = nrm((BATCH, D_MODEL), 1.0)
    ctx = nrm((BATCH, CTX_LEN, D_MODEL), 1.0)
    c_ctx = nrm((D_MODEL,), 1.0)
    w_mod = nrm((L, D_MODEL, 6 * D_MODEL), 0.5 * D_MODEL ** -0.5)
    b_mod = nrm((L, 6 * D_MODEL), 0.02)
    norm_w = 1.0 + nrm((L, 4, D_MODEL), 0.02)
    w_in = nrm((L, D_MODEL, D_IN), D_MODEL ** -0.5)
    w_out = nrm((L, D_MODEL, D_MODEL), D_MODEL ** -0.5)
    da_lambda = nrm((L, 4, DA_QK), 0.1)
    da_subln = 1.0 + nrm((L, 2 * DA_QK), 0.02)
    ml_conv_w = nrm((L, ML_CONV, 2 * GROUP_W), ML_CONV ** -0.5)
    ml_conv_b = nrm((L, 2 * GROUP_W), 0.02)
    f_init = jnp.linspace(3.0, 6.0, ML_HEADS, dtype=f32)
    ml_gate_b = jnp.concatenate([nrm((L, ML_HEADS), 0.1), f_init + nrm((L, ML_HEADS), 0.1),
                                 nrm((L, ML_HEADS), 0.1), f_init + nrm((L, ML_HEADS), 0.1)], axis=-1)
    ml_norm = 1.0 + nrm((L, GROUP_W), 0.02)
    mla_q_norm = 1.0 + nrm((L, MLA_Q_RANK), 0.02)
    mla_w_uq = nrm((L, MLA_Q_RANK, MLA_HEADS * (MLA_NOPE + MLA_ROPE)), MLA_Q_RANK ** -0.5)
    mla_kv_norm = 1.0 + nrm((L, MLA_KV_RANK), 0.02)
    mla_w_ukv = nrm((L, MLA_KV_RANK, MLA_HEADS * (MLA_NOPE + MLA_V)), MLA_KV_RANK ** -0.5)
    s5_a_re = -0.5 + nrm((L, 2, S5_NGROUPS, S5_STATE), 0.01)
    s5_a_im = math.pi * jnp.arange(S5_STATE, dtype=f32) + nrm((L, 2, S5_NGROUPS, S5_STATE), 0.01)
    s5_log_step = jax.random.uniform(keys[next(counter)], (L, 2, S5_NGROUPS), f32,
                                     minval=math.log(1e-3), maxval=math.log(1e-1))
    s5_b_re = nrm((L, 2, S5_NGROUPS, S5_STATE, S5_GROUP), (2 * S5_GROUP) ** -0.5)
    s5_b_im = nrm((L, 2, S5_NGROUPS, S5_STATE, S5_GROUP), (2 * S5_GROUP) ** -0.5)
    s5_c_re = nrm((L, 2, S5_NGROUPS, S5_GROUP, S5_STATE), (2 * S5_STATE) ** -0.5)
    s5_c_im = nrm((L, 2, S5_NGROUPS, S5_GROUP, S5_STATE), (2 * S5_STATE) ** -0.5)
    s5_d = nrm((L, S5_CH), 1.0)
    s5_w_glu = nrm((L, S5_CH, S5_CH), S5_CH ** -0.5)
    s5_b_glu = nrm((L, S5_CH), 0.02)
    moe_w_router = nrm((L, D_MODEL, N_EXPERTS), D_MODEL ** -0.5)
    moe_b_router = nrm((L, N_EXPERTS), 0.01)
    moe_w_gate_up = nrm((L, N_EXPERTS, D_MODEL, 2 * D_FF), D_MODEL ** -0.5)
    moe_b_gate_up = nrm((L, N_EXPERTS, 2 * D_FF), 0.02)
    moe_w_down = nrm((L, N_EXPERTS, D_FF, D_MODEL), D_FF ** -0.5)
    moe_b_down = nrm((L, N_EXPERTS, D_MODEL), 0.02)
    return {'x': x, 'c': c, 'ctx': ctx, 'c_ctx': c_ctx, 'w_mod': w_mod, 'b_mod': b_mod, 'norm_w': norm_w,
            'w_in': w_in, 'w_out': w_out, 'da_lambda': da_lambda, 'da_subln': da_subln,
            'ml_conv_w': ml_conv_w, 'ml_conv_b': ml_conv_b, 'ml_gate_b': ml_gate_b, 'ml_norm': ml_norm,
            'mla_q_norm': mla_q_norm, 'mla_w_uq': mla_w_uq, 'mla_kv_norm': mla_kv_norm, 'mla_w_ukv': mla_w_ukv,
            's5_a_re': s5_a_re, 's5_a_im': s5_a_im, 's5_log_step': s5_log_step, 's5_b_re': s5_b_re,
            's5_b_im': s5_b_im, 's5_c_re': s5_c_re, 's5_c_im': s5_c_im, 's5_d': s5_d, 's5_w_glu': s5_w_glu,
            's5_b_glu': s5_b_glu, 'moe_w_router': moe_w_router, 'moe_b_router': moe_b_router,
            'moe_w_gate_up': moe_w_gate_up, 'moe_b_gate_up': moe_b_gate_up, 'moe_w_down': moe_w_down,
            'moe_b_down': moe_b_down}


def reference(x, c, ctx, c_ctx, w_mod, b_mod, norm_w, w_in, w_out, da_lambda, da_subln,
              ml_conv_w, ml_conv_b, ml_gate_b, ml_norm, mla_q_norm, mla_w_uq, mla_kv_norm, mla_w_ukv,
              s5_a_re, s5_a_im, s5_log_step, s5_b_re, s5_b_im, s5_c_re, s5_c_im, s5_d, s5_w_glu, s5_b_glu,
              moe_w_router, moe_b_router, moe_w_gate_up, moe_b_gate_up, moe_w_down, moe_b_down):
    bsz, n_lat, d = x.shape
    n_rows = n_lat // GRID_W
    rows = jnp.repeat(jnp.arange(n_rows), GRID_W)
    cols = jnp.tile(jnp.arange(GRID_W), n_rows)
    cos_da, sin_da = axial_rope_tables(rows, cols, DA_QK)
    cos_mla, sin_mla = axial_rope_tables(rows, cols, MLA_ROPE)
    col_splits = [DA_COLS, DA_COLS + ML_COLS, DA_COLS + ML_COLS + MLA_COLS]
    split_ctx_lat = lambda a: (a[:, :CTX_LEN], a[:, CTX_LEN:])
    x_lat, x_ctx = x, ctx
    for l in range(DEPTH):
        with_ctx = l < DEPTH - 1
        lambda_init = 0.8 - 0.6 * math.exp(-0.3 * l)
        mod_lat = jnp.split((jax.nn.silu(c) @ w_mod[l] + b_mod[l])[:, None, :], 6, axis=-1)
        mod_ctx = jnp.split((jax.nn.silu(c_ctx) @ w_mod[l] + b_mod[l])[None, None, :], 6, axis=-1)

        h = jnp.concatenate([modulate(x_ctx, norm_w[l, 0], mod_ctx[0], mod_ctx[1]),
                             modulate(x_lat, norm_w[l, 0], mod_lat[0], mod_lat[1])], axis=1)
        p = h @ w_in[l]
        p_da, p_ml, p_mla, p_s5 = jnp.split(p, col_splits, axis=-1)
        ya = diff_attn_mixer(*split_ctx_lat(p_da), cos_da, sin_da, da_lambda[l], da_subln[l], lambda_init, with_ctx)
        yb = mlstm_mixer(*split_ctx_lat(p_ml), ml_conv_w[l], ml_conv_b[l], ml_gate_b[l], ml_norm[l], with_ctx)
        yc = mla_mixer(*split_ctx_lat(p_mla), cos_mla, sin_mla, mla_q_norm[l], mla_w_uq[l],
                       mla_kv_norm[l], mla_w_ukv[l], with_ctx)
        yd = s5_mixer(*split_ctx_lat(p_s5), s5_a_re[l], s5_a_im[l], s5_log_step[l], s5_b_re[l], s5_b_im[l],
                      s5_c_re[l], s5_c_im[l], s5_d[l], s5_w_glu[l], s5_b_glu[l], with_ctx)
        o_lat = jnp.concatenate([ya[1], yb[1], yc[1], yd[1]], axis=-1) @ w_out[l]
        x_lat = x_lat + mod_lat[2] * rmsnorm(o_lat, norm_w[l, 1])

        f_lat = modulate(x_lat, norm_w[l, 2], mod_lat[3], mod_lat[4])
        if with_ctx:
            o_ctx = jnp.concatenate([ya[0], yb[0], yc[0], yd[0]], axis=-1) @ w_out[l]
            x_ctx = x_ctx + mod_ctx[2] * rmsnorm(o_ctx, norm_w[l, 1])
            f_ctx = modulate(x_ctx, norm_w[l, 2], mod_ctx[3], mod_ctx[4])
            m_all = moe_ffn(jnp.concatenate([f_ctx, f_lat], axis=1).reshape(-1, d), moe_w_router[l], moe_b_router[l],
                            moe_w_gate_up[l], moe_b_gate_up[l], moe_w_down[l], moe_b_down[l])
            m_all = m_all.reshape(bsz, CTX_LEN + n_lat, d)
            x_ctx = x_ctx + mod_ctx[5] * rmsnorm(m_all[:, :CTX_LEN], norm_w[l, 3])
            m_lat = m_all[:, CTX_LEN:]
        else:
            m_lat = moe_ffn(f_lat.reshape(-1, d), moe_w_router[l], moe_b_router[l], moe_w_gate_up[l],
                            moe_b_gate_up[l], moe_w_down[l], moe_b_down[l]).reshape(bsz, n_lat, d)
        x_lat = x_lat + mod_lat[5] * rmsnorm(m_lat, norm_w[l, 3])
    return x_lat
```

```python
import functools
import math

import numpy as np
import jax
import jax.numpy as jnp
from jax import lax
from jax.experimental import pallas as pl
from jax.experimental.pallas import tpu as pltpu

F32, BF16, I32 = jnp.float32, jnp.bfloat16, jnp.int32
NORM_EPS = 1e-6
GRID_W = 64
ROPE_THETA = 10000.0
GROUP_W = 256
DA_QK = 32
ML_CHUNK = 128
MLA_HEADS, MLA_NOPE, MLA_ROPE, MLA_Q_RANK, MLA_KV_RANK = 4, 64, 32, 192, 128
S5_NGROUPS, S5_GROUP, S5_STATE = 16, 16, 64
N_EXPERTS, TOP_K = 32, 4
SWIGLU_ALPHA, SWIGLU_LIMIT = 1.702, 7.0
NEG_INF = float("-inf")

ROW_TILE = 256
MOE_TILE = 256
S5_CHUNK = 64
VMEM_LIMIT = 56 * 1024 * 1024

IN_DA, IN_ML, IN_G, IN_MLA, IN_S5 = 768, 1024, 128, 512, 256
IN_COLS = IN_DA + IN_ML + IN_G + IN_MLA + IN_S5


def _params(sem, vmem=None):
    return pltpu.CompilerParams(dimension_semantics=sem, vmem_limit_bytes=vmem)


def _dotf(a, b):
    return jnp.dot(a, b, preferred_element_type=F32)


def _dot_nt(a, b):
    return lax.dot_general(a, b, (((1,), (1,)), ((), ())), preferred_element_type=F32)


def _split2(a):
    hi = a.astype(BF16)
    lo = (a - hi.astype(F32)).astype(BF16)
    return hi, lo


def _dot3(a, b):
    ah, al = _split2(a)
    bh, bl = _split2(b)
    return _dotf(ah, bh) + _dotf(ah, bl) + _dotf(al, bh)


def _dot_exact_lhs(lhs_b, a):
    a1 = a.astype(BF16)
    r1 = a - a1.astype(F32)
    a2 = r1.astype(BF16)
    a3 = (r1 - a2.astype(F32)).astype(BF16)
    return _dotf(lhs_b, a1) + _dotf(lhs_b, a2) + _dotf(lhs_b, a3)


def _rms(x, w):
    ms = jnp.mean(x * x, axis=-1, keepdims=True)
    return x * lax.rsqrt(ms + NORM_EPS) * w


def _head_rms(a, width):
    n = a.shape[-1]
    sh = int(math.log2(width))
    r = lax.broadcasted_iota(I32, (n, n), 0) >> sh
    c = lax.broadcasted_iota(I32, (n, n), 1) >> sh
    g = jnp.where(r == c, 1.0 / width, 0.0).astype(BF16)
    hi, lo = _split2(a * a)
    ms = _dotf(hi, g) + _dotf(lo, g)
    return a * lax.rsqrt(ms + NORM_EPS)


def _sigmoid(x):
    return jax.nn.sigmoid(x)


def _log_sigmoid(x):
    return jnp.minimum(x, 0.0) - jnp.log(1.0 + jnp.exp(-jnp.abs(x)))


def _mod_kernel(c_ref, w_ref, b_ref, o_ref):
    c = c_ref[...]
    o_ref[0] = _dot3(c * _sigmoid(c), w_ref[0]) + b_ref[0]


def _mod_call(cc, w_mod, b_mod):
    n_layers, d, n = w_mod.shape
    tn = 1536
    return pl.pallas_call(
        _mod_kernel,
        grid=(n_layers, n // tn),
        in_specs=[pl.BlockSpec((16, d), lambda l, j: (0, 0)),
                  pl.BlockSpec((1, d, tn), lambda l, j: (l, 0, j)),
                  pl.BlockSpec((1, 1, tn), lambda l, j: (l, 0, j))],
        out_specs=pl.BlockSpec((1, 16, tn), lambda l, j: (l, 0, j)),
        out_shape=jax.ShapeDtypeStruct((n_layers, 16, n), F32),
        compiler_params=_params(("parallel", "parallel"), VMEM_LIMIT),
        name="mod_vectors",
    )(cc, w_mod, b_mod.reshape(n_layers, 1, n))


def _in_kernel(x_ref, sh_ref, sc_ref, nw_ref, w_ref, c_ref, s_ref,
               da_ref, ml_ref, g_ref, mla_ref, s5_ref, *, qscale):
    h = _rms(x_ref[0], nw_ref[...]) * (1.0 + sc_ref[0]) + sh_ref[0]
    hb = h.astype(BF16)
    da = _dotf(hb, w_ref[:, 0:IN_DA])
    c = c_ref[...]
    s = s_ref[...]
    q1, q2, k1, k2 = da[:, 0:128], da[:, 128:256], da[:, 256:384], da[:, 384:512]
    da_ref[0, :, 0:128] = ((q1 * c - q2 * s) * qscale).astype(BF16)
    da_ref[0, :, 128:256] = ((q2 * c + q1 * s) * qscale).astype(BF16)
    da_ref[0, :, 256:384] = (k1 * c - k2 * s).astype(BF16)
    da_ref[0, :, 384:512] = (k2 * c + k1 * s).astype(BF16)
    da_ref[0, :, 512:768] = da[:, 512:768].astype(BF16)
    o = IN_DA
    ml_ref[0] = _dotf(hb, w_ref[:, o:o + IN_ML])
    o += IN_ML
    g_ref[0] = _dotf(hb, w_ref[:, o:o + IN_G])
    o += IN_G
    mla_ref[0] = _dotf(hb, w_ref[:, o:o + IN_MLA])
    o += IN_MLA
    s5_ref[...] = _dotf(hb, w_ref[:, o:o + IN_S5])


def _in_call(xs, mod_l, nw, w_in_p, cos_da, sin_da, n_ctx_blk):
    b, t, d = xs.shape
    tm = ROW_TILE

    def mrow(bi, ti):
        return jnp.where(ti < n_ctx_blk, b, bi)

    row3 = lambda bi, ti: (bi, ti, 0)
    return pl.pallas_call(
        functools.partial(_in_kernel, qscale=DA_QK ** -0.5),
        grid=(b, t // tm),
        in_specs=[pl.BlockSpec((1, tm, d), row3),
                  pl.BlockSpec((1, 1, d), lambda bi, ti: (mrow(bi, ti), 0, 0)),
                  pl.BlockSpec((1, 1, d), lambda bi, ti: (mrow(bi, ti), 0, 1)),
                  pl.BlockSpec((1, d), lambda bi, ti: (0, 0)),
                  pl.BlockSpec((d, IN_COLS), lambda bi, ti: (0, 0)),
                  pl.BlockSpec((tm, 128), lambda bi, ti: (ti, 0)),
                  pl.BlockSpec((tm, 128), lambda bi, ti: (ti, 0))],
        out_specs=[pl.BlockSpec((1, tm, IN_DA), row3),
                   pl.BlockSpec((1, tm, IN_ML), row3),
                   pl.BlockSpec((1, tm, IN_G), row3),
                   pl.BlockSpec((1, tm, IN_MLA), row3),
                   pl.BlockSpec((tm, IN_S5), lambda bi, ti: (ti, bi))],
        out_shape=[jax.ShapeDtypeStruct((b, t, IN_DA), BF16),
                   jax.ShapeDtypeStruct((b, t, IN_ML), F32),
                   jax.ShapeDtypeStruct((b, t, IN_G), F32),
                   jax.ShapeDtypeStruct((b, t, IN_MLA), F32),
                   jax.ShapeDtypeStruct((t, b * IN_S5), F32)],
        compiler_params=_params(("parallel", "parallel"), VMEM_LIMIT),
        name="in_proj",
    )(xs, mod_l, mod_l, nw, w_in_p, cos_da, sin_da)


def _softmax_rows(s):
    mx = jnp.max(s, axis=-1, keepdims=True)
    p = jnp.exp(s - mx)
    return p, jnp.sum(p, axis=-1, keepdims=True)


def _da_kernel(q_ref, k_ref, v_ref, lam_ref, sub_ref, o_ref, *, lambda_init):
    q = q_ref[0]
    k = k_ref[0]
    v = v_ref[0]
    lp = lam_ref[...]
    lam = (jnp.exp(jnp.sum(lp[0:1] * lp[1:2], axis=-1, keepdims=True))
           - jnp.exp(jnp.sum(lp[2:3] * lp[3:4], axis=-1, keepdims=True)) + lambda_init)
    lane = lax.broadcasted_iota(I32, (1, GROUP_W), 1)
    grp = (lane & 127) >> 4
    head = lane >> 6
    acc = jnp.zeros((q.shape[0], GROUP_W), F32)
    for h in range(4):
        w = None
        for m in range(2):
            qm = jnp.where(grp == m * 4 + h, q, jnp.zeros_like(q))
            p, l = _softmax_rows(_dot_nt(qm, k))
            w = p * (1.0 / l) if m == 0 else w - p * (lam / l)
        vm = jnp.where(head == h, v, jnp.zeros_like(v))
        acc = acc + _dotf(w.astype(BF16), vm)
    y = _head_rms(acc, 64) * sub_ref[...] * (1.0 - lambda_init)
    o_ref[0] = y.astype(BF16)


def _mla_attn_kernel(q_ref, k_ref, v_ref, o_ref):
    q = q_ref[0]
    k = k_ref[0]
    v = v_ref[0]
    head = lax.broadcasted_iota(I32, (1, GROUP_W), 1) >> 6
    acc = jnp.zeros((q.shape[0], GROUP_W), F32)
    for h in range(MLA_HEADS):
        sl = slice(128 * h, 128 * (h + 1))
        p, l = _softmax_rows(_dot_nt(q[:, sl], k[:, sl]))
        vm = jnp.where(head == h, v, jnp.zeros_like(v))
        acc = acc + _dotf((p * (1.0 / l)).astype(BF16), vm)
    o_ref[0] = acc.astype(BF16)


def _attn_call(kernel, name, q_arr, k_arr, v_arr, extra, *, q_blk0, n_q_blk, n_keys,
               q_col, k_col, v_col, q_w):
    b = q_arr.shape[0]
    tq = ROW_TILE
    in_specs = [pl.BlockSpec((1, tq, q_w), lambda bi, qi: (bi, qi + q_blk0, q_col)),
                pl.BlockSpec((1, n_keys, q_w), lambda bi, qi: (bi, 0, k_col)),
                pl.BlockSpec((1, n_keys, GROUP_W), lambda bi, qi: (bi, 0, v_col))]
    in_specs += [pl.BlockSpec(e.shape, lambda bi, qi: (0, 0)) for e in extra]
    return pl.pallas_call(
        kernel,
        grid=(b, n_q_blk),
        in_specs=in_specs,
        out_specs=pl.BlockSpec((1, tq, GROUP_W), lambda bi, qi: (bi, qi, 0)),
        out_shape=jax.ShapeDtypeStruct((b, n_q_blk * tq, GROUP_W), BF16),
        compiler_params=_params(("parallel", "arbitrary"), VMEM_LIMIT),
        name=name,
    )(q_arr, k_arr, v_arr, *extra)


def _mla_prep_kernel(p_ref, qn_ref, wq_ref, kvn_ref, wkv_ref, c_ref, sa_ref, sb_ref,
                     q_ref, k_ref, v_ref, *, scale):
    p = p_ref[0]
    cq, ckv, kr = p[:, 0:256], p[:, 256:384], p[:, 384:512]
    msq = jnp.sum(cq * cq, axis=-1, keepdims=True) * (1.0 / MLA_Q_RANK)
    qn = (cq * lax.rsqrt(msq + NORM_EPS) * qn_ref[...]).astype(BF16)
    q = _dotf(qn, wq_ref[...])
    c = c_ref[...]
    sa = sa_ref[...]
    sb = sb_ref[...]

    def rope(a):
        return a * c + pltpu.roll(a, 112, 1) * sa + pltpu.roll(a, 16, 1) * sb

    for h in range(MLA_HEADS):
        sl = slice(128 * h, 128 * (h + 1))
        q_ref[0, :, sl] = (rope(q[:, sl]) * scale).astype(BF16)
    kvn = (_rms(ckv, kvn_ref[...])).astype(BF16)
    kv = _dotf(kvn, wkv_ref[...])
    krr = rope(kr)
    for h in range(MLA_HEADS):
        sl = slice(128 * h, 128 * (h + 1))
        k_ref[0, :, sl] = (kv[:, sl] + krr).astype(BF16)
    v_ref[0] = kv[:, 512:768].astype(BF16)


def _mla_prep_call(p_mla, qn, wq, kvn, wkv, cm, sa, sb):
    b, t, _ = p_mla.shape
    tm = ROW_TILE
    row3 = lambda bi, ti: (bi, ti, 0)
    const = lambda bi, ti: (0, 0)
    tab = lambda bi, ti: (ti, 0)
    return pl.pallas_call(
        functools.partial(_mla_prep_kernel, scale=(MLA_NOPE + MLA_ROPE) ** -0.5),
        grid=(b, t // tm),
        in_specs=[pl.BlockSpec((1, tm, IN_MLA), row3),
                  pl.BlockSpec(qn.shape, const), pl.BlockSpec(wq.shape, const),
                  pl.BlockSpec(kvn.shape, const), pl.BlockSpec(wkv.shape, const),
                  pl.BlockSpec((tm, 128), tab), pl.BlockSpec((tm, 128), tab),
                  pl.BlockSpec((tm, 128), tab)],
        out_specs=[pl.BlockSpec((1, tm, 512), row3), pl.BlockSpec((1, tm, 512), row3),
                   pl.BlockSpec((1, tm, GROUP_W), row3)],
        out_shape=[jax.ShapeDtypeStruct((b, t, 512), BF16),
                   jax.ShapeDtypeStruct((b, t, 512), BF16),
                   jax.ShapeDtypeStruct((b, t, GROUP_W), BF16)],
        compiler_params=_params(("parallel", "parallel"), VMEM_LIMIT),
        name="mla_prep",
    )(p_mla, qn, wq, kvn, wkv, cm, sa, sb)


def _ml_kernel(p_ref, g_ref, cw_ref, cb_ref, gb_ref, nw_ref, y_ref, hf_ref, c_ref, m_ref,
               *, n_ctx_chunks, n_chunks):
    cl = ML_CHUNK
    t_total = n_chunks * cl
    row = lax.broadcasted_iota(I32, (cl, 1), 0)
    ti = lax.broadcasted_iota(I32, (cl, cl), 0)
    si = lax.broadcasted_iota(I32, (cl, cl), 1)
    lane128 = lax.broadcasted_iota(I32, (1, 128), 1)
    lane256 = lax.broadcasted_iota(I32, (1, 256), 1)
    lane384 = lax.broadcasted_iota(I32, (1, 384), 1)
    rowhead = lax.broadcasted_iota(I32, (256, 1), 0) >> 6
    r384 = lax.broadcasted_iota(I32, (256, 384), 0) >> 6
    c384 = lax.broadcasted_iota(I32, (256, 384), 1)
    bdmask = jnp.logical_or(jnp.logical_and(c384 < 256, (c384 >> 6) == r384), c384 == 256 + r384)
    hmask = [(lane256 >> 6) == h for h in range(4)]
    cmask = [jnp.logical_or(jnp.logical_and(lane384 < 256, (lane384 >> 6) == h), lane384 == 256 + h)
             for h in range(4)]
    is_f = jnp.logical_and(((lane128 >> 2) & 1) == 1, lane128 < 16)
    ones_tail = jnp.broadcast_to(jnp.where(lane128 < 4, 1.0, 0.0).astype(F32), (cl, 128))
    w0, w1, w2 = cw_ref[0:1], cw_ref[1:2], cw_ref[2:3]
    cb = cb_ref[...]
    gb = gb_ref[...]
    nw = nw_ref[...]

    def chunk(c, reverse):
        s0 = pl.multiple_of(c * cl, cl)
        x = p_ref[0, pl.ds(s0, cl), 0:512]
        sp = pl.multiple_of(jnp.maximum(s0 - 8, 0), 8)
        sn = pl.multiple_of(jnp.minimum(s0 + cl, t_total - 8), 8)
        has_prev = jnp.logical_and(c != 0, c != n_ctx_chunks).astype(F32)
        has_next = jnp.logical_and(c != n_ctx_chunks - 1, c != n_chunks - 1).astype(F32)
        prev_row = p_ref[0, pl.ds(sp, 8), 0:512][7:8] * has_prev
        next_row = p_ref[0, pl.ds(sn, 8), 0:512][0:1] * has_next
        xp = jnp.where(row == 0, prev_row, pltpu.roll(x, 1, 0))
        xn = jnp.where(row == cl - 1, next_row, pltpu.roll(x, cl - 1, 0))
        z = xp * w0 + x * w1 + xn * w2 + cb
        qk = z * _sigmoid(z)
        q = qk[:, 0:256]
        k = qk[:, 256:512] * (64 ** -0.5)
        v = p_ref[0, pl.ds(s0, cl), 512:768]
        o = p_ref[0, pl.ds(s0, cl), 768:1024]
        g = g_ref[0, pl.ds(s0, cl), :] + gb
        gsel = jnp.where(is_f, _log_sigmoid(g), g)
        tri = (si >= ti) if reverse else (si <= ti)
        bc = _dot_exact_lhs(jnp.where(tri, 1.0, 0.0).astype(BF16), gsel)
        g_t = gsel.T
        bc_t = bc.T
        tot = bc[0:1] if reverse else bc[cl - 1:cl]
        qb = q.astype(BF16)
        kb = k.astype(BF16)
        vaug = jnp.concatenate([v, ones_tail], axis=1).astype(BF16)
        m_all = m_ref[...]
        off = 8 if reverse else 0
        acc = jnp.zeros((cl, 384), F32)
        wi_full = jnp.zeros((cl, 384), F32)
        emt_full = jnp.zeros((cl, 256), F32)
        kw = jnp.zeros((cl, 256), F32)
        dec_full = jnp.zeros((256, 1), F32)
        m_new_all = m_all
        for h in range(4):
            il, fl = off + h, off + 4 + h
            bcol, brow = bc[:, fl:fl + 1], bc_t[fl:fl + 1, :]
            irow, icol = g_t[il:il + 1, :], gsel[:, il:il + 1]
            m_h = m_all[:, h:h + 1]
            lw = jnp.where(tri, bcol - brow + irow, NEG_INF)
            linter = bcol + m_h
            mt = jnp.maximum(linter, jnp.max(lw, axis=-1, keepdims=True))
            dm = jnp.exp(lw - mt)
            s = _dot_nt(jnp.where(hmask[h], qb, jnp.zeros_like(qb)), kb)
            acc = acc + _dotf((s * dm).astype(BF16),
                              jnp.where(cmask[h], vaug, jnp.zeros_like(vaug)))
            wi_full = jnp.where(cmask[h], jnp.exp(linter - mt), wi_full)
            emt_full = jnp.where(hmask[h], jnp.exp(-mt), emt_full)
            b_last = tot[:, fl:fl + 1]
            lupd = b_last - bcol + icol
            m_new = jnp.maximum(b_last + m_h, jnp.max(lupd, axis=0, keepdims=True))
            kw = jnp.where(hmask[h], k * jnp.exp(lupd - m_new), kw)
            dec_full = jnp.where(rowhead == h, jnp.exp(b_last + m_h - m_new), dec_full)
            m_new_all = jnp.where(lane128 == h, m_new, m_new_all)
        c_aug = c_ref[...]
        nd = acc + wi_full * _dotf(qb, c_aug.astype(BF16))
        den_full = jnp.zeros((cl, 256), F32)
        for h in range(4):
            den_full = jnp.where(hmask[h], nd[:, 256 + h:257 + h], den_full)
        hout = nd[:, 0:256] / jnp.maximum(jnp.abs(den_full), emt_full)
        upd = _dotf(kw.T.astype(BF16), vaug)
        c_ref[...] = dec_full * c_aug + jnp.where(bdmask, upd, 0.0)
        m_ref[...] = m_new_all
        return s0, hout, o

    def fwd_body(i, carry):
        s0, hout, _ = chunk(i, False)
        hf_ref[pl.ds(s0, cl), :] = hout
        return carry

    def bwd_body(i, carry):
        c = jnp.where(i < n_ctx_chunks, n_ctx_chunks - 1 - i, n_chunks - 1 - (i - n_ctx_chunks))
        s0, hout, o = chunk(c, True)
        gated = _sigmoid(o) * (hf_ref[pl.ds(s0, cl), :] + hout)
        y_ref[0, pl.ds(s0, cl), :] = (_head_rms(gated, 64) * nw).astype(BF16)
        return carry

    c_ref[...] = jnp.zeros_like(c_ref)
    m_ref[...] = jnp.zeros_like(m_ref)
    lax.fori_loop(0, n_chunks, fwd_body, 0)
    c_ref[...] = jnp.zeros_like(c_ref)
    m_ref[...] = jnp.zeros_like(m_ref)
    lax.fori_loop(0, n_chunks, bwd_body, 0)


def _ml_call(p_ml, p_g, cw, cb, gb, nw, n_ctx):
    b, t, _ = p_ml.shape
    const = lambda bi: (0, 0)
    return pl.pallas_call(
        functools.partial(_ml_kernel, n_ctx_chunks=n_ctx // ML_CHUNK, n_chunks=t // ML_CHUNK),
        grid=(b,),
        in_specs=[pl.BlockSpec((1, t, IN_ML), lambda bi: (bi, 0, 0)),
                  pl.BlockSpec((1, t, IN_G), lambda bi: (bi, 0, 0)),
                  pl.BlockSpec(cw.shape, const), pl.BlockSpec(cb.shape, const),
                  pl.BlockSpec(gb.shape, const), pl.BlockSpec(nw.shape, const)],
        out_specs=pl.BlockSpec((1, t, GROUP_W), lambda bi: (bi, 0, 0)),
        out_shape=jax.ShapeDtypeStruct((b, t, GROUP_W), BF16),
        scratch_shapes=[pltpu.VMEM((t, GROUP_W), F32), pltpu.VMEM((256, 384), F32),
                        pltpu.VMEM((1, 128), F32)],
        compiler_params=_params(("parallel",), VMEM_LIMIT),
        name="mlstm",
    )(p_ml, p_g, cw, cb, gb, nw)


def _s5_kernel(u_ref, are_ref, aim_ref, ls_ref, bre_ref, bim_ref, cre_ref, cim_ref, y_ref,
               ar_s, ai_s, bcat_s, ccat_s, st_s, bu_s, *, tc, nb):
    d = pl.program_id(0)
    i = pl.program_id(1)
    ns = S5_NGROUPS * S5_STATE

    @pl.when(i == 0)
    def _init():
        are = jnp.minimum(are_ref[0], -1e-4)
        aim = aim_ref[0]
        dt = jnp.exp(ls_ref[0])
        mag = jnp.exp(dt * are)
        abr = mag * jnp.cos(dt * aim)
        abi = mag * jnp.sin(dt * aim)
        inv = 1.0 / (are * are + aim * aim)
        fre = ((abr - 1.0) * are + abi * aim) * inv
        fim = (abi * are - (abr - 1.0) * aim) * inv
        bre = bre_ref[0]
        bim = bim_ref[0]
        bcat_s[:, 0:ns] = (bre * fre - bim * fim).astype(BF16)
        bcat_s[:, ns:2 * ns] = (bre * fim + bim * fre).astype(BF16)
        ccat_s[0:ns, :] = cre_ref[0].astype(BF16)
        ccat_s[ns:2 * ns, :] = (-cim_ref[0]).astype(BF16)
        ar_s[...] = jnp.broadcast_to(abr, (nb, ns))
        ai_s[...] = jnp.broadcast_to(abi, (nb, ns))
        st_s[...] = jnp.zeros_like(st_s)

    bu_s[...] = _dotf(u_ref[...].astype(BF16), bcat_s[...])
    ar = ar_s[...]
    ai = ai_s[...]

    def body(j, carry):
        xr, xi = carry
        t = j + d * (tc - 1 - 2 * j)
        r0 = pl.multiple_of(t * nb, nb)
        nr = ar * xr - ai * xi + bu_s[pl.ds(r0, nb), 0:ns]
        ni = ar * xi + ai * xr + bu_s[pl.ds(r0, nb), ns:2 * ns]
        bu_s[pl.ds(r0, nb), 0:ns] = nr
        bu_s[pl.ds(r0, nb), ns:2 * ns] = ni
        return nr, ni

    xr, xi = lax.fori_loop(0, tc, body, (st_s[0], st_s[1]))
    st_s[0] = xr
    st_s[1] = xi
    y_ref[0] = _dotf(bu_s[...].astype(BF16), ccat_s[...])


def _s5_call(u_tm, are, aim, ls, bre, bim, cre, cim, n_ctx, nb):
    rows, gw = u_tm.shape
    tc = S5_CHUNK
    n_chunks = rows // (tc * nb)
    n_ctx_chunks = n_ctx // tc
    ns = S5_NGROUPS * S5_STATE

    def chunk_of(d, i):
        rev = jnp.where(i < n_ctx_chunks, n_ctx_chunks - 1 - i, n_chunks - 1 - (i - n_ctx_chunks))
        return jnp.where(d == 0, i, rev)

    vec = pl.BlockSpec((1, 1, ns), lambda d, i: (d, 0, 0))
    return pl.pallas_call(
        functools.partial(_s5_kernel, tc=tc, nb=nb),
        grid=(2, n_chunks),
        in_specs=[pl.BlockSpec((tc * nb, gw), lambda d, i: (chunk_of(d, i), 0)),
                  vec, vec, vec,
                  pl.BlockSpec((1, gw, ns), lambda d, i: (d, 0, 0)),
                  pl.BlockSpec((1, gw, ns), lambda d, i: (d, 0, 0)),
                  pl.BlockSpec((1, ns, gw), lambda d, i: (d, 0, 0)),
                  pl.BlockSpec((1, ns, gw), lambda d, i: (d, 0, 0))],
        out_specs=pl.BlockSpec((1, tc * nb, gw), lambda d, i: (d, chunk_of(d, i), 0)),
        out_shape=jax.ShapeDtypeStruct((2, rows, gw), F32),
        scratch_shapes=[pltpu.VMEM((nb, ns), F32), pltpu.VMEM((nb, ns), F32),
                        pltpu.VMEM((gw, 2 * ns), BF16), pltpu.VMEM((2 * ns, gw), BF16),
                        pltpu.VMEM((2, nb, ns), F32), pltpu.VMEM((tc * nb, 2 * ns), F32)],
        compiler_params=_params(("arbitrary", "arbitrary"), VMEM_LIMIT),
        name="s5_scan",
    )(u_tm, are, aim, ls, bre, bim, cre, cim)


def _glu_kernel(yf_ref, yb_ref, u_ref, d_ref, w_ref, b_ref, o_ref):
    y = yf_ref[...] + yb_ref[...] + u_ref[...] * d_ref[...]
    g = y * (0.5 * (1.0 + jnp.tanh(math.sqrt(2.0 / math.pi) * (y + 0.044715 * (y * y * y)))))
    z = _dotf(g.astype(BF16), w_ref[...]) + b_ref[...]
    o_ref[0] = (g * _sigmoid(z)).astype(BF16)


def _glu_call(ys, u_t, dsk, w, bias, nb):
    t = u_t.shape[0]
    tm = ROW_TILE
    gw = GROUP_W
    const = lambda bi, ti: (0, 0)
    return pl.pallas_call(
        _glu_kernel,
        grid=(nb, t // tm),
        in_specs=[pl.BlockSpec((None, tm, gw), lambda bi, ti: (0, ti, bi)),
                  pl.BlockSpec((None, tm, gw), lambda bi, ti: (1, ti, bi)),
                  pl.BlockSpec((tm, gw), lambda bi, ti: (ti, bi)),
                  pl.BlockSpec(dsk.shape, const), pl.BlockSpec(w.shape, const),
                  pl.BlockSpec(bias.shape, const)],
        out_specs=pl.BlockSpec((1, tm, gw), lambda bi, ti: (bi, ti, 0)),
        out_shape=jax.ShapeDtypeStruct((nb, t, gw), BF16),
        compiler_params=_params(("parallel", "parallel"), VMEM_LIMIT),
        name="s5_glu",
    )(ys, ys, u_t, dsk, w, bias)


def _out_kernel(ya_ref, yb_ref, yc_ref, yd_ref, w_ref, x_ref, g1_ref, sh2_ref, sc2_ref,
                nw1_ref, nw2_ref, wr_ref, br_ref, xn_ref, f_ref, te_ref, tg_ref):
    o = (_dotf(ya_ref[0], w_ref[0:256]) + _dotf(yb_ref[0], w_ref[256:512])
         + _dotf(yc_ref[0], w_ref[512:768]) + _dotf(yd_ref[0], w_ref[768:1024]))
    xn = x_ref[0] + g1_ref[0] * _rms(o, nw1_ref[...])
    xn_ref[0] = xn
    f = _rms(xn, nw2_ref[...]) * (1.0 + sc2_ref[0]) + sh2_ref[0]
    f_ref[0] = f
    lg = _dot3(f, wr_ref[...]) + br_ref[...]
    tm = lg.shape[0]
    lane = lax.broadcasted_iota(I32, (tm, 128), 1)
    tops, idxs = [], []
    for _ in range(TOP_K):
        mx = jnp.max(lg, axis=-1, keepdims=True)
        idx = jnp.min(jnp.where(lg == mx, lane, 128), axis=-1, keepdims=True)
        tops.append(mx)
        idxs.append(idx)
        lg = jnp.where(lane == idx, NEG_INF, lg)
    ex = [jnp.exp(tv - tops[0]) for tv in tops]
    inv = 1.0 / (ex[0] + ex[1] + ex[2] + ex[3])
    l8 = lax.broadcasted_iota(I32, (tm, 8), 1)
    te = jnp.zeros((tm, 8), I32)
    tg = jnp.zeros((tm, 8), F32)
    for kk in range(TOP_K):
        te = jnp.where(l8 == kk, idxs[kk], te)
        tg = jnp.where(l8 == kk, ex[kk] * inv, tg)
    te_ref[0] = te
    tg_ref[0] = tg


def _out_call(ya, yb, yc, yd, w_out, xs, mod_l, nw1, nw2, wr, br, n_ctx_blk, blk0):
    b, t, d = xs.shape
    tm = ROW_TILE
    nblk = t // tm - blk0
    t_out = nblk * tm

    def mrow(bi, ti):
        return jnp.where(ti + blk0 < n_ctx_blk, b, bi)

    row3 = lambda bi, ti: (bi, ti + blk0, 0)
    out3 = lambda bi, ti: (bi, ti, 0)
    const = lambda bi, ti: (0, 0)
    modspec = lambda j: pl.BlockSpec((1, 1, d), lambda bi, ti: (mrow(bi, ti), 0, j))
    yspec = lambda y: pl.BlockSpec((1, tm, GROUP_W), row3 if y.shape[1] == t else out3)
    return pl.pallas_call(
        _out_kernel,
        grid=(b, nblk),
        in_specs=[yspec(ya), yspec(yb), yspec(yc), yspec(yd), pl.BlockSpec((d, d), const),
                  pl.BlockSpec((1, tm, d), row3), modspec(2), modspec(3), modspec(4),
                  pl.BlockSpec((1, d), const), pl.BlockSpec((1, d), const),
                  pl.BlockSpec((d, 128), const), pl.BlockSpec((1, 128), const)],
        out_specs=[pl.BlockSpec((1, tm, d), out3), pl.BlockSpec((1, tm, d), out3),
                   pl.BlockSpec((1, tm, 8), out3), pl.BlockSpec((1, tm, 8), out3)],
        out_shape=[jax.ShapeDtypeStruct((b, t_out, d), F32),
                   jax.ShapeDtypeStruct((b, t_out, d), F32),
                   jax.ShapeDtypeStruct((b, t_out, 8), I32),
                   jax.ShapeDtypeStruct((b, t_out, 8), F32)],
        compiler_params=_params(("parallel", "parallel"), VMEM_LIMIT),
        name="out_proj_router",
    )(ya, yb, yc, yd, w_out, xs, mod_l, mod_l, mod_l, nw1, nw2, wr, br)


def _moe_kernel(be_ref, nu_ref, src_ref, srcn_ref, dst_ref, gate_ref, f_hbm,
                wgu_ref, bgu_ref, wd_ref, bd_ref, out_hbm, xbuf, ybuf, gsem, ssem, *, tm, d_ff):
    i = pl.program_id(0)
    nu = nu_ref[0]
    slot = i % 2

    def gather(tok_ref, s):
        def body(r, carry):
            pltpu.make_async_copy(f_hbm.at[pl.ds(tok_ref[0, 0, r], 1)],
                                  xbuf.at[s, pl.ds(r, 1)], gsem.at[s]).start()
            return carry
        lax.fori_loop(0, tm, body, 0)

    def wait_gather(s):
        pltpu.make_async_copy(f_hbm.at[pl.ds(0, tm)], xbuf.at[s], gsem.at[s]).wait()

    def wait_scatter(s):
        pltpu.make_async_copy(ybuf.at[s], out_hbm.at[pl.ds(0, tm)], ssem.at[s]).wait()

    @pl.when(i == 0)
    def _():
        gather(src_ref, 0)
        n_rows = out_hbm.shape[0]
        ybuf[1] = jnp.zeros((tm, ybuf.shape[2]), F32)
        for part in range(2):
            fill = pltpu.make_async_copy(ybuf.at[1], out_hbm.at[pl.ds(n_rows - (part + 1) * tm, tm)],
                                         ssem.at[1])
            fill.start()
            fill.wait()

    @pl.when(i + 1 < nu)
    def _():
        gather(srcn_ref, 1 - slot)

    @pl.when(i < nu)
    def _():
        wait_gather(slot)

        @pl.when(i >= 2)
        def _():
            wait_scatter(slot)

        x = xbuf[slot].astype(BF16)
        gu = _dotf(x, wgu_ref[0]) + bgu_ref[0]
        gate = jnp.minimum(gu[:, 0:d_ff], SWIGLU_LIMIT)
        up = jnp.clip(gu[:, d_ff:2 * d_ff], -SWIGLU_LIMIT, SWIGLU_LIMIT)
        act = (up + 1.0) * gate * _sigmoid(SWIGLU_ALPHA * gate)
        y = _dotf(act.astype(BF16), wd_ref[0]) + bd_ref[0]
        ybuf[slot] = y * gate_ref[0]

        def sbody(r, carry):
            pltpu.make_async_copy(ybuf.at[slot, pl.ds(r, 1)],
                                  out_hbm.at[pl.ds(dst_ref[0, 0, r], 1)], ssem.at[slot]).start()
            return carry
        lax.fori_loop(0, tm, sbody, 0)

        @pl.when(i == nu - 1)
        def _():
            @pl.when(i >= 1)
            def _():
                wait_scatter(1 - slot)
            wait_scatter(slot)


def _moe_call(block_expert, n_used, src, dst, gates, f_rows, wgu, bgu, wd, bd, n_out_rows):
    n_blocks, _, tm = src.shape
    d = f_rows.shape[1]
    d_ff = wd.shape[1]
    nxt = lambda i, be, nu: (jnp.minimum(i + 1, n_blocks - 1), 0, 0)
    cur = lambda i, be, nu: (i, 0, 0)
    ex = lambda i, be, nu: (be[i], 0, 0)
    smem = functools.partial(pl.BlockSpec, memory_space=pltpu.SMEM)
    grid_spec = pltpu.PrefetchScalarGridSpec(
        num_scalar_prefetch=2,
        grid=(n_blocks,),
        in_specs=[smem((1, 1, tm), cur), smem((1, 1, tm), nxt), smem((1, 1, tm), cur),
                  pl.BlockSpec((1, tm, 1), cur),
                  pl.BlockSpec(memory_space=pl.ANY),
                  pl.BlockSpec((1, d, 2 * d_ff), ex), pl.BlockSpec((1, 1, 2 * d_ff), ex),
                  pl.BlockSpec((1, d_ff, d), ex), pl.BlockSpec((1, 1, d), ex)],
        out_specs=pl.BlockSpec(memory_space=pl.ANY),
        scratch_shapes=[pltpu.VMEM((2, tm, d), F32), pltpu.VMEM((2, tm, d), F32),
                        pltpu.SemaphoreType.DMA((2,)), pltpu.SemaphoreType.DMA((2,))])
    return pl.pallas_call(
        functools.partial(_moe_kernel, tm=tm, d_ff=d_ff),
        grid_spec=grid_spec,
        out_shape=jax.ShapeDtypeStruct((n_out_rows, d), F32),
        compiler_params=_params(("arbitrary",), VMEM_LIMIT),
        name="moe_experts",
    )(block_expert, n_used, src, src, dst, gates, f_rows, wgu, bgu, wd, bd)


def _route(te, tg, n_tok, tm):
    n_assign = n_tok * TOP_K
    flat_e = te.reshape(-1)
    order = jnp.argsort(flat_e, stable=True).astype(I32)
    sorted_e = flat_e[order]
    counts = jnp.bincount(flat_e, length=N_EXPERTS).astype(I32)
    padded = (counts + tm - 1) // tm * tm
    group_start = jnp.cumsum(counts) - counts
    padded_end = jnp.cumsum(padded)
    dest = (padded_end - padded)[sorted_e] + jnp.arange(n_assign, dtype=I32) - group_start[sorted_e]
    n_blocks = -(-(n_assign + N_EXPERTS * (tm - 1)) // tm)
    n_slots = n_blocks * tm
    slot_assign = jnp.full((n_slots,), -1, I32).at[dest].set(order)
    valid = slot_assign >= 0
    sa = jnp.maximum(slot_assign, 0)
    tok = sa // TOP_K
    kk = sa - tok * TOP_K
    src = jnp.where(valid, tok, 0)
    trash = TOP_K * n_tok + jnp.arange(n_slots, dtype=I32) % (2 * tm)
    dst = jnp.where(valid, kk * n_tok + tok, trash)
    gate = jnp.where(valid, tg.reshape(-1)[sa], 0.0)
    block_expert = jnp.minimum(
        jnp.searchsorted(padded_end, jnp.arange(n_blocks, dtype=I32) * tm, side="right"),
        N_EXPERTS - 1).astype(I32)
    n_used = (padded_end[-1] // tm).astype(I32).reshape(1)
    return (block_expert, n_used, src.reshape(n_blocks, 1, tm), dst.reshape(n_blocks, 1, tm),
            gate.reshape(n_blocks, tm, 1), TOP_K * n_tok + 2 * tm)


def _fin_kernel(o0_ref, o1_ref, o2_ref, o3_ref, x_ref, g2_ref, nw_ref, out_ref):
    m = o0_ref[...] + o1_ref[...] + o2_ref[...] + o3_ref[...]
    out_ref[0] = x_ref[0] + g2_ref[0] * _rms(m, nw_ref[...])


def _fin_call(o4, xn, mod_l, nw, n_ctx_blk, blk0):
    b, t, d = xn.shape
    tm = ROW_TILE
    nblk = t // tm
    tok_blk = b * nblk

    def mrow(bi, ti):
        return jnp.where(ti + blk0 < n_ctx_blk, b, bi)

    def ospec(kk):
        return pl.BlockSpec((tm, d), lambda bi, ti: (kk * tok_blk + bi * nblk + ti, 0))

    return pl.pallas_call(
        _fin_kernel,
        grid=(b, nblk),
        in_specs=[ospec(0), ospec(1), ospec(2), ospec(3),
                  pl.BlockSpec((1, tm, d), lambda bi, ti: (bi, ti, 0)),
                  pl.BlockSpec((1, 1, d), lambda bi, ti: (mrow(bi, ti), 0, 5)),
                  pl.BlockSpec((1, d), lambda bi, ti: (0, 0))],
        out_specs=pl.BlockSpec((1, tm, d), lambda bi, ti: (bi, ti, 0)),
        out_shape=jax.ShapeDtypeStruct((b, t, d), F32),
        compiler_params=_params(("parallel", "parallel"), VMEM_LIMIT),
        name="moe_combine",
    )(o4, o4, o4, o4, xn, mod_l, nw)


def _in_proj_columns():
    cols = np.full((IN_COLS,), -1, np.int64)
    for sec in range(2):
        for n in range(256):
            part, hm, j = n // 128, (n % 128) // 16, n % 16
            m, h = hm // 4, hm % 4
            cols[sec * 256 + n] = sec * 256 + h * 64 + m * 32 + part * 16 + j
    cols[512:768] = np.arange(512, 768)
    o, s = IN_DA, 768
    cols[o:o + 1024] = s + np.arange(1024)
    o, s = o + IN_ML, s + 1024
    cols[o:o + 16] = s + np.arange(16)
    o, s = o + IN_G, s + 16
    cols[o:o + MLA_Q_RANK] = s + np.arange(MLA_Q_RANK)
    cols[o + 256:o + 256 + MLA_KV_RANK] = s + MLA_Q_RANK + np.arange(MLA_KV_RANK)
    cols[o + 384 + 64:o + 384 + 96] = s + MLA_Q_RANK + MLA_KV_RANK + np.arange(MLA_ROPE)
    o, s = o + IN_MLA, s + MLA_Q_RANK + MLA_KV_RANK + MLA_ROPE
    cols[o:o + 256] = s + np.arange(256)
    return cols


def _take_cols(w, cols):
    valid = jnp.asarray(cols >= 0)
    return jnp.where(valid, jnp.take(w, jnp.asarray(np.maximum(cols, 0)), axis=-1), 0.0)


def _rope_tables(n_ctx, n_lat):
    pos = jnp.arange(n_lat)
    inv = ROPE_THETA ** (-jnp.arange(8, dtype=F32) / 8)
    ang = jnp.concatenate([(pos // GRID_W)[:, None] * inv, (pos % GRID_W)[:, None] * inv], axis=-1)
    cos = jnp.concatenate([jnp.ones((n_ctx, 16), F32), jnp.cos(ang)], axis=0)
    sin = jnp.concatenate([jnp.zeros((n_ctx, 16), F32), jnp.sin(ang)], axis=0)
    t = n_ctx + n_lat
    cos_da, sin_da = jnp.tile(cos, (1, 8)), jnp.tile(sin, (1, 8))
    one, zero = jnp.ones((t, 64), F32), jnp.zeros((t, 64), F32)
    z16, z32 = jnp.zeros((t, 16), F32), jnp.zeros((t, 32), F32)
    cm = jnp.concatenate([one, cos, cos, jnp.ones((t, 32), F32)], axis=1)
    sa = jnp.concatenate([zero, -sin, z16, z32], axis=1)
    sb = jnp.concatenate([zero, z16, sin, z32], axis=1)
    return cos_da, sin_da, cm, sa, sb


def _mla_weights(w_uq, w_ukv):
    hd = MLA_NOPE + MLA_ROPE
    wq = jnp.zeros((256, 512), F32)
    wkv = jnp.zeros((MLA_KV_RANK, 768), F32)
    for h in range(MLA_HEADS):
        wq = wq.at[:MLA_Q_RANK, 128 * h:128 * h + hd].set(w_uq[:, hd * h:hd * (h + 1)])
        wkv = wkv.at[:, 128 * h:128 * h + MLA_NOPE].set(w_ukv[:, 128 * h:128 * h + MLA_NOPE])
        wkv = wkv.at[:, 512 + 64 * h:512 + 64 * (h + 1)].set(w_ukv[:, 128 * h + MLA_NOPE:128 * (h + 1)])
    return wq.astype(BF16), wkv.astype(BF16)


def _s5_layout(a_re, a_im, log_step, b_re, b_im, c_re, c_im):
    ns = S5_NGROUPS * S5_STATE
    eye = jnp.eye(S5_NGROUPS, dtype=F32)
    are = a_re.reshape(2, 1, ns)
    aim = a_im.reshape(2, 1, ns)
    ls = jnp.repeat(log_step, S5_STATE, axis=-1).reshape(2, 1, ns)
    bd_b = lambda w: jnp.einsum("dgph,gk->dghkp", w, eye).reshape(2, GROUP_W, ns)
    bd_c = lambda w: jnp.einsum("dghp,gk->dgpkh", w, eye).reshape(2, ns, GROUP_W)
    return are, aim, ls, bd_b(b_re), bd_b(b_im), bd_c(c_re), bd_c(c_im)


def kernel(x, c, ctx, c_ctx, w_mod, b_mod, norm_w, w_in, w_out, da_lambda, da_subln, ml_conv_w,
           ml_conv_b, ml_gate_b, ml_norm, mla_q_norm, mla_w_uq, mla_kv_norm, mla_w_ukv, s5_a_re,
           s5_a_im, s5_log_step, s5_b_re, s5_b_im, s5_c_re, s5_c_im, s5_d, s5_w_glu, s5_b_glu,
           moe_w_router, moe_b_router, moe_w_gate_up, moe_b_gate_up, moe_w_down, moe_b_down):
    bsz, n_lat, d = x.shape
    n_ctx = ctx.shape[1]
    t = n_ctx + n_lat
    depth = w_mod.shape[0]
    tm = ROW_TILE
    assert n_ctx % tm == 0 and n_lat % tm == 0 and bsz % 8 == 0 and bsz < 16
    assert n_ctx % ML_CHUNK == 0 and n_ctx % S5_CHUNK == 0
    n_ctx_blk = n_ctx // tm

    cc = jnp.zeros((16, d), F32).at[:bsz].set(c).at[bsz].set(c_ctx)
    mod = _mod_call(cc, w_mod, b_mod)
    cos_da, sin_da, cm, sa, sb = _rope_tables(n_ctx, n_lat)
    in_cols = _in_proj_columns()
    xs = jnp.concatenate([ctx, x], axis=1)

    for l in range(depth):
        last = l == depth - 1
        lambda_init = 0.8 - 0.6 * math.exp(-0.3 * l)
        mod_l = mod[l].reshape(16, 1, 6 * d)
        w_in_p = _take_cols(w_in[l], in_cols).astype(BF16)
        p_da, p_ml, p_g, p_mla, u_t = _in_call(xs, mod_l, norm_w[l, 0].reshape(1, d), w_in_p,
                                               cos_da, sin_da, n_ctx_blk)

        da_extra = [da_lambda[l], jnp.tile(da_subln[l], 4).reshape(1, GROUP_W)]
        da_kern = functools.partial(_da_kernel, lambda_init=lambda_init)
        da_kw = dict(q_col=0, k_col=1, v_col=2, q_w=GROUP_W)
        ya = _attn_call(da_kern, "diff_attn", p_da, p_da, p_da, da_extra, q_blk0=n_ctx_blk,
                        n_q_blk=n_lat // tm, n_keys=t, **da_kw)
        wq, wkv = _mla_weights(mla_w_uq[l], mla_w_ukv[l])
        qn = jnp.zeros((1, 256), F32).at[0, :MLA_Q_RANK].set(mla_q_norm[l])
        q_mla, k_mla, v_mla = _mla_prep_call(p_mla, qn, wq, mla_kv_norm[l].reshape(1, -1), wkv,
                                             cm, sa, sb)
        mla_kw = dict(q_col=0, k_col=0, v_col=0, q_w=512)
        yc = _attn_call(_mla_attn_kernel, "mla_attn", q_mla, k_mla, v_mla, [], q_blk0=n_ctx_blk,
                        n_q_blk=n_lat // tm, n_keys=t, **mla_kw)
        if not last:
            ya_c = _attn_call(da_kern, "diff_attn_ctx", p_da, p_da, p_da, da_extra, q_blk0=0,
                              n_q_blk=n_ctx_blk, n_keys=n_ctx, **da_kw)
            yc_c = _attn_call(_mla_attn_kernel, "mla_attn_ctx", q_mla, k_mla, v_mla, [], q_blk0=0,
                              n_q_blk=n_ctx_blk, n_keys=n_ctx, **mla_kw)
            ya = jnp.concatenate([ya_c, ya], axis=1)
            yc = jnp.concatenate([yc_c, yc], axis=1)

        gb = jnp.zeros((1, 128), F32).at[0, :16].set(ml_gate_b[l])
        yb = _ml_call(p_ml, p_g, ml_conv_w[l], ml_conv_b[l].reshape(1, -1), gb,
                      ml_norm[l].reshape(1, -1), n_ctx)

        s5p = _s5_layout(s5_a_re[l], s5_a_im[l], s5_log_step[l], s5_b_re[l], s5_b_im[l],
                         s5_c_re[l], s5_c_im[l])
        ys = _s5_call(u_t.reshape(t * bsz, GROUP_W), *s5p, n_ctx, bsz)
        yd = _glu_call(ys.reshape(2, t, bsz * GROUP_W), u_t, s5_d[l].reshape(1, -1),
                       s5_w_glu[l].astype(BF16), s5_b_glu[l].reshape(1, -1), bsz)

        blk0 = n_ctx_blk if last else 0
        wr = jnp.zeros((d, 128), F32).at[:, :N_EXPERTS].set(moe_w_router[l])
        br = jnp.full((1, 128), -1e30, F32).at[0, :N_EXPERTS].set(moe_b_router[l])
        xn, f, te, tg = _out_call(ya, yb, yc, yd, w_out[l].astype(BF16), xs, mod_l,
                                  norm_w[l, 1].reshape(1, d), norm_w[l, 2].reshape(1, d), wr, br,
                                  n_ctx_blk, blk0)

        t_moe = t - blk0 * tm
        n_tok = bsz * t_moe
        te_m = te[:, :, :TOP_K].reshape(n_tok, TOP_K)
        tg_m = tg[:, :, :TOP_K].reshape(n_tok, TOP_K)
        be, nu, src, dst, gates, n_out_rows = _route(te_m, tg_m, n_tok, MOE_TILE)
        o4 = _moe_call(be, nu, src, dst, gates, f.reshape(n_tok, d),
                       moe_w_gate_up[l].astype(BF16), moe_b_gate_up[l].reshape(N_EXPERTS, 1, -1),
                       moe_w_down[l].astype(BF16), moe_b_down[l].reshape(N_EXPERTS, 1, -1),
                       n_out_rows)
        xs = _fin_call(o4, xn, mod_l, norm_w[l, 3].reshape(1, d), n_ctx_blk, blk0)
    return xs
```

```python
import functools
import math

import numpy as np
import jax
import jax.numpy as jnp
from jax import lax
from jax.experimental import pallas as pl
from jax.experimental.pallas import tpu as pltpu

F32, BF16, I32 = jnp.float32, jnp.bfloat16, jnp.int32
NORM_EPS = 1e-6
GRID_W = 64
ROPE_THETA = 10000.0
GROUP_W = 256
DA_QK = 32
ML_CHUNK = 128
MLA_HEADS, MLA_NOPE, MLA_ROPE, MLA_Q_RANK, MLA_KV_RANK = 4, 64, 32, 192, 128
S5_NGROUPS, S5_GROUP, S5_STATE = 16, 16, 64
N_EXPERTS, TOP_K = 32, 4
SWIGLU_ALPHA, SWIGLU_LIMIT = 1.702, 7.0
NEG_INF = float("-inf")

ROW_TILE = 256
MOE_TILE = 256
S5_CHUNK = 64
RANK_TILE = 512
DMA_UNROLL = 8
VMEM_LIMIT = 56 * 1024 * 1024

IN_DA, IN_ML, IN_G, IN_MLA, IN_S5 = 768, 1024, 128, 512, 256
IN_COLS = IN_DA + IN_ML + IN_G + IN_MLA + IN_S5


def _params(sem, vmem=None):
    return pltpu.CompilerParams(dimension_semantics=sem, vmem_limit_bytes=vmem)


def _dotf(a, b):
    return jnp.dot(a, b, preferred_element_type=F32)


def _dot_nt(a, b):
    return lax.dot_general(a, b, (((1,), (1,)), ((), ())), preferred_element_type=F32)


def _split2(a):
    hi = a.astype(BF16)
    lo = (a - hi.astype(F32)).astype(BF16)
    return hi, lo


def _dot3(a, b):
    ah, al = _split2(a)
    bh, bl = _split2(b)
    return _dotf(ah, bh) + _dotf(ah, bl) + _dotf(al, bh)


def _dot_exact_lhs(lhs_b, a):
    a1 = a.astype(BF16)
    r1 = a - a1.astype(F32)
    a2 = r1.astype(BF16)
    a3 = (r1 - a2.astype(F32)).astype(BF16)
    return _dotf(lhs_b, a1) + _dotf(lhs_b, a2) + _dotf(lhs_b, a3)


def _rms(x, w):
    ms = jnp.mean(x * x, axis=-1, keepdims=True)
    return x * lax.rsqrt(ms + NORM_EPS) * w


def _head_rms(a, width):
    n = a.shape[-1]
    sh = int(math.log2(width))
    r = lax.broadcasted_iota(I32, (n, n), 0) >> sh
    c = lax.broadcasted_iota(I32, (n, n), 1) >> sh
    g = jnp.where(r == c, 1.0 / width, 0.0).astype(BF16)
    hi, lo = _split2(a * a)
    ms = _dotf(hi, g) + _dotf(lo, g)
    return a * lax.rsqrt(ms + NORM_EPS)


def _store_token_tiles(ref, val):
    tm, d = val.shape
    nt = d // 128
    for j in range(nt):
        ref[pl.ds(j, tm, stride=nt), :] = val[:, 128 * j:128 * (j + 1)]


def _load_token_tiles(ref, tm, nt, base=0, lead=()):
    parts = [ref[lead + (pl.ds(base + j, tm, stride=nt), slice(None))] for j in range(nt)]
    return parts


def _sigmoid(x):
    return jax.nn.sigmoid(x)


def _log_sigmoid(x):
    return jnp.minimum(x, 0.0) - jnp.log(1.0 + jnp.exp(-jnp.abs(x)))


def _mod_kernel(c_ref, w_ref, b_ref, o_ref):
    c = c_ref[...]
    o_ref[0] = _dot3(c * _sigmoid(c), w_ref[0]) + b_ref[0]


def _mod_call(cc, w_mod, b_mod):
    n_layers, d, n = w_mod.shape
    tn = 1536
    return pl.pallas_call(
        _mod_kernel,
        grid=(n_layers, n // tn),
        in_specs=[pl.BlockSpec((16, d), lambda l, j: (0, 0)),
                  pl.BlockSpec((1, d, tn), lambda l, j: (l, 0, j)),
                  pl.BlockSpec((1, 1, tn), lambda l, j: (l, 0, j))],
        out_specs=pl.BlockSpec((1, 16, tn), lambda l, j: (l, 0, j)),
        out_shape=jax.ShapeDtypeStruct((n_layers, 16, n), F32),
        compiler_params=_params(("parallel", "parallel"), VMEM_LIMIT),
        name="mod_vectors",
    )(cc, w_mod, b_mod.reshape(n_layers, 1, n))


def _in_kernel(x_ref, sh_ref, sc_ref, nw_ref, w_ref, c_ref, s_ref,
               da_ref, ml_ref, g_ref, mla_ref, s5_ref, *, qscale):
    h = _rms(x_ref[0], nw_ref[...]) * (1.0 + sc_ref[0]) + sh_ref[0]
    hb = h.astype(BF16)
    da = _dotf(hb, w_ref[:, 0:IN_DA])
    c = c_ref[...]
    s = s_ref[...]
    q1, q2, k1, k2 = da[:, 0:128], da[:, 128:256], da[:, 256:384], da[:, 384:512]
    da_ref[0, :, 0:128] = ((q1 * c - q2 * s) * qscale).astype(BF16)
    da_ref[0, :, 128:256] = ((q2 * c + q1 * s) * qscale).astype(BF16)
    da_ref[0, :, 256:384] = (k1 * c - k2 * s).astype(BF16)
    da_ref[0, :, 384:512] = (k2 * c + k1 * s).astype(BF16)
    da_ref[0, :, 512:768] = da[:, 512:768].astype(BF16)
    o = IN_DA
    ml_ref[0] = _dotf(hb, w_ref[:, o:o + IN_ML])
    o += IN_ML
    g_ref[0] = _dotf(hb, w_ref[:, o:o + IN_G])
    o += IN_G
    mla_ref[0] = _dotf(hb, w_ref[:, o:o + IN_MLA])
    o += IN_MLA
    s5_ref[...] = _dotf(hb, w_ref[:, o:o + IN_S5])


def _in_call(xs, mod_l, nw, w_in_p, cos_da, sin_da, n_ctx_blk):
    b, t, d = xs.shape
    tm = ROW_TILE

    def mrow(bi, ti):
        return jnp.where(ti < n_ctx_blk, b, bi)

    row3 = lambda bi, ti: (bi, ti, 0)
    return pl.pallas_call(
        functools.partial(_in_kernel, qscale=DA_QK ** -0.5),
        grid=(b, t // tm),
        in_specs=[pl.BlockSpec((1, tm, d), row3),
                  pl.BlockSpec((1, 1, d), lambda bi, ti: (mrow(bi, ti), 0, 0)),
                  pl.BlockSpec((1, 1, d), lambda bi, ti: (mrow(bi, ti), 0, 1)),
                  pl.BlockSpec((1, d), lambda bi, ti: (0, 0)),
                  pl.BlockSpec((d, IN_COLS), lambda bi, ti: (0, 0)),
                  pl.BlockSpec((tm, 128), lambda bi, ti: (ti, 0)),
                  pl.BlockSpec((tm, 128), lambda bi, ti: (ti, 0))],
        out_specs=[pl.BlockSpec((1, tm, IN_DA), row3),
                   pl.BlockSpec((1, tm, IN_ML), row3),
                   pl.BlockSpec((1, tm, IN_G), row3),
                   pl.BlockSpec((1, tm, IN_MLA), row3),
                   pl.BlockSpec((tm, IN_S5), lambda bi, ti: (ti, bi))],
        out_shape=[jax.ShapeDtypeStruct((b, t, IN_DA), BF16),
                   jax.ShapeDtypeStruct((b, t, IN_ML), F32),
                   jax.ShapeDtypeStruct((b, t, IN_G), F32),
                   jax.ShapeDtypeStruct((b, t, IN_MLA), F32),
                   jax.ShapeDtypeStruct((t, b * IN_S5), F32)],
        compiler_params=_params(("parallel", "parallel"), VMEM_LIMIT),
        name="in_proj",
    )(xs, mod_l, mod_l, nw, w_in_p, cos_da, sin_da)


def _softmax_rows(s):
    mx = jnp.max(s, axis=-1, keepdims=True)
    p = jnp.exp(s - mx)
    return p, jnp.sum(p, axis=-1, keepdims=True)


def _da_kernel(q_ref, k_ref, v_ref, lam_ref, sub_ref, o_ref, *, lambda_init):
    q = q_ref[0]
    k = k_ref[0]
    v = v_ref[0]
    lp = lam_ref[...]
    lam = (jnp.exp(jnp.sum(lp[0:1] * lp[1:2], axis=-1, keepdims=True))
           - jnp.exp(jnp.sum(lp[2:3] * lp[3:4], axis=-1, keepdims=True)) + lambda_init)
    lane = lax.broadcasted_iota(I32, (1, GROUP_W), 1)
    grp = (lane & 127) >> 4
    head = lane >> 6
    acc = jnp.zeros((q.shape[0], GROUP_W), F32)
    for h in range(4):
        w = None
        for m in range(2):
            qm = jnp.where(grp == m * 4 + h, q, jnp.zeros_like(q))
            p, l = _softmax_rows(_dot_nt(qm, k))
            w = p * (1.0 / l) if m == 0 else w - p * (lam / l)
        vm = jnp.where(head == h, v, jnp.zeros_like(v))
        acc = acc + _dotf(w.astype(BF16), vm)
    y = _head_rms(acc, 64) * sub_ref[...] * (1.0 - lambda_init)
    o_ref[0] = y.astype(BF16)


def _mla_attn_kernel(q_ref, k_ref, v_ref, o_ref):
    q = q_ref[0]
    k = k_ref[0]
    v = v_ref[0]
    head = lax.broadcasted_iota(I32, (1, GROUP_W), 1) >> 6
    acc = jnp.zeros((q.shape[0], GROUP_W), F32)
    for h in range(MLA_HEADS):
        sl = slice(128 * h, 128 * (h + 1))
        p, l = _softmax_rows(_dot_nt(q[:, sl], k[:, sl]))
        vm = jnp.where(head == h, v, jnp.zeros_like(v))
        acc = acc + _dotf((p * (1.0 / l)).astype(BF16), vm)
    o_ref[0] = acc.astype(BF16)


def _attn_call(kernel, name, q_arr, k_arr, v_arr, extra, *, q_blk0, n_q_blk, n_keys,
               q_col, k_col, v_col, q_w):
    b = q_arr.shape[0]
    tq = ROW_TILE
    in_specs = [pl.BlockSpec((1, tq, q_w), lambda bi, qi: (bi, qi + q_blk0, q_col)),
                pl.BlockSpec((1, n_keys, q_w), lambda bi, qi: (bi, 0, k_col)),
                pl.BlockSpec((1, n_keys, GROUP_W), lambda bi, qi: (bi, 0, v_col))]
    in_specs += [pl.BlockSpec(e.shape, lambda bi, qi: (0, 0)) for e in extra]
    return pl.pallas_call(
        kernel,
        grid=(b, n_q_blk),
        in_specs=in_specs,
        out_specs=pl.BlockSpec((1, tq, GROUP_W), lambda bi, qi: (bi, qi, 0)),
        out_shape=jax.ShapeDtypeStruct((b, n_q_blk * tq, GROUP_W), BF16),
        compiler_params=_params(("parallel", "arbitrary"), VMEM_LIMIT),
        name=name,
    )(q_arr, k_arr, v_arr, *extra)


def _mla_prep_kernel(p_ref, qn_ref, wq_ref, kvn_ref, wkv_ref, c_ref, sa_ref, sb_ref,
                     q_ref, k_ref, v_ref, *, scale):
    p = p_ref[0]
    cq, ckv, kr = p[:, 0:256], p[:, 256:384], p[:, 384:512]
    msq = jnp.sum(cq * cq, axis=-1, keepdims=True) * (1.0 / MLA_Q_RANK)
    qn = (cq * lax.rsqrt(msq + NORM_EPS) * qn_ref[...]).astype(BF16)
    q = _dotf(qn, wq_ref[...])
    c = c_ref[...]
    sa = sa_ref[...]
    sb = sb_ref[...]

    def rope(a):
        return a * c + pltpu.roll(a, 112, 1) * sa + pltpu.roll(a, 16, 1) * sb

    for h in range(MLA_HEADS):
        sl = slice(128 * h, 128 * (h + 1))
        q_ref[0, :, sl] = (rope(q[:, sl]) * scale).astype(BF16)
    kvn = (_rms(ckv, kvn_ref[...])).astype(BF16)
    kv = _dotf(kvn, wkv_ref[...])
    krr = rope(kr)
    for h in range(MLA_HEADS):
        sl = slice(128 * h, 128 * (h + 1))
        k_ref[0, :, sl] = (kv[:, sl] + krr).astype(BF16)
    v_ref[0] = kv[:, 512:768].astype(BF16)


def _mla_prep_call(p_mla, qn, wq, kvn, wkv, cm, sa, sb):
    b, t, _ = p_mla.shape
    tm = ROW_TILE
    row3 = lambda bi, ti: (bi, ti, 0)
    const = lambda bi, ti: (0, 0)
    tab = lambda bi, ti: (ti, 0)
    return pl.pallas_call(
        functools.partial(_mla_prep_kernel, scale=(MLA_NOPE + MLA_ROPE) ** -0.5),
        grid=(b, t // tm),
        in_specs=[pl.BlockSpec((1, tm, IN_MLA), row3),
                  pl.BlockSpec(qn.shape, const), pl.BlockSpec(wq.shape, const),
                  pl.BlockSpec(kvn.shape, const), pl.BlockSpec(wkv.shape, const),
                  pl.BlockSpec((tm, 128), tab), pl.BlockSpec((tm, 128), tab),
                  pl.BlockSpec((tm, 128), tab)],
        out_specs=[pl.BlockSpec((1, tm, 512), row3), pl.BlockSpec((1, tm, 512), row3),
                   pl.BlockSpec((1, tm, GROUP_W), row3)],
        out_shape=[jax.ShapeDtypeStruct((b, t, 512), BF16),
                   jax.ShapeDtypeStruct((b, t, 512), BF16),
                   jax.ShapeDtypeStruct((b, t, GROUP_W), BF16)],
        compiler_params=_params(("parallel", "parallel"), VMEM_LIMIT),
        name="mla_prep",
    )(p_mla, qn, wq, kvn, wkv, cm, sa, sb)


def _ml_kernel(p_ref, g_ref, cw_ref, cb_ref, gb_ref, nw_ref, y_ref, hf_ref, c_ref, m_ref,
               *, n_ctx_chunks, n_chunks):
    cl = ML_CHUNK
    t_total = n_chunks * cl
    row = lax.broadcasted_iota(I32, (cl, 1), 0)
    ti = lax.broadcasted_iota(I32, (cl, cl), 0)
    si = lax.broadcasted_iota(I32, (cl, cl), 1)
    lane128 = lax.broadcasted_iota(I32, (1, 128), 1)
    lane256 = lax.broadcasted_iota(I32, (1, 256), 1)
    lane384 = lax.broadcasted_iota(I32, (1, 384), 1)
    rowhead = lax.broadcasted_iota(I32, (256, 1), 0) >> 6
    r384 = lax.broadcasted_iota(I32, (256, 384), 0) >> 6
    c384 = lax.broadcasted_iota(I32, (256, 384), 1)
    bdmask = jnp.logical_or(jnp.logical_and(c384 < 256, (c384 >> 6) == r384), c384 == 256 + r384)
    hmask = [(lane256 >> 6) == h for h in range(4)]
    cmask = [jnp.logical_or(jnp.logical_and(lane384 < 256, (lane384 >> 6) == h), lane384 == 256 + h)
             for h in range(4)]
    is_f = jnp.logical_and(((lane128 >> 2) & 1) == 1, lane128 < 16)
    ones_tail = jnp.broadcast_to(jnp.where(lane128 < 4, 1.0, 0.0).astype(F32), (cl, 128))
    w0, w1, w2 = cw_ref[0:1], cw_ref[1:2], cw_ref[2:3]
    cb = cb_ref[...]
    gb = gb_ref[...]
    nw = nw_ref[...]

    def chunk(c, reverse):
        s0 = pl.multiple_of(c * cl, cl)
        x = p_ref[0, pl.ds(s0, cl), 0:512]
        sp = pl.multiple_of(jnp.maximum(s0 - 8, 0), 8)
        sn = pl.multiple_of(jnp.minimum(s0 + cl, t_total - 8), 8)
        has_prev = jnp.logical_and(c != 0, c != n_ctx_chunks).astype(F32)
        has_next = jnp.logical_and(c != n_ctx_chunks - 1, c != n_chunks - 1).astype(F32)
        prev_row = p_ref[0, pl.ds(sp, 8), 0:512][7:8] * has_prev
        next_row = p_ref[0, pl.ds(sn, 8), 0:512][0:1] * has_next
        xp = jnp.where(row == 0, prev_row, pltpu.roll(x, 1, 0))
        xn = jnp.where(row == cl - 1, next_row, pltpu.roll(x, cl - 1, 0))
        z = xp * w0 + x * w1 + xn * w2 + cb
        qk = z * _sigmoid(z)
        q = qk[:, 0:256]
        k = qk[:, 256:512] * (64 ** -0.5)
        v = p_ref[0, pl.ds(s0, cl), 512:768]
        o = p_ref[0, pl.ds(s0, cl), 768:1024]
        g = g_ref[0, pl.ds(s0, cl), :] + gb
        gsel = jnp.where(is_f, _log_sigmoid(g), g)
        tri = (si >= ti) if reverse else (si <= ti)
        bc = _dot_exact_lhs(jnp.where(tri, 1.0, 0.0).astype(BF16), gsel)
        g_t = gsel.T
        bc_t = bc.T
        tot = bc[0:1] if reverse else bc[cl - 1:cl]
        qb = q.astype(BF16)
        kb = k.astype(BF16)
        vaug = jnp.concatenate([v, ones_tail], axis=1).astype(BF16)
        m_all = m_ref[...]
        off = 8 if reverse else 0
        acc = jnp.zeros((cl, 384), F32)
        wi_full = jnp.zeros((cl, 384), F32)
        emt_full = jnp.zeros((cl, 256), F32)
        kw = jnp.zeros((cl, 256), F32)
        dec_full = jnp.zeros((256, 1), F32)
        m_new_all = m_all
        for h in range(4):
            il, fl = off + h, off + 4 + h
            bcol, brow = bc[:, fl:fl + 1], bc_t[fl:fl + 1, :]
            irow, icol = g_t[il:il + 1, :], gsel[:, il:il + 1]
            m_h = m_all[:, h:h + 1]
            lw = jnp.where(tri, bcol - brow + irow, NEG_INF)
            linter = bcol + m_h
            mt = jnp.maximum(linter, jnp.max(lw, axis=-1, keepdims=True))
            dm = jnp.exp(lw - mt)
            s = _dot_nt(jnp.where(hmask[h], qb, jnp.zeros_like(qb)), kb)
            acc = acc + _dotf((s * dm).astype(BF16),
                              jnp.where(cmask[h], vaug, jnp.zeros_like(vaug)))
            wi_full = jnp.where(cmask[h], jnp.exp(linter - mt), wi_full)
            emt_full = jnp.where(hmask[h], jnp.exp(-mt), emt_full)
            b_last = tot[:, fl:fl + 1]
            lupd = b_last - bcol + icol
            m_new = jnp.maximum(b_last + m_h, jnp.max(lupd, axis=0, keepdims=True))
            kw = jnp.where(hmask[h], k * jnp.exp(lupd - m_new), kw)
            dec_full = jnp.where(rowhead == h, jnp.exp(b_last + m_h - m_new), dec_full)
            m_new_all = jnp.where(lane128 == h, m_new, m_new_all)
        c_aug = c_ref[...]
        nd = acc + wi_full * _dotf(qb, c_aug.astype(BF16))
        den_full = jnp.zeros((cl, 256), F32)
        for h in range(4):
            den_full = jnp.where(hmask[h], nd[:, 256 + h:257 + h], den_full)
        hout = nd[:, 0:256] / jnp.maximum(jnp.abs(den_full), emt_full)
        upd = _dotf(kw.T.astype(BF16), vaug)
        c_ref[...] = dec_full * c_aug + jnp.where(bdmask, upd, 0.0)
        m_ref[...] = m_new_all
        return s0, hout, o

    def fwd_body(i, carry):
        s0, hout, _ = chunk(i, False)
        hf_ref[pl.ds(s0, cl), :] = hout
        return carry

    def bwd_body(i, carry):
        c = jnp.where(i < n_ctx_chunks, n_ctx_chunks - 1 - i, n_chunks - 1 - (i - n_ctx_chunks))
        s0, hout, o = chunk(c, True)
        gated = _sigmoid(o) * (hf_ref[pl.ds(s0, cl), :] + hout)
        y_ref[0, pl.ds(s0, cl), :] = (_head_rms(gated, 64) * nw).astype(BF16)
        return carry

    c_ref[...] = jnp.zeros_like(c_ref)
    m_ref[...] = jnp.zeros_like(m_ref)
    lax.fori_loop(0, n_chunks, fwd_body, 0)
    c_ref[...] = jnp.zeros_like(c_ref)
    m_ref[...] = jnp.zeros_like(m_ref)
    lax.fori_loop(0, n_chunks, bwd_body, 0)


def _ml_call(p_ml, p_g, cw, cb, gb, nw, n_ctx):
    b, t, _ = p_ml.shape
    const = lambda bi: (0, 0)
    return pl.pallas_call(
        functools.partial(_ml_kernel, n_ctx_chunks=n_ctx // ML_CHUNK, n_chunks=t // ML_CHUNK),
        grid=(b,),
        in_specs=[pl.BlockSpec((1, t, IN_ML), lambda bi: (bi, 0, 0)),
                  pl.BlockSpec((1, t, IN_G), lambda bi: (bi, 0, 0)),
                  pl.BlockSpec(cw.shape, const), pl.BlockSpec(cb.shape, const),
                  pl.BlockSpec(gb.shape, const), pl.BlockSpec(nw.shape, const)],
        out_specs=pl.BlockSpec((1, t, GROUP_W), lambda bi: (bi, 0, 0)),
        out_shape=jax.ShapeDtypeStruct((b, t, GROUP_W), BF16),
        scratch_shapes=[pltpu.VMEM((t, GROUP_W), F32), pltpu.VMEM((256, 384), F32),
                        pltpu.VMEM((1, 128), F32)],
        compiler_params=_params(("parallel",), VMEM_LIMIT),
        name="mlstm",
    )(p_ml, p_g, cw, cb, gb, nw)


def _s5_kernel(u_ref, are_ref, aim_ref, ls_ref, bre_ref, bim_ref, cre_ref, cim_ref, y_ref,
               ar_s, ai_s, bcat_s, ccat_s, st_s, bu_s, *, tc, nb):
    d = pl.program_id(0)
    i = pl.program_id(1)
    ns = S5_NGROUPS * S5_STATE

    @pl.when(i == 0)
    def _init():
        are = jnp.minimum(are_ref[0], -1e-4)
        aim = aim_ref[0]
        dt = jnp.exp(ls_ref[0])
        mag = jnp.exp(dt * are)
        abr = mag * jnp.cos(dt * aim)
        abi = mag * jnp.sin(dt * aim)
        inv = 1.0 / (are * are + aim * aim)
        fre = ((abr - 1.0) * are + abi * aim) * inv
        fim = (abi * are - (abr - 1.0) * aim) * inv
        bre = bre_ref[0]
        bim = bim_ref[0]
        bcat_s[:, 0:ns] = (bre * fre - bim * fim).astype(BF16)
        bcat_s[:, ns:2 * ns] = (bre * fim + bim * fre).astype(BF16)
        ccat_s[0:ns, :] = cre_ref[0].astype(BF16)
        ccat_s[ns:2 * ns, :] = (-cim_ref[0]).astype(BF16)
        ar_s[...] = jnp.broadcast_to(abr, (nb, ns))
        ai_s[...] = jnp.broadcast_to(abi, (nb, ns))
        st_s[...] = jnp.zeros_like(st_s)

    bu_s[...] = _dotf(u_ref[...].astype(BF16), bcat_s[...])
    ar = ar_s[...]
    ai = ai_s[...]

    def body(j, carry):
        xr, xi = carry
        t = j + d * (tc - 1 - 2 * j)
        r0 = pl.multiple_of(t * nb, nb)
        nr = ar * xr - ai * xi + bu_s[pl.ds(r0, nb), 0:ns]
        ni = ar * xi + ai * xr + bu_s[pl.ds(r0, nb), ns:2 * ns]
        bu_s[pl.ds(r0, nb), 0:ns] = nr
        bu_s[pl.ds(r0, nb), ns:2 * ns] = ni
        return nr, ni

    xr, xi = lax.fori_loop(0, tc, body, (st_s[0], st_s[1]))
    st_s[0] = xr
    st_s[1] = xi
    y_ref[0] = _dotf(bu_s[...].astype(BF16), ccat_s[...])


def _s5_call(u_tm, are, aim, ls, bre, bim, cre, cim, n_ctx, nb):
    rows, gw = u_tm.shape
    tc = S5_CHUNK
    n_chunks = rows // (tc * nb)
    n_ctx_chunks = n_ctx // tc
    ns = S5_NGROUPS * S5_STATE

    def chunk_of(d, i):
        rev = jnp.where(i < n_ctx_chunks, n_ctx_chunks - 1 - i, n_chunks - 1 - (i - n_ctx_chunks))
        return jnp.where(d == 0, i, rev)

    vec = pl.BlockSpec((1, 1, ns), lambda d, i: (d, 0, 0))
    return pl.pallas_call(
        functools.partial(_s5_kernel, tc=tc, nb=nb),
        grid=(2, n_chunks),
        in_specs=[pl.BlockSpec((tc * nb, gw), lambda d, i: (chunk_of(d, i), 0)),
                  vec, vec, vec,
                  pl.BlockSpec((1, gw, ns), lambda d, i: (d, 0, 0)),
                  pl.BlockSpec((1, gw, ns), lambda d, i: (d, 0, 0)),
                  pl.BlockSpec((1, ns, gw), lambda d, i: (d, 0, 0)),
                  pl.BlockSpec((1, ns, gw), lambda d, i: (d, 0, 0))],
        out_specs=pl.BlockSpec((1, tc * nb, gw), lambda d, i: (d, chunk_of(d, i), 0)),
        out_shape=jax.ShapeDtypeStruct((2, rows, gw), F32),
        scratch_shapes=[pltpu.VMEM((nb, ns), F32), pltpu.VMEM((nb, ns), F32),
                        pltpu.VMEM((gw, 2 * ns), BF16), pltpu.VMEM((2 * ns, gw), BF16),
                        pltpu.VMEM((2, nb, ns), F32), pltpu.VMEM((tc * nb, 2 * ns), F32)],
        compiler_params=_params(("arbitrary", "arbitrary"), VMEM_LIMIT),
        name="s5_scan",
    )(u_tm, are, aim, ls, bre, bim, cre, cim)


def _glu_kernel(yf_ref, yb_ref, u_ref, d_ref, w_ref, b_ref, o_ref):
    y = yf_ref[...] + yb_ref[...] + u_ref[...] * d_ref[...]
    g = y * (0.5 * (1.0 + jnp.tanh(math.sqrt(2.0 / math.pi) * (y + 0.044715 * (y * y * y)))))
    z = _dotf(g.astype(BF16), w_ref[...]) + b_ref[...]
    o_ref[0] = (g * _sigmoid(z)).astype(BF16)


def _glu_call(ys, u_t, dsk, w, bias, nb):
    t = u_t.shape[0]
    tm = ROW_TILE
    gw = GROUP_W
    const = lambda bi, ti: (0, 0)
    return pl.pallas_call(
        _glu_kernel,
        grid=(nb, t // tm),
        in_specs=[pl.BlockSpec((None, tm, gw), lambda bi, ti: (0, ti, bi)),
                  pl.BlockSpec((None, tm, gw), lambda bi, ti: (1, ti, bi)),
                  pl.BlockSpec((tm, gw), lambda bi, ti: (ti, bi)),
                  pl.BlockSpec(dsk.shape, const), pl.BlockSpec(w.shape, const),
                  pl.BlockSpec(bias.shape, const)],
        out_specs=pl.BlockSpec((1, tm, gw), lambda bi, ti: (bi, ti, 0)),
        out_shape=jax.ShapeDtypeStruct((nb, t, gw), BF16),
        compiler_params=_params(("parallel", "parallel"), VMEM_LIMIT),
        name="s5_glu",
    )(ys, ys, u_t, dsk, w, bias)


def _out_kernel(ya_ref, yb_ref, yc_ref, yd_ref, w_ref, x_ref, g1_ref, sh2_ref, sc2_ref,
                nw1_ref, nw2_ref, wr_ref, br_ref, xn_ref, f_ref, te_ref, tg_ref):
    o = (_dotf(ya_ref[0], w_ref[0:256]) + _dotf(yb_ref[0], w_ref[256:512])
         + _dotf(yc_ref[0], w_ref[512:768]) + _dotf(yd_ref[0], w_ref[768:1024]))
    xn = x_ref[0] + g1_ref[0] * _rms(o, nw1_ref[...])
    xn_ref[0] = xn
    f = _rms(xn, nw2_ref[...]) * (1.0 + sc2_ref[0]) + sh2_ref[0]
    _store_token_tiles(f_ref, f)
    lg = _dot3(f, wr_ref[...]) + br_ref[...]
    tm = lg.shape[0]
    lane = lax.broadcasted_iota(I32, (tm, 128), 1)
    tops, idxs = [], []
    for _ in range(TOP_K):
        mx = jnp.max(lg, axis=-1, keepdims=True)
        idx = jnp.min(jnp.where(lg == mx, lane, 128), axis=-1, keepdims=True)
        tops.append(mx)
        idxs.append(idx)
        lg = jnp.where(lane == idx, NEG_INF, lg)
    ex = [jnp.exp(tv - tops[0]) for tv in tops]
    inv = 1.0 / (ex[0] + ex[1] + ex[2] + ex[3])
    l8 = lax.broadcasted_iota(I32, (tm, 8), 1)
    te = jnp.zeros((tm, 8), I32)
    tg = jnp.zeros((tm, 8), F32)
    for kk in range(TOP_K):
        te = jnp.where(l8 == kk, idxs[kk], te)
        tg = jnp.where(l8 == kk, ex[kk] * inv, tg)
    te_ref[0] = te
    tg_ref[0] = tg


def _out_call(ya, yb, yc, yd, w_out, xs, mod_l, nw1, nw2, wr, br, n_ctx_blk, blk0):
    b, t, d = xs.shape
    tm = ROW_TILE
    nblk = t // tm - blk0
    t_out = nblk * tm

    def mrow(bi, ti):
        return jnp.where(ti + blk0 < n_ctx_blk, b, bi)

    row3 = lambda bi, ti: (bi, ti + blk0, 0)
    out3 = lambda bi, ti: (bi, ti, 0)
    const = lambda bi, ti: (0, 0)
    modspec = lambda j: pl.BlockSpec((1, 1, d), lambda bi, ti: (mrow(bi, ti), 0, j))
    yspec = lambda y: pl.BlockSpec((1, tm, GROUP_W), row3 if y.shape[1] == t else out3)
    return pl.pallas_call(
        _out_kernel,
        grid=(b, nblk),
        in_specs=[yspec(ya), yspec(yb), yspec(yc), yspec(yd), pl.BlockSpec((d, d), const),
                  pl.BlockSpec((1, tm, d), row3), modspec(2), modspec(3), modspec(4),
                  pl.BlockSpec((1, d), const), pl.BlockSpec((1, d), const),
                  pl.BlockSpec((d, 128), const), pl.BlockSpec((1, 128), const)],
        out_specs=[pl.BlockSpec((1, tm, d), out3),
                   pl.BlockSpec((tm * d // 128, 128), lambda bi, ti: (bi * nblk + ti, 0)),
                   pl.BlockSpec((1, tm, 8), out3), pl.BlockSpec((1, tm, 8), out3)],
        out_shape=[jax.ShapeDtypeStruct((b, t_out, d), F32),
                   jax.ShapeDtypeStruct((b * t_out * d // 128, 128), F32),
                   jax.ShapeDtypeStruct((b, t_out, 8), I32),
                   jax.ShapeDtypeStruct((b, t_out, 8), F32)],
        compiler_params=_params(("parallel", "parallel"), VMEM_LIMIT),
        name="out_proj_router",
    )(ya, yb, yc, yd, w_out, xs, mod_l, mod_l, mod_l, nw1, nw2, wr, br)


def _rank_kernel(te_ref, rank_ref, cnt_ref, carry_ref):
    i = pl.program_id(0)

    @pl.when(i == 0)
    def _():
        carry_ref[...] = jnp.zeros_like(carry_ref)

    te = te_ref[...]
    tb = te.shape[0]
    lane = lax.broadcasted_iota(I32, (tb, 128), 1)
    l8 = lax.broadcasted_iota(I32, (tb, 8), 1)
    below = (lax.broadcasted_iota(I32, (tb, tb), 0)
             > lax.broadcasted_iota(I32, (tb, tb), 1))
    lstrict = jnp.where(below, 1.0, 0.0).astype(BF16)
    base = carry_ref[...]
    out = jnp.zeros((tb, 8), I32)
    for k in range(TOP_K):
        oh = jnp.where(lane == te[:, k:k + 1], 1.0, 0.0)
        before = _dotf(lstrict, oh.astype(BF16)) + base
        rank_k = jnp.sum(oh * before, axis=-1, keepdims=True)
        out = jnp.where(l8 == k, rank_k.astype(I32), out)
        base = base + jnp.sum(oh, axis=0, keepdims=True)
    rank_ref[...] = out
    carry_ref[...] = base
    cnt_ref[...] = base.astype(I32)


def _rank_call(te):
    n_tok = te.shape[0]
    tb = RANK_TILE
    return pl.pallas_call(
        _rank_kernel,
        grid=(n_tok // tb,),
        in_specs=[pl.BlockSpec((tb, 8), lambda i: (i, 0))],
        out_specs=[pl.BlockSpec((tb, 8), lambda i: (i, 0)), pl.BlockSpec((1, 128), lambda i: (0, 0))],
        out_shape=[jax.ShapeDtypeStruct((n_tok, 8), I32), jax.ShapeDtypeStruct((1, 128), I32)],
        scratch_shapes=[pltpu.VMEM((1, 128), F32)],
        compiler_params=_params(("arbitrary",)),
        name="moe_rank",
    )(te)


def _dispatch_kernel(pad_ref, dest_ref, f_hbm, xs_hbm, zbuf, sem, zsem, *, tb, tm, nt):
    i = pl.program_id(0)
    s = i % 2

    @pl.when(i == 0)
    def _():
        zbuf[...] = jnp.zeros_like(zbuf)
        fills = [pltpu.make_async_copy(
            zbuf, xs_hbm.at[pl.ds(pl.multiple_of(pad_ref[e] * nt, nt), tm * nt)], zsem)
            for e in range(N_EXPERTS)]
        for fill in fills:
            fill.start()
        for fill in fills:
            fill.wait()

        def fill_unused(blk, carry):
            tail = pltpu.make_async_copy(
                zbuf, xs_hbm.at[pl.ds(pl.multiple_of(blk * (tm * nt), tm * nt), tm * nt)], zsem)
            tail.start()
            tail.wait()
            return carry
        lax.fori_loop(pad_ref[N_EXPERTS], xs_hbm.shape[0] // (tm * nt), fill_unused, 0)

    def body(g, carry):
        r0 = g * DMA_UNROLL
        dsts = [pl.multiple_of(dest_ref[0, 0, r0 * TOP_K + j] * nt, nt)
                for j in range(DMA_UNROLL * TOP_K)]
        for u in range(DMA_UNROLL):
            src = f_hbm.at[pl.ds(pl.multiple_of((i * tb + r0 + u) * nt, nt), nt)]
            for k in range(TOP_K):
                pltpu.make_async_copy(src, xs_hbm.at[pl.ds(dsts[u * TOP_K + k], nt)],
                                      sem.at[s]).start()
        return carry

    lax.fori_loop(0, tb // DMA_UNROLL, body, 0)

    def wait_step(ss):
        rows = tb * TOP_K * nt
        pltpu.make_async_copy(f_hbm.at[pl.ds(0, rows)], xs_hbm.at[pl.ds(0, rows)], sem.at[ss]).wait()

    @pl.when(i >= 1)
    def _():
        wait_step(1 - s)

    @pl.when(i == pl.num_programs(0) - 1)
    def _():
        wait_step(s)


def _dispatch_call(pad_start, dest3, f_tiles, n_slots, tm):
    n_blk, _, per = dest3.shape
    tb = per // TOP_K
    nt = f_tiles.shape[0] // (n_blk * tb)
    grid_spec = pltpu.PrefetchScalarGridSpec(
        num_scalar_prefetch=1,
        grid=(n_blk,),
        in_specs=[pl.BlockSpec((1, 1, per), lambda i, pad: (i, 0, 0), memory_space=pltpu.SMEM),
                  pl.BlockSpec(memory_space=pl.ANY)],
        out_specs=pl.BlockSpec(memory_space=pl.ANY),
        scratch_shapes=[pltpu.VMEM((tm * nt, 128), F32), pltpu.SemaphoreType.DMA((2,)),
                        pltpu.SemaphoreType.DMA(())])
    return pl.pallas_call(
        functools.partial(_dispatch_kernel, tb=tb, tm=tm, nt=nt),
        grid_spec=grid_spec,
        out_shape=jax.ShapeDtypeStruct(((n_slots + tm) * nt, 128), F32),
        compiler_params=_params(("arbitrary",), VMEM_LIMIT),
        name="moe_dispatch",
    )(pad_start, dest3, f_tiles)


def _expert_kernel(be_ref, nu_ref, xs_ref, wgu_ref, bgu_ref, wd_ref, bd_ref, ys_ref,
                   wgu_s, wd_s, *, tm, d_ff):
    i = pl.program_id(0)
    nt = xs_ref.shape[0] // tm

    @pl.when(i < nu_ref[0])
    def _():
        @pl.when(jnp.logical_or(i == 0, be_ref[i] != be_ref[jnp.maximum(i - 1, 0)]))
        def _():
            for c in range(0, wgu_s.shape[0], 128):
                wgu_s[c:c + 128, :] = wgu_ref[0, c:c + 128, :].astype(BF16)
            for c in range(0, wd_s.shape[0], 128):
                wd_s[c:c + 128, :] = wd_ref[0, c:c + 128, :].astype(BF16)

        x = jnp.concatenate(_load_token_tiles(xs_ref, tm, nt), axis=1).astype(BF16)
        gu = _dotf(x, wgu_s[...]) + bgu_ref[0]
        gate = jnp.minimum(gu[:, 0:d_ff], SWIGLU_LIMIT)
        up = jnp.clip(gu[:, d_ff:2 * d_ff], -SWIGLU_LIMIT, SWIGLU_LIMIT)
        act = (up + 1.0) * gate * _sigmoid(SWIGLU_ALPHA * gate)
        _store_token_tiles(ys_ref, _dotf(act.astype(BF16), wd_s[...]) + bd_ref[0])

    @pl.when(i >= nu_ref[0])
    def _():
        ys_ref[...] = jnp.zeros_like(ys_ref)


def _expert_call(block_expert, n_used, xs_tiles, wgu, bgu, wd, bd, n_blocks, tm):
    d, two_ff = wgu.shape[1:]
    d_ff = two_ff // 2
    nt = d // 128
    ex = lambda i, be, nu: (be[i], 0, 0)
    grid_spec = pltpu.PrefetchScalarGridSpec(
        num_scalar_prefetch=2,
        grid=(n_blocks,),
        in_specs=[pl.BlockSpec((tm * nt, 128), lambda i, be, nu: (jnp.minimum(i, nu[0] - 1), 0)),
                  pl.BlockSpec((1, d, two_ff), ex), pl.BlockSpec((1, 1, two_ff), ex),
                  pl.BlockSpec((1, d_ff, d), ex), pl.BlockSpec((1, 1, d), ex)],
        out_specs=pl.BlockSpec((tm * nt, 128), lambda i, be, nu: (i, 0)),
        scratch_shapes=[pltpu.VMEM((d, two_ff), BF16), pltpu.VMEM((d_ff, d), BF16)])
    return pl.pallas_call(
        functools.partial(_expert_kernel, tm=tm, d_ff=d_ff),
        grid_spec=grid_spec,
        out_shape=jax.ShapeDtypeStruct((n_blocks * tm * nt, 128), F32),
        compiler_params=_params(("arbitrary",), VMEM_LIMIT),
        name="moe_experts",
    )(block_expert, n_used, xs_tiles, wgu, bgu, wd, bd)


def _route(te, rank, counts, n_tok, tm):
    n_blocks = -(-(n_tok * TOP_K + N_EXPERTS * (tm - 1)) // tm)
    padded = (counts + tm - 1) // tm * tm
    padded_end = jnp.cumsum(padded)
    group_start = padded_end - padded
    dest = jnp.take(group_start, te) + rank
    block_start = jnp.arange(n_blocks, dtype=I32) * tm
    block_expert = jnp.minimum(
        jnp.sum((padded_end[None, :] <= block_start[:, None]).astype(I32), axis=1), N_EXPERTS - 1)
    n_used = (padded_end[-1] // tm).astype(I32).reshape(1)
    pad_table = jnp.concatenate([group_start + counts, n_used])
    return dest, pad_table, block_expert, n_used, n_blocks


def _fin_kernel(dcur_ref, dnxt_ref, g_ref, x_ref, g2_ref, nw_ref, ys_hbm, out_ref, buf, sem,
                *, tb, nt):
    i = pl.program_id(0)
    s = i % 2

    def gather(dref, ss):
        def body(g, carry):
            r0 = g * DMA_UNROLL
            srcs = [pl.multiple_of(dref[0, 0, r0 * TOP_K + j] * nt, nt)
                    for j in range(DMA_UNROLL * TOP_K)]
            for u in range(DMA_UNROLL):
                for k in range(TOP_K):
                    dst = pl.multiple_of((k * tb + r0 + u) * nt, nt)
                    pltpu.make_async_copy(ys_hbm.at[pl.ds(srcs[u * TOP_K + k], nt)],
                                          buf.at[ss, pl.ds(dst, nt)], sem.at[ss]).start()
            return carry
        lax.fori_loop(0, tb // DMA_UNROLL, body, 0)

    @pl.when(i == 0)
    def _():
        gather(dcur_ref, 0)

    @pl.when(i + 1 < pl.num_programs(0))
    def _():
        gather(dnxt_ref, 1 - s)

    pltpu.make_async_copy(ys_hbm.at[pl.ds(0, TOP_K * tb * nt)], buf.at[s], sem.at[s]).wait()
    gates = g_ref[...]
    parts = []
    for j in range(nt):
        acc = None
        for k in range(TOP_K):
            v = buf[s, pl.ds(k * tb * nt + j, tb, stride=nt), :] * gates[:, k:k + 1]
            acc = v if acc is None else acc + v
        parts.append(acc)
    m = jnp.concatenate(parts, axis=1)
    out_ref[0] = x_ref[0] + g2_ref[0] * _rms(m, nw_ref[...])


def _fin_call(dest3, gates, ys_tiles, xn, mod_l, nw, n_ctx_blk, blk0):
    b, t, d = xn.shape
    tb = ROW_TILE
    nblk = t // tb
    nt = d // 128
    n_blk = b * nblk

    def mrow(i):
        return jnp.where(i % nblk + blk0 < n_ctx_blk, b, i // nblk)

    smem = functools.partial(pl.BlockSpec, memory_space=pltpu.SMEM)
    return pl.pallas_call(
        functools.partial(_fin_kernel, tb=tb, nt=nt),
        grid=(n_blk,),
        in_specs=[smem((1, 1, tb * TOP_K), lambda i: (i, 0, 0)),
                  smem((1, 1, tb * TOP_K), lambda i: (jnp.minimum(i + 1, n_blk - 1), 0, 0)),
                  pl.BlockSpec((tb, 8), lambda i: (i, 0)),
                  pl.BlockSpec((1, tb, d), lambda i: (i // nblk, i % nblk, 0)),
                  pl.BlockSpec((1, 1, d), lambda i: (mrow(i), 0, 5)),
                  pl.BlockSpec((1, d), lambda i: (0, 0)),
                  pl.BlockSpec(memory_space=pl.ANY)],
        out_specs=pl.BlockSpec((1, tb, d), lambda i: (i // nblk, i % nblk, 0)),
        out_shape=jax.ShapeDtypeStruct((b, t, d), F32),
        scratch_shapes=[pltpu.VMEM((2, TOP_K * tb * nt, 128), F32), pltpu.SemaphoreType.DMA((2,))],
        compiler_params=_params(("arbitrary",), VMEM_LIMIT),
        name="moe_combine",
    )(dest3, dest3, gates, xn, mod_l, nw, ys_tiles)


def _in_proj_columns():
    cols = np.full((IN_COLS,), -1, np.int64)
    for sec in range(2):
        for n in range(256):
            part, hm, j = n // 128, (n % 128) // 16, n % 16
            m, h = hm // 4, hm % 4
            cols[sec * 256 + n] = sec * 256 + h * 64 + m * 32 + part * 16 + j
    cols[512:768] = np.arange(512, 768)
    o, s = IN_DA, 768
    cols[o:o + 1024] = s + np.arange(1024)
    o, s = o + IN_ML, s + 1024
    cols[o:o + 16] = s + np.arange(16)
    o, s = o + IN_G, s + 16
    cols[o:o + MLA_Q_RANK] = s + np.arange(MLA_Q_RANK)
    cols[o + 256:o + 256 + MLA_KV_RANK] = s + MLA_Q_RANK + np.arange(MLA_KV_RANK)
    cols[o + 384 + 64:o + 384 + 96] = s + MLA_Q_RANK + MLA_KV_RANK + np.arange(MLA_ROPE)
    o, s = o + IN_MLA, s + MLA_Q_RANK + MLA_KV_RANK + MLA_ROPE
    cols[o:o + 256] = s + np.arange(256)
    return cols


def _take_cols(w, cols):
    valid = jnp.asarray(cols >= 0)
    return jnp.where(valid, jnp.take(w, jnp.asarray(np.maximum(cols, 0)), axis=-1), 0.0)


def _rope_tables(n_ctx, n_lat):
    pos = jnp.arange(n_lat)
    inv = ROPE_THETA ** (-jnp.arange(8, dtype=F32) / 8)
    ang = jnp.concatenate([(pos // GRID_W)[:, None] * inv, (pos % GRID_W)[:, None] * inv], axis=-1)
    cos = jnp.concatenate([jnp.ones((n_ctx, 16), F32), jnp.cos(ang)], axis=0)
    sin = jnp.concatenate([jnp.zeros((n_ctx, 16), F32), jnp.sin(ang)], axis=0)
    t = n_ctx + n_lat
    cos_da, sin_da = jnp.tile(cos, (1, 8)), jnp.tile(sin, (1, 8))
    one, zero = jnp.ones((t, 64), F32), jnp.zeros((t, 64), F32)
    z16, z32 = jnp.zeros((t, 16), F32), jnp.zeros((t, 32), F32)
    cm = jnp.concatenate([one, cos, cos, jnp.ones((t, 32), F32)], axis=1)
    sa = jnp.concatenate([zero, -sin, z16, z32], axis=1)
    sb = jnp.concatenate([zero, z16, sin, z32], axis=1)
    return cos_da, sin_da, cm, sa, sb


def _mla_weights(w_uq, w_ukv):
    hd = MLA_NOPE + MLA_ROPE
    wq = jnp.zeros((256, 512), F32)
    wkv = jnp.zeros((MLA_KV_RANK, 768), F32)
    for h in range(MLA_HEADS):
        wq = wq.at[:MLA_Q_RANK, 128 * h:128 * h + hd].set(w_uq[:, hd * h:hd * (h + 1)])
        wkv = wkv.at[:, 128 * h:128 * h + MLA_NOPE].set(w_ukv[:, 128 * h:128 * h + MLA_NOPE])
        wkv = wkv.at[:, 512 + 64 * h:512 + 64 * (h + 1)].set(w_ukv[:, 128 * h + MLA_NOPE:128 * (h + 1)])
    return wq.astype(BF16), wkv.astype(BF16)


def _s5_layout(a_re, a_im, log_step, b_re, b_im, c_re, c_im):
    ns = S5_NGROUPS * S5_STATE
    eye = jnp.eye(S5_NGROUPS, dtype=F32)
    are = a_re.reshape(2, 1, ns)
    aim = a_im.reshape(2, 1, ns)
    ls = jnp.repeat(log_step, S5_STATE, axis=-1).reshape(2, 1, ns)
    bd_b = lambda w: jnp.einsum("dgph,gk->dghkp", w, eye).reshape(2, GROUP_W, ns)
    bd_c = lambda w: jnp.einsum("dghp,gk->dgpkh", w, eye).reshape(2, ns, GROUP_W)
    return are, aim, ls, bd_b(b_re), bd_b(b_im), bd_c(c_re), bd_c(c_im)


def kernel(x, c, ctx, c_ctx, w_mod, b_mod, norm_w, w_in, w_out, da_lambda, da_subln, ml_conv_w,
           ml_conv_b, ml_gate_b, ml_norm, mla_q_norm, mla_w_uq, mla_kv_norm, mla_w_ukv, s5_a_re,
           s5_a_im, s5_log_step, s5_b_re, s5_b_im, s5_c_re, s5_c_im, s5_d, s5_w_glu, s5_b_glu,
           moe_w_router, moe_b_router, moe_w_gate_up, moe_b_gate_up, moe_w_down, moe_b_down):
    bsz, n_lat, d = x.shape
    n_ctx = ctx.shape[1]
    t = n_ctx + n_lat
    depth = w_mod.shape[0]
    tm = ROW_TILE
    assert n_ctx % tm == 0 and n_lat % tm == 0 and bsz % 8 == 0 and bsz < 16
    assert n_ctx % ML_CHUNK == 0 and n_ctx % S5_CHUNK == 0
    n_ctx_blk = n_ctx // tm

    cc = jnp.zeros((16, d), F32).at[:bsz].set(c).at[bsz].set(c_ctx)
    mod = _mod_call(cc, w_mod, b_mod)
    cos_da, sin_da, cm, sa, sb = _rope_tables(n_ctx, n_lat)
    in_cols = _in_proj_columns()
    xs = jnp.concatenate([ctx, x], axis=1)

    for l in range(depth):
        last = l == depth - 1
        lambda_init = 0.8 - 0.6 * math.exp(-0.3 * l)
        mod_l = mod[l].reshape(16, 1, 6 * d)
        w_in_p = _take_cols(w_in[l], in_cols).astype(BF16)
        p_da, p_ml, p_g, p_mla, u_t = _in_call(xs, mod_l, norm_w[l, 0].reshape(1, d), w_in_p,
                                               cos_da, sin_da, n_ctx_blk)

        da_extra = [da_lambda[l], jnp.tile(da_subln[l], 4).reshape(1, GROUP_W)]
        da_kern = functools.partial(_da_kernel, lambda_init=lambda_init)
        da_kw = dict(q_col=0, k_col=1, v_col=2, q_w=GROUP_W)
        ya = _attn_call(da_kern, "diff_attn", p_da, p_da, p_da, da_extra, q_blk0=n_ctx_blk,
                        n_q_blk=n_lat // tm, n_keys=t, **da_kw)
        wq, wkv = _mla_weights(mla_w_uq[l], mla_w_ukv[l])
        qn = jnp.zeros((1, 256), F32).at[0, :MLA_Q_RANK].set(mla_q_norm[l])
        q_mla, k_mla, v_mla = _mla_prep_call(p_mla, qn, wq, mla_kv_norm[l].reshape(1, -1), wkv,
                                             cm, sa, sb)
        mla_kw = dict(q_col=0, k_col=0, v_col=0, q_w=512)
        yc = _attn_call(_mla_attn_kernel, "mla_attn", q_mla, k_mla, v_mla, [], q_blk0=n_ctx_blk,
                        n_q_blk=n_lat // tm, n_keys=t, **mla_kw)
        if not last:
            ya_c = _attn_call(da_kern, "diff_attn_ctx", p_da, p_da, p_da, da_extra, q_blk0=0,
                              n_q_blk=n_ctx_blk, n_keys=n_ctx, **da_kw)
            yc_c = _attn_call(_mla_attn_kernel, "mla_attn_ctx", q_mla, k_mla, v_mla, [], q_blk0=0,
                              n_q_blk=n_ctx_blk, n_keys=n_ctx, **mla_kw)
            ya = jnp.concatenate([ya_c, ya], axis=1)
            yc = jnp.concatenate([yc_c, yc], axis=1)

        gb = jnp.zeros((1, 128), F32).at[0, :16].set(ml_gate_b[l])
        yb = _ml_call(p_ml, p_g, ml_conv_w[l], ml_conv_b[l].reshape(1, -1), gb,
                      ml_norm[l].reshape(1, -1), n_ctx)

        s5p = _s5_layout(s5_a_re[l], s5_a_im[l], s5_log_step[l], s5_b_re[l], s5_b_im[l],
                         s5_c_re[l], s5_c_im[l])
        ys = _s5_call(u_t.reshape(t * bsz, GROUP_W), *s5p, n_ctx, bsz)
        yd = _glu_call(ys.reshape(2, t, bsz * GROUP_W), u_t, s5_d[l].reshape(1, -1),
                       s5_w_glu[l].astype(BF16), s5_b_glu[l].reshape(1, -1), bsz)

        blk0 = n_ctx_blk if last else 0
        wr = jnp.zeros((d, 128), F32).at[:, :N_EXPERTS].set(moe_w_router[l])
        br = jnp.full((1, 128), -1e30, F32).at[0, :N_EXPERTS].set(moe_b_router[l])
        xn, f, te, tg = _out_call(ya, yb, yc, yd, w_out[l].astype(BF16), xs, mod_l,
                                  norm_w[l, 1].reshape(1, d), norm_w[l, 2].reshape(1, d), wr, br,
                                  n_ctx_blk, blk0)

        t_moe = t - blk0 * tm
        n_tok = bsz * t_moe
        te = te.reshape(n_tok, 8)
        rank, cnt = _rank_call(te)
        dest, pad_start, be, nu, n_blocks = _route(te[:, :TOP_K], rank[:, :TOP_K],
                                                   cnt[0, :N_EXPERTS], n_tok, MOE_TILE)
        dest3 = dest.reshape(n_tok // tm, 1, tm * TOP_K)
        xs_tiles = _dispatch_call(pad_start, dest3, f, n_blocks * MOE_TILE, MOE_TILE)
        ys_tiles = _expert_call(be, nu, xs_tiles, moe_w_gate_up[l],
                                moe_b_gate_up[l].reshape(N_EXPERTS, 1, -1), moe_w_down[l],
                                moe_b_down[l].reshape(N_EXPERTS, 1, -1), n_blocks, MOE_TILE)
        xs = _fin_call(dest3, tg.reshape(n_tok, 8), ys_tiles, xn, mod_l,
                       norm_w[l, 3].reshape(1, d), n_ctx_blk, blk0)
    return xs
```

```python
import functools
import math

import numpy as np
import jax
import jax.numpy as jnp
from jax import lax
from jax.experimental import pallas as pl
from jax.experimental.pallas import tpu as pltpu

F32, BF16, I32 = jnp.float32, jnp.bfloat16, jnp.int32
NORM_EPS = 1e-6
GRID_W = 64
ROPE_THETA = 10000.0
GROUP_W = 256
DA_QK = 32
ML_CHUNK = 128
MLA_HEADS, MLA_NOPE, MLA_ROPE, MLA_Q_RANK, MLA_KV_RANK = 4, 64, 32, 192, 128
S5_NGROUPS, S5_GROUP, S5_STATE = 16, 16, 64
N_EXPERTS, TOP_K = 32, 4
SWIGLU_ALPHA, SWIGLU_LIMIT = 1.702, 7.0
NEG_INF = float("-inf")

ROW_TILE = 256
MOE_TILE = 256
S5_CHUNK = 64
RANK_TILE = 512
DMA_UNROLL = 8
VMEM_LIMIT = 56 * 1024 * 1024

IN_DA, IN_ML, IN_G, IN_MLA, IN_S5 = 768, 1024, 128, 512, 256
IN_COLS = IN_DA + IN_ML + IN_G + IN_MLA + IN_S5


def _params(sem, vmem=None):
    return pltpu.CompilerParams(dimension_semantics=sem, vmem_limit_bytes=vmem)


def _dotf(a, b):
    return jnp.dot(a, b, preferred_element_type=F32)


def _dot_nt(a, b):
    return lax.dot_general(a, b, (((1,), (1,)), ((), ())), preferred_element_type=F32)


def _split2(a):
    hi = a.astype(BF16)
    lo = (a - hi.astype(F32)).astype(BF16)
    return hi, lo


def _dot3(a, b):
    ah, al = _split2(a)
    bh, bl = _split2(b)
    return _dotf(ah, bh) + _dotf(ah, bl) + _dotf(al, bh)


def _dot_exact_lhs(lhs_b, a):
    a1 = a.astype(BF16)
    r1 = a - a1.astype(F32)
    a2 = r1.astype(BF16)
    a3 = (r1 - a2.astype(F32)).astype(BF16)
    return _dotf(lhs_b, a1) + _dotf(lhs_b, a2) + _dotf(lhs_b, a3)


def _rms(x, w):
    ms = jnp.mean(x * x, axis=-1, keepdims=True)
    return x * lax.rsqrt(ms + NORM_EPS) * w


def _head_rms(a, width):
    n = a.shape[-1]
    sh = int(math.log2(width))
    r = lax.broadcasted_iota(I32, (n, n), 0) >> sh
    c = lax.broadcasted_iota(I32, (n, n), 1) >> sh
    g = jnp.where(r == c, 1.0 / width, 0.0).astype(BF16)
    hi, lo = _split2(a * a)
    ms = _dotf(hi, g) + _dotf(lo, g)
    return a * lax.rsqrt(ms + NORM_EPS)


def _store_token_tiles(ref, val):
    tm, d = val.shape
    nt = d // 128
    for j in range(nt):
        ref[pl.ds(j, tm, stride=nt), :] = val[:, 128 * j:128 * (j + 1)]


def _load_token_tiles(ref, tm, nt, base=0, lead=()):
    parts = [ref[lead + (pl.ds(base + j, tm, stride=nt), slice(None))] for j in range(nt)]
    return parts


def _sigmoid(x):
    return jax.nn.sigmoid(x)


def _log_sigmoid(x):
    return jnp.minimum(x, 0.0) - jnp.log(1.0 + jnp.exp(-jnp.abs(x)))


def _mod_kernel(c_ref, w_ref, b_ref, o_ref):
    c = c_ref[...]
    o_ref[0] = _dot3(c * _sigmoid(c), w_ref[0]) + b_ref[0]


def _mod_call(cc, w_mod, b_mod):
    n_layers, d, n = w_mod.shape
    tn = 1536
    return pl.pallas_call(
        _mod_kernel,
        grid=(n_layers, n // tn),
        in_specs=[pl.BlockSpec((16, d), lambda l, j: (0, 0)),
                  pl.BlockSpec((1, d, tn), lambda l, j: (l, 0, j)),
                  pl.BlockSpec((1, 1, tn), lambda l, j: (l, 0, j))],
        out_specs=pl.BlockSpec((1, 16, tn), lambda l, j: (l, 0, j)),
        out_shape=jax.ShapeDtypeStruct((n_layers, 16, n), F32),
        compiler_params=_params(("parallel", "parallel"), VMEM_LIMIT),
        name="mod_vectors",
    )(cc, w_mod, b_mod.reshape(n_layers, 1, n))


def _in_kernel(x_ref, sh_ref, sc_ref, nw_ref, w_ref, c_ref, s_ref,
               da_ref, ml_ref, g_ref, mla_ref, s5_ref, *, qscale):
    h = _rms(x_ref[0], nw_ref[...]) * (1.0 + sc_ref[0]) + sh_ref[0]
    hb = h.astype(BF16)
    da = _dotf(hb, w_ref[:, 0:IN_DA])
    c = c_ref[...]
    s = s_ref[...]
    q1, q2, k1, k2 = da[:, 0:128], da[:, 128:256], da[:, 256:384], da[:, 384:512]
    da_ref[0, :, 0:128] = ((q1 * c - q2 * s) * qscale).astype(BF16)
    da_ref[0, :, 128:256] = ((q2 * c + q1 * s) * qscale).astype(BF16)
    da_ref[0, :, 256:384] = (k1 * c - k2 * s).astype(BF16)
    da_ref[0, :, 384:512] = (k2 * c + k1 * s).astype(BF16)
    da_ref[0, :, 512:768] = da[:, 512:768].astype(BF16)
    o = IN_DA
    ml_ref[0] = _dotf(hb, w_ref[:, o:o + IN_ML])
    o += IN_ML
    g_ref[0] = _dotf(hb, w_ref[:, o:o + IN_G])
    o += IN_G
    mla_ref[0] = _dotf(hb, w_ref[:, o:o + IN_MLA])
    o += IN_MLA
    s5_ref[...] = _dotf(hb, w_ref[:, o:o + IN_S5])


def _in_call(xs, mod_l, nw, w_in_p, cos_da, sin_da, n_ctx_blk):
    b, t, d = xs.shape
    tm = ROW_TILE

    def mrow(bi, ti):
        return jnp.where(ti < n_ctx_blk, b, bi)

    row3 = lambda bi, ti: (bi, ti, 0)
    return pl.pallas_call(
        functools.partial(_in_kernel, qscale=DA_QK ** -0.5),
        grid=(b, t // tm),
        in_specs=[pl.BlockSpec((1, tm, d), row3),
                  pl.BlockSpec((1, 1, d), lambda bi, ti: (mrow(bi, ti), 0, 0)),
                  pl.BlockSpec((1, 1, d), lambda bi, ti: (mrow(bi, ti), 0, 1)),
                  pl.BlockSpec((1, d), lambda bi, ti: (0, 0)),
                  pl.BlockSpec((d, IN_COLS), lambda bi, ti: (0, 0)),
                  pl.BlockSpec((tm, 128), lambda bi, ti: (ti, 0)),
                  pl.BlockSpec((tm, 128), lambda bi, ti: (ti, 0))],
        out_specs=[pl.BlockSpec((1, tm, IN_DA), row3),
                   pl.BlockSpec((1, tm, IN_ML), row3),
                   pl.BlockSpec((1, tm, IN_G), row3),
                   pl.BlockSpec((1, tm, IN_MLA), row3),
                   pl.BlockSpec((tm, IN_S5), lambda bi, ti: (ti, bi))],
        out_shape=[jax.ShapeDtypeStruct((b, t, IN_DA), BF16),
                   jax.ShapeDtypeStruct((b, t, IN_ML), F32),
                   jax.ShapeDtypeStruct((b, t, IN_G), F32),
                   jax.ShapeDtypeStruct((b, t, IN_MLA), F32),
                   jax.ShapeDtypeStruct((t, b * IN_S5), F32)],
        compiler_params=_params(("parallel", "parallel"), VMEM_LIMIT),
        name="in_proj",
    )(xs, mod_l, mod_l, nw, w_in_p, cos_da, sin_da)


def _softmax_rows(s):
    mx = jnp.max(s, axis=-1, keepdims=True)
    p = jnp.exp(s - mx)
    return p, jnp.sum(p, axis=-1, keepdims=True)


def _da_kernel(q_ref, k_ref, v_ref, lam_ref, sub_ref, o_ref, *, lambda_init):
    q = q_ref[0]
    k = k_ref[0]
    v = v_ref[0]
    lp = lam_ref[...]
    lam = (jnp.exp(jnp.sum(lp[0:1] * lp[1:2], axis=-1, keepdims=True))
           - jnp.exp(jnp.sum(lp[2:3] * lp[3:4], axis=-1, keepdims=True)) + lambda_init)
    lane = lax.broadcasted_iota(I32, (1, GROUP_W), 1)
    grp = (lane & 127) >> 4
    head = lane >> 6
    acc = jnp.zeros((q.shape[0], GROUP_W), F32)
    for h in range(4):
        w = None
        for m in range(2):
            qm = jnp.where(grp == m * 4 + h, q, jnp.zeros_like(q))
            p, l = _softmax_rows(_dot_nt(qm, k))
            w = p * (1.0 / l) if m == 0 else w - p * (lam / l)
        vm = jnp.where(head == h, v, jnp.zeros_like(v))
        acc = acc + _dotf(w.astype(BF16), vm)
    y = _head_rms(acc, 64) * sub_ref[...] * (1.0 - lambda_init)
    o_ref[0] = y.astype(BF16)


def _mla_attn_kernel(q_ref, k_ref, v_ref, o_ref):
    q = q_ref[0]
    k = k_ref[0]
    v = v_ref[0]
    head = lax.broadcasted_iota(I32, (1, GROUP_W), 1) >> 6
    acc = jnp.zeros((q.shape[0], GROUP_W), F32)
    for h in range(MLA_HEADS):
        sl = slice(128 * h, 128 * (h + 1))
        p, l = _softmax_rows(_dot_nt(q[:, sl], k[:, sl]))
        vm = jnp.where(head == h, v, jnp.zeros_like(v))
        acc = acc + _dotf((p * (1.0 / l)).astype(BF16), vm)
    o_ref[0] = acc.astype(BF16)


def _attn_call(kernel, name, q_arr, k_arr, v_arr, extra, *, q_blk0, n_q_blk, n_keys,
               q_col, k_col, v_col, q_w):
    b = q_arr.shape[0]
    tq = ROW_TILE
    in_specs = [pl.BlockSpec((1, tq, q_w), lambda bi, qi: (bi, qi + q_blk0, q_col)),
                pl.BlockSpec((1, n_keys, q_w), lambda bi, qi: (bi, 0, k_col)),
                pl.BlockSpec((1, n_keys, GROUP_W), lambda bi, qi: (bi, 0, v_col))]
    in_specs += [pl.BlockSpec(e.shape, lambda bi, qi: (0, 0)) for e in extra]
    return pl.pallas_call(
        kernel,
        grid=(b, n_q_blk),
        in_specs=in_specs,
        out_specs=pl.BlockSpec((1, tq, GROUP_W), lambda bi, qi: (bi, qi, 0)),
        out_shape=jax.ShapeDtypeStruct((b, n_q_blk * tq, GROUP_W), BF16),
        compiler_params=_params(("parallel", "arbitrary"), VMEM_LIMIT),
        name=name,
    )(q_arr, k_arr, v_arr, *extra)


def _mla_prep_kernel(p_ref, qn_ref, wq_ref, kvn_ref, wkv_ref, c_ref, sa_ref, sb_ref,
                     q_ref, k_ref, v_ref, *, scale):
    p = p_ref[0]
    cq, ckv, kr = p[:, 0:256], p[:, 256:384], p[:, 384:512]
    msq = jnp.sum(cq * cq, axis=-1, keepdims=True) * (1.0 / MLA_Q_RANK)
    qn = (cq * lax.rsqrt(msq + NORM_EPS) * qn_ref[...]).astype(BF16)
    q = _dotf(qn, wq_ref[...])
    c = c_ref[...]
    sa = sa_ref[...]
    sb = sb_ref[...]

    def rope(a):
        return a * c + pltpu.roll(a, 112, 1) * sa + pltpu.roll(a, 16, 1) * sb

    for h in range(MLA_HEADS):
        sl = slice(128 * h, 128 * (h + 1))
        q_ref[0, :, sl] = (rope(q[:, sl]) * scale).astype(BF16)
    kvn = (_rms(ckv, kvn_ref[...])).astype(BF16)
    kv = _dotf(kvn, wkv_ref[...])
    krr = rope(kr)
    for h in range(MLA_HEADS):
        sl = slice(128 * h, 128 * (h + 1))
        k_ref[0, :, sl] = (kv[:, sl] + krr).astype(BF16)
    v_ref[0] = kv[:, 512:768].astype(BF16)


def _mla_prep_call(p_mla, qn, wq, kvn, wkv, cm, sa, sb):
    b, t, _ = p_mla.shape
    tm = ROW_TILE
    row3 = lambda bi, ti: (bi, ti, 0)
    const = lambda bi, ti: (0, 0)
    tab = lambda bi, ti: (ti, 0)
    return pl.pallas_call(
        functools.partial(_mla_prep_kernel, scale=(MLA_NOPE + MLA_ROPE) ** -0.5),
        grid=(b, t // tm),
        in_specs=[pl.BlockSpec((1, tm, IN_MLA), row3),
                  pl.BlockSpec(qn.shape, const), pl.BlockSpec(wq.shape, const),
                  pl.BlockSpec(kvn.shape, const), pl.BlockSpec(wkv.shape, const),
                  pl.BlockSpec((tm, 128), tab), pl.BlockSpec((tm, 128), tab),
                  pl.BlockSpec((tm, 128), tab)],
        out_specs=[pl.BlockSpec((1, tm, 512), row3), pl.BlockSpec((1, tm, 512), row3),
                   pl.BlockSpec((1, tm, GROUP_W), row3)],
        out_shape=[jax.ShapeDtypeStruct((b, t, 512), BF16),
                   jax.ShapeDtypeStruct((b, t, 512), BF16),
                   jax.ShapeDtypeStruct((b, t, GROUP_W), BF16)],
        compiler_params=_params(("parallel", "parallel"), VMEM_LIMIT),
        name="mla_prep",
    )(p_mla, qn, wq, kvn, wkv, cm, sa, sb)


def _ml_kernel(p_ref, g_ref, cw_ref, cb_ref, gb_ref, nw_ref, y_ref, hf_ref, c_ref, m_ref,
               *, n_ctx_chunks, n_chunks):
    cl = ML_CHUNK
    t_total = n_chunks * cl
    row = lax.broadcasted_iota(I32, (cl, 1), 0)
    ti = lax.broadcasted_iota(I32, (cl, cl), 0)
    si = lax.broadcasted_iota(I32, (cl, cl), 1)
    lane128 = lax.broadcasted_iota(I32, (1, 128), 1)
    lane256 = lax.broadcasted_iota(I32, (1, 256), 1)
    lane384 = lax.broadcasted_iota(I32, (1, 384), 1)
    rowhead = lax.broadcasted_iota(I32, (256, 1), 0) >> 6
    r384 = lax.broadcasted_iota(I32, (256, 384), 0) >> 6
    c384 = lax.broadcasted_iota(I32, (256, 384), 1)
    bdmask = jnp.logical_or(jnp.logical_and(c384 < 256, (c384 >> 6) == r384), c384 == 256 + r384)
    hmask = [(lane256 >> 6) == h for h in range(4)]
    cmask = [jnp.logical_or(jnp.logical_and(lane384 < 256, (lane384 >> 6) == h), lane384 == 256 + h)
             for h in range(4)]
    is_f = jnp.logical_and(((lane128 >> 2) & 1) == 1, lane128 < 16)
    ones_tail = jnp.broadcast_to(jnp.where(lane128 < 4, 1.0, 0.0).astype(F32), (cl, 128))
    w0, w1, w2 = cw_ref[0:1], cw_ref[1:2], cw_ref[2:3]
    cb = cb_ref[...]
    gb = gb_ref[...]
    nw = nw_ref[...]

    def chunk(c, reverse):
        s0 = pl.multiple_of(c * cl, cl)
        x = p_ref[0, pl.ds(s0, cl), 0:512]
        sp = pl.multiple_of(jnp.maximum(s0 - 8, 0), 8)
        sn = pl.multiple_of(jnp.minimum(s0 + cl, t_total - 8), 8)
        has_prev = jnp.logical_and(c != 0, c != n_ctx_chunks).astype(F32)
        has_next = jnp.logical_and(c != n_ctx_chunks - 1, c != n_chunks - 1).astype(F32)
        prev_row = p_ref[0, pl.ds(sp, 8), 0:512][7:8] * has_prev
        next_row = p_ref[0, pl.ds(sn, 8), 0:512][0:1] * has_next
        xp = jnp.where(row == 0, prev_row, pltpu.roll(x, 1, 0))
        xn = jnp.where(row == cl - 1, next_row, pltpu.roll(x, cl - 1, 0))
        z = xp * w0 + x * w1 + xn * w2 + cb
        qk = z * _sigmoid(z)
        q = qk[:, 0:256]
        k = qk[:, 256:512] * (64 ** -0.5)
        v = p_ref[0, pl.ds(s0, cl), 512:768]
        o = p_ref[0, pl.ds(s0, cl), 768:1024]
        g = g_ref[0, pl.ds(s0, cl), :] + gb
        gsel = jnp.where(is_f, _log_sigmoid(g), g)
        tri = (si >= ti) if reverse else (si <= ti)
        bc = _dot_exact_lhs(jnp.where(tri, 1.0, 0.0).astype(BF16), gsel)
        g_t = gsel.T
        bc_t = bc.T
        tot = bc[0:1] if reverse else bc[cl - 1:cl]
        qb = q.astype(BF16)
        kb = k.astype(BF16)
        vaug = jnp.concatenate([v, ones_tail], axis=1).astype(BF16)
        m_all = m_ref[...]
        off = 8 if reverse else 0
        acc = jnp.zeros((cl, 384), F32)
        wi_full = jnp.zeros((cl, 384), F32)
        emt_full = jnp.zeros((cl, 256), F32)
        kw = jnp.zeros((cl, 256), F32)
        dec_full = jnp.zeros((256, 1), F32)
        m_new_all = m_all
        for h in range(4):
            il, fl = off + h, off + 4 + h
            bcol, brow = bc[:, fl:fl + 1], bc_t[fl:fl + 1, :]
            irow, icol = g_t[il:il + 1, :], gsel[:, il:il + 1]
            m_h = m_all[:, h:h + 1]
            lw = jnp.where(tri, bcol - brow + irow, NEG_INF)
            linter = bcol + m_h
            mt = jnp.maximum(linter, jnp.max(lw, axis=-1, keepdims=True))
            dm = jnp.exp(lw - mt)
            s = _dot_nt(jnp.where(hmask[h], qb, jnp.zeros_like(qb)), kb)
            acc = acc + _dotf((s * dm).astype(BF16),
                              jnp.where(cmask[h], vaug, jnp.zeros_like(vaug)))
            wi_full = jnp.where(cmask[h], jnp.exp(linter - mt), wi_full)
            emt_full = jnp.where(hmask[h], jnp.exp(-mt), emt_full)
            b_last = tot[:, fl:fl + 1]
            lupd = b_last - bcol + icol
            m_new = jnp.maximum(b_last + m_h, jnp.max(lupd, axis=0, keepdims=True))
            kw = jnp.where(hmask[h], k * jnp.exp(lupd - m_new), kw)
            dec_full = jnp.where(rowhead == h, jnp.exp(b_last + m_h - m_new), dec_full)
            m_new_all = jnp.where(lane128 == h, m_new, m_new_all)
        c_aug = c_ref[...]
        nd = acc + wi_full * _dotf(qb, c_aug.astype(BF16))
        den_full = jnp.zeros((cl, 256), F32)
        for h in range(4):
            den_full = jnp.where(hmask[h], nd[:, 256 + h:257 + h], den_full)
        hout = nd[:, 0:256] / jnp.maximum(jnp.abs(den_full), emt_full)
        upd = _dotf(kw.T.astype(BF16), vaug)
        c_ref[...] = dec_full * c_aug + jnp.where(bdmask, upd, 0.0)
        m_ref[...] = m_new_all
        return s0, hout, o

    def fwd_body(i, carry):
        s0, hout, _ = chunk(i, False)
        hf_ref[pl.ds(s0, cl), :] = hout
        return carry

    def bwd_body(i, carry):
        c = jnp.where(i < n_ctx_chunks, n_ctx_chunks - 1 - i, n_chunks - 1 - (i - n_ctx_chunks))
        s0, hout, o = chunk(c, True)
        gated = _sigmoid(o) * (hf_ref[pl.ds(s0, cl), :] + hout)
        y_ref[0, pl.ds(s0, cl), :] = (_head_rms(gated, 64) * nw).astype(BF16)
        return carry

    c_ref[...] = jnp.zeros_like(c_ref)
    m_ref[...] = jnp.zeros_like(m_ref)
    lax.fori_loop(0, n_chunks, fwd_body, 0)
    c_ref[...] = jnp.zeros_like(c_ref)
    m_ref[...] = jnp.zeros_like(m_ref)
    lax.fori_loop(0, n_chunks, bwd_body, 0)


def _ml_call(p_ml, p_g, cw, cb, gb, nw, n_ctx):
    b, t, _ = p_ml.shape
    const = lambda bi: (0, 0)
    return pl.pallas_call(
        functools.partial(_ml_kernel, n_ctx_chunks=n_ctx // ML_CHUNK, n_chunks=t // ML_CHUNK),
        grid=(b,),
        in_specs=[pl.BlockSpec((1, t, IN_ML), lambda bi: (bi, 0, 0)),
                  pl.BlockSpec((1, t, IN_G), lambda bi: (bi, 0, 0)),
                  pl.BlockSpec(cw.shape, const), pl.BlockSpec(cb.shape, const),
                  pl.BlockSpec(gb.shape, const), pl.BlockSpec(nw.shape, const)],
        out_specs=pl.BlockSpec((1, t, GROUP_W), lambda bi: (bi, 0, 0)),
        out_shape=jax.ShapeDtypeStruct((b, t, GROUP_W), BF16),
        scratch_shapes=[pltpu.VMEM((t, GROUP_W), F32), pltpu.VMEM((256, 384), F32),
                        pltpu.VMEM((1, 128), F32)],
        compiler_params=_params(("parallel",), VMEM_LIMIT),
        name="mlstm",
    )(p_ml, p_g, cw, cb, gb, nw)


def _s5_kernel(u_ref, are_ref, aim_ref, ls_ref, bre_ref, bim_ref, cre_ref, cim_ref, y_ref,
               ar_s, ai_s, bcat_s, ccat_s, st_s, bu_s, *, tc, nb):
    d = pl.program_id(0)
    i = pl.program_id(1)
    ns = S5_NGROUPS * S5_STATE

    @pl.when(i == 0)
    def _init():
        are = jnp.minimum(are_ref[0], -1e-4)
        aim = aim_ref[0]
        dt = jnp.exp(ls_ref[0])
        mag = jnp.exp(dt * are)
        abr = mag * jnp.cos(dt * aim)
        abi = mag * jnp.sin(dt * aim)
        inv = 1.0 / (are * are + aim * aim)
        fre = ((abr - 1.0) * are + abi * aim) * inv
        fim = (abi * are - (abr - 1.0) * aim) * inv
        bre = bre_ref[0]
        bim = bim_ref[0]
        bcat_s[:, 0:ns] = (bre * fre - bim * fim).astype(BF16)
        bcat_s[:, ns:2 * ns] = (bre * fim + bim * fre).astype(BF16)
        ccat_s[0:ns, :] = cre_ref[0].astype(BF16)
        ccat_s[ns:2 * ns, :] = (-cim_ref[0]).astype(BF16)
        ar_s[...] = jnp.broadcast_to(abr, (nb, ns))
        ai_s[...] = jnp.broadcast_to(abi, (nb, ns))
        st_s[...] = jnp.zeros_like(st_s)

    bu_s[...] = _dotf(u_ref[...].astype(BF16), bcat_s[...])
    ar = ar_s[...]
    ai = ai_s[...]

    def body(j, carry):
        xr, xi = carry
        t = j + d * (tc - 1 - 2 * j)
        r0 = pl.multiple_of(t * nb, nb)
        nr = ar * xr - ai * xi + bu_s[pl.ds(r0, nb), 0:ns]
        ni = ar * xi + ai * xr + bu_s[pl.ds(r0, nb), ns:2 * ns]
        bu_s[pl.ds(r0, nb), 0:ns] = nr
        bu_s[pl.ds(r0, nb), ns:2 * ns] = ni
        return nr, ni

    xr, xi = lax.fori_loop(0, tc, body, (st_s[0], st_s[1]))
    st_s[0] = xr
    st_s[1] = xi
    y_ref[0] = _dotf(bu_s[...].astype(BF16), ccat_s[...])


def _s5_call(u_tm, are, aim, ls, bre, bim, cre, cim, n_ctx, nb):
    rows, gw = u_tm.shape
    tc = S5_CHUNK
    n_chunks = rows // (tc * nb)
    n_ctx_chunks = n_ctx // tc
    ns = S5_NGROUPS * S5_STATE

    def chunk_of(d, i):
        rev = jnp.where(i < n_ctx_chunks, n_ctx_chunks - 1 - i, n_chunks - 1 - (i - n_ctx_chunks))
        return jnp.where(d == 0, i, rev)

    vec = pl.BlockSpec((1, 1, ns), lambda d, i: (d, 0, 0))
    return pl.pallas_call(
        functools.partial(_s5_kernel, tc=tc, nb=nb),
        grid=(2, n_chunks),
        in_specs=[pl.BlockSpec((tc * nb, gw), lambda d, i: (chunk_of(d, i), 0)),
                  vec, vec, vec,
                  pl.BlockSpec((1, gw, ns), lambda d, i: (d, 0, 0)),
                  pl.BlockSpec((1, gw, ns), lambda d, i: (d, 0, 0)),
                  pl.BlockSpec((1, ns, gw), lambda d, i: (d, 0, 0)),
                  pl.BlockSpec((1, ns, gw), lambda d, i: (d, 0, 0))],
        out_specs=pl.BlockSpec((1, tc * nb, gw), lambda d, i: (d, chunk_of(d, i), 0)),
        out_shape=jax.ShapeDtypeStruct((2, rows, gw), F32),
        scratch_shapes=[pltpu.VMEM((nb, ns), F32), pltpu.VMEM((nb, ns), F32),
                        pltpu.VMEM((gw, 2 * ns), BF16), pltpu.VMEM((2 * ns, gw), BF16),
                        pltpu.VMEM((2, nb, ns), F32), pltpu.VMEM((tc * nb, 2 * ns), F32)],
        compiler_params=_params(("arbitrary", "arbitrary"), VMEM_LIMIT),
        name="s5_scan",
    )(u_tm, are, aim, ls, bre, bim, cre, cim)


def _glu_kernel(yf_ref, yb_ref, u_ref, d_ref, w_ref, b_ref, o_ref):
    y = yf_ref[...] + yb_ref[...] + u_ref[...] * d_ref[...]
    g = y * (0.5 * (1.0 + jnp.tanh(math.sqrt(2.0 / math.pi) * (y + 0.044715 * (y * y * y)))))
    z = _dotf(g.astype(BF16), w_ref[...]) + b_ref[...]
    o_ref[0] = (g * _sigmoid(z)).astype(BF16)


def _glu_call(ys, u_t, dsk, w, bias, nb):
    t = u_t.shape[0]
    tm = ROW_TILE
    gw = GROUP_W
    const = lambda bi, ti: (0, 0)
    return pl.pallas_call(
        _glu_kernel,
        grid=(nb, t // tm),
        in_specs=[pl.BlockSpec((None, tm, gw), lambda bi, ti: (0, ti, bi)),
                  pl.BlockSpec((None, tm, gw), lambda bi, ti: (1, ti, bi)),
                  pl.BlockSpec((tm, gw), lambda bi, ti: (ti, bi)),
                  pl.BlockSpec(dsk.shape, const), pl.BlockSpec(w.shape, const),
                  pl.BlockSpec(bias.shape, const)],
        out_specs=pl.BlockSpec((1, tm, gw), lambda bi, ti: (bi, ti, 0)),
        out_shape=jax.ShapeDtypeStruct((nb, t, gw), BF16),
        compiler_params=_params(("parallel", "parallel"), VMEM_LIMIT),
        name="s5_glu",
    )(ys, ys, u_t, dsk, w, bias)


def _out_kernel(ya_ref, yb_ref, yc_ref, yd_ref, w_ref, x_ref, g1_ref, sh2_ref, sc2_ref,
                nw1_ref, nw2_ref, wr_ref, br_ref, xn_ref, f_ref, te_ref, tg_ref):
    o = (_dotf(ya_ref[0], w_ref[0:256]) + _dotf(yb_ref[0], w_ref[256:512])
         + _dotf(yc_ref[0], w_ref[512:768]) + _dotf(yd_ref[0], w_ref[768:1024]))
    xn = x_ref[0] + g1_ref[0] * _rms(o, nw1_ref[...])
    xn_ref[0] = xn
    f = _rms(xn, nw2_ref[...]) * (1.0 + sc2_ref[0]) + sh2_ref[0]
    _store_token_tiles(f_ref, f)
    lg = _dot3(f, wr_ref[...]) + br_ref[...]
    tm = lg.shape[0]
    lane = lax.broadcasted_iota(I32, (tm, 128), 1)
    tops, idxs = [], []
    for _ in range(TOP_K):
        mx = jnp.max(lg, axis=-1, keepdims=True)
        idx = jnp.min(jnp.where(lg == mx, lane, 128), axis=-1, keepdims=True)
        tops.append(mx)
        idxs.append(idx)
        lg = jnp.where(lane == idx, NEG_INF, lg)
    ex = [jnp.exp(tv - tops[0]) for tv in tops]
    inv = 1.0 / (ex[0] + ex[1] + ex[2] + ex[3])
    l8 = lax.broadcasted_iota(I32, (tm, 8), 1)
    te = jnp.zeros((tm, 8), I32)
    tg = jnp.zeros((tm, 8), F32)
    for kk in range(TOP_K):
        te = jnp.where(l8 == kk, idxs[kk], te)
        tg = jnp.where(l8 == kk, ex[kk] * inv, tg)
    te_ref[0] = te
    tg_ref[0] = tg


def _out_call(ya, yb, yc, yd, w_out, xs, mod_l, nw1, nw2, wr, br, n_ctx_blk, blk0):
    b, t, d = xs.shape
    tm = ROW_TILE
    nblk = t // tm - blk0
    t_out = nblk * tm

    def mrow(bi, ti):
        return jnp.where(ti + blk0 < n_ctx_blk, b, bi)

    row3 = lambda bi, ti: (bi, ti + blk0, 0)
    out3 = lambda bi, ti: (bi, ti, 0)
    const = lambda bi, ti: (0, 0)
    modspec = lambda j: pl.BlockSpec((1, 1, d), lambda bi, ti: (mrow(bi, ti), 0, j))
    yspec = lambda y: pl.BlockSpec((1, tm, GROUP_W), row3 if y.shape[1] == t else out3)
    return pl.pallas_call(
        _out_kernel,
        grid=(b, nblk),
        in_specs=[yspec(ya), yspec(yb), yspec(yc), yspec(yd), pl.BlockSpec((d, d), const),
                  pl.BlockSpec((1, tm, d), row3), modspec(2), modspec(3), modspec(4),
                  pl.BlockSpec((1, d), const), pl.BlockSpec((1, d), const),
                  pl.BlockSpec((d, 128), const), pl.BlockSpec((1, 128), const)],
        out_specs=[pl.BlockSpec((1, tm, d), out3),
                   pl.BlockSpec((tm * d // 128, 128), lambda bi, ti: (bi * nblk + ti, 0)),
                   pl.BlockSpec((1, tm, 8), out3), pl.BlockSpec((1, tm, 8), out3)],
        out_shape=[jax.ShapeDtypeStruct((b, t_out, d), F32),
                   jax.ShapeDtypeStruct((b * t_out * d // 128, 128), F32),
                   jax.ShapeDtypeStruct((b, t_out, 8), I32),
                   jax.ShapeDtypeStruct((b, t_out, 8), F32)],
        compiler_params=_params(("parallel", "parallel"), VMEM_LIMIT),
        name="out_proj_router",
    )(ya, yb, yc, yd, w_out, xs, mod_l, mod_l, mod_l, nw1, nw2, wr, br)


def _rank_kernel(te_ref, rank_ref, cnt_ref, carry_ref):
    i = pl.program_id(0)

    @pl.when(i == 0)
    def _():
        carry_ref[...] = jnp.zeros_like(carry_ref)

    te = te_ref[...]
    tb = te.shape[0]
    lane = lax.broadcasted_iota(I32, (tb, 128), 1)
    l8 = lax.broadcasted_iota(I32, (tb, 8), 1)
    below = (lax.broadcasted_iota(I32, (tb, tb), 0)
             > lax.broadcasted_iota(I32, (tb, tb), 1))
    lstrict = jnp.where(below, 1.0, 0.0).astype(BF16)
    base = carry_ref[...]
    out = jnp.zeros((tb, 8), I32)
    for k in range(TOP_K):
        oh = jnp.where(lane == te[:, k:k + 1], 1.0, 0.0)
        before = _dotf(lstrict, oh.astype(BF16)) + base
        rank_k = jnp.sum(oh * before, axis=-1, keepdims=True)
        out = jnp.where(l8 == k, rank_k.astype(I32), out)
        base = base + jnp.sum(oh, axis=0, keepdims=True)
    rank_ref[...] = out
    carry_ref[...] = base
    cnt_ref[...] = base.astype(I32)


def _rank_call(te):
    n_tok = te.shape[0]
    tb = RANK_TILE
    return pl.pallas_call(
        _rank_kernel,
        grid=(n_tok // tb,),
        in_specs=[pl.BlockSpec((tb, 8), lambda i: (i, 0))],
        out_specs=[pl.BlockSpec((tb, 8), lambda i: (i, 0)), pl.BlockSpec((1, 128), lambda i: (0, 0))],
        out_shape=[jax.ShapeDtypeStruct((n_tok, 8), I32), jax.ShapeDtypeStruct((1, 128), I32)],
        scratch_shapes=[pltpu.VMEM((1, 128), F32)],
        compiler_params=_params(("arbitrary",)),
        name="moe_rank",
    )(te)


def _dispatch_kernel(pad_ref, dest_ref, f_ref, xs_hbm, zbuf, sem, zsem, *, tb, tm, nt):
    i = pl.program_id(0)

    @pl.when(i == 0)
    def _():
        zbuf[...] = jnp.zeros_like(zbuf)
        fills = [pltpu.make_async_copy(
            zbuf, xs_hbm.at[pl.ds(pl.multiple_of(pad_ref[e] * nt, nt), tm * nt)], zsem)
            for e in range(N_EXPERTS)]
        for fill in fills:
            fill.start()
        for fill in fills:
            fill.wait()

        def fill_unused(blk, carry):
            tail = pltpu.make_async_copy(
                zbuf, xs_hbm.at[pl.ds(pl.multiple_of(blk * (tm * nt), tm * nt), tm * nt)], zsem)
            tail.start()
            tail.wait()
            return carry
        lax.fori_loop(pad_ref[N_EXPERTS], xs_hbm.shape[0] // (tm * nt), fill_unused, 0)

    def body(g, carry):
        r0 = g * DMA_UNROLL
        dsts = [pl.multiple_of(dest_ref[0, 0, r0 * TOP_K + j] * nt, nt)
                for j in range(DMA_UNROLL * TOP_K)]
        for u in range(DMA_UNROLL):
            src = f_ref.at[pl.ds(pl.multiple_of((r0 + u) * nt, nt), nt)]
            for k in range(TOP_K):
                pltpu.make_async_copy(src, xs_hbm.at[pl.ds(dsts[u * TOP_K + k], nt)],
                                      sem).start(priority=k % 2)
        return carry

    lax.fori_loop(0, tb // DMA_UNROLL, body, 0)
    for k in range(TOP_K):
        pltpu.make_async_copy(f_ref, xs_hbm.at[pl.ds(0, tb * nt)], sem).wait()


def _dispatch_call(pad_start, dest3, f_tiles, n_slots, tm):
    n_blk, _, per = dest3.shape
    tb = per // TOP_K
    nt = f_tiles.shape[0] // (n_blk * tb)
    grid_spec = pltpu.PrefetchScalarGridSpec(
        num_scalar_prefetch=1,
        grid=(n_blk,),
        in_specs=[pl.BlockSpec((1, 1, per), lambda i, pad: (i, 0, 0), memory_space=pltpu.SMEM),
                  pl.BlockSpec((tb * nt, 128), lambda i, pad: (i, 0))],
        out_specs=pl.BlockSpec(memory_space=pl.ANY),
        scratch_shapes=[pltpu.VMEM((tm * nt, 128), F32), pltpu.SemaphoreType.DMA(()),
                        pltpu.SemaphoreType.DMA(())])
    return pl.pallas_call(
        functools.partial(_dispatch_kernel, tb=tb, tm=tm, nt=nt),
        grid_spec=grid_spec,
        out_shape=jax.ShapeDtypeStruct(((n_slots + tm) * nt, 128), F32),
        compiler_params=_params(("arbitrary",), VMEM_LIMIT),
        name="moe_dispatch",
    )(pad_start, dest3, f_tiles)


def _expert_kernel(be_ref, nu_ref, xs_ref, wgu_ref, bgu_ref, wd_ref, bd_ref, ys_ref,
                   wgu_s, wd_s, *, tm, d_ff):
    i = pl.program_id(0)
    nt = xs_ref.shape[0] // tm

    @pl.when(i < nu_ref[0])
    def _():
        @pl.when(jnp.logical_or(i == 0, be_ref[i] != be_ref[jnp.maximum(i - 1, 0)]))
        def _():
            for c in range(0, wgu_s.shape[0], 128):
                wgu_s[c:c + 128, :] = wgu_ref[0, c:c + 128, :].astype(BF16)
            for c in range(0, wd_s.shape[0], 128):
                wd_s[c:c + 128, :] = wd_ref[0, c:c + 128, :].astype(BF16)

        x = jnp.concatenate(_load_token_tiles(xs_ref, tm, nt), axis=1).astype(BF16)
        gu = _dotf(x, wgu_s[...]) + bgu_ref[0]
        gate = jnp.minimum(gu[:, 0:d_ff], SWIGLU_LIMIT)
        up = jnp.clip(gu[:, d_ff:2 * d_ff], -SWIGLU_LIMIT, SWIGLU_LIMIT)
        act = (up + 1.0) * gate * _sigmoid(SWIGLU_ALPHA * gate)
        _store_token_tiles(ys_ref, _dotf(act.astype(BF16), wd_s[...]) + bd_ref[0])

    @pl.when(i >= nu_ref[0])
    def _():
        ys_ref[...] = jnp.zeros_like(ys_ref)


def _expert_call(block_expert, n_used, xs_tiles, wgu, bgu, wd, bd, n_blocks, tm):
    d, two_ff = wgu.shape[1:]
    d_ff = two_ff // 2
    nt = d // 128
    ex = lambda i, be, nu: (be[i], 0, 0)
    grid_spec = pltpu.PrefetchScalarGridSpec(
        num_scalar_prefetch=2,
        grid=(n_blocks,),
        in_specs=[pl.BlockSpec((tm * nt, 128), lambda i, be, nu: (jnp.minimum(i, nu[0] - 1), 0)),
                  pl.BlockSpec((1, d, two_ff), ex), pl.BlockSpec((1, 1, two_ff), ex),
                  pl.BlockSpec((1, d_ff, d), ex), pl.BlockSpec((1, 1, d), ex)],
        out_specs=pl.BlockSpec((tm * nt, 128), lambda i, be, nu: (i, 0)),
        scratch_shapes=[pltpu.VMEM((d, two_ff), BF16), pltpu.VMEM((d_ff, d), BF16)])
    return pl.pallas_call(
        functools.partial(_expert_kernel, tm=tm, d_ff=d_ff),
        grid_spec=grid_spec,
        out_shape=jax.ShapeDtypeStruct((n_blocks * tm * nt, 128), F32),
        compiler_params=_params(("arbitrary",), VMEM_LIMIT),
        name="moe_experts",
    )(block_expert, n_used, xs_tiles, wgu, bgu, wd, bd)


def _route(te, rank, counts, n_tok, tm):
    n_blocks = -(-(n_tok * TOP_K + N_EXPERTS * (tm - 1)) // tm)
    padded = (counts + tm - 1) // tm * tm
    padded_end = jnp.cumsum(padded)
    group_start = padded_end - padded
    dest = jnp.take(group_start, te) + rank
    block_start = jnp.arange(n_blocks, dtype=I32) * tm
    block_expert = jnp.minimum(
        jnp.sum((padded_end[None, :] <= block_start[:, None]).astype(I32), axis=1), N_EXPERTS - 1)
    n_used = (padded_end[-1] // tm).astype(I32).reshape(1)
    pad_table = jnp.concatenate([group_start + counts, n_used])
    return dest, pad_table, block_expert, n_used, n_blocks


def _fin_kernel(dcur_ref, dnxt_ref, g_ref, x_ref, g2_ref, nw_ref, ys_hbm, out_ref, buf, sem,
                *, tb, nt):
    i = pl.program_id(0)
    s = i % 2

    def gather(dref, ss):
        def body(g, carry):
            r0 = g * DMA_UNROLL
            srcs = [pl.multiple_of(dref[0, 0, r0 * TOP_K + j] * nt, nt)
                    for j in range(DMA_UNROLL * TOP_K)]
            for u in range(DMA_UNROLL):
                for k in range(TOP_K):
                    dst = pl.multiple_of((k * tb + r0 + u) * nt, nt)
                    pltpu.make_async_copy(ys_hbm.at[pl.ds(srcs[u * TOP_K + k], nt)],
                                          buf.at[ss, pl.ds(dst, nt)], sem.at[ss]).start()
            return carry
        lax.fori_loop(0, tb // DMA_UNROLL, body, 0)

    @pl.when(i == 0)
    def _():
        gather(dcur_ref, 0)

    @pl.when(i + 1 < pl.num_programs(0))
    def _():
        gather(dnxt_ref, 1 - s)

    pltpu.make_async_copy(ys_hbm.at[pl.ds(0, TOP_K * tb * nt)], buf.at[s], sem.at[s]).wait()
    gates = g_ref[...]
    parts = []
    for j in range(nt):
        acc = None
        for k in range(TOP_K):
            v = buf[s, pl.ds(k * tb * nt + j, tb, stride=nt), :] * gates[:, k:k + 1]
            acc = v if acc is None else acc + v
        parts.append(acc)
    m = jnp.concatenate(parts, axis=1)
    out_ref[0] = x_ref[0] + g2_ref[0] * _rms(m, nw_ref[...])


def _fin_call(dest3, gates, ys_tiles, xn, mod_l, nw, n_ctx_blk, blk0):
    b, t, d = xn.shape
    tb = ROW_TILE
    nblk = t // tb
    nt = d // 128
    n_blk = b * nblk

    def mrow(i):
        return jnp.where(i % nblk + blk0 < n_ctx_blk, b, i // nblk)

    smem = functools.partial(pl.BlockSpec, memory_space=pltpu.SMEM)
    return pl.pallas_call(
        functools.partial(_fin_kernel, tb=tb, nt=nt),
        grid=(n_blk,),
        in_specs=[smem((1, 1, tb * TOP_K), lambda i: (i, 0, 0)),
                  smem((1, 1, tb * TOP_K), lambda i: (jnp.minimum(i + 1, n_blk - 1), 0, 0)),
                  pl.BlockSpec((tb, 8), lambda i: (i, 0)),
                  pl.BlockSpec((1, tb, d), lambda i: (i // nblk, i % nblk, 0)),
                  pl.BlockSpec((1, 1, d), lambda i: (mrow(i), 0, 5)),
                  pl.BlockSpec((1, d), lambda i: (0, 0)),
                  pl.BlockSpec(memory_space=pl.ANY)],
        out_specs=pl.BlockSpec((1, tb, d), lambda i: (i // nblk, i % nblk, 0)),
        out_shape=jax.ShapeDtypeStruct((b, t, d), F32),
        scratch_shapes=[pltpu.VMEM((2, TOP_K * tb * nt, 128), F32), pltpu.SemaphoreType.DMA((2,))],
        compiler_params=_params(("arbitrary",), VMEM_LIMIT),
        name="moe_combine",
    )(dest3, dest3, gates, xn, mod_l, nw, ys_tiles)


def _in_proj_columns():
    cols = np.full((IN_COLS,), -1, np.int64)
    for sec in range(2):
        for n in range(256):
            part, hm, j = n // 128, (n % 128) // 16, n % 16
            m, h = hm // 4, hm % 4
            cols[sec * 256 + n] = sec * 256 + h * 64 + m * 32 + part * 16 + j
    cols[512:768] = np.arange(512, 768)
    o, s = IN_DA, 768
    cols[o:o + 1024] = s + np.arange(1024)
    o, s = o + IN_ML, s + 1024
    cols[o:o + 16] = s + np.arange(16)
    o, s = o + IN_G, s + 16
    cols[o:o + MLA_Q_RANK] = s + np.arange(MLA_Q_RANK)
    cols[o + 256:o + 256 + MLA_KV_RANK] = s + MLA_Q_RANK + np.arange(MLA_KV_RANK)
    cols[o + 384 + 64:o + 384 + 96] = s + MLA_Q_RANK + MLA_KV_RANK + np.arange(MLA_ROPE)
    o, s = o + IN_MLA, s + MLA_Q_RANK + MLA_KV_RANK + MLA_ROPE
    cols[o:o + 256] = s + np.arange(256)
    return cols


def _take_cols(w, cols):
    valid = jnp.asarray(cols >= 0)
    return jnp.where(valid, jnp.take(w, jnp.asarray(np.maximum(cols, 0)), axis=-1), 0.0)


def _rope_tables(n_ctx, n_lat):
    pos = jnp.arange(n_lat)
    inv = ROPE_THETA ** (-jnp.arange(8, dtype=F32) / 8)
    ang = jnp.concatenate([(pos // GRID_W)[:, None] * inv, (pos % GRID_W)[:, None] * inv], axis=-1)
    cos = jnp.concatenate([jnp.ones((n_ctx, 16), F32), jnp.cos(ang)], axis=0)
    sin = jnp.concatenate([jnp.zeros((n_ctx, 16), F32), jnp.sin(ang)], axis=0)
    t = n_ctx + n_lat
    cos_da, sin_da = jnp.tile(cos, (1, 8)), jnp.tile(sin, (1, 8))
    one, zero = jnp.ones((t, 64), F32), jnp.zeros((t, 64), F32)
    z16, z32 = jnp.zeros((t, 16), F32), jnp.zeros((t, 32), F32)
    cm = jnp.concatenate([one, cos, cos, jnp.ones((t, 32), F32)], axis=1)
    sa = jnp.concatenate([zero, -sin, z16, z32], axis=1)
    sb = jnp.concatenate([zero, z16, sin, z32], axis=1)
    return cos_da, sin_da, cm, sa, sb


def _mla_weights(w_uq, w_ukv):
    hd = MLA_NOPE + MLA_ROPE
    wq = jnp.zeros((256, 512), F32)
    wkv = jnp.zeros((MLA_KV_RANK, 768), F32)
    for h in range(MLA_HEADS):
        wq = wq.at[:MLA_Q_RANK, 128 * h:128 * h + hd].set(w_uq[:, hd * h:hd * (h + 1)])
        wkv = wkv.at[:, 128 * h:128 * h + MLA_NOPE].set(w_ukv[:, 128 * h:128 * h + MLA_NOPE])
        wkv = wkv.at[:, 512 + 64 * h:512 + 64 * (h + 1)].set(w_ukv[:, 128 * h + MLA_NOPE:128 * (h + 1)])
    return wq.astype(BF16), wkv.astype(BF16)


def _s5_layout(a_re, a_im, log_step, b_re, b_im, c_re, c_im):
    ns = S5_NGROUPS * S5_STATE
    eye = jnp.eye(S5_NGROUPS, dtype=F32)
    are = a_re.reshape(2, 1, ns)
    aim = a_im.reshape(2, 1, ns)
    ls = jnp.repeat(log_step, S5_STATE, axis=-1).reshape(2, 1, ns)
    bd_b = lambda w: jnp.einsum("dgph,gk->dghkp", w, eye).reshape(2, GROUP_W, ns)
    bd_c = lambda w: jnp.einsum("dghp,gk->dgpkh", w, eye).reshape(2, ns, GROUP_W)
    return are, aim, ls, bd_b(b_re), bd_b(b_im), bd_c(c_re), bd_c(c_im)


def kernel(x, c, ctx, c_ctx, w_mod, b_mod, norm_w, w_in, w_out, da_lambda, da_subln, ml_conv_w,
           ml_conv_b, ml_gate_b, ml_norm, mla_q_norm, mla_w_uq, mla_kv_norm, mla_w_ukv, s5_a_re,
           s5_a_im, s5_log_step, s5_b_re, s5_b_im, s5_c_re, s5_c_im, s5_d, s5_w_glu, s5_b_glu,
           moe_w_router, moe_b_router, moe_w_gate_up, moe_b_gate_up, moe_w_down, moe_b_down):
    bsz, n_lat, d = x.shape
    n_ctx = ctx.shape[1]
    t = n_ctx + n_lat
    depth = w_mod.shape[0]
    tm = ROW_TILE
    assert n_ctx % tm == 0 and n_lat % tm == 0 and bsz % 8 == 0 and bsz < 16
    assert n_ctx % ML_CHUNK == 0 and n_ctx % S5_CHUNK == 0
    n_ctx_blk = n_ctx // tm

    cc = jnp.zeros((16, d), F32).at[:bsz].set(c).at[bsz].set(c_ctx)
    mod = _mod_call(cc, w_mod, b_mod)
    cos_da, sin_da, cm, sa, sb = _rope_tables(n_ctx, n_lat)
    in_cols = _in_proj_columns()
    xs = jnp.concatenate([ctx, x], axis=1)

    for l in range(depth):
        last = l == depth - 1
        lambda_init = 0.8 - 0.6 * math.exp(-0.3 * l)
        mod_l = mod[l].reshape(16, 1, 6 * d)
        w_in_p = _take_cols(w_in[l], in_cols).astype(BF16)
        p_da, p_ml, p_g, p_mla, u_t = _in_call(xs, mod_l, norm_w[l, 0].reshape(1, d), w_in_p,
                                               cos_da, sin_da, n_ctx_blk)

        da_extra = [da_lambda[l], jnp.tile(da_subln[l], 4).reshape(1, GROUP_W)]
        da_kern = functools.partial(_da_kernel, lambda_init=lambda_init)
        da_kw = dict(q_col=0, k_col=1, v_col=2, q_w=GROUP_W)
        ya = _attn_call(da_kern, "diff_attn", p_da, p_da, p_da, da_extra, q_blk0=n_ctx_blk,
                        n_q_blk=n_lat // tm, n_keys=t, **da_kw)
        wq, wkv = _mla_weights(mla_w_uq[l], mla_w_ukv[l])
        qn = jnp.zeros((1, 256), F32).at[0, :MLA_Q_RANK].set(mla_q_norm[l])
        q_mla, k_mla, v_mla = _mla_prep_call(p_mla, qn, wq, mla_kv_norm[l].reshape(1, -1), wkv,
                                             cm, sa, sb)
        mla_kw = dict(q_col=0, k_col=0, v_col=0, q_w=512)
        yc = _attn_call(_mla_attn_kernel, "mla_attn", q_mla, k_mla, v_mla, [], q_blk0=n_ctx_blk,
                        n_q_blk=n_lat // tm, n_keys=t, **mla_kw)
        if not last:
            ya_c = _attn_call(da_kern, "diff_attn_ctx", p_da, p_da, p_da, da_extra, q_blk0=0,
                              n_q_blk=n_ctx_blk, n_keys=n_ctx, **da_kw)
            yc_c = _attn_call(_mla_attn_kernel, "mla_attn_ctx", q_mla, k_mla, v_mla, [], q_blk0=0,
                              n_q_blk=n_ctx_blk, n_keys=n_ctx, **mla_kw)
            ya = jnp.concatenate([ya_c, ya], axis=1)
            yc = jnp.concatenate([yc_c, yc], axis=1)

        gb = jnp.zeros((1, 128), F32).at[0, :16].set(ml_gate_b[l])
        yb = _ml_call(p_ml, p_g, ml_conv_w[l], ml_conv_b[l].reshape(1, -1), gb,
                      ml_norm[l].reshape(1, -1), n_ctx)

        s5p = _s5_layout(s5_a_re[l], s5_a_im[l], s5_log_step[l], s5_b_re[l], s5_b_im[l],
                         s5_c_re[l], s5_c_im[l])
        ys = _s5_call(u_t.reshape(t * bsz, GROUP_W), *s5p, n_ctx, bsz)
        yd = _glu_call(ys.reshape(2, t, bsz * GROUP_W), u_t, s5_d[l].reshape(1, -1),
                       s5_w_glu[l].astype(BF16), s5_b_glu[l].reshape(1, -1), bsz)

        blk0 = n_ctx_blk if last else 0
        wr = jnp.zeros((d, 128), F32).at[:, :N_EXPERTS].set(moe_w_router[l])
        br = jnp.full((1, 128), -1e30, F32).at[0, :N_EXPERTS].set(moe_b_router[l])
        xn, f, te, tg = _out_call(ya, yb, yc, yd, w_out[l].astype(BF16), xs, mod_l,
                                  norm_w[l, 1].reshape(1, d), norm_w[l, 2].reshape(1, d), wr, br,
                                  n_ctx_blk, blk0)

        t_moe = t - blk0 * tm
        n_tok = bsz * t_moe
        te = te.reshape(n_tok, 8)
        rank, cnt = _rank_call(te)
        dest, pad_start, be, nu, n_blocks = _route(te[:, :TOP_K], rank[:, :TOP_K],
                                                   cnt[0, :N_EXPERTS], n_tok, MOE_TILE)
        dest3 = dest.reshape(n_tok // tm, 1, tm * TOP_K)
        xs_tiles = _dispatch_call(pad_start, dest3, f, n_blocks * MOE_TILE, MOE_TILE)
        ys_tiles = _expert_call(be, nu, xs_tiles, moe_w_gate_up[l],
                                moe_b_gate_up[l].reshape(N_EXPERTS, 1, -1), moe_w_down[l],
                                moe_b_down[l].reshape(N_EXPERTS, 1, -1), n_blocks, MOE_TILE)
        xs = _fin_call(dest3, tg.reshape(n_tok, 8), ys_tiles, xn, mod_l,
                       norm_w[l, 3].reshape(1, d), n_ctx_blk, blk0)
    return xs
```

```python
import functools
import math

import numpy as np
import jax
import jax.numpy as jnp
from jax import lax
from jax.experimental import pallas as pl
from jax.experimental.pallas import tpu as pltpu

F32, BF16, I32 = jnp.float32, jnp.bfloat16, jnp.int32
NORM_EPS = 1e-6
GRID_W = 64
ROPE_THETA = 10000.0
GROUP_W = 256
DA_QK = 32
ML_CHUNK = 128
MLA_HEADS, MLA_NOPE, MLA_ROPE, MLA_Q_RANK, MLA_KV_RANK = 4, 64, 32, 192, 128
S5_NGROUPS, S5_GROUP, S5_STATE = 16, 16, 64
N_EXPERTS, TOP_K = 32, 4
SWIGLU_ALPHA, SWIGLU_LIMIT = 1.702, 7.0
NEG_INF = float("-inf")

ROW_TILE = 256
MOE_TILE = 256
MOE_ROWS = 256
S5_CHUNK = 64
RANK_TILE = 512
DMA_UNROLL = 8
VMEM_LIMIT = 56 * 1024 * 1024

IN_DA, IN_ML, IN_G, IN_MLA, IN_S5 = 768, 1024, 128, 512, 256
IN_COLS = IN_DA + IN_ML + IN_G + IN_MLA + IN_S5


def _params(sem, vmem=None):
    return pltpu.CompilerParams(dimension_semantics=sem, vmem_limit_bytes=vmem)


def _dotf(a, b):
    return jnp.dot(a, b, preferred_element_type=F32)


def _dot_nt(a, b):
    return lax.dot_general(a, b, (((1,), (1,)), ((), ())), preferred_element_type=F32)


def _split2(a):
    hi = a.astype(BF16)
    lo = (a - hi.astype(F32)).astype(BF16)
    return hi, lo


def _dot3(a, b):
    ah, al = _split2(a)
    bh, bl = _split2(b)
    return _dotf(ah, bh) + _dotf(ah, bl) + _dotf(al, bh)


def _dot_exact_lhs(lhs_b, a):
    a1 = a.astype(BF16)
    r1 = a - a1.astype(F32)
    a2 = r1.astype(BF16)
    a3 = (r1 - a2.astype(F32)).astype(BF16)
    return _dotf(lhs_b, a1) + _dotf(lhs_b, a2) + _dotf(lhs_b, a3)


def _rms(x, w):
    ms = jnp.mean(x * x, axis=-1, keepdims=True)
    return x * lax.rsqrt(ms + NORM_EPS) * w


def _head_rms(a, width):
    n = a.shape[-1]
    sh = int(math.log2(width))
    r = lax.broadcasted_iota(I32, (n, n), 0) >> sh
    c = lax.broadcasted_iota(I32, (n, n), 1) >> sh
    g = jnp.where(r == c, 1.0 / width, 0.0).astype(BF16)
    hi, lo = _split2(a * a)
    ms = _dotf(hi, g) + _dotf(lo, g)
    return a * lax.rsqrt(ms + NORM_EPS)


def _store_token_tiles(ref, val, base=0):
    tm, d = val.shape
    nt = d // 128
    for j in range(nt):
        ref[pl.ds(base + j, tm, stride=nt), :] = val[:, 128 * j:128 * (j + 1)]


def _load_token_tiles(ref, tm, nt, base=0, lead=()):
    parts = [ref[lead + (pl.ds(base + j, tm, stride=nt), slice(None))] for j in range(nt)]
    return parts


def _sigmoid(x):
    return jax.nn.sigmoid(x)


def _log_sigmoid(x):
    return jnp.minimum(x, 0.0) - jnp.log(1.0 + jnp.exp(-jnp.abs(x)))


def _mod_kernel(c_ref, w_ref, b_ref, o_ref):
    c = c_ref[...]
    o_ref[0] = _dot3(c * _sigmoid(c), w_ref[0]) + b_ref[0]


def _mod_call(cc, w_mod, b_mod):
    n_layers, d, n = w_mod.shape
    tn = 1536
    return pl.pallas_call(
        _mod_kernel,
        grid=(n_layers, n // tn),
        in_specs=[pl.BlockSpec((16, d), lambda l, j: (0, 0)),
                  pl.BlockSpec((1, d, tn), lambda l, j: (l, 0, j)),
                  pl.BlockSpec((1, 1, tn), lambda l, j: (l, 0, j))],
        out_specs=pl.BlockSpec((1, 16, tn), lambda l, j: (l, 0, j)),
        out_shape=jax.ShapeDtypeStruct((n_layers, 16, n), F32),
        compiler_params=_params(("parallel", "parallel"), VMEM_LIMIT),
        name="mod_vectors",
    )(cc, w_mod, b_mod.reshape(n_layers, 1, n))


def _in_kernel(x_ref, sh_ref, sc_ref, nw_ref, w_ref, c_ref, s_ref,
               da_ref, ml_ref, g_ref, mla_ref, s5_ref, *, qscale):
    h = _rms(x_ref[0], nw_ref[...]) * (1.0 + sc_ref[0]) + sh_ref[0]
    hb = h.astype(BF16)
    da = _dotf(hb, w_ref[:, 0:IN_DA])
    c = c_ref[...]
    s = s_ref[...]
    q1, q2, k1, k2 = da[:, 0:128], da[:, 128:256], da[:, 256:384], da[:, 384:512]
    da_ref[0, :, 0:128] = ((q1 * c - q2 * s) * qscale).astype(BF16)
    da_ref[0, :, 128:256] = ((q2 * c + q1 * s) * qscale).astype(BF16)
    da_ref[0, :, 256:384] = (k1 * c - k2 * s).astype(BF16)
    da_ref[0, :, 384:512] = (k2 * c + k1 * s).astype(BF16)
    da_ref[0, :, 512:768] = da[:, 512:768].astype(BF16)
    o = IN_DA
    ml_ref[0] = _dotf(hb, w_ref[:, o:o + IN_ML])
    o += IN_ML
    g_ref[0] = _dotf(hb, w_ref[:, o:o + IN_G])
    o += IN_G
    mla_ref[0] = _dotf(hb, w_ref[:, o:o + IN_MLA])
    o += IN_MLA
    s5_ref[...] = _dotf(hb, w_ref[:, o:o + IN_S5])


def _in_call(xs, mod_l, nw, w_in_p, cos_da, sin_da, n_ctx_blk):
    b, t, d = xs.shape
    tm = ROW_TILE

    def mrow(bi, ti):
        return jnp.where(ti < n_ctx_blk, b, bi)

    row3 = lambda bi, ti: (bi, ti, 0)
    return pl.pallas_call(
        functools.partial(_in_kernel, qscale=DA_QK ** -0.5 * LOG2_E),
        grid=(b, t // tm),
        in_specs=[pl.BlockSpec((1, tm, d), row3),
                  pl.BlockSpec((1, 1, d), lambda bi, ti: (mrow(bi, ti), 0, 0)),
                  pl.BlockSpec((1, 1, d), lambda bi, ti: (mrow(bi, ti), 0, 1)),
                  pl.BlockSpec((1, d), lambda bi, ti: (0, 0)),
                  pl.BlockSpec((d, IN_COLS), lambda bi, ti: (0, 0)),
                  pl.BlockSpec((tm, 128), lambda bi, ti: (ti, 0)),
                  pl.BlockSpec((tm, 128), lambda bi, ti: (ti, 0))],
        out_specs=[pl.BlockSpec((1, tm, IN_DA), row3),
                   pl.BlockSpec((1, tm, IN_ML), row3),
                   pl.BlockSpec((1, tm, IN_G), row3),
                   pl.BlockSpec((1, tm, IN_MLA), row3),
                   pl.BlockSpec((tm, IN_S5), lambda bi, ti: (ti, bi))],
        out_shape=[jax.ShapeDtypeStruct((b, t, IN_DA), BF16),
                   jax.ShapeDtypeStruct((b, t, IN_ML), F32),
                   jax.ShapeDtypeStruct((b, t, IN_G), F32),
                   jax.ShapeDtypeStruct((b, t, IN_MLA), F32),
                   jax.ShapeDtypeStruct((t, b * IN_S5), F32)],
        compiler_params=_params(("parallel", "parallel"), VMEM_LIMIT),
        name="in_proj",
    )(xs, mod_l, mod_l, nw, w_in_p, cos_da, sin_da)


LOG2_E = 1.0 / math.log(2.0)


def _softmax_rows(s):
    mx = jnp.max(s, axis=-1, keepdims=True)
    p = jnp.exp2(s - mx)
    return p, jnp.sum(p, axis=-1, keepdims=True)


def _da_kernel(q_ref, k_ref, v_ref, lam_ref, sub_ref, o_ref, *, lambda_init):
    q = q_ref[0]
    k = k_ref[0]
    v = v_ref[0]
    lp = lam_ref[...]
    lam = (jnp.exp(jnp.sum(lp[0:1] * lp[1:2], axis=-1, keepdims=True))
           - jnp.exp(jnp.sum(lp[2:3] * lp[3:4], axis=-1, keepdims=True)) + lambda_init)
    lane = lax.broadcasted_iota(I32, (1, GROUP_W), 1)
    grp = (lane & 127) >> 4
    head = lane >> 6
    acc = jnp.zeros((q.shape[0], GROUP_W), F32)
    for h in range(4):
        ps, rs = [], []
        for m in range(2):
            qm = jnp.where(grp == m * 4 + h, q, jnp.zeros_like(q))
            p, l = _softmax_rows(_dot_nt(qm, k))
            ps.append(p.astype(BF16))
            rs.append(((1.0 if m == 0 else lam) / l).astype(BF16))
        w = ps[0] * rs[0] - ps[1] * rs[1]
        vm = jnp.where(head == h, v, jnp.zeros_like(v))
        acc = acc + _dotf(w, vm)
    y = _head_rms(acc, 64) * sub_ref[...] * (1.0 - lambda_init)
    o_ref[0] = y.astype(BF16)


def _mla_attn_kernel(q_ref, k_ref, v_ref, o_ref):
    q = q_ref[0]
    k = k_ref[0]
    v = v_ref[0]
    head = lax.broadcasted_iota(I32, (1, GROUP_W), 1) >> 6
    acc = jnp.zeros((q.shape[0], GROUP_W), F32)
    for h in range(MLA_HEADS):
        sl = slice(128 * h, 128 * (h + 1))
        p, l = _softmax_rows(_dot_nt(q[:, sl], k[:, sl]))
        vm = jnp.where(head == h, v, jnp.zeros_like(v))
        acc = acc + _dotf(p.astype(BF16), vm) * (1.0 / l)
    o_ref[0] = acc.astype(BF16)


def _attn_call(kernel, name, q_arr, k_arr, v_arr, extra, *, q_blk0, n_q_blk, n_keys,
               q_col, k_col, v_col, q_w):
    b = q_arr.shape[0]
    tq = ROW_TILE
    in_specs = [pl.BlockSpec((1, tq, q_w), lambda bi, qi: (bi, qi + q_blk0, q_col)),
                pl.BlockSpec((1, n_keys, q_w), lambda bi, qi: (bi, 0, k_col)),
                pl.BlockSpec((1, n_keys, GROUP_W), lambda bi, qi: (bi, 0, v_col))]
    in_specs += [pl.BlockSpec(e.shape, lambda bi, qi: (0, 0)) for e in extra]
    return pl.pallas_call(
        kernel,
        grid=(b, n_q_blk),
        in_specs=in_specs,
        out_specs=pl.BlockSpec((1, tq, GROUP_W), lambda bi, qi: (bi, qi, 0)),
        out_shape=jax.ShapeDtypeStruct((b, n_q_blk * tq, GROUP_W), BF16),
        compiler_params=_params(("parallel", "arbitrary"), VMEM_LIMIT),
        name=name,
    )(q_arr, k_arr, v_arr, *extra)


def _mla_prep_kernel(p_ref, qn_ref, wq_ref, kvn_ref, wkv_ref, c_ref, sa_ref, sb_ref,
                     q_ref, k_ref, v_ref, *, scale):
    p = p_ref[0]
    cq, ckv, kr = p[:, 0:256], p[:, 256:384], p[:, 384:512]
    msq = jnp.sum(cq * cq, axis=-1, keepdims=True) * (1.0 / MLA_Q_RANK)
    qn = (cq * lax.rsqrt(msq + NORM_EPS) * qn_ref[...]).astype(BF16)
    q = _dotf(qn, wq_ref[...])
    c = c_ref[...]
    sa = sa_ref[...]
    sb = sb_ref[...]

    def rope(a):
        return a * c + pltpu.roll(a, 112, 1) * sa + pltpu.roll(a, 16, 1) * sb

    for h in range(MLA_HEADS):
        sl = slice(128 * h, 128 * (h + 1))
        q_ref[0, :, sl] = (rope(q[:, sl]) * scale).astype(BF16)
    kvn = (_rms(ckv, kvn_ref[...])).astype(BF16)
    kv = _dotf(kvn, wkv_ref[...])
    krr = rope(kr)
    for h in range(MLA_HEADS):
        sl = slice(128 * h, 128 * (h + 1))
        k_ref[0, :, sl] = (kv[:, sl] + krr).astype(BF16)
    v_ref[0] = kv[:, 512:768].astype(BF16)


def _mla_prep_call(p_mla, qn, wq, kvn, wkv, cm, sa, sb):
    b, t, _ = p_mla.shape
    tm = ROW_TILE
    row3 = lambda bi, ti: (bi, ti, 0)
    const = lambda bi, ti: (0, 0)
    tab = lambda bi, ti: (ti, 0)
    return pl.pallas_call(
        functools.partial(_mla_prep_kernel, scale=(MLA_NOPE + MLA_ROPE) ** -0.5 * LOG2_E),
        grid=(b, t // tm),
        in_specs=[pl.BlockSpec((1, tm, IN_MLA), row3),
                  pl.BlockSpec(qn.shape, const), pl.BlockSpec(wq.shape, const),
                  pl.BlockSpec(kvn.shape, const), pl.BlockSpec(wkv.shape, const),
                  pl.BlockSpec((tm, 128), tab), pl.BlockSpec((tm, 128), tab),
                  pl.BlockSpec((tm, 128), tab)],
        out_specs=[pl.BlockSpec((1, tm, 512), row3), pl.BlockSpec((1, tm, 512), row3),
                   pl.BlockSpec((1, tm, GROUP_W), row3)],
        out_shape=[jax.ShapeDtypeStruct((b, t, 512), BF16),
                   jax.ShapeDtypeStruct((b, t, 512), BF16),
                   jax.ShapeDtypeStruct((b, t, GROUP_W), BF16)],
        compiler_params=_params(("parallel", "parallel"), VMEM_LIMIT),
        name="mla_prep",
    )(p_mla, qn, wq, kvn, wkv, cm, sa, sb)


def _ml_kernel(p_ref, g_ref, cw_ref, cb_ref, gb_ref, nw_ref, y_ref, hf_ref, c_ref, m_ref,
               *, n_ctx_chunks, n_chunks):
    cl = ML_CHUNK
    t_total = n_chunks * cl
    row = lax.broadcasted_iota(I32, (cl, 1), 0)
    ti = lax.broadcasted_iota(I32, (cl, cl), 0)
    si = lax.broadcasted_iota(I32, (cl, cl), 1)
    lane128 = lax.broadcasted_iota(I32, (1, 128), 1)
    lane256 = lax.broadcasted_iota(I32, (1, 256), 1)
    lane384 = lax.broadcasted_iota(I32, (1, 384), 1)
    rowhead = lax.broadcasted_iota(I32, (256, 1), 0) >> 6
    r384 = lax.broadcasted_iota(I32, (256, 384), 0) >> 6
    c384 = lax.broadcasted_iota(I32, (256, 384), 1)
    bdmask = jnp.logical_or(jnp.logical_and(c384 < 256, (c384 >> 6) == r384), c384 == 256 + r384)
    hmask = [(lane256 >> 6) == h for h in range(4)]
    cmask = [jnp.logical_or(jnp.logical_and(lane384 < 256, (lane384 >> 6) == h), lane384 == 256 + h)
             for h in range(4)]
    is_f = jnp.logical_and(((lane128 >> 2) & 1) == 1, lane128 < 16)
    ones_tail = jnp.broadcast_to(jnp.where(lane128 < 4, 1.0, 0.0).astype(F32), (cl, 128))
    w0, w1, w2 = cw_ref[0:1], cw_ref[1:2], cw_ref[2:3]
    cb = cb_ref[...]
    gb = gb_ref[...]
    nw = nw_ref[...]

    def chunk(c, reverse):
        s0 = pl.multiple_of(c * cl, cl)
        x = p_ref[0, pl.ds(s0, cl), 0:512]
        sp = pl.multiple_of(jnp.maximum(s0 - 8, 0), 8)
        sn = pl.multiple_of(jnp.minimum(s0 + cl, t_total - 8), 8)
        has_prev = jnp.logical_and(c != 0, c != n_ctx_chunks).astype(F32)
        has_next = jnp.logical_and(c != n_ctx_chunks - 1, c != n_chunks - 1).astype(F32)
        prev_row = p_ref[0, pl.ds(sp, 8), 0:512][7:8] * has_prev
        next_row = p_ref[0, pl.ds(sn, 8), 0:512][0:1] * has_next
        xp = jnp.where(row == 0, prev_row, pltpu.roll(x, 1, 0))
        xn = jnp.where(row == cl - 1, next_row, pltpu.roll(x, cl - 1, 0))
        z = xp * w0 + x * w1 + xn * w2 + cb
        qk = z * _sigmoid(z)
        q = qk[:, 0:256]
        k = qk[:, 256:512] * (64 ** -0.5)
        v = p_ref[0, pl.ds(s0, cl), 512:768]
        g = g_ref[0, pl.ds(s0, cl), :] + gb
        gsel = jnp.where(is_f, _log_sigmoid(g), g)
        tri = (si >= ti) if reverse else (si <= ti)
        bc = _dot_exact_lhs(jnp.where(tri, 1.0, 0.0).astype(BF16), gsel)
        g_t = gsel.T
        bc_t = bc.T
        tot = bc[0:1] if reverse else bc[cl - 1:cl]
        qb = q.astype(BF16)
        kb = k.astype(BF16)
        vaug = jnp.concatenate([v, ones_tail], axis=1).astype(BF16)
        m_all = m_ref[...]
        off = 8 if reverse else 0
        acc = jnp.zeros((cl, 384), F32)
        wi_full = jnp.zeros((cl, 384), F32)
        emt_full = jnp.zeros((cl, 256), F32)
        kw = jnp.zeros((cl, 256), F32)
        dec_full = jnp.zeros((256, 1), F32)
        m_new_all = m_all
        for h in range(4):
            il, fl = off + h, off + 4 + h
            bcol, brow = bc[:, fl:fl + 1], bc_t[fl:fl + 1, :]
            irow, icol = g_t[il:il + 1, :], gsel[:, il:il + 1]
            m_h = m_all[:, h:h + 1]
            lw = jnp.where(tri, bcol - brow + irow, NEG_INF)
            linter = bcol + m_h
            mt = jnp.maximum(linter, jnp.max(lw, axis=-1, keepdims=True))
            dm = jnp.exp(lw - mt)
            s = _dot_nt(jnp.where(hmask[h], qb, jnp.zeros_like(qb)), kb)
            acc = acc + _dotf((s * dm).astype(BF16),
                              jnp.where(cmask[h], vaug, jnp.zeros_like(vaug)))
            wi_full = jnp.where(cmask[h], jnp.exp(linter - mt), wi_full)
            emt_full = jnp.where(hmask[h], jnp.exp(-mt), emt_full)
            b_last = tot[:, fl:fl + 1]
            lupd = b_last - bcol + icol
            m_new = jnp.maximum(b_last + m_h, jnp.max(lupd, axis=0, keepdims=True))
            kw = jnp.where(hmask[h], k * jnp.exp(lupd - m_new), kw)
            dec_full = jnp.where(rowhead == h, jnp.exp(b_last + m_h - m_new), dec_full)
            m_new_all = jnp.where(lane128 == h, m_new, m_new_all)
        c_aug = c_ref[...]
        nd = acc + wi_full * _dotf(qb, c_aug.astype(BF16))
        den_full = jnp.zeros((cl, 256), F32)
        for h in range(4):
            den_full = jnp.where(hmask[h], nd[:, 256 + h:257 + h], den_full)
        hout = nd[:, 0:256] / jnp.maximum(jnp.abs(den_full), emt_full)
        upd = _dotf(kw.T.astype(BF16), vaug)
        c_ref[...] = dec_full * c_aug + jnp.where(bdmask, upd, 0.0)
        m_ref[...] = m_new_all
        return s0, hout

    def fwd_body(i, carry):
        s0, hout = chunk(i, False)
        hf_ref[pl.ds(s0, cl), :] = hout
        return carry

    def bwd_body(i, carry):
        c = jnp.where(i < n_ctx_chunks, n_ctx_chunks - 1 - i, n_chunks - 1 - (i - n_ctx_chunks))
        s0, hout = chunk(c, True)
        o = p_ref[0, pl.ds(s0, cl), 768:1024]
        gated = _sigmoid(o) * (hf_ref[pl.ds(s0, cl), :] + hout)
        y_ref[0, pl.ds(s0, cl), :] = (_head_rms(gated, 64) * nw).astype(BF16)
        return carry

    c_ref[...] = jnp.zeros_like(c_ref)
    m_ref[...] = jnp.zeros_like(m_ref)
    lax.fori_loop(0, n_chunks, fwd_body, 0)
    c_ref[...] = jnp.zeros_like(c_ref)
    m_ref[...] = jnp.zeros_like(m_ref)
    lax.fori_loop(0, n_chunks, bwd_body, 0)


def _ml_call(p_ml, p_g, cw, cb, gb, nw, n_ctx):
    b, t, _ = p_ml.shape
    const = lambda bi: (0, 0)
    return pl.pallas_call(
        functools.partial(_ml_kernel, n_ctx_chunks=n_ctx // ML_CHUNK, n_chunks=t // ML_CHUNK),
        grid=(b,),
        in_specs=[pl.BlockSpec((1, t, IN_ML), lambda bi: (bi, 0, 0)),
                  pl.BlockSpec((1, t, IN_G), lambda bi: (bi, 0, 0)),
                  pl.BlockSpec(cw.shape, const), pl.BlockSpec(cb.shape, const),
                  pl.BlockSpec(gb.shape, const), pl.BlockSpec(nw.shape, const)],
        out_specs=pl.BlockSpec((1, t, GROUP_W), lambda bi: (bi, 0, 0)),
        out_shape=jax.ShapeDtypeStruct((b, t, GROUP_W), BF16),
        scratch_shapes=[pltpu.VMEM((t, GROUP_W), F32), pltpu.VMEM((256, 384), F32),
                        pltpu.VMEM((1, 128), F32)],
        compiler_params=_params(("parallel",), VMEM_LIMIT),
        name="mlstm",
    )(p_ml, p_g, cw, cb, gb, nw)


def _s5_kernel(u_ref, are_ref, aim_ref, ls_ref, bre_ref, bim_ref, cre_ref, cim_ref, y_ref,
               ar_s, ai_s, bcat_s, ccat_s, st_s, bu_s, *, tc, nb):
    d = pl.program_id(0)
    i = pl.program_id(1)
    ns = S5_NGROUPS * S5_STATE

    @pl.when(i == 0)
    def _init():
        are = jnp.minimum(are_ref[0], -1e-4)
        aim = aim_ref[0]
        dt = jnp.exp(ls_ref[0])
        mag = jnp.exp(dt * are)
        abr = mag * jnp.cos(dt * aim)
        abi = mag * jnp.sin(dt * aim)
        inv = 1.0 / (are * are + aim * aim)
        fre = ((abr - 1.0) * are + abi * aim) * inv
        fim = (abi * are - (abr - 1.0) * aim) * inv
        bre = bre_ref[0]
        bim = bim_ref[0]
        bcat_s[:, 0:ns] = (bre * fre - bim * fim).astype(BF16)
        bcat_s[:, ns:2 * ns] = (bre * fim + bim * fre).astype(BF16)
        ccat_s[0:ns, :] = cre_ref[0].astype(BF16)
        ccat_s[ns:2 * ns, :] = (-cim_ref[0]).astype(BF16)
        ar_s[...] = jnp.broadcast_to(abr, (nb, ns))
        ai_s[...] = jnp.broadcast_to(abi, (nb, ns))
        st_s[...] = jnp.zeros_like(st_s)

    bu_s[...] = _dotf(u_ref[...].astype(BF16), bcat_s[...])
    ar = ar_s[...]
    ai = ai_s[...]

    def body(j, carry):
        xr, xi = carry
        t = j + d * (tc - 1 - 2 * j)
        r0 = pl.multiple_of(t * nb, nb)
        nr = ar * xr - ai * xi + bu_s[pl.ds(r0, nb), 0:ns]
        ni = ar * xi + ai * xr + bu_s[pl.ds(r0, nb), ns:2 * ns]
        bu_s[pl.ds(r0, nb), 0:ns] = nr
        bu_s[pl.ds(r0, nb), ns:2 * ns] = ni
        return nr, ni

    xr, xi = lax.fori_loop(0, tc, body, (st_s[0], st_s[1]))
    st_s[0] = xr
    st_s[1] = xi
    y_ref[0] = _dotf(bu_s[...].astype(BF16), ccat_s[...])


def _s5_call(u_tm, are, aim, ls, bre, bim, cre, cim, n_ctx, nb):
    rows, gw = u_tm.shape
    tc = S5_CHUNK
    n_chunks = rows // (tc * nb)
    n_ctx_chunks = n_ctx // tc
    ns = S5_NGROUPS * S5_STATE

    def chunk_of(d, i):
        rev = jnp.where(i < n_ctx_chunks, n_ctx_chunks - 1 - i, n_chunks - 1 - (i - n_ctx_chunks))
        return jnp.where(d == 0, i, rev)

    vec = pl.BlockSpec((1, 1, ns), lambda d, i: (d, 0, 0))
    return pl.pallas_call(
        functools.partial(_s5_kernel, tc=tc, nb=nb),
        grid=(2, n_chunks),
        in_specs=[pl.BlockSpec((tc * nb, gw), lambda d, i: (chunk_of(d, i), 0)),
                  vec, vec, vec,
                  pl.BlockSpec((1, gw, ns), lambda d, i: (d, 0, 0)),
                  pl.BlockSpec((1, gw, ns), lambda d, i: (d, 0, 0)),
                  pl.BlockSpec((1, ns, gw), lambda d, i: (d, 0, 0)),
                  pl.BlockSpec((1, ns, gw), lambda d, i: (d, 0, 0))],
        out_specs=pl.BlockSpec((1, tc * nb, gw), lambda d, i: (d, chunk_of(d, i), 0)),
        out_shape=jax.ShapeDtypeStruct((2, rows, gw), F32),
        scratch_shapes=[pltpu.VMEM((nb, ns), F32), pltpu.VMEM((nb, ns), F32),
                        pltpu.VMEM((gw, 2 * ns), BF16), pltpu.VMEM((2 * ns, gw), BF16),
                        pltpu.VMEM((2, nb, ns), F32), pltpu.VMEM((tc * nb, 2 * ns), F32)],
        compiler_params=_params(("arbitrary", "arbitrary"), VMEM_LIMIT),
        name="s5_scan",
    )(u_tm, are, aim, ls, bre, bim, cre, cim)


def _glu_kernel(yf_ref, yb_ref, u_ref, d_ref, w_ref, b_ref, o_ref):
    y = yf_ref[...] + yb_ref[...] + u_ref[...] * d_ref[...]
    g = y * (0.5 * (1.0 + jnp.tanh(math.sqrt(2.0 / math.pi) * (y + 0.044715 * (y * y * y)))))
    z = _dotf(g.astype(BF16), w_ref[...]) + b_ref[...]
    o_ref[0] = (g * _sigmoid(z)).astype(BF16)


def _glu_call(ys, u_t, dsk, w, bias, nb):
    t = u_t.shape[0]
    tm = ROW_TILE
    gw = GROUP_W
    const = lambda bi, ti: (0, 0)
    return pl.pallas_call(
        _glu_kernel,
        grid=(nb, t // tm),
        in_specs=[pl.BlockSpec((None, tm, gw), lambda bi, ti: (0, ti, bi)),
                  pl.BlockSpec((None, tm, gw), lambda bi, ti: (1, ti, bi)),
                  pl.BlockSpec((tm, gw), lambda bi, ti: (ti, bi)),
                  pl.BlockSpec(dsk.shape, const), pl.BlockSpec(w.shape, const),
                  pl.BlockSpec(bias.shape, const)],
        out_specs=pl.BlockSpec((1, tm, gw), lambda bi, ti: (bi, ti, 0)),
        out_shape=jax.ShapeDtypeStruct((nb, t, gw), BF16),
        compiler_params=_params(("parallel", "parallel"), VMEM_LIMIT),
        name="s5_glu",
    )(ys, ys, u_t, dsk, w, bias)


def _out_kernel(ya_ref, yb_ref, yc_ref, yd_ref, w_ref, x_ref, g1_ref, sh2_ref, sc2_ref,
                nw1_ref, nw2_ref, wr_ref, br_ref, xn_ref, f_ref, te_ref, tg_ref):
    o = (_dotf(ya_ref[0], w_ref[0:256]) + _dotf(yb_ref[0], w_ref[256:512])
         + _dotf(yc_ref[0], w_ref[512:768]) + _dotf(yd_ref[0], w_ref[768:1024]))
    xn = x_ref[0] + g1_ref[0] * _rms(o, nw1_ref[...])
    xn_ref[0] = xn
    f = _rms(xn, nw2_ref[...]) * (1.0 + sc2_ref[0]) + sh2_ref[0]
    _store_token_tiles(f_ref, f)
    lg = _dot3(f, wr_ref[...]) + br_ref[...]
    tm = lg.shape[0]
    lane = lax.broadcasted_iota(I32, (tm, 128), 1)
    tops, idxs = [], []
    for _ in range(TOP_K):
        mx = jnp.max(lg, axis=-1, keepdims=True)
        idx = jnp.min(jnp.where(lg == mx, lane, 128), axis=-1, keepdims=True)
        tops.append(mx)
        idxs.append(idx)
        lg = jnp.where(lane == idx, NEG_INF, lg)
    ex = [jnp.exp(tv - tops[0]) for tv in tops]
    inv = 1.0 / (ex[0] + ex[1] + ex[2] + ex[3])
    l8 = lax.broadcasted_iota(I32, (tm, 8), 1)
    te = jnp.zeros((tm, 8), I32)
    tg = jnp.zeros((tm, 8), F32)
    for kk in range(TOP_K):
        te = jnp.where(l8 == kk, idxs[kk], te)
        tg = jnp.where(l8 == kk, ex[kk] * inv, tg)
    te_ref[0] = te
    tg_ref[0] = tg


def _out_call(ya, yb, yc, yd, w_out, xs, mod_l, nw1, nw2, wr, br, n_ctx_blk, blk0):
    b, t, d = xs.shape
    tm = ROW_TILE
    nblk = t // tm - blk0
    t_out = nblk * tm

    def mrow(bi, ti):
        return jnp.where(ti + blk0 < n_ctx_blk, b, bi)

    row3 = lambda bi, ti: (bi, ti + blk0, 0)
    out3 = lambda bi, ti: (bi, ti, 0)
    const = lambda bi, ti: (0, 0)
    modspec = lambda j: pl.BlockSpec((1, 1, d), lambda bi, ti: (mrow(bi, ti), 0, j))
    yspec = lambda y: pl.BlockSpec((1, tm, GROUP_W), row3 if y.shape[1] == t else out3)
    return pl.pallas_call(
        _out_kernel,
        grid=(b, nblk),
        in_specs=[yspec(ya), yspec(yb), yspec(yc), yspec(yd), pl.BlockSpec((d, d), const),
                  pl.BlockSpec((1, tm, d), row3), modspec(2), modspec(3), modspec(4),
                  pl.BlockSpec((1, d), const), pl.BlockSpec((1, d), const),
                  pl.BlockSpec((d, 128), const), pl.BlockSpec((1, 128), const)],
        out_specs=[pl.BlockSpec((1, tm, d), out3),
                   pl.BlockSpec((tm * d // 128, 128), lambda bi, ti: (bi * nblk + ti, 0)),
                   pl.BlockSpec((1, tm, 8), out3), pl.BlockSpec((1, tm, 8), out3)],
        out_shape=[jax.ShapeDtypeStruct((b, t_out, d), F32),
                   jax.ShapeDtypeStruct((b * t_out * d // 128, 128), F32),
                   jax.ShapeDtypeStruct((b, t_out, 8), I32),
                   jax.ShapeDtypeStruct((b, t_out, 8), F32)],
        compiler_params=_params(("parallel", "parallel"), VMEM_LIMIT),
        name="out_proj_router",
    )(ya, yb, yc, yd, w_out, xs, mod_l, mod_l, mod_l, nw1, nw2, wr, br)


def _rank_kernel(te_ref, rank_ref, cnt_ref, carry_ref):
    i = pl.program_id(0)

    @pl.when(i == 0)
    def _():
        carry_ref[...] = jnp.zeros_like(carry_ref)

    te = te_ref[...]
    tb = te.shape[0]
    lane = lax.broadcasted_iota(I32, (tb, 128), 1)
    l8 = lax.broadcasted_iota(I32, (tb, 8), 1)
    below = (lax.broadcasted_iota(I32, (tb, tb), 0)
             > lax.broadcasted_iota(I32, (tb, tb), 1))
    lstrict = jnp.where(below, 1.0, 0.0).astype(BF16)
    base = carry_ref[...]
    out = jnp.zeros((tb, 8), I32)
    for k in range(TOP_K):
        oh = jnp.where(lane == te[:, k:k + 1], 1.0, 0.0)
        before = _dotf(lstrict, oh.astype(BF16)) + base
        rank_k = jnp.sum(oh * before, axis=-1, keepdims=True)
        out = jnp.where(l8 == k, rank_k.astype(I32), out)
        base = base + jnp.sum(oh, axis=0, keepdims=True)
    rank_ref[...] = out
    carry_ref[...] = base
    cnt_ref[...] = base.astype(I32)


def _rank_call(te):
    n_tok = te.shape[0]
    tb = RANK_TILE
    return pl.pallas_call(
        _rank_kernel,
        grid=(n_tok // tb,),
        in_specs=[pl.BlockSpec((tb, 8), lambda i: (i, 0))],
        out_specs=[pl.BlockSpec((tb, 8), lambda i: (i, 0)), pl.BlockSpec((1, 128), lambda i: (0, 0))],
        out_shape=[jax.ShapeDtypeStruct((n_tok, 8), I32), jax.ShapeDtypeStruct((1, 128), I32)],
        scratch_shapes=[pltpu.VMEM((1, 128), F32)],
        compiler_params=_params(("arbitrary",)),
        name="moe_rank",
    )(te)


def _dispatch_kernel(pad_ref, dest_ref, f_ref, xs_hbm, zbuf, sem, zsem, *, tb, tm, nt):
    i = pl.program_id(0)

    @pl.when(i == 0)
    def _():
        zbuf[...] = jnp.zeros_like(zbuf)
        fills = [pltpu.make_async_copy(
            zbuf, xs_hbm.at[pl.ds(pl.multiple_of(pad_ref[e] * nt, nt), tm * nt)], zsem)
            for e in range(N_EXPERTS)]
        for fill in fills:
            fill.start()
        for fill in fills:
            fill.wait()

        def fill_unused(blk, carry):
            tail = pltpu.make_async_copy(
                zbuf, xs_hbm.at[pl.ds(pl.multiple_of(blk * (tm * nt), tm * nt), tm * nt)], zsem)
            tail.start()
            tail.wait()
            return carry
        lax.fori_loop(pad_ref[N_EXPERTS], xs_hbm.shape[0] // (tm * nt), fill_unused, 0)

    def body(g, carry):
        r0 = g * DMA_UNROLL
        dsts = [pl.multiple_of(dest_ref[0, 0, r0 * TOP_K + j] * nt, nt)
                for j in range(DMA_UNROLL * TOP_K)]
        for u in range(DMA_UNROLL):
            src = f_ref.at[pl.ds(pl.multiple_of((r0 + u) * nt, nt), nt)]
            for k in range(TOP_K):
                pltpu.make_async_copy(src, xs_hbm.at[pl.ds(dsts[u * TOP_K + k], nt)],
                                      sem).start(priority=k % 2)
        return carry

    lax.fori_loop(0, tb // DMA_UNROLL, body, 0)
    for k in range(TOP_K):
        pltpu.make_async_copy(f_ref, xs_hbm.at[pl.ds(0, tb * nt)], sem).wait()


def _dispatch_call(pad_start, dest3, f_tiles, n_slots, tm):
    n_blk, _, per = dest3.shape
    tb = per // TOP_K
    nt = f_tiles.shape[0] // (n_blk * tb)
    grid_spec = pltpu.PrefetchScalarGridSpec(
        num_scalar_prefetch=1,
        grid=(n_blk,),
        in_specs=[pl.BlockSpec((1, 1, per), lambda i, pad: (i, 0, 0), memory_space=pltpu.SMEM),
                  pl.BlockSpec((tb * nt, 128), lambda i, pad: (i, 0))],
        out_specs=pl.BlockSpec(memory_space=pl.ANY),
        scratch_shapes=[pltpu.VMEM((tm * nt, 128), F32), pltpu.SemaphoreType.DMA(()),
                        pltpu.SemaphoreType.DMA(())])
    return pl.pallas_call(
        functools.partial(_dispatch_kernel, tb=tb, tm=tm, nt=nt),
        grid_spec=grid_spec,
        out_shape=jax.ShapeDtypeStruct(((n_slots + tm) * nt, 128), F32),
        compiler_params=_params(("arbitrary",), VMEM_LIMIT),
        name="moe_dispatch",
    )(pad_start, dest3, f_tiles)


def _expert_kernel(be_ref, nu_ref, xs_ref, wgu_ref, bgu_ref, wd_ref, bd_ref, ys_ref,
                   wgu_s, wd_s, *, tm, d_ff):
    i = pl.program_id(0)
    nt = xs_ref.shape[0] // tm

    @pl.when(i < nu_ref[0])
    def _():
        @pl.when(jnp.logical_or(i == 0, be_ref[i] != be_ref[jnp.maximum(i - 1, 0)]))
        def _():
            for c in range(0, wgu_s.shape[0], 128):
                wgu_s[c:c + 128, :] = wgu_ref[c:c + 128, :].astype(BF16)
            for c in range(0, wd_s.shape[0], 128):
                wd_s[c:c + 128, :] = wd_ref[c:c + 128, :].astype(BF16)

        for r0 in range(0, tm, MOE_ROWS):
            x = jnp.concatenate(_load_token_tiles(xs_ref, MOE_ROWS, nt, base=r0 * nt),
                                axis=1).astype(BF16)
            gu = _dotf(x, wgu_s[...]) + bgu_ref[...]
            gate = jnp.minimum(gu[:, 0:d_ff], SWIGLU_LIMIT)
            up = jnp.clip(gu[:, d_ff:2 * d_ff], -SWIGLU_LIMIT, SWIGLU_LIMIT)
            act = (up + 1.0) * gate * _sigmoid(SWIGLU_ALPHA * gate)
            _store_token_tiles(ys_ref, _dotf(act.astype(BF16), wd_s[...]) + bd_ref[...],
                               base=r0 * nt)

    @pl.when(i >= nu_ref[0])
    def _():
        ys_ref[...] = jnp.zeros_like(ys_ref)


def _expert_call(block_expert, n_used, xs_tiles, wgu, bgu, wd, bd, layer, n_blocks, tm):
    d, two_ff = wgu.shape[2:]
    d_ff = two_ff // 2
    nt = d // 128
    ex = lambda i, be, nu: (layer, be[i], 0, 0)
    grid_spec = pltpu.PrefetchScalarGridSpec(
        num_scalar_prefetch=2,
        grid=(n_blocks,),
        in_specs=[pl.BlockSpec((tm * nt, 128), lambda i, be, nu: (jnp.minimum(i, nu[0] - 1), 0)),
                  pl.BlockSpec((None, None, d, two_ff), ex),
                  pl.BlockSpec((None, None, 1, two_ff), ex),
                  pl.BlockSpec((None, None, d_ff, d), ex),
                  pl.BlockSpec((None, None, 1, d), ex)],
        out_specs=pl.BlockSpec((tm * nt, 128), lambda i, be, nu: (i, 0)),
        scratch_shapes=[pltpu.VMEM((d, two_ff), BF16), pltpu.VMEM((d_ff, d), BF16)])
    return pl.pallas_call(
        functools.partial(_expert_kernel, tm=tm, d_ff=d_ff),
        grid_spec=grid_spec,
        out_shape=jax.ShapeDtypeStruct((n_blocks * tm * nt, 128), F32),
        compiler_params=_params(("arbitrary",), VMEM_LIMIT),
        name="moe_experts",
    )(block_expert, n_used, xs_tiles, wgu, bgu, wd, bd)


def _route(te, rank, counts, n_tok, tm):
    n_blocks = -(-(n_tok * TOP_K + N_EXPERTS * (tm - 1)) // tm)
    padded = (counts + tm - 1) // tm * tm
    padded_end = jnp.cumsum(padded)
    group_start = padded_end - padded
    dest = jnp.take(group_start, te) + rank
    block_start = jnp.arange(n_blocks, dtype=I32) * tm
    block_expert = jnp.minimum(
        jnp.sum((padded_end[None, :] <= block_start[:, None]).astype(I32), axis=1), N_EXPERTS - 1)
    n_used = (padded_end[-1] // tm).astype(I32).reshape(1)
    pad_table = jnp.concatenate([group_start + counts, n_used])
    return dest, pad_table, block_expert, n_used, n_blocks


def _fin_kernel(dcur_ref, dnxt_ref, g_ref, x_ref, g2_ref, nw_ref, ys_hbm, out_ref, buf, sem,
                *, tb, nt):
    i = pl.program_id(0)
    s = i % 2

    def gather(dref, ss):
        def body(g, carry):
            r0 = g * DMA_UNROLL
            srcs = [pl.multiple_of(dref[0, 0, r0 * TOP_K + j] * nt, nt)
                    for j in range(DMA_UNROLL * TOP_K)]
            for u in range(DMA_UNROLL):
                for k in range(TOP_K):
                    dst = pl.multiple_of((k * tb + r0 + u) * nt, nt)
                    pltpu.make_async_copy(ys_hbm.at[pl.ds(srcs[u * TOP_K + k], nt)],
                                          buf.at[ss, pl.ds(dst, nt)],
                                          sem.at[ss]).start(priority=k % 2)
            return carry
        lax.fori_loop(0, tb // DMA_UNROLL, body, 0)

    @pl.when(i == 0)
    def _():
        gather(dcur_ref, 0)

    @pl.when(i + 1 < pl.num_programs(0))
    def _():
        gather(dnxt_ref, 1 - s)

    pltpu.make_async_copy(ys_hbm.at[pl.ds(0, TOP_K * tb * nt)], buf.at[s], sem.at[s]).wait()
    gates = g_ref[...]
    parts = []
    for j in range(nt):
        acc = None
        for k in range(TOP_K):
            v = buf[s, pl.ds(k * tb * nt + j, tb, stride=nt), :] * gates[:, k:k + 1]
            acc = v if acc is None else acc + v
        parts.append(acc)
    m = jnp.concatenate(parts, axis=1)
    out_ref[0] = x_ref[0] + g2_ref[0] * _rms(m, nw_ref[...])


def _fin_call(dest3, gates, ys_tiles, xn, mod_l, nw, n_ctx_blk, blk0):
    b, t, d = xn.shape
    tb = ROW_TILE
    nblk = t // tb
    nt = d // 128
    n_blk = b * nblk

    def mrow(i):
        return jnp.where(i % nblk + blk0 < n_ctx_blk, b, i // nblk)

    smem = functools.partial(pl.BlockSpec, memory_space=pltpu.SMEM)
    return pl.pallas_call(
        functools.partial(_fin_kernel, tb=tb, nt=nt),
        grid=(n_blk,),
        in_specs=[smem((1, 1, tb * TOP_K), lambda i: (i, 0, 0)),
                  smem((1, 1, tb * TOP_K), lambda i: (jnp.minimum(i + 1, n_blk - 1), 0, 0)),
                  pl.BlockSpec((tb, 8), lambda i: (i, 0)),
                  pl.BlockSpec((1, tb, d), lambda i: (i // nblk, i % nblk, 0)),
                  pl.BlockSpec((1, 1, d), lambda i: (mrow(i), 0, 5)),
                  pl.BlockSpec((1, d), lambda i: (0, 0)),
                  pl.BlockSpec(memory_space=pl.ANY)],
        out_specs=pl.BlockSpec((1, tb, d), lambda i: (i // nblk, i % nblk, 0)),
        out_shape=jax.ShapeDtypeStruct((b, t, d), F32),
        scratch_shapes=[pltpu.VMEM((2, TOP_K * tb * nt, 128), F32), pltpu.SemaphoreType.DMA((2,))],
        compiler_params=_params(("arbitrary",), VMEM_LIMIT),
        name="moe_combine",
    )(dest3, dest3, gates, xn, mod_l, nw, ys_tiles)


def _in_proj_columns():
    cols = np.full((IN_COLS,), -1, np.int64)
    for sec in range(2):
        for n in range(256):
            part, hm, j = n // 128, (n % 128) // 16, n % 16
            m, h = hm // 4, hm % 4
            cols[sec * 256 + n] = sec * 256 + h * 64 + m * 32 + part * 16 + j
    cols[512:768] = np.arange(512, 768)
    o, s = IN_DA, 768
    cols[o:o + 1024] = s + np.arange(1024)
    o, s = o + IN_ML, s + 1024
    cols[o:o + 16] = s + np.arange(16)
    o, s = o + IN_G, s + 16
    cols[o:o + MLA_Q_RANK] = s + np.arange(MLA_Q_RANK)
    cols[o + 256:o + 256 + MLA_KV_RANK] = s + MLA_Q_RANK + np.arange(MLA_KV_RANK)
    cols[o + 384 + 64:o + 384 + 96] = s + MLA_Q_RANK + MLA_KV_RANK + np.arange(MLA_ROPE)
    o, s = o + IN_MLA, s + MLA_Q_RANK + MLA_KV_RANK + MLA_ROPE
    cols[o:o + 256] = s + np.arange(256)
    return cols


def _take_cols(w, cols):
    valid = jnp.asarray(cols >= 0)
    return jnp.where(valid, jnp.take(w, jnp.asarray(np.maximum(cols, 0)), axis=-1), 0.0)


def _rope_tables(n_ctx, n_lat):
    pos = jnp.arange(n_lat)
    inv = ROPE_THETA ** (-jnp.arange(8, dtype=F32) / 8)
    ang = jnp.concatenate([(pos // GRID_W)[:, None] * inv, (pos % GRID_W)[:, None] * inv], axis=-1)
    cos = jnp.concatenate([jnp.ones((n_ctx, 16), F32), jnp.cos(ang)], axis=0)
    sin = jnp.concatenate([jnp.zeros((n_ctx, 16), F32), jnp.sin(ang)], axis=0)
    t = n_ctx + n_lat
    cos_da, sin_da = jnp.tile(cos, (1, 8)), jnp.tile(sin, (1, 8))
    one, zero = jnp.ones((t, 64), F32), jnp.zeros((t, 64), F32)
    z16, z32 = jnp.zeros((t, 16), F32), jnp.zeros((t, 32), F32)
    cm = jnp.concatenate([one, cos, cos, jnp.ones((t, 32), F32)], axis=1)
    sa = jnp.concatenate([zero, -sin, z16, z32], axis=1)
    sb = jnp.concatenate([zero, z16, sin, z32], axis=1)
    return cos_da, sin_da, cm, sa, sb


def _mla_weights(w_uq, w_ukv):
    hd = MLA_NOPE + MLA_ROPE
    wq = jnp.zeros((256, 512), F32)
    wkv = jnp.zeros((MLA_KV_RANK, 768), F32)
    for h in range(MLA_HEADS):
        wq = wq.at[:MLA_Q_RANK, 128 * h:128 * h + hd].set(w_uq[:, hd * h:hd * (h + 1)])
        wkv = wkv.at[:, 128 * h:128 * h + MLA_NOPE].set(w_ukv[:, 128 * h:128 * h + MLA_NOPE])
        wkv = wkv.at[:, 512 + 64 * h:512 + 64 * (h + 1)].set(w_ukv[:, 128 * h + MLA_NOPE:128 * (h + 1)])
    return wq.astype(BF16), wkv.astype(BF16)


def _s5_layout(a_re, a_im, log_step, b_re, b_im, c_re, c_im):
    ns = S5_NGROUPS * S5_STATE
    eye = jnp.eye(S5_NGROUPS, dtype=F32)
    are = a_re.reshape(2, 1, ns)
    aim = a_im.reshape(2, 1, ns)
    ls = jnp.repeat(log_step, S5_STATE, axis=-1).reshape(2, 1, ns)
    bd_b = lambda w: jnp.einsum("dgph,gk->dghkp", w, eye).reshape(2, GROUP_W, ns)
    bd_c = lambda w: jnp.einsum("dghp,gk->dgpkh", w, eye).reshape(2, ns, GROUP_W)
    return are, aim, ls, bd_b(b_re), bd_b(b_im), bd_c(c_re), bd_c(c_im)


def kernel(x, c, ctx, c_ctx, w_mod, b_mod, norm_w, w_in, w_out, da_lambda, da_subln, ml_conv_w,
           ml_conv_b, ml_gate_b, ml_norm, mla_q_norm, mla_w_uq, mla_kv_norm, mla_w_ukv, s5_a_re,
           s5_a_im, s5_log_step, s5_b_re, s5_b_im, s5_c_re, s5_c_im, s5_d, s5_w_glu, s5_b_glu,
           moe_w_router, moe_b_router, moe_w_gate_up, moe_b_gate_up, moe_w_down, moe_b_down):
    bsz, n_lat, d = x.shape
    n_ctx = ctx.shape[1]
    t = n_ctx + n_lat
    depth = w_mod.shape[0]
    tm = ROW_TILE
    assert n_ctx % tm == 0 and n_lat % tm == 0 and bsz % 8 == 0 and bsz < 16
    assert n_ctx % ML_CHUNK == 0 and n_ctx % S5_CHUNK == 0
    n_ctx_blk = n_ctx // tm

    cc = jnp.zeros((16, d), F32).at[:bsz].set(c).at[bsz].set(c_ctx)
    mod = _mod_call(cc, w_mod, b_mod)
    cos_da, sin_da, cm, sa, sb = _rope_tables(n_ctx, n_lat)
    in_cols = _in_proj_columns()
    xs = jnp.concatenate([ctx, x], axis=1)

    for l in range(depth):
        last = l == depth - 1
        lambda_init = 0.8 - 0.6 * math.exp(-0.3 * l)
        mod_l = mod[l].reshape(16, 1, 6 * d)
        w_in_p = _take_cols(w_in[l], in_cols).astype(BF16)
        p_da, p_ml, p_g, p_mla, u_t = _in_call(xs, mod_l, norm_w[l, 0].reshape(1, d), w_in_p,
                                               cos_da, sin_da, n_ctx_blk)

        da_extra = [da_lambda[l], jnp.tile(da_subln[l], 4).reshape(1, GROUP_W)]
        da_kern = functools.partial(_da_kernel, lambda_init=lambda_init)
        da_kw = dict(q_col=0, k_col=1, v_col=2, q_w=GROUP_W)
        ya = _attn_call(da_kern, "diff_attn", p_da, p_da, p_da, da_extra, q_blk0=n_ctx_blk,
                        n_q_blk=n_lat // tm, n_keys=t, **da_kw)
        wq, wkv = _mla_weights(mla_w_uq[l], mla_w_ukv[l])
        qn = jnp.zeros((1, 256), F32).at[0, :MLA_Q_RANK].set(mla_q_norm[l])
        q_mla, k_mla, v_mla = _mla_prep_call(p_mla, qn, wq, mla_kv_norm[l].reshape(1, -1), wkv,
                                             cm, sa, sb)
        mla_kw = dict(q_col=0, k_col=0, v_col=0, q_w=512)
        yc = _attn_call(_mla_attn_kernel, "mla_attn", q_mla, k_mla, v_mla, [], q_blk0=n_ctx_blk,
                        n_q_blk=n_lat // tm, n_keys=t, **mla_kw)
        if not last:
            ya_c = _attn_call(da_kern, "diff_attn_ctx", p_da, p_da, p_da, da_extra, q_blk0=0,
                              n_q_blk=n_ctx_blk, n_keys=n_ctx, **da_kw)
            yc_c = _attn_call(_mla_attn_kernel, "mla_attn_ctx", q_mla, k_mla, v_mla, [], q_blk0=0,
                              n_q_blk=n_ctx_blk, n_keys=n_ctx, **mla_kw)
            ya = jnp.concatenate([ya_c, ya], axis=1)
            yc = jnp.concatenate([yc_c, yc], axis=1)

        gb = jnp.zeros((1, 128), F32).at[0, :16].set(ml_gate_b[l])
        yb = _ml_call(p_ml, p_g, ml_conv_w[l], ml_conv_b[l].reshape(1, -1), gb,
                      ml_norm[l].reshape(1, -1), n_ctx)

        s5p = _s5_layout(s5_a_re[l], s5_a_im[l], s5_log_step[l], s5_b_re[l], s5_b_im[l],
                         s5_c_re[l], s5_c_im[l])
        ys = _s5_call(u_t.reshape(t * bsz, GROUP_W), *s5p, n_ctx, bsz)
        yd = _glu_call(ys.reshape(2, t, bsz * GROUP_W), u_t, s5_d[l].reshape(1, -1),
                       s5_w_glu[l].astype(BF16), s5_b_glu[l].reshape(1, -1), bsz)

        blk0 = n_ctx_blk if last else 0
        wr = jnp.zeros((d, 128), F32).at[:, :N_EXPERTS].set(moe_w_router[l])
        br = jnp.full((1, 128), -1e30, F32).at[0, :N_EXPERTS].set(moe_b_router[l])
        xn, f, te, tg = _out_call(ya, yb, yc, yd, w_out[l].astype(BF16), xs, mod_l,
                                  norm_w[l, 1].reshape(1, d), norm_w[l, 2].reshape(1, d), wr, br,
                                  n_ctx_blk, blk0)

        t_moe = t - blk0 * tm
        n_tok = bsz * t_moe
        te = te.reshape(n_tok, 8)
        rank, cnt = _rank_call(te)
        dest, pad_start, be, nu, n_blocks = _route(te[:, :TOP_K], rank[:, :TOP_K],
                                                   cnt[0, :N_EXPERTS], n_tok, MOE_TILE)
        dest3 = dest.reshape(n_tok // tm, 1, tm * TOP_K)
        xs_tiles = _dispatch_call(pad_start, dest3, f, n_blocks * MOE_TILE, MOE_TILE)
        ys_tiles = _expert_call(be, nu, xs_tiles, moe_w_gate_up,
                                moe_b_gate_up.reshape(depth, N_EXPERTS, 1, -1), moe_w_down,
                                moe_b_down.reshape(depth, N_EXPERTS, 1, -1), l, n_blocks, MOE_TILE)
        xs = _fin_call(dest3, tg.reshape(n_tok, 8), ys_tiles, xn, mod_l,
                       norm_w[l, 3].reshape(1, d), n_ctx_blk, blk0)
    return xs
```

```python
import functools
import math

import numpy as np
import jax
import jax.numpy as jnp
from jax import lax
from jax.experimental import pallas as pl
from jax.experimental.pallas import tpu as pltpu

F32, BF16, I32 = jnp.float32, jnp.bfloat16, jnp.int32
NORM_EPS = 1e-6
GRID_W = 64
ROPE_THETA = 10000.0
GROUP_W = 256
DA_QK = 32
ML_CHUNK = 128
MLA_HEADS, MLA_NOPE, MLA_ROPE, MLA_Q_RANK, MLA_KV_RANK = 4, 64, 32, 192, 128
S5_NGROUPS, S5_GROUP, S5_STATE = 16, 16, 64
N_EXPERTS, TOP_K = 32, 4
SWIGLU_ALPHA, SWIGLU_LIMIT = 1.702, 7.0
NEG_INF = float("-inf")

ROW_TILE = 256
MOE_TILE = 256
MOE_ROWS = 256
S5_CHUNK = 64
RANK_TILE = 512
DMA_UNROLL = 8
VMEM_LIMIT = 56 * 1024 * 1024

IN_DA, IN_ML, IN_G, IN_MLA, IN_S5 = 768, 1024, 128, 512, 256
IN_COLS = IN_DA + IN_ML + IN_G + IN_MLA + IN_S5


def _params(sem, vmem=None):
    return pltpu.CompilerParams(dimension_semantics=sem, vmem_limit_bytes=vmem)


def _dotf(a, b):
    return jnp.dot(a, b, preferred_element_type=F32)


def _dot_nt(a, b):
    return lax.dot_general(a, b, (((1,), (1,)), ((), ())), preferred_element_type=F32)


def _split2(a):
    hi = a.astype(BF16)
    lo = (a - hi.astype(F32)).astype(BF16)
    return hi, lo


def _dot3(a, b):
    ah, al = _split2(a)
    bh, bl = _split2(b)
    return _dotf(ah, bh) + _dotf(ah, bl) + _dotf(al, bh)


def _dot_exact_rhs(a, rhs_b):
    a1 = a.astype(BF16)
    r1 = a - a1.astype(F32)
    a2 = r1.astype(BF16)
    a3 = (r1 - a2.astype(F32)).astype(BF16)
    return _dotf(a1, rhs_b) + _dotf(a2, rhs_b) + _dotf(a3, rhs_b)


def _rms(x, w):
    ms = jnp.mean(x * x, axis=-1, keepdims=True)
    return x * lax.rsqrt(ms + NORM_EPS) * w


def _head_rms(a, width):
    n = a.shape[-1]
    sh = int(math.log2(width))
    r = lax.broadcasted_iota(I32, (n, n), 0) >> sh
    c = lax.broadcasted_iota(I32, (n, n), 1) >> sh
    g = jnp.where(r == c, 1.0 / width, 0.0).astype(BF16)
    hi, lo = _split2(a * a)
    ms = _dotf(hi, g) + _dotf(lo, g)
    return a * lax.rsqrt(ms + NORM_EPS)


def _store_token_tiles(ref, val, base=0):
    tm, d = val.shape
    nt = d // 128
    for j in range(nt):
        ref[pl.ds(base + j, tm, stride=nt), :] = val[:, 128 * j:128 * (j + 1)]


def _load_token_tiles(ref, tm, nt, base=0, lead=()):
    parts = [ref[lead + (pl.ds(base + j, tm, stride=nt), slice(None))] for j in range(nt)]
    return parts


def _sigmoid(x):
    return jax.nn.sigmoid(x)


def _log_sigmoid(x):
    return jnp.minimum(x, 0.0) - jnp.log(1.0 + jnp.exp(-jnp.abs(x)))


def _mod_kernel(c_ref, w_ref, b_ref, o_ref):
    c = c_ref[...]
    o_ref[0] = _dot3(c * _sigmoid(c), w_ref[0]) + b_ref[0]


def _mod_call(cc, w_mod, b_mod):
    n_layers, d, n = w_mod.shape
    tn = 1536
    return pl.pallas_call(
        _mod_kernel,
        grid=(n_layers, n // tn),
        in_specs=[pl.BlockSpec((16, d), lambda l, j: (0, 0)),
                  pl.BlockSpec((1, d, tn), lambda l, j: (l, 0, j)),
                  pl.BlockSpec((1, 1, tn), lambda l, j: (l, 0, j))],
        out_specs=pl.BlockSpec((1, 16, tn), lambda l, j: (l, 0, j)),
        out_shape=jax.ShapeDtypeStruct((n_layers, 16, n), F32),
        compiler_params=_params(("parallel", "parallel"), VMEM_LIMIT),
        name="mod_vectors",
    )(cc, w_mod, b_mod.reshape(n_layers, 1, n))


def _in_kernel(x_ref, sh_ref, sc_ref, nw_ref, w_ref, c_ref, s_ref,
               da_ref, ml_ref, g_ref, mla_ref, s5_ref, *, qscale):
    h = _rms(x_ref[0], nw_ref[...]) * (1.0 + sc_ref[0]) + sh_ref[0]
    hb = h.astype(BF16)
    da = _dotf(hb, w_ref[:, 0:IN_DA])
    c = c_ref[...]
    s = s_ref[...]
    q1, q2, k1, k2 = da[:, 0:128], da[:, 128:256], da[:, 256:384], da[:, 384:512]
    da_ref[0, :, 0:128] = ((q1 * c - q2 * s) * qscale).astype(BF16)
    da_ref[0, :, 128:256] = ((q2 * c + q1 * s) * qscale).astype(BF16)
    da_ref[0, :, 256:384] = (k1 * c - k2 * s).astype(BF16)
    da_ref[0, :, 384:512] = (k2 * c + k1 * s).astype(BF16)
    da_ref[0, :, 512:768] = da[:, 512:768].astype(BF16)
    o = IN_DA
    ml_ref[0] = _dotf(hb, w_ref[:, o:o + IN_ML])
    o += IN_ML
    g_ref[0] = _dotf(hb, w_ref[:, o:o + IN_G])
    o += IN_G
    mla_ref[0] = _dotf(hb, w_ref[:, o:o + IN_MLA])
    o += IN_MLA
    s5_ref[...] = _dotf(hb, w_ref[:, o:o + IN_S5])


def _in_call(xs, mod_l, nw, w_in_p, cos_da, sin_da, n_ctx_blk):
    b, t, d = xs.shape
    tm = ROW_TILE

    def mrow(bi, ti):
        return jnp.where(ti < n_ctx_blk, b, bi)

    row3 = lambda bi, ti: (bi, ti, 0)
    return pl.pallas_call(
        functools.partial(_in_kernel, qscale=DA_QK ** -0.5 * LOG2_E),
        grid=(b, t // tm),
        in_specs=[pl.BlockSpec((1, tm, d), row3),
                  pl.BlockSpec((1, 1, d), lambda bi, ti: (mrow(bi, ti), 0, 0)),
                  pl.BlockSpec((1, 1, d), lambda bi, ti: (mrow(bi, ti), 0, 1)),
                  pl.BlockSpec((1, d), lambda bi, ti: (0, 0)),
                  pl.BlockSpec((d, IN_COLS), lambda bi, ti: (0, 0)),
                  pl.BlockSpec((tm, 128), lambda bi, ti: (ti, 0)),
                  pl.BlockSpec((tm, 128), lambda bi, ti: (ti, 0))],
        out_specs=[pl.BlockSpec((1, tm, IN_DA), row3),
                   pl.BlockSpec((1, tm, IN_ML), row3),
                   pl.BlockSpec((1, tm, IN_G), row3),
                   pl.BlockSpec((1, tm, IN_MLA), row3),
                   pl.BlockSpec((tm, IN_S5), lambda bi, ti: (ti, bi))],
        out_shape=[jax.ShapeDtypeStruct((b, t, IN_DA), BF16),
                   jax.ShapeDtypeStruct((b, t, IN_ML), F32),
                   jax.ShapeDtypeStruct((b, t, IN_G), F32),
                   jax.ShapeDtypeStruct((b, t, IN_MLA), F32),
                   jax.ShapeDtypeStruct((t, b * IN_S5), F32)],
        compiler_params=_params(("parallel", "parallel"), VMEM_LIMIT),
        name="in_proj",
    )(xs, mod_l, mod_l, nw, w_in_p, cos_da, sin_da)


LOG2_E = 1.0 / math.log(2.0)


def _softmax_rows(s):
    mx = jnp.max(s, axis=-1, keepdims=True)
    p = jnp.exp2(s - mx)
    return p, jnp.sum(p, axis=-1, keepdims=True)


def _da_kernel(q_ref, k_ref, v_ref, lam_ref, sub_ref, o_ref, *, lambda_init):
    q = q_ref[0]
    k = k_ref[0]
    v = v_ref[0]
    lp = lam_ref[...]
    lam = (jnp.exp(jnp.sum(lp[0:1] * lp[1:2], axis=-1, keepdims=True))
           - jnp.exp(jnp.sum(lp[2:3] * lp[3:4], axis=-1, keepdims=True)) + lambda_init)
    lane = lax.broadcasted_iota(I32, (1, GROUP_W), 1)
    grp = (lane & 127) >> 4
    head = lane >> 6
    acc = jnp.zeros((q.shape[0], GROUP_W), F32)
    for h in range(4):
        ps, rs = [], []
        for m in range(2):
            qm = jnp.where(grp == m * 4 + h, q, jnp.zeros_like(q))
            p, l = _softmax_rows(_dot_nt(qm, k))
            ps.append(p.astype(BF16))
            rs.append(((1.0 if m == 0 else lam) / l).astype(BF16))
        w = ps[0] * rs[0] - ps[1] * rs[1]
        vm = jnp.where(head == h, v, jnp.zeros_like(v))
        acc = acc + _dotf(w, vm)
    y = _head_rms(acc, 64) * sub_ref[...] * (1.0 - lambda_init)
    o_ref[0] = y.astype(BF16)


def _mla_attn_kernel(q_ref, k_ref, v_ref, o_ref):
    q = q_ref[0]
    k = k_ref[0]
    v = v_ref[0]
    head = lax.broadcasted_iota(I32, (1, GROUP_W), 1) >> 6
    acc = jnp.zeros((q.shape[0], GROUP_W), F32)
    for h in range(MLA_HEADS):
        sl = slice(128 * h, 128 * (h + 1))
        p, l = _softmax_rows(_dot_nt(q[:, sl], k[:, sl]))
        vm = jnp.where(head == h, v, jnp.zeros_like(v))
        acc = acc + _dotf(p.astype(BF16), vm) * (1.0 / l)
    o_ref[0] = acc.astype(BF16)


def _attn_call(kernel, name, q_arr, k_arr, v_arr, extra, *, q_blk0, n_q_blk, n_keys,
               q_col, k_col, v_col, q_w):
    b = q_arr.shape[0]
    tq = ROW_TILE
    in_specs = [pl.BlockSpec((1, tq, q_w), lambda bi, qi: (bi, qi + q_blk0, q_col)),
                pl.BlockSpec((1, n_keys, q_w), lambda bi, qi: (bi, 0, k_col)),
                pl.BlockSpec((1, n_keys, GROUP_W), lambda bi, qi: (bi, 0, v_col))]
    in_specs += [pl.BlockSpec(e.shape, lambda bi, qi: (0, 0)) for e in extra]
    return pl.pallas_call(
        kernel,
        grid=(b, n_q_blk),
        in_specs=in_specs,
        out_specs=pl.BlockSpec((1, tq, GROUP_W), lambda bi, qi: (bi, qi, 0)),
        out_shape=jax.ShapeDtypeStruct((b, n_q_blk * tq, GROUP_W), BF16),
        compiler_params=_params(("parallel", "arbitrary"), VMEM_LIMIT),
        name=name,
    )(q_arr, k_arr, v_arr, *extra)


def _mla_prep_kernel(p_ref, qn_ref, wq_ref, kvn_ref, wkv_ref, c_ref, sa_ref, sb_ref,
                     q_ref, k_ref, v_ref, *, scale):
    p = p_ref[0]
    cq, ckv, kr = p[:, 0:256], p[:, 256:384], p[:, 384:512]
    msq = jnp.sum(cq * cq, axis=-1, keepdims=True) * (1.0 / MLA_Q_RANK)
    qn = (cq * lax.rsqrt(msq + NORM_EPS) * qn_ref[...]).astype(BF16)
    q = _dotf(qn, wq_ref[...])
    c = c_ref[...]
    sa = sa_ref[...]
    sb = sb_ref[...]

    def rope(a):
        return a * c + pltpu.roll(a, 112, 1) * sa + pltpu.roll(a, 16, 1) * sb

    for h in range(MLA_HEADS):
        sl = slice(128 * h, 128 * (h + 1))
        q_ref[0, :, sl] = (rope(q[:, sl]) * scale).astype(BF16)
    kvn = (_rms(ckv, kvn_ref[...])).astype(BF16)
    kv = _dotf(kvn, wkv_ref[...])
    krr = rope(kr)
    for h in range(MLA_HEADS):
        sl = slice(128 * h, 128 * (h + 1))
        k_ref[0, :, sl] = (kv[:, sl] + krr).astype(BF16)
    v_ref[0] = kv[:, 512:768].astype(BF16)


def _mla_prep_call(p_mla, qn, wq, kvn, wkv, cm, sa, sb):
    b, t, _ = p_mla.shape
    tm = ROW_TILE
    row3 = lambda bi, ti: (bi, ti, 0)
    const = lambda bi, ti: (0, 0)
    tab = lambda bi, ti: (ti, 0)
    return pl.pallas_call(
        functools.partial(_mla_prep_kernel, scale=(MLA_NOPE + MLA_ROPE) ** -0.5 * LOG2_E),
        grid=(b, t // tm),
        in_specs=[pl.BlockSpec((1, tm, IN_MLA), row3),
                  pl.BlockSpec(qn.shape, const), pl.BlockSpec(wq.shape, const),
                  pl.BlockSpec(kvn.shape, const), pl.BlockSpec(wkv.shape, const),
                  pl.BlockSpec((tm, 128), tab), pl.BlockSpec((tm, 128), tab),
                  pl.BlockSpec((tm, 128), tab)],
        out_specs=[pl.BlockSpec((1, tm, 512), row3), pl.BlockSpec((1, tm, 512), row3),
                   pl.BlockSpec((1, tm, GROUP_W), row3)],
        out_shape=[jax.ShapeDtypeStruct((b, t, 512), BF16),
                   jax.ShapeDtypeStruct((b, t, 512), BF16),
                   jax.ShapeDtypeStruct((b, t, GROUP_W), BF16)],
        compiler_params=_params(("parallel", "parallel"), VMEM_LIMIT),
        name="mla_prep",
    )(p_mla, qn, wq, kvn, wkv, cm, sa, sb)


def _ml_kernel(p_ref, g_ref, cw_ref, cb_ref, gb_ref, nw_ref, y_ref,
               qt_s, k_s, vt_s, ot_s, gt_s, hf_s, c_s, m_s, *, n_ctx_chunks, n_chunks):
    cl = ML_CHUNK
    t_total = n_chunks * cl
    row = lax.broadcasted_iota(I32, (cl, 1), 0)
    si = lax.broadcasted_iota(I32, (cl, cl), 0)
    ti = lax.broadcasted_iota(I32, (cl, cl), 1)
    lane128 = lax.broadcasted_iota(I32, (1, 128), 1)
    lane256 = lax.broadcasted_iota(I32, (1, 256), 1)
    row128 = lax.broadcasted_iota(I32, (128, 1), 0)
    hmask = [(lane256 >> 6) == h for h in range(4)]
    is_f = jnp.logical_and(((lane128 >> 2) & 1) == 1, lane128 < 16)
    r_nd = lax.broadcasted_iota(I32, (384, 512), 0)
    c_nd = lax.broadcasted_iota(I32, (384, 512), 1) >> 7
    nd_head = jnp.where(r_nd < 256, r_nd >> 6, r_nd - 256)
    nd_mask = nd_head == c_nd
    nd_ones = jnp.where(jnp.logical_and(nd_mask, r_nd >= 256), 1.0, 0.0).astype(BF16)
    r_st = lax.broadcasted_iota(I32, (384, 256), 0)
    st_head = jnp.where(r_st < 256, r_st >> 6, r_st - 256)
    st_mask = st_head == (lax.broadcasted_iota(I32, (384, 256), 1) >> 6)
    st_rowhead = jnp.where(lax.broadcasted_iota(I32, (384, 1), 0) < 256,
                           lax.broadcasted_iota(I32, (384, 1), 0) >> 6,
                           lax.broadcasted_iota(I32, (384, 1), 0) - 256)
    w0, w1, w2 = cw_ref[0:1], cw_ref[1:2], cw_ref[2:3]
    cb = cb_ref[...]
    gb = gb_ref[...]

    def prep_body(c, carry):
        s0 = pl.multiple_of(c * cl, cl)
        x = p_ref[0, pl.ds(s0, cl), 0:512]
        sp = pl.multiple_of(jnp.maximum(s0 - 8, 0), 8)
        sn = pl.multiple_of(jnp.minimum(s0 + cl, t_total - 8), 8)
        has_prev = jnp.logical_and(c != 0, c != n_ctx_chunks).astype(F32)
        has_next = jnp.logical_and(c != n_ctx_chunks - 1, c != n_chunks - 1).astype(F32)
        prev_row = p_ref[0, pl.ds(sp, 8), 0:512][7:8] * has_prev
        next_row = p_ref[0, pl.ds(sn, 8), 0:512][0:1] * has_next
        xp = jnp.where(row == 0, prev_row, pltpu.roll(x, 1, 0))
        xn = jnp.where(row == cl - 1, next_row, pltpu.roll(x, cl - 1, 0))
        z = xp * w0 + x * w1 + xn * w2 + cb
        qk = z * _sigmoid(z)
        qt_s[c] = qk[:, 0:256].T.astype(BF16)
        k_s[c] = (qk[:, 256:512] * (64 ** -0.5)).astype(BF16)
        vt_s[c] = p_ref[0, pl.ds(s0, cl), 512:768].T
        ot_s[c] = p_ref[0, pl.ds(s0, cl), 768:1024].T
        g = g_ref[0, pl.ds(s0, cl), :] + gb
        gt_s[c] = jnp.where(is_f, _log_sigmoid(g), g).T[0:16]
        return carry

    def rows_to_blocks(rows, n):
        return jnp.concatenate([jnp.broadcast_to(r, (n, 128)) for r in rows], axis=0)

    def head_rows(rows):
        out = jnp.zeros((128, 128), F32)
        for h in range(4):
            out = jnp.where(row128 == h, rows[h], out)
        return out

    def chunk(c, reverse):
        g_t = gt_s[c]
        tri = (si >= ti) if reverse else (si <= ti)
        bc_t = _dot_exact_rhs(g_t, jnp.where(tri, 1.0, 0.0).astype(BF16))
        qt = qt_s[c]
        kb = k_s[c]
        m_all = m_s[...]
        off = 8 if reverse else 0
        a_rows, r_rows, tots, m_old = [], [], [], []
        for h in range(4):
            il, fl = off + h, off + 4 + h
            a_rows.append(bc_t[fl:fl + 1, :])
            r_rows.append(g_t[il:il + 1, :] - bc_t[fl:fl + 1, :])
            tots.append(bc_t[fl:fl + 1, 0:1] if reverse else bc_t[fl:fl + 1, cl - 1:cl])
            m_old.append(m_all[:, h:h + 1])
        r_all = jnp.concatenate(r_rows, axis=1)
        d_t = jnp.broadcast_to(r_all, (cl, 4 * cl)).T + rows_to_blocks(a_rows, cl)
        lw = jnp.where(jnp.concatenate([tri] * 4, axis=0), d_t, NEG_INF)
        mt, wi, emt = [], [], []
        for h in range(4):
            linter = a_rows[h] + m_old[h]
            mt_h = jnp.maximum(linter, jnp.max(lw[cl * h:cl * (h + 1)], axis=0, keepdims=True))
            mt.append(mt_h)
            wi.append(jnp.exp(linter - mt_h))
            emt.append(jnp.exp(-mt_h))
        kstack = jnp.concatenate([jnp.where(hmask[h], kb, jnp.zeros_like(kb)) for h in range(4)],
                                 axis=0)
        s_t = _dotf(kstack, qt)
        w_t = (s_t * jnp.exp(lw - rows_to_blocks(mt, cl))).astype(BF16)
        vt = vt_s[c]
        vt4 = jnp.concatenate([vt.astype(BF16)] * 4, axis=1)
        lhs_nd = jnp.concatenate([vt4, jnp.zeros((128, 512), BF16)], axis=0)
        lhs_nd = jnp.where(nd_mask, lhs_nd, jnp.zeros_like(lhs_nd)) + nd_ones
        c_aug = c_s[...]
        nd = _dotf(lhs_nd, w_t) + jnp.concatenate(
            [rows_to_blocks(wi, 64), head_rows(wi)], axis=0) * _dotf(c_aug.astype(BF16), qt)
        den = [nd[256 + h:257 + h, :] for h in range(4)]
        h_t = nd[0:256] / jnp.maximum(jnp.abs(rows_to_blocks(den, 64)), rows_to_blocks(emt, 64))
        wupd, decs, m_new_all = [], [], m_all
        for h in range(4):
            lupd = tots[h] + r_rows[h]
            m_new = jnp.maximum(tots[h] + m_old[h], jnp.max(lupd, axis=-1, keepdims=True))
            wupd.append(jnp.exp(lupd - m_new))
            decs.append(jnp.exp(tots[h] + m_old[h] - m_new))
            m_new_all = jnp.where(lane128 == h, m_new, m_new_all)
        lhs_u = jnp.concatenate([vt * rows_to_blocks(wupd, 64), head_rows(wupd)], axis=0)
        upd = _dotf(lhs_u.astype(BF16), kb)
        dec_col = jnp.zeros((384, 1), F32)
        for h in range(4):
            dec_col = jnp.where(st_rowhead == h, decs[h], dec_col)
        c_s[...] = dec_col * c_aug + jnp.where(st_mask, upd, 0.0)
        m_s[...] = m_new_all
        return h_t

    def fwd_body(i, carry):
        hf_s[i] = chunk(i, False)
        return carry

    def bwd_body(i, carry):
        c = jnp.where(i < n_ctx_chunks, n_ctx_chunks - 1 - i, n_chunks - 1 - (i - n_ctx_chunks))
        gated = _sigmoid(ot_s[c]) * (hf_s[c] + chunk(c, True))
        ms = [jnp.mean(jnp.square(gated[64 * h:64 * (h + 1)]), axis=0, keepdims=True)
              for h in range(4)]
        y_t = gated * lax.rsqrt(rows_to_blocks(ms, 64) + NORM_EPS) * nw_ref[...]
        y_ref[0, pl.ds(pl.multiple_of(c * cl, cl), cl), :] = y_t.T.astype(BF16)
        return carry

    lax.fori_loop(0, n_chunks, prep_body, 0)
    c_s[...] = jnp.zeros_like(c_s)
    m_s[...] = jnp.zeros_like(m_s)
    lax.fori_loop(0, n_chunks, fwd_body, 0)
    c_s[...] = jnp.zeros_like(c_s)
    m_s[...] = jnp.zeros_like(m_s)
    lax.fori_loop(0, n_chunks, bwd_body, 0)


def _ml_call(p_ml, p_g, cw, cb, gb, nw, n_ctx):
    b, t, _ = p_ml.shape
    nc = t // ML_CHUNK
    const = lambda bi: (0, 0)
    nw_col = jnp.broadcast_to(nw.reshape(GROUP_W, 1), (GROUP_W, 128))
    return pl.pallas_call(
        functools.partial(_ml_kernel, n_ctx_chunks=n_ctx // ML_CHUNK, n_chunks=nc),
        grid=(b,),
        in_specs=[pl.BlockSpec((1, t, IN_ML), lambda bi: (bi, 0, 0)),
                  pl.BlockSpec((1, t, IN_G), lambda bi: (bi, 0, 0)),
                  pl.BlockSpec(cw.shape, const), pl.BlockSpec(cb.shape, const),
                  pl.BlockSpec(gb.shape, const), pl.BlockSpec(nw_col.shape, const)],
        out_specs=pl.BlockSpec((1, t, GROUP_W), lambda bi: (bi, 0, 0)),
        out_shape=jax.ShapeDtypeStruct((b, t, GROUP_W), BF16),
        scratch_shapes=[pltpu.VMEM((nc, GROUP_W, 128), BF16), pltpu.VMEM((nc, 128, GROUP_W), BF16),
                        pltpu.VMEM((nc, GROUP_W, 128), F32), pltpu.VMEM((nc, GROUP_W, 128), F32),
                        pltpu.VMEM((nc, 16, 128), F32), pltpu.VMEM((nc, GROUP_W, 128), F32),
                        pltpu.VMEM((384, GROUP_W), F32), pltpu.VMEM((1, 128), F32)],
        compiler_params=_params(("parallel",), VMEM_LIMIT),
        name="mlstm",
    )(p_ml, p_g, cw, cb, gb, nw_col)


def _s5_kernel(u_ref, are_ref, aim_ref, ls_ref, bre_ref, bim_ref, cre_ref, cim_ref, y_ref,
               ar_s, ai_s, bcat_s, ccat_s, st_s, bu_s, *, tc, nb):
    d = pl.program_id(0)
    i = pl.program_id(1)
    ns = S5_NGROUPS * S5_STATE

    @pl.when(i == 0)
    def _init():
        are = jnp.minimum(are_ref[0], -1e-4)
        aim = aim_ref[0]
        dt = jnp.exp(ls_ref[0])
        mag = jnp.exp(dt * are)
        abr = mag * jnp.cos(dt * aim)
        abi = mag * jnp.sin(dt * aim)
        inv = 1.0 / (are * are + aim * aim)
        fre = ((abr - 1.0) * are + abi * aim) * inv
        fim = (abi * are - (abr - 1.0) * aim) * inv
        bre = bre_ref[0]
        bim = bim_ref[0]
        bcat_s[:, 0:ns] = (bre * fre - bim * fim).astype(BF16)
        bcat_s[:, ns:2 * ns] = (bre * fim + bim * fre).astype(BF16)
        ccat_s[0:ns, :] = cre_ref[0].astype(BF16)
        ccat_s[ns:2 * ns, :] = (-cim_ref[0]).astype(BF16)
        ar_s[...] = jnp.broadcast_to(abr, (nb, ns))
        ai_s[...] = jnp.broadcast_to(abi, (nb, ns))
        st_s[...] = jnp.zeros_like(st_s)

    bu_s[...] = _dotf(u_ref[...].astype(BF16), bcat_s[...])
    ar = ar_s[...]
    ai = ai_s[...]

    def body(j, carry):
        xr, xi = carry
        t = j + d * (tc - 1 - 2 * j)
        r0 = pl.multiple_of(t * nb, nb)
        nr = ar * xr - ai * xi + bu_s[pl.ds(r0, nb), 0:ns]
        ni = ar * xi + ai * xr + bu_s[pl.ds(r0, nb), ns:2 * ns]
        bu_s[pl.ds(r0, nb), 0:ns] = nr
        bu_s[pl.ds(r0, nb), ns:2 * ns] = ni
        return nr, ni

    xr, xi = lax.fori_loop(0, tc, body, (st_s[0], st_s[1]))
    st_s[0] = xr
    st_s[1] = xi
    y_ref[0] = _dotf(bu_s[...].astype(BF16), ccat_s[...])


def _s5_call(u_tm, are, aim, ls, bre, bim, cre, cim, n_ctx, nb):
    rows, gw = u_tm.shape
    tc = S5_CHUNK
    n_chunks = rows // (tc * nb)
    n_ctx_chunks = n_ctx // tc
    ns = S5_NGROUPS * S5_STATE

    def chunk_of(d, i):
        rev = jnp.where(i < n_ctx_chunks, n_ctx_chunks - 1 - i, n_chunks - 1 - (i - n_ctx_chunks))
        return jnp.where(d == 0, i, rev)

    vec = pl.BlockSpec((1, 1, ns), lambda d, i: (d, 0, 0))
    return pl.pallas_call(
        functools.partial(_s5_kernel, tc=tc, nb=nb),
        grid=(2, n_chunks),
        in_specs=[pl.BlockSpec((tc * nb, gw), lambda d, i: (chunk_of(d, i), 0)),
                  vec, vec, vec,
                  pl.BlockSpec((1, gw, ns), lambda d, i: (d, 0, 0)),
                  pl.BlockSpec((1, gw, ns), lambda d, i: (d, 0, 0)),
                  pl.BlockSpec((1, ns, gw), lambda d, i: (d, 0, 0)),
                  pl.BlockSpec((1, ns, gw), lambda d, i: (d, 0, 0))],
        out_specs=pl.BlockSpec((1, tc * nb, gw), lambda d, i: (d, chunk_of(d, i), 0)),
        out_shape=jax.ShapeDtypeStruct((2, rows, gw), F32),
        scratch_shapes=[pltpu.VMEM((nb, ns), F32), pltpu.VMEM((nb, ns), F32),
                        pltpu.VMEM((gw, 2 * ns), BF16), pltpu.VMEM((2 * ns, gw), BF16),
                        pltpu.VMEM((2, nb, ns), F32), pltpu.VMEM((tc * nb, 2 * ns), F32)],
        compiler_params=_params(("arbitrary", "arbitrary"), VMEM_LIMIT),
        name="s5_scan",
    )(u_tm, are, aim, ls, bre, bim, cre, cim)


def _glu_kernel(yf_ref, yb_ref, u_ref, d_ref, w_ref, b_ref, o_ref):
    y = yf_ref[...] + yb_ref[...] + u_ref[...] * d_ref[...]
    g = y * (0.5 * (1.0 + jnp.tanh(math.sqrt(2.0 / math.pi) * (y + 0.044715 * (y * y * y)))))
    z = _dotf(g.astype(BF16), w_ref[...]) + b_ref[...]
    o_ref[0] = (g * _sigmoid(z)).astype(BF16)


def _glu_call(ys, u_t, dsk, w, bias, nb):
    t = u_t.shape[0]
    tm = ROW_TILE
    gw = GROUP_W
    const = lambda bi, ti: (0, 0)
    return pl.pallas_call(
        _glu_kernel,
        grid=(nb, t // tm),
        in_specs=[pl.BlockSpec((None, tm, gw), lambda bi, ti: (0, ti, bi)),
                  pl.BlockSpec((None, tm, gw), lambda bi, ti: (1, ti, bi)),
                  pl.BlockSpec((tm, gw), lambda bi, ti: (ti, bi)),
                  pl.BlockSpec(dsk.shape, const), pl.BlockSpec(w.shape, const),
                  pl.BlockSpec(bias.shape, const)],
        out_specs=pl.BlockSpec((1, tm, gw), lambda bi, ti: (bi, ti, 0)),
        out_shape=jax.ShapeDtypeStruct((nb, t, gw), BF16),
        compiler_params=_params(("parallel", "parallel"), VMEM_LIMIT),
        name="s5_glu",
    )(ys, ys, u_t, dsk, w, bias)


def _out_kernel(ya_ref, yb_ref, yc_ref, yd_ref, w_ref, x_ref, g1_ref, sh2_ref, sc2_ref,
                nw1_ref, nw2_ref, wr_ref, br_ref, xn_ref, f_ref, te_ref, tg_ref):
    o = (_dotf(ya_ref[0], w_ref[0:256]) + _dotf(yb_ref[0], w_ref[256:512])
         + _dotf(yc_ref[0], w_ref[512:768]) + _dotf(yd_ref[0], w_ref[768:1024]))
    xn = x_ref[0] + g1_ref[0] * _rms(o, nw1_ref[...])
    xn_ref[0] = xn
    f = _rms(xn, nw2_ref[...]) * (1.0 + sc2_ref[0]) + sh2_ref[0]
    _store_token_tiles(f_ref, f)
    lg = _dot3(f, wr_ref[...]) + br_ref[...]
    tm = lg.shape[0]
    lane = lax.broadcasted_iota(I32, (tm, 128), 1).astype(F32)
    tops, idxs = [], []
    for _ in range(TOP_K):
        mx = jnp.max(lg, axis=-1, keepdims=True)
        idx = jnp.min(jnp.where(lg == mx, lane, 128.0), axis=-1, keepdims=True)
        tops.append(mx)
        idxs.append(idx.astype(I32))
        lg = jnp.where(lane == idx, NEG_INF, lg)
    ex = [jnp.exp(tv - tops[0]) for tv in tops]
    inv = 1.0 / (ex[0] + ex[1] + ex[2] + ex[3])
    l8 = lax.broadcasted_iota(I32, (tm, 8), 1)
    te = jnp.zeros((tm, 8), I32)
    tg = jnp.zeros((tm, 8), F32)
    for kk in range(TOP_K):
        te = jnp.where(l8 == kk, idxs[kk], te)
        tg = jnp.where(l8 == kk, ex[kk] * inv, tg)
    te_ref[0] = te
    tg_ref[0] = tg


def _out_call(ya, yb, yc, yd, w_out, xs, mod_l, nw1, nw2, wr, br, n_ctx_blk, blk0):
    b, t, d = xs.shape
    tm = ROW_TILE
    nblk = t // tm - blk0
    t_out = nblk * tm

    def mrow(bi, ti):
        return jnp.where(ti + blk0 < n_ctx_blk, b, bi)

    row3 = lambda bi, ti: (bi, ti + blk0, 0)
    out3 = lambda bi, ti: (bi, ti, 0)
    const = lambda bi, ti: (0, 0)
    modspec = lambda j: pl.BlockSpec((1, 1, d), lambda bi, ti: (mrow(bi, ti), 0, j))
    yspec = lambda y: pl.BlockSpec((1, tm, GROUP_W), row3 if y.shape[1] == t else out3)
    return pl.pallas_call(
        _out_kernel,
        grid=(b, nblk),
        in_specs=[yspec(ya), yspec(yb), yspec(yc), yspec(yd), pl.BlockSpec((d, d), const),
                  pl.BlockSpec((1, tm, d), row3), modspec(2), modspec(3), modspec(4),
                  pl.BlockSpec((1, d), const), pl.BlockSpec((1, d), const),
                  pl.BlockSpec((d, 128), const), pl.BlockSpec((1, 128), const)],
        out_specs=[pl.BlockSpec((1, tm, d), out3),
                   pl.BlockSpec((tm * d // 128, 128), lambda bi, ti: (bi * nblk + ti, 0)),
                   pl.BlockSpec((1, tm, 8), out3), pl.BlockSpec((1, tm, 8), out3)],
        out_shape=[jax.ShapeDtypeStruct((b, t_out, d), F32),
                   jax.ShapeDtypeStruct((b * t_out * d // 128, 128), F32),
                   jax.ShapeDtypeStruct((b, t_out, 8), I32),
                   jax.ShapeDtypeStruct((b, t_out, 8), F32)],
        compiler_params=_params(("parallel", "parallel"), VMEM_LIMIT),
        name="out_proj_router",
    )(ya, yb, yc, yd, w_out, xs, mod_l, mod_l, mod_l, nw1, nw2, wr, br)


def _rank_kernel(te_ref, rank_ref, cnt_ref, carry_ref):
    i = pl.program_id(0)

    @pl.when(i == 0)
    def _():
        carry_ref[...] = jnp.zeros_like(carry_ref)

    te = te_ref[...]
    tb = te.shape[0]
    lane = lax.broadcasted_iota(I32, (tb, 128), 1)
    l8 = lax.broadcasted_iota(I32, (tb, 8), 1)
    below = (lax.broadcasted_iota(I32, (tb, tb), 0)
             > lax.broadcasted_iota(I32, (tb, tb), 1))
    lstrict = jnp.where(below, 1.0, 0.0).astype(BF16)
    base = carry_ref[...]
    out = jnp.zeros((tb, 8), I32)
    for k in range(TOP_K):
        oh = jnp.where(lane == te[:, k:k + 1], 1.0, 0.0)
        before = _dotf(lstrict, oh.astype(BF16)) + base
        rank_k = jnp.sum(oh * before, axis=-1, keepdims=True)
        out = jnp.where(l8 == k, rank_k.astype(I32), out)
        base = base + jnp.sum(oh, axis=0, keepdims=True)
    rank_ref[...] = out
    carry_ref[...] = base
    cnt_ref[...] = base.astype(I32)


def _rank_call(te):
    n_tok = te.shape[0]
    tb = RANK_TILE
    return pl.pallas_call(
        _rank_kernel,
        grid=(n_tok // tb,),
        in_specs=[pl.BlockSpec((tb, 8), lambda i: (i, 0))],
        out_specs=[pl.BlockSpec((tb, 8), lambda i: (i, 0)), pl.BlockSpec((1, 128), lambda i: (0, 0))],
        out_shape=[jax.ShapeDtypeStruct((n_tok, 8), I32), jax.ShapeDtypeStruct((1, 128), I32)],
        scratch_shapes=[pltpu.VMEM((1, 128), F32)],
        compiler_params=_params(("arbitrary",)),
        name="moe_rank",
    )(te)


def _dispatch_kernel(pad_ref, dest_ref, f_ref, xs_hbm, zbuf, sem, zsem, *, tb, tm, nt):
    i = pl.program_id(0)

    @pl.when(i == 0)
    def _():
        zbuf[...] = jnp.zeros_like(zbuf)
        fills = [pltpu.make_async_copy(
            zbuf, xs_hbm.at[pl.ds(pl.multiple_of(pad_ref[e] * nt, nt), tm * nt)], zsem)
            for e in range(N_EXPERTS)]
        for fill in fills:
            fill.start()
        for fill in fills:
            fill.wait()

        def fill_unused(blk, carry):
            tail = pltpu.make_async_copy(
                zbuf, xs_hbm.at[pl.ds(pl.multiple_of(blk * (tm * nt), tm * nt), tm * nt)], zsem)
            tail.start()
            tail.wait()
            return carry
        lax.fori_loop(pad_ref[N_EXPERTS], xs_hbm.shape[0] // (tm * nt), fill_unused, 0)

    def body(g, carry):
        r0 = g * DMA_UNROLL
        dsts = [pl.multiple_of(dest_ref[0, 0, r0 * TOP_K + j] * nt, nt)
                for j in range(DMA_UNROLL * TOP_K)]
        for u in range(DMA_UNROLL):
            src = f_ref.at[pl.ds(pl.multiple_of((r0 + u) * nt, nt), nt)]
            for k in range(TOP_K):
                pltpu.make_async_copy(src, xs_hbm.at[pl.ds(dsts[u * TOP_K + k], nt)],
                                      sem).start(priority=k % 2)
        return carry

    lax.fori_loop(0, tb // DMA_UNROLL, body, 0)
    for k in range(TOP_K):
        pltpu.make_async_copy(f_ref, xs_hbm.at[pl.ds(0, tb * nt)], sem).wait()


def _dispatch_call(pad_start, dest3, f_tiles, n_slots, tm):
    n_blk, _, per = dest3.shape
    tb = per // TOP_K
    nt = f_tiles.shape[0] // (n_blk * tb)
    grid_spec = pltpu.PrefetchScalarGridSpec(
        num_scalar_prefetch=1,
        grid=(n_blk,),
        in_specs=[pl.BlockSpec((1, 1, per), lambda i, pad: (i, 0, 0), memory_space=pltpu.SMEM),
                  pl.BlockSpec((tb * nt, 128), lambda i, pad: (i, 0))],
        out_specs=pl.BlockSpec(memory_space=pl.ANY),
        scratch_shapes=[pltpu.VMEM((tm * nt, 128), F32), pltpu.SemaphoreType.DMA(()),
                        pltpu.SemaphoreType.DMA(())])
    return pl.pallas_call(
        functools.partial(_dispatch_kernel, tb=tb, tm=tm, nt=nt),
        grid_spec=grid_spec,
        out_shape=jax.ShapeDtypeStruct(((n_slots + tm) * nt, 128), F32),
        compiler_params=_params(("arbitrary",), VMEM_LIMIT),
        name="moe_dispatch",
    )(pad_start, dest3, f_tiles)


def _expert_kernel(be_ref, nu_ref, xs_ref, wgu_ref, bgu_ref, wd_ref, bd_ref, ys_ref,
                   wgu_s, wd_s, *, tm, d_ff):
    i = pl.program_id(0)
    nt = xs_ref.shape[0] // tm

    @pl.when(i < nu_ref[0])
    def _():
        @pl.when(jnp.logical_or(i == 0, be_ref[i] != be_ref[jnp.maximum(i - 1, 0)]))
        def _():
            for c in range(0, wgu_s.shape[0], 128):
                wgu_s[c:c + 128, :] = wgu_ref[c:c + 128, :].astype(BF16)
            for c in range(0, wd_s.shape[0], 128):
                wd_s[c:c + 128, :] = wd_ref[c:c + 128, :].astype(BF16)

        for r0 in range(0, tm, MOE_ROWS):
            x = jnp.concatenate(_load_token_tiles(xs_ref, MOE_ROWS, nt, base=r0 * nt),
                                axis=1).astype(BF16)
            gu = _dotf(x, wgu_s[...]) + bgu_ref[...]
            gate = jnp.minimum(gu[:, 0:d_ff], SWIGLU_LIMIT)
            up = jnp.clip(gu[:, d_ff:2 * d_ff], -SWIGLU_LIMIT, SWIGLU_LIMIT)
            act = (up + 1.0) * gate * _sigmoid(SWIGLU_ALPHA * gate)
            _store_token_tiles(ys_ref, _dotf(act.astype(BF16), wd_s[...]) + bd_ref[...],
                               base=r0 * nt)

    @pl.when(i >= nu_ref[0])
    def _():
        ys_ref[...] = jnp.zeros_like(ys_ref)


def _expert_call(block_expert, n_used, xs_tiles, wgu, bgu, wd, bd, layer, n_blocks, tm):
    d, two_ff = wgu.shape[2:]
    d_ff = two_ff // 2
    nt = d // 128
    ex = lambda i, be, nu: (layer, be[i], 0, 0)
    grid_spec = pltpu.PrefetchScalarGridSpec(
        num_scalar_prefetch=2,
        grid=(n_blocks,),
        in_specs=[pl.BlockSpec((tm * nt, 128), lambda i, be, nu: (jnp.minimum(i, nu[0] - 1), 0)),
                  pl.BlockSpec((None, None, d, two_ff), ex),
                  pl.BlockSpec((None, None, 1, two_ff), ex),
                  pl.BlockSpec((None, None, d_ff, d), ex),
                  pl.BlockSpec((None, None, 1, d), ex)],
        out_specs=pl.BlockSpec((tm * nt, 128), lambda i, be, nu: (i, 0)),
        scratch_shapes=[pltpu.VMEM((d, two_ff), BF16), pltpu.VMEM((d_ff, d), BF16)])
    return pl.pallas_call(
        functools.partial(_expert_kernel, tm=tm, d_ff=d_ff),
        grid_spec=grid_spec,
        out_shape=jax.ShapeDtypeStruct((n_blocks * tm * nt, 128), F32),
        compiler_params=_params(("arbitrary",), VMEM_LIMIT),
        name="moe_experts",
    )(block_expert, n_used, xs_tiles, wgu, bgu, wd, bd)


def _route(te, rank, counts, n_tok, tm):
    n_blocks = -(-(n_tok * TOP_K + N_EXPERTS * (tm - 1)) // tm)
    padded = (counts + tm - 1) // tm * tm
    padded_end = jnp.cumsum(padded)
    group_start = padded_end - padded
    dest = jnp.take(group_start, te) + rank
    block_start = jnp.arange(n_blocks, dtype=I32) * tm
    block_expert = jnp.minimum(
        jnp.sum((padded_end[None, :] <= block_start[:, None]).astype(I32), axis=1), N_EXPERTS - 1)
    n_used = (padded_end[-1] // tm).astype(I32).reshape(1)
    pad_table = jnp.concatenate([group_start + counts, n_used])
    return dest, pad_table, block_expert, n_used, n_blocks


def _fin_kernel(dcur_ref, dnxt_ref, g_ref, x_ref, g2_ref, nw_ref, ys_hbm, out_ref, buf, sem,
                *, tb, nt):
    i = pl.program_id(0)
    s = i % 2

    def gather(dref, ss):
        def body(g, carry):
            r0 = g * DMA_UNROLL
            srcs = [pl.multiple_of(dref[0, 0, r0 * TOP_K + j] * nt, nt)
                    for j in range(DMA_UNROLL * TOP_K)]
            for u in range(DMA_UNROLL):
                for k in range(TOP_K):
                    dst = pl.multiple_of((k * tb + r0 + u) * nt, nt)
                    pltpu.make_async_copy(ys_hbm.at[pl.ds(srcs[u * TOP_K + k], nt)],
                                          buf.at[ss, pl.ds(dst, nt)],
                                          sem.at[ss]).start(priority=k % 2)
            return carry
        lax.fori_loop(0, tb // DMA_UNROLL, body, 0)

    @pl.when(i == 0)
    def _():
        gather(dcur_ref, 0)

    @pl.when(i + 1 < pl.num_programs(0))
    def _():
        gather(dnxt_ref, 1 - s)

    pltpu.make_async_copy(ys_hbm.at[pl.ds(0, TOP_K * tb * nt)], buf.at[s], sem.at[s]).wait()
    gates = g_ref[...]
    parts = []
    for j in range(nt):
        acc = None
        for k in range(TOP_K):
            v = buf[s, pl.ds(k * tb * nt + j, tb, stride=nt), :] * gates[:, k:k + 1]
            acc = v if acc is None else acc + v
        parts.append(acc)
    m = jnp.concatenate(parts, axis=1)
    out_ref[0] = x_ref[0] + g2_ref[0] * _rms(m, nw_ref[...])


def _fin_call(dest3, gates, ys_tiles, xn, mod_l, nw, n_ctx_blk, blk0):
    b, t, d = xn.shape
    tb = ROW_TILE
    nblk = t // tb
    nt = d // 128
    n_blk = b * nblk

    def mrow(i):
        return jnp.where(i % nblk + blk0 < n_ctx_blk, b, i // nblk)

    smem = functools.partial(pl.BlockSpec, memory_space=pltpu.SMEM)
    return pl.pallas_call(
        functools.partial(_fin_kernel, tb=tb, nt=nt),
        grid=(n_blk,),
        in_specs=[smem((1, 1, tb * TOP_K), lambda i: (i, 0, 0)),
                  smem((1, 1, tb * TOP_K), lambda i: (jnp.minimum(i + 1, n_blk - 1), 0, 0)),
                  pl.BlockSpec((tb, 8), lambda i: (i, 0)),
                  pl.BlockSpec((1, tb, d), lambda i: (i // nblk, i % nblk, 0)),
                  pl.BlockSpec((1, 1, d), lambda i: (mrow(i), 0, 5)),
                  pl.BlockSpec((1, d), lambda i: (0, 0)),
                  pl.BlockSpec(memory_space=pl.ANY)],
        out_specs=pl.BlockSpec((1, tb, d), lambda i: (i // nblk, i % nblk, 0)),
        out_shape=jax.ShapeDtypeStruct((b, t, d), F32),
        scratch_shapes=[pltpu.VMEM((2, TOP_K * tb * nt, 128), F32), pltpu.SemaphoreType.DMA((2,))],
        compiler_params=_params(("arbitrary",), VMEM_LIMIT),
        name="moe_combine",
    )(dest3, dest3, gates, xn, mod_l, nw, ys_tiles)


def _in_proj_columns():
    cols = np.full((IN_COLS,), -1, np.int64)
    for sec in range(2):
        for n in range(256):
            part, hm, j = n // 128, (n % 128) // 16, n % 16
            m, h = hm // 4, hm % 4
            cols[sec * 256 + n] = sec * 256 + h * 64 + m * 32 + part * 16 + j
    cols[512:768] = np.arange(512, 768)
    o, s = IN_DA, 768
    cols[o:o + 1024] = s + np.arange(1024)
    o, s = o + IN_ML, s + 1024
    cols[o:o + 16] = s + np.arange(16)
    o, s = o + IN_G, s + 16
    cols[o:o + MLA_Q_RANK] = s + np.arange(MLA_Q_RANK)
    cols[o + 256:o + 256 + MLA_KV_RANK] = s + MLA_Q_RANK + np.arange(MLA_KV_RANK)
    cols[o + 384 + 64:o + 384 + 96] = s + MLA_Q_RANK + MLA_KV_RANK + np.arange(MLA_ROPE)
    o, s = o + IN_MLA, s + MLA_Q_RANK + MLA_KV_RANK + MLA_ROPE
    cols[o:o + 256] = s + np.arange(256)
    return cols


def _take_cols(w, cols):
    valid = jnp.asarray(cols >= 0)
    return jnp.where(valid, jnp.take(w, jnp.asarray(np.maximum(cols, 0)), axis=-1), 0.0)


def _rope_tables(n_ctx, n_lat):
    pos = jnp.arange(n_lat)
    inv = ROPE_THETA ** (-jnp.arange(8, dtype=F32) / 8)
    ang = jnp.concatenate([(pos // GRID_W)[:, None] * inv, (pos % GRID_W)[:, None] * inv], axis=-1)
    cos = jnp.concatenate([jnp.ones((n_ctx, 16), F32), jnp.cos(ang)], axis=0)
    sin = jnp.concatenate([jnp.zeros((n_ctx, 16), F32), jnp.sin(ang)], axis=0)
    t = n_ctx + n_lat
    cos_da, sin_da = jnp.tile(cos, (1, 8)), jnp.tile(sin, (1, 8))
    one, zero = jnp.ones((t, 64), F32), jnp.zeros((t, 64), F32)
    z16, z32 = jnp.zeros((t, 16), F32), jnp.zeros((t, 32), F32)
    cm = jnp.concatenate([one, cos, cos, jnp.ones((t, 32), F32)], axis=1)
    sa = jnp.concatenate([zero, -sin, z16, z32], axis=1)
    sb = jnp.concatenate([zero, z16, sin, z32], axis=1)
    return cos_da, sin_da, cm, sa, sb


def _mla_weights(w_uq, w_ukv):
    hd = MLA_NOPE + MLA_ROPE
    wq = jnp.zeros((256, 512), F32)
    wkv = jnp.zeros((MLA_KV_RANK, 768), F32)
    for h in range(MLA_HEADS):
        wq = wq.at[:MLA_Q_RANK, 128 * h:128 * h + hd].set(w_uq[:, hd * h:hd * (h + 1)])
        wkv = wkv.at[:, 128 * h:128 * h + MLA_NOPE].set(w_ukv[:, 128 * h:128 * h + MLA_NOPE])
        wkv = wkv.at[:, 512 + 64 * h:512 + 64 * (h + 1)].set(w_ukv[:, 128 * h + MLA_NOPE:128 * (h + 1)])
    return wq.astype(BF16), wkv.astype(BF16)


def _s5_layout(a_re, a_im, log_step, b_re, b_im, c_re, c_im):
    ns = S5_NGROUPS * S5_STATE
    eye = jnp.eye(S5_NGROUPS, dtype=F32)
    are = a_re.reshape(2, 1, ns)
    aim = a_im.reshape(2, 1, ns)
    ls = jnp.repeat(log_step, S5_STATE, axis=-1).reshape(2, 1, ns)
    bd_b = lambda w: jnp.einsum("dgph,gk->dghkp", w, eye).reshape(2, GROUP_W, ns)
    bd_c = lambda w: jnp.einsum("dghp,gk->dgpkh", w, eye).reshape(2, ns, GROUP_W)
    return are, aim, ls, bd_b(b_re), bd_b(b_im), bd_c(c_re), bd_c(c_im)


def kernel(x, c, ctx, c_ctx, w_mod, b_mod, norm_w, w_in, w_out, da_lambda, da_subln, ml_conv_w,
           ml_conv_b, ml_gate_b, ml_norm, mla_q_norm, mla_w_uq, mla_kv_norm, mla_w_ukv, s5_a_re,
           s5_a_im, s5_log_step, s5_b_re, s5_b_im, s5_c_re, s5_c_im, s5_d, s5_w_glu, s5_b_glu,
           moe_w_router, moe_b_router, moe_w_gate_up, moe_b_gate_up, moe_w_down, moe_b_down):
    bsz, n_lat, d = x.shape
    n_ctx = ctx.shape[1]
    t = n_ctx + n_lat
    depth = w_mod.shape[0]
    tm = ROW_TILE
    assert n_ctx % tm == 0 and n_lat % tm == 0 and bsz % 8 == 0 and bsz < 16
    assert n_ctx % ML_CHUNK == 0 and n_ctx % S5_CHUNK == 0
    n_ctx_blk = n_ctx // tm

    cc = jnp.zeros((16, d), F32).at[:bsz].set(c).at[bsz].set(c_ctx)
    mod = _mod_call(cc, w_mod, b_mod)
    cos_da, sin_da, cm, sa, sb = _rope_tables(n_ctx, n_lat)
    in_cols = _in_proj_columns()
    xs = jnp.concatenate([ctx, x], axis=1)

    for l in range(depth):
        last = l == depth - 1
        lambda_init = 0.8 - 0.6 * math.exp(-0.3 * l)
        mod_l = mod[l].reshape(16, 1, 6 * d)
        w_in_p = _take_cols(w_in[l], in_cols).astype(BF16)
        p_da, p_ml, p_g, p_mla, u_t = _in_call(xs, mod_l, norm_w[l, 0].reshape(1, d), w_in_p,
                                               cos_da, sin_da, n_ctx_blk)

        da_extra = [da_lambda[l], jnp.tile(da_subln[l], 4).reshape(1, GROUP_W)]
        da_kern = functools.partial(_da_kernel, lambda_init=lambda_init)
        da_kw = dict(q_col=0, k_col=1, v_col=2, q_w=GROUP_W)
        ya = _attn_call(da_kern, "diff_attn", p_da, p_da, p_da, da_extra, q_blk0=n_ctx_blk,
                        n_q_blk=n_lat // tm, n_keys=t, **da_kw)
        wq, wkv = _mla_weights(mla_w_uq[l], mla_w_ukv[l])
        qn = jnp.zeros((1, 256), F32).at[0, :MLA_Q_RANK].set(mla_q_norm[l])
        q_mla, k_mla, v_mla = _mla_prep_call(p_mla, qn, wq, mla_kv_norm[l].reshape(1, -1), wkv,
                                             cm, sa, sb)
        mla_kw = dict(q_col=0, k_col=0, v_col=0, q_w=512)
        yc = _attn_call(_mla_attn_kernel, "mla_attn", q_mla, k_mla, v_mla, [], q_blk0=n_ctx_blk,
                        n_q_blk=n_lat // tm, n_keys=t, **mla_kw)
        if not last:
            ya_c = _attn_call(da_kern, "diff_attn_ctx", p_da, p_da, p_da, da_extra, q_blk0=0,
                              n_q_blk=n_ctx_blk, n_keys=n_ctx, **da_kw)
            yc_c = _attn_call(_mla_attn_kernel, "mla_attn_ctx", q_mla, k_mla, v_mla, [], q_blk0=0,
                              n_q_blk=n_ctx_blk, n_keys=n_ctx, **mla_kw)
            ya = jnp.concatenate([ya_c, ya], axis=1)
            yc = jnp.concatenate([yc_c, yc], axis=1)

        gb = jnp.zeros((1, 128), F32).at[0, :16].set(ml_gate_b[l])
        yb = _ml_call(p_ml, p_g, ml_conv_w[l], ml_conv_b[l].reshape(1, -1), gb,
                      ml_norm[l].reshape(1, -1), n_ctx)

        s5p = _s5_layout(s5_a_re[l], s5_a_im[l], s5_log_step[l], s5_b_re[l], s5_b_im[l],
                         s5_c_re[l], s5_c_im[l])
        ys = _s5_call(u_t.reshape(t * bsz, GROUP_W), *s5p, n_ctx, bsz)
        yd = _glu_call(ys.reshape(2, t, bsz * GROUP_W), u_t, s5_d[l].reshape(1, -1),
                       s5_w_glu[l].astype(BF16), s5_b_glu[l].reshape(1, -1), bsz)

        blk0 = n_ctx_blk if last else 0
        wr = jnp.zeros((d, 128), F32).at[:, :N_EXPERTS].set(moe_w_router[l])
        br = jnp.full((1, 128), -1e30, F32).at[0, :N_EXPERTS].set(moe_b_router[l])
        xn, f, te, tg = _out_call(ya, yb, yc, yd, w_out[l].astype(BF16), xs, mod_l,
                                  norm_w[l, 1].reshape(1, d), norm_w[l, 2].reshape(1, d), wr, br,
                                  n_ctx_blk, blk0)

        t_moe = t - blk0 * tm
        n_tok = bsz * t_moe
        te = te.reshape(n_tok, 8)
        rank, cnt = _rank_call(te)
        dest, pad_start, be, nu, n_blocks = _route(te[:, :TOP_K], rank[:, :TOP_K],
                                                   cnt[0, :N_EXPERTS], n_tok, MOE_TILE)
        dest3 = dest.reshape(n_tok // tm, 1, tm * TOP_K)
        xs_tiles = _dispatch_call(pad_start, dest3, f, n_blocks * MOE_TILE, MOE_TILE)
        ys_tiles = _expert_call(be, nu, xs_tiles, moe_w_gate_up,
                                moe_b_gate_up.reshape(depth, N_EXPERTS, 1, -1), moe_w_down,
                                moe_b_down.reshape(depth, N_EXPERTS, 1, -1), l, n_blocks, MOE_TILE)
        xs = _fin_call(dest3, tg.reshape(n_tok, 8), ys_tiles, xn, mod_l,
                       norm_w[l, 3].reshape(1, d), n_ctx_blk, blk0)
    return xs
```

```python
import functools
import math

import numpy as np
import jax
import jax.numpy as jnp
from jax import lax
from jax.experimental import pallas as pl
from jax.experimental.pallas import tpu as pltpu

F32, BF16, I32 = jnp.float32, jnp.bfloat16, jnp.int32
NORM_EPS = 1e-6
GRID_W = 64
ROPE_THETA = 10000.0
GROUP_W = 256
DA_QK = 32
ML_CHUNK = 128
MLA_HEADS, MLA_NOPE, MLA_ROPE, MLA_Q_RANK, MLA_KV_RANK = 4, 64, 32, 192, 128
S5_NGROUPS, S5_GROUP, S5_STATE = 16, 16, 64
N_EXPERTS, TOP_K = 32, 4
SWIGLU_ALPHA, SWIGLU_LIMIT = 1.702, 7.0
NEG_INF = float("-inf")

ROW_TILE = 256
MOE_TILE = 256
MOE_ROWS = 256
S5_CHUNK = 64
RANK_TILE = 512
DMA_UNROLL = 8
VMEM_LIMIT = 56 * 1024 * 1024

IN_DA, IN_ML, IN_G, IN_MLA, IN_S5 = 768, 1024, 128, 512, 256
IN_COLS = IN_DA + IN_ML + IN_G + IN_MLA + IN_S5


def _params(sem, vmem=None):
    return pltpu.CompilerParams(dimension_semantics=sem, vmem_limit_bytes=vmem)


def _dotf(a, b):
    return jnp.dot(a, b, preferred_element_type=F32)


def _dot_nt(a, b):
    return lax.dot_general(a, b, (((1,), (1,)), ((), ())), preferred_element_type=F32)


def _split2(a):
    hi = a.astype(BF16)
    lo = (a - hi.astype(F32)).astype(BF16)
    return hi, lo


def _dot3(a, b):
    ah, al = _split2(a)
    bh, bl = _split2(b)
    return _dotf(ah, bh) + _dotf(ah, bl) + _dotf(al, bh)


def _dot_exact_rhs(a, rhs_b):
    a1 = a.astype(BF16)
    r1 = a - a1.astype(F32)
    a2 = r1.astype(BF16)
    a3 = (r1 - a2.astype(F32)).astype(BF16)
    return _dotf(a1, rhs_b) + _dotf(a2, rhs_b) + _dotf(a3, rhs_b)


def _rms(x, w):
    ms = jnp.mean(x * x, axis=-1, keepdims=True)
    return x * lax.rsqrt(ms + NORM_EPS) * w


def _head_rms(a, width):
    n = a.shape[-1]
    sh = int(math.log2(width))
    r = lax.broadcasted_iota(I32, (n, n), 0) >> sh
    c = lax.broadcasted_iota(I32, (n, n), 1) >> sh
    g = jnp.where(r == c, 1.0 / width, 0.0).astype(BF16)
    hi, lo = _split2(a * a)
    ms = _dotf(hi, g) + _dotf(lo, g)
    return a * lax.rsqrt(ms + NORM_EPS)


def _store_token_tiles(ref, val, base=0):
    tm, d = val.shape
    nt = d // 128
    for j in range(nt):
        ref[pl.ds(base + j, tm, stride=nt), :] = val[:, 128 * j:128 * (j + 1)]


def _load_token_tiles(ref, tm, nt, base=0, lead=()):
    parts = [ref[lead + (pl.ds(base + j, tm, stride=nt), slice(None))] for j in range(nt)]
    return parts


def _sigmoid(x):
    return jax.nn.sigmoid(x)


def _log_sigmoid(x):
    return jnp.minimum(x, 0.0) - jnp.log(1.0 + jnp.exp(-jnp.abs(x)))


def _mod_kernel(c_ref, w_ref, b_ref, o_ref):
    c = c_ref[...]
    o_ref[0] = _dot3(c * _sigmoid(c), w_ref[0]) + b_ref[0]


def _mod_call(cc, w_mod, b_mod):
    n_layers, d, n = w_mod.shape
    tn = 1536
    return pl.pallas_call(
        _mod_kernel,
        grid=(n_layers, n // tn),
        in_specs=[pl.BlockSpec((16, d), lambda l, j: (0, 0)),
                  pl.BlockSpec((1, d, tn), lambda l, j: (l, 0, j)),
                  pl.BlockSpec((1, 1, tn), lambda l, j: (l, 0, j))],
        out_specs=pl.BlockSpec((1, 16, tn), lambda l, j: (l, 0, j)),
        out_shape=jax.ShapeDtypeStruct((n_layers, 16, n), F32),
        compiler_params=_params(("parallel", "parallel"), VMEM_LIMIT),
        name="mod_vectors",
    )(cc, w_mod, b_mod.reshape(n_layers, 1, n))


def _in_kernel(xa_ref, xb_ref, sh_ref, sc_ref, nw_ref, w_ref, c_ref, s_ref,
               da_ref, ml_ref, g_ref, mla_ref, s5_ref, *, qscale, n_a_blk):
    x = jnp.where(pl.program_id(1) < n_a_blk, xa_ref[0], xb_ref[0])
    h = _rms(x, nw_ref[...]) * (1.0 + sc_ref[0]) + sh_ref[0]
    hb = h.astype(BF16)
    da = _dotf(hb, w_ref[:, 0:IN_DA])
    c = c_ref[...]
    s = s_ref[...]
    q1, q2, k1, k2 = da[:, 0:128], da[:, 128:256], da[:, 256:384], da[:, 384:512]
    da_ref[0, :, 0:128] = ((q1 * c - q2 * s) * qscale).astype(BF16)
    da_ref[0, :, 128:256] = ((q2 * c + q1 * s) * qscale).astype(BF16)
    da_ref[0, :, 256:384] = (k1 * c - k2 * s).astype(BF16)
    da_ref[0, :, 384:512] = (k2 * c + k1 * s).astype(BF16)
    da_ref[0, :, 512:768] = da[:, 512:768].astype(BF16)
    o = IN_DA
    ml_ref[0] = _dotf(hb, w_ref[:, o:o + IN_ML])
    o += IN_ML
    g_ref[0] = _dotf(hb, w_ref[:, o:o + IN_G])
    o += IN_G
    mla_ref[0] = _dotf(hb, w_ref[:, o:o + IN_MLA])
    o += IN_MLA
    s5_ref[...] = _dotf(hb, w_ref[:, o:o + IN_S5])


def _stream_specs(xa, xb, blk0):
    tm = ROW_TILE
    d = xa.shape[2]
    n_a = xa.shape[1] // tm
    spec_a = pl.BlockSpec((1, tm, d), lambda bi, ti: (bi, jnp.minimum(ti + blk0, n_a - 1), 0))
    spec_b = pl.BlockSpec((1, tm, d), lambda bi, ti: (bi, jnp.maximum(ti + blk0 - n_a, 0), 0))
    return spec_a, spec_b, n_a


def _in_call(xa, xb, mod_l, nw, w_in_p, cos_da, sin_da, n_ctx_blk):
    b, _, d = xa.shape
    t = xa.shape[1] + (0 if xb is None else xb.shape[1])
    tm = ROW_TILE
    spec_a, spec_b, n_a_blk = _stream_specs(xa, xb, 0)

    def mrow(bi, ti):
        return jnp.where(ti < n_ctx_blk, b, bi)

    row3 = lambda bi, ti: (bi, ti, 0)
    return pl.pallas_call(
        functools.partial(_in_kernel, qscale=DA_QK ** -0.5 * LOG2_E, n_a_blk=n_a_blk),
        grid=(b, t // tm),
        in_specs=[spec_a, spec_b,
                  pl.BlockSpec((1, 1, d), lambda bi, ti: (mrow(bi, ti), 0, 0)),
                  pl.BlockSpec((1, 1, d), lambda bi, ti: (mrow(bi, ti), 0, 1)),
                  pl.BlockSpec((1, d), lambda bi, ti: (0, 0)),
                  pl.BlockSpec((d, IN_COLS), lambda bi, ti: (0, 0)),
                  pl.BlockSpec((tm, 128), lambda bi, ti: (ti, 0)),
                  pl.BlockSpec((tm, 128), lambda bi, ti: (ti, 0))],
        out_specs=[pl.BlockSpec((1, tm, IN_DA), row3),
                   pl.BlockSpec((1, tm, IN_ML), row3),
                   pl.BlockSpec((1, tm, IN_G), row3),
                   pl.BlockSpec((1, tm, IN_MLA), row3),
                   pl.BlockSpec((tm, IN_S5), lambda bi, ti: (ti, bi))],
        out_shape=[jax.ShapeDtypeStruct((b, t, IN_DA), BF16),
                   jax.ShapeDtypeStruct((b, t, IN_ML), F32),
                   jax.ShapeDtypeStruct((b, t, IN_G), F32),
                   jax.ShapeDtypeStruct((b, t, IN_MLA), F32),
                   jax.ShapeDtypeStruct((t, b * IN_S5), F32)],
        compiler_params=_params(("parallel", "parallel"), VMEM_LIMIT),
        name="in_proj",
    )(xa, xa if xb is None else xb, mod_l, mod_l, nw, w_in_p, cos_da, sin_da)


LOG2_E = 1.0 / math.log(2.0)


def _softmax_rows(s):
    mx = jnp.max(s, axis=-1, keepdims=True)
    p = jnp.exp2(s - mx)
    return p, jnp.sum(p, axis=-1, keepdims=True)


def _da_kernel(q_ref, k_ref, v_ref, lam_ref, sub_ref, o_ref, *, lambda_init):
    q = q_ref[0]
    k = k_ref[0]
    v = v_ref[0]
    lp = lam_ref[...]
    lam = (jnp.exp(jnp.sum(lp[0:1] * lp[1:2], axis=-1, keepdims=True))
           - jnp.exp(jnp.sum(lp[2:3] * lp[3:4], axis=-1, keepdims=True)) + lambda_init)
    lane = lax.broadcasted_iota(I32, (1, GROUP_W), 1)
    grp = (lane & 127) >> 4
    head = lane >> 6
    acc = jnp.zeros((q.shape[0], GROUP_W), F32)
    for h in range(4):
        ps, rs = [], []
        for m in range(2):
            qm = jnp.where(grp == m * 4 + h, q, jnp.zeros_like(q))
            p, l = _softmax_rows(_dot_nt(qm, k))
            ps.append(p.astype(BF16))
            rs.append(((1.0 if m == 0 else lam) / l).astype(BF16))
        w = ps[0] * rs[0] - ps[1] * rs[1]
        vm = jnp.where(head == h, v, jnp.zeros_like(v))
        acc = acc + _dotf(w, vm)
    y = _head_rms(acc, 64) * sub_ref[...] * (1.0 - lambda_init)
    o_ref[0] = y.astype(BF16)


def _mla_attn_kernel(q_ref, k_ref, v_ref, o_ref):
    q = q_ref[0]
    k = k_ref[0]
    v = v_ref[0]
    head = lax.broadcasted_iota(I32, (1, GROUP_W), 1) >> 6
    acc = jnp.zeros((q.shape[0], GROUP_W), F32)
    for h in range(MLA_HEADS):
        sl = slice(128 * h, 128 * (h + 1))
        p, l = _softmax_rows(_dot_nt(q[:, sl], k[:, sl]))
        vm = jnp.where(head == h, v, jnp.zeros_like(v))
        acc = acc + _dotf(p.astype(BF16), vm) * (1.0 / l)
    o_ref[0] = acc.astype(BF16)


def _attn_call(kernel, name, q_arr, k_arr, v_arr, extra, *, q_blk0, n_q_blk, n_keys,
               q_col, k_col, v_col, q_w):
    b = q_arr.shape[0]
    tq = ROW_TILE
    in_specs = [pl.BlockSpec((1, tq, q_w), lambda bi, qi: (bi, qi + q_blk0, q_col)),
                pl.BlockSpec((1, n_keys, q_w), lambda bi, qi: (bi, 0, k_col)),
                pl.BlockSpec((1, n_keys, GROUP_W), lambda bi, qi: (bi, 0, v_col))]
    in_specs += [pl.BlockSpec(e.shape, lambda bi, qi: (0, 0)) for e in extra]
    return pl.pallas_call(
        kernel,
        grid=(b, n_q_blk),
        in_specs=in_specs,
        out_specs=pl.BlockSpec((1, tq, GROUP_W), lambda bi, qi: (bi, qi, 0)),
        out_shape=jax.ShapeDtypeStruct((b, n_q_blk * tq, GROUP_W), BF16),
        compiler_params=_params(("parallel", "arbitrary"), VMEM_LIMIT),
        name=name,
    )(q_arr, k_arr, v_arr, *extra)


def _mla_prep_kernel(p_ref, qn_ref, wq_ref, kvn_ref, wkv_ref, c_ref, sa_ref, sb_ref,
                     q_ref, k_ref, v_ref, *, scale):
    p = p_ref[0]
    cq, ckv, kr = p[:, 0:256], p[:, 256:384], p[:, 384:512]
    msq = jnp.sum(cq * cq, axis=-1, keepdims=True) * (1.0 / MLA_Q_RANK)
    qn = (cq * lax.rsqrt(msq + NORM_EPS) * qn_ref[...]).astype(BF16)
    q = _dotf(qn, wq_ref[...])
    c = c_ref[...]
    sa = sa_ref[...]
    sb = sb_ref[...]

    def rope(a):
        return a * c + pltpu.roll(a, 112, 1) * sa + pltpu.roll(a, 16, 1) * sb

    for h in range(MLA_HEADS):
        sl = slice(128 * h, 128 * (h + 1))
        q_ref[0, :, sl] = (rope(q[:, sl]) * scale).astype(BF16)
    kvn = (_rms(ckv, kvn_ref[...])).astype(BF16)
    kv = _dotf(kvn, wkv_ref[...])
    krr = rope(kr)
    for h in range(MLA_HEADS):
        sl = slice(128 * h, 128 * (h + 1))
        k_ref[0, :, sl] = (kv[:, sl] + krr).astype(BF16)
    v_ref[0] = kv[:, 512:768].astype(BF16)


def _mla_prep_call(p_mla, qn, wq, kvn, wkv, cm, sa, sb):
    b, t, _ = p_mla.shape
    tm = ROW_TILE
    row3 = lambda bi, ti: (bi, ti, 0)
    const = lambda bi, ti: (0, 0)
    tab = lambda bi, ti: (ti, 0)
    return pl.pallas_call(
        functools.partial(_mla_prep_kernel, scale=(MLA_NOPE + MLA_ROPE) ** -0.5 * LOG2_E),
        grid=(b, t // tm),
        in_specs=[pl.BlockSpec((1, tm, IN_MLA), row3),
                  pl.BlockSpec(qn.shape, const), pl.BlockSpec(wq.shape, const),
                  pl.BlockSpec(kvn.shape, const), pl.BlockSpec(wkv.shape, const),
                  pl.BlockSpec((tm, 128), tab), pl.BlockSpec((tm, 128), tab),
                  pl.BlockSpec((tm, 128), tab)],
        out_specs=[pl.BlockSpec((1, tm, 512), row3), pl.BlockSpec((1, tm, 512), row3),
                   pl.BlockSpec((1, tm, GROUP_W), row3)],
        out_shape=[jax.ShapeDtypeStruct((b, t, 512), BF16),
                   jax.ShapeDtypeStruct((b, t, 512), BF16),
                   jax.ShapeDtypeStruct((b, t, GROUP_W), BF16)],
        compiler_params=_params(("parallel", "parallel"), VMEM_LIMIT),
        name="mla_prep",
    )(p_mla, qn, wq, kvn, wkv, cm, sa, sb)


def _ml_kernel(p_ref, g_ref, cw_ref, cb_ref, gb_ref, nw_ref, y_ref,
               qt_s, k_s, vt_s, ot_s, gt_s, hf_s, c_s, m_s, *, n_ctx_chunks, n_chunks):
    cl = ML_CHUNK
    t_total = n_chunks * cl
    row = lax.broadcasted_iota(I32, (cl, 1), 0)
    si = lax.broadcasted_iota(I32, (cl, cl), 0)
    ti = lax.broadcasted_iota(I32, (cl, cl), 1)
    lane128 = lax.broadcasted_iota(I32, (1, 128), 1)
    lane256 = lax.broadcasted_iota(I32, (1, 256), 1)
    row128 = lax.broadcasted_iota(I32, (128, 1), 0)
    hmask = [(lane256 >> 6) == h for h in range(4)]
    is_f = jnp.logical_and(((lane128 >> 2) & 1) == 1, lane128 < 16)
    r_nd = lax.broadcasted_iota(I32, (384, 512), 0)
    c_nd = lax.broadcasted_iota(I32, (384, 512), 1) >> 7
    nd_head = jnp.where(r_nd < 256, r_nd >> 6, r_nd - 256)
    nd_mask = nd_head == c_nd
    nd_ones = jnp.where(jnp.logical_and(nd_mask, r_nd >= 256), 1.0, 0.0).astype(BF16)
    r_st = lax.broadcasted_iota(I32, (384, 256), 0)
    st_head = jnp.where(r_st < 256, r_st >> 6, r_st - 256)
    st_mask = st_head == (lax.broadcasted_iota(I32, (384, 256), 1) >> 6)
    st_rowhead = jnp.where(lax.broadcasted_iota(I32, (384, 1), 0) < 256,
                           lax.broadcasted_iota(I32, (384, 1), 0) >> 6,
                           lax.broadcasted_iota(I32, (384, 1), 0) - 256)
    w0, w1, w2 = cw_ref[0:1], cw_ref[1:2], cw_ref[2:3]
    cb = cb_ref[...]
    gb = gb_ref[...]

    def prep_body(c, carry):
        s0 = pl.multiple_of(c * cl, cl)
        x = p_ref[0, pl.ds(s0, cl), 0:512]
        sp = pl.multiple_of(jnp.maximum(s0 - 8, 0), 8)
        sn = pl.multiple_of(jnp.minimum(s0 + cl, t_total - 8), 8)
        has_prev = jnp.logical_and(c != 0, c != n_ctx_chunks).astype(F32)
        has_next = jnp.logical_and(c != n_ctx_chunks - 1, c != n_chunks - 1).astype(F32)
        prev_row = p_ref[0, pl.ds(sp, 8), 0:512][7:8] * has_prev
        next_row = p_ref[0, pl.ds(sn, 8), 0:512][0:1] * has_next
        xp = jnp.where(row == 0, prev_row, pltpu.roll(x, 1, 0))
        xn = jnp.where(row == cl - 1, next_row, pltpu.roll(x, cl - 1, 0))
        z = xp * w0 + x * w1 + xn * w2 + cb
        qk = z * _sigmoid(z)
        qt_s[c] = qk[:, 0:256].T.astype(BF16)
        k_s[c] = (qk[:, 256:512] * (64 ** -0.5)).astype(BF16)
        vt_s[c] = p_ref[0, pl.ds(s0, cl), 512:768].T
        ot_s[c] = p_ref[0, pl.ds(s0, cl), 768:1024].T
        g = g_ref[0, pl.ds(s0, cl), :] + gb
        gt_s[c] = jnp.where(is_f, _log_sigmoid(g), g).T[0:16]
        return carry

    def rows_to_blocks(rows, n):
        return jnp.concatenate([jnp.broadcast_to(r, (n, 128)) for r in rows], axis=0)

    def head_rows(rows):
        out = jnp.zeros((128, 128), F32)
        for h in range(4):
            out = jnp.where(row128 == h, rows[h], out)
        return out

    def chunk(c, reverse):
        g_t = gt_s[c]
        tri = (si >= ti) if reverse else (si <= ti)
        bc_t = _dot_exact_rhs(g_t, jnp.where(tri, 1.0, 0.0).astype(BF16))
        qt = qt_s[c]
        kb = k_s[c]
        m_all = m_s[...]
        off = 8 if reverse else 0
        a_rows, r_rows, tots, m_old = [], [], [], []
        for h in range(4):
            il, fl = off + h, off + 4 + h
            a_rows.append(bc_t[fl:fl + 1, :])
            r_rows.append(g_t[il:il + 1, :] - bc_t[fl:fl + 1, :])
            tots.append(bc_t[fl:fl + 1, 0:1] if reverse else bc_t[fl:fl + 1, cl - 1:cl])
            m_old.append(m_all[:, h:h + 1])
        r_all = jnp.concatenate(r_rows, axis=1)
        d_t = jnp.broadcast_to(r_all, (cl, 4 * cl)).T + rows_to_blocks(a_rows, cl)
        lw = jnp.where(jnp.concatenate([tri] * 4, axis=0), d_t, NEG_INF)
        mt, wi, emt = [], [], []
        for h in range(4):
            linter = a_rows[h] + m_old[h]
            mt_h = jnp.maximum(linter, jnp.max(lw[cl * h:cl * (h + 1)], axis=0, keepdims=True))
            mt.append(mt_h)
            wi.append(jnp.exp(linter - mt_h))
            emt.append(jnp.exp(-mt_h))
        kstack = jnp.concatenate([jnp.where(hmask[h], kb, jnp.zeros_like(kb)) for h in range(4)],
                                 axis=0)
        s_t = _dotf(kstack, qt)
        w_t = (s_t * jnp.exp(lw - rows_to_blocks(mt, cl))).astype(BF16)
        vt = vt_s[c]
        vt4 = jnp.concatenate([vt.astype(BF16)] * 4, axis=1)
        lhs_nd = jnp.concatenate([vt4, jnp.zeros((128, 512), BF16)], axis=0)
        lhs_nd = jnp.where(nd_mask, lhs_nd, jnp.zeros_like(lhs_nd)) + nd_ones
        c_aug = c_s[...]
        nd = _dotf(lhs_nd, w_t) + jnp.concatenate(
            [rows_to_blocks(wi, 64), head_rows(wi)], axis=0) * _dotf(c_aug.astype(BF16), qt)
        den = [nd[256 + h:257 + h, :] for h in range(4)]
        h_t = nd[0:256] / jnp.maximum(jnp.abs(rows_to_blocks(den, 64)), rows_to_blocks(emt, 64))
        wupd, decs, m_new_all = [], [], m_all
        for h in range(4):
            lupd = tots[h] + r_rows[h]
            m_new = jnp.maximum(tots[h] + m_old[h], jnp.max(lupd, axis=-1, keepdims=True))
            wupd.append(jnp.exp(lupd - m_new))
            decs.append(jnp.exp(tots[h] + m_old[h] - m_new))
            m_new_all = jnp.where(lane128 == h, m_new, m_new_all)
        lhs_u = jnp.concatenate([vt * rows_to_blocks(wupd, 64), head_rows(wupd)], axis=0)
        upd = _dotf(lhs_u.astype(BF16), kb)
        dec_col = jnp.zeros((384, 1), F32)
        for h in range(4):
            dec_col = jnp.where(st_rowhead == h, decs[h], dec_col)
        c_s[...] = dec_col * c_aug + jnp.where(st_mask, upd, 0.0)
        m_s[...] = m_new_all
        return h_t

    def fwd_body(i, carry):
        hf_s[i] = chunk(i, False)
        return carry

    def bwd_body(i, carry):
        c = jnp.where(i < n_ctx_chunks, n_ctx_chunks - 1 - i, n_chunks - 1 - (i - n_ctx_chunks))
        gated = _sigmoid(ot_s[c]) * (hf_s[c] + chunk(c, True))
        ms = [jnp.mean(jnp.square(gated[64 * h:64 * (h + 1)]), axis=0, keepdims=True)
              for h in range(4)]
        y_t = gated * lax.rsqrt(rows_to_blocks(ms, 64) + NORM_EPS) * nw_ref[...]
        y_ref[0, pl.ds(pl.multiple_of(c * cl, cl), cl), :] = y_t.T.astype(BF16)
        return carry

    lax.fori_loop(0, n_chunks, prep_body, 0)
    c_s[...] = jnp.zeros_like(c_s)
    m_s[...] = jnp.zeros_like(m_s)
    lax.fori_loop(0, n_chunks, fwd_body, 0)
    c_s[...] = jnp.zeros_like(c_s)
    m_s[...] = jnp.zeros_like(m_s)
    lax.fori_loop(0, n_chunks, bwd_body, 0)


def _ml_call(p_ml, p_g, cw, cb, gb, nw, n_ctx):
    b, t, _ = p_ml.shape
    nc = t // ML_CHUNK
    const = lambda bi: (0, 0)
    nw_col = jnp.broadcast_to(nw.reshape(GROUP_W, 1), (GROUP_W, 128))
    return pl.pallas_call(
        functools.partial(_ml_kernel, n_ctx_chunks=n_ctx // ML_CHUNK, n_chunks=nc),
        grid=(b,),
        in_specs=[pl.BlockSpec((1, t, IN_ML), lambda bi: (bi, 0, 0)),
                  pl.BlockSpec((1, t, IN_G), lambda bi: (bi, 0, 0)),
                  pl.BlockSpec(cw.shape, const), pl.BlockSpec(cb.shape, const),
                  pl.BlockSpec(gb.shape, const), pl.BlockSpec(nw_col.shape, const)],
        out_specs=pl.BlockSpec((1, t, GROUP_W), lambda bi: (bi, 0, 0)),
        out_shape=jax.ShapeDtypeStruct((b, t, GROUP_W), BF16),
        scratch_shapes=[pltpu.VMEM((nc, GROUP_W, 128), BF16), pltpu.VMEM((nc, 128, GROUP_W), BF16),
                        pltpu.VMEM((nc, GROUP_W, 128), F32), pltpu.VMEM((nc, GROUP_W, 128), F32),
                        pltpu.VMEM((nc, 16, 128), F32), pltpu.VMEM((nc, GROUP_W, 128), F32),
                        pltpu.VMEM((384, GROUP_W), F32), pltpu.VMEM((1, 128), F32)],
        compiler_params=_params(("parallel",), VMEM_LIMIT),
        name="mlstm",
    )(p_ml, p_g, cw, cb, gb, nw_col)


def _s5_kernel(u_ref, are_ref, aim_ref, ls_ref, bre_ref, bim_ref, cre_ref, cim_ref, y_ref,
               ar_s, ai_s, bcat_s, ccat_s, st_s, bu_s, *, tc, nb):
    d = pl.program_id(0)
    i = pl.program_id(1)
    ns = S5_NGROUPS * S5_STATE

    @pl.when(i == 0)
    def _init():
        are = jnp.minimum(are_ref[0], -1e-4)
        aim = aim_ref[0]
        dt = jnp.exp(ls_ref[0])
        mag = jnp.exp(dt * are)
        abr = mag * jnp.cos(dt * aim)
        abi = mag * jnp.sin(dt * aim)
        inv = 1.0 / (are * are + aim * aim)
        fre = ((abr - 1.0) * are + abi * aim) * inv
        fim = (abi * are - (abr - 1.0) * aim) * inv
        bre = bre_ref[0]
        bim = bim_ref[0]
        bcat_s[:, 0:ns] = (bre * fre - bim * fim).astype(BF16)
        bcat_s[:, ns:2 * ns] = (bre * fim + bim * fre).astype(BF16)
        ccat_s[0:ns, :] = cre_ref[0].astype(BF16)
        ccat_s[ns:2 * ns, :] = (-cim_ref[0]).astype(BF16)
        ar_s[...] = jnp.broadcast_to(abr, (nb, ns))
        ai_s[...] = jnp.broadcast_to(abi, (nb, ns))
        st_s[...] = jnp.zeros_like(st_s)

    bu_s[...] = _dotf(u_ref[...].astype(BF16), bcat_s[...])
    ar = ar_s[...]
    ai = ai_s[...]

    def body(j, carry):
        xr, xi = carry
        t = j + d * (tc - 1 - 2 * j)
        r0 = pl.multiple_of(t * nb, nb)
        nr = ar * xr - ai * xi + bu_s[pl.ds(r0, nb), 0:ns]
        ni = ar * xi + ai * xr + bu_s[pl.ds(r0, nb), ns:2 * ns]
        bu_s[pl.ds(r0, nb), 0:ns] = nr
        bu_s[pl.ds(r0, nb), ns:2 * ns] = ni
        return nr, ni

    xr, xi = lax.fori_loop(0, tc, body, (st_s[0], st_s[1]))
    st_s[0] = xr
    st_s[1] = xi
    y = _dotf(bu_s[...].astype(BF16), ccat_s[...])
    for half in range(y.shape[1] // 128):
        y_ref[0, half] = y[:, 128 * half:128 * (half + 1)]


def _s5_call(u_tm, are, aim, ls, bre, bim, cre, cim, n_ctx, nb):
    rows, gw = u_tm.shape
    tc = S5_CHUNK
    n_chunks = rows // (tc * nb)
    n_ctx_chunks = n_ctx // tc
    ns = S5_NGROUPS * S5_STATE

    def chunk_of(d, i):
        rev = jnp.where(i < n_ctx_chunks, n_ctx_chunks - 1 - i, n_chunks - 1 - (i - n_ctx_chunks))
        return jnp.where(d == 0, i, rev)

    vec = pl.BlockSpec((1, 1, ns), lambda d, i: (d, 0, 0))
    return pl.pallas_call(
        functools.partial(_s5_kernel, tc=tc, nb=nb),
        grid=(2, n_chunks),
        in_specs=[pl.BlockSpec((tc * nb, gw), lambda d, i: (chunk_of(d, i), 0)),
                  vec, vec, vec,
                  pl.BlockSpec((1, gw, ns), lambda d, i: (d, 0, 0)),
                  pl.BlockSpec((1, gw, ns), lambda d, i: (d, 0, 0)),
                  pl.BlockSpec((1, ns, gw), lambda d, i: (d, 0, 0)),
                  pl.BlockSpec((1, ns, gw), lambda d, i: (d, 0, 0))],
        out_specs=pl.BlockSpec((1, gw // 128, tc * nb, 128),
                               lambda d, i: (d, 0, chunk_of(d, i), 0)),
        out_shape=jax.ShapeDtypeStruct((2, gw // 128, rows, 128), F32),
        scratch_shapes=[pltpu.VMEM((nb, ns), F32), pltpu.VMEM((nb, ns), F32),
                        pltpu.VMEM((gw, 2 * ns), BF16), pltpu.VMEM((2 * ns, gw), BF16),
                        pltpu.VMEM((2, nb, ns), F32), pltpu.VMEM((tc * nb, 2 * ns), F32)],
        compiler_params=_params(("arbitrary", "arbitrary"), VMEM_LIMIT),
        name="s5_scan",
    )(u_tm, are, aim, ls, bre, bim, cre, cim)


def _glu_kernel(ys_ref, u_ref, d_ref, w_ref, b_ref, o_ref, *, nb, tq):
    gw = GROUP_W
    rows = []
    for b in range(nb):
        y_b = jnp.concatenate(
            [ys_ref[0, half, pl.ds(b, tq, stride=nb), :] + ys_ref[1, half, pl.ds(b, tq, stride=nb), :]
             for half in range(gw // 128)], axis=1)
        rows.append(y_b + u_ref[:, gw * b:gw * (b + 1)] * d_ref[...])
    y = jnp.concatenate(rows, axis=0)
    g = y * (0.5 * (1.0 + jnp.tanh(math.sqrt(2.0 / math.pi) * (y + 0.044715 * (y * y * y)))))
    z = _dotf(g.astype(BF16), w_ref[...]) + b_ref[...]
    out = (g * _sigmoid(z)).astype(BF16)
    for b in range(nb):
        o_ref[b] = out[tq * b:tq * (b + 1)]


def _glu_call(ys, u_t, dsk, w, bias, nb):
    t = u_t.shape[0]
    tq = ROW_TILE // nb
    gw = GROUP_W
    const = lambda ti: (0, 0)
    return pl.pallas_call(
        functools.partial(_glu_kernel, nb=nb, tq=tq),
        grid=(t // tq,),
        in_specs=[pl.BlockSpec((2, gw // 128, tq * nb, 128), lambda ti: (0, 0, ti, 0)),
                  pl.BlockSpec((tq, nb * gw), lambda ti: (ti, 0)),
                  pl.BlockSpec(dsk.shape, const), pl.BlockSpec(w.shape, const),
                  pl.BlockSpec(bias.shape, const)],
        out_specs=pl.BlockSpec((nb, tq, gw), lambda ti: (0, ti, 0)),
        out_shape=jax.ShapeDtypeStruct((nb, t, gw), BF16),
        compiler_params=_params(("parallel",), VMEM_LIMIT),
        name="s5_glu",
    )(ys, u_t, dsk, w, bias)


def _out_kernel(ya_ref, yb_ref, yc_ref, yd_ref, w_ref, xa_ref, xb_ref, g1_ref, sh2_ref, sc2_ref,
                nw1_ref, nw2_ref, wr_ref, br_ref, xn_ref, f_ref, te_ref, tg_ref, *, n_a_blk, blk0):
    o = (_dotf(ya_ref[0], w_ref[0:256]) + _dotf(yb_ref[0], w_ref[256:512])
         + _dotf(yc_ref[0], w_ref[512:768]) + _dotf(yd_ref[0], w_ref[768:1024]))
    x = jnp.where(pl.program_id(1) + blk0 < n_a_blk, xa_ref[0], xb_ref[0])
    xn = x + g1_ref[0] * _rms(o, nw1_ref[...])
    xn_ref[0] = xn
    f = _rms(xn, nw2_ref[...]) * (1.0 + sc2_ref[0]) + sh2_ref[0]
    _store_token_tiles(f_ref, f)
    lg = _dot3(f, wr_ref[...]) + br_ref[...]
    tm = lg.shape[0]
    lane = lax.broadcasted_iota(I32, (tm, 128), 1).astype(F32)
    tops, idxs = [], []
    for _ in range(TOP_K):
        mx = jnp.max(lg, axis=-1, keepdims=True)
        idx = jnp.min(jnp.where(lg == mx, lane, 128.0), axis=-1, keepdims=True)
        tops.append(mx)
        idxs.append(idx.astype(I32))
        lg = jnp.where(lane == idx, NEG_INF, lg)
    ex = [jnp.exp(tv - tops[0]) for tv in tops]
    inv = 1.0 / (ex[0] + ex[1] + ex[2] + ex[3])
    l8 = lax.broadcasted_iota(I32, (tm, 8), 1)
    te = jnp.zeros((tm, 8), I32)
    tg = jnp.zeros((tm, 8), F32)
    for kk in range(TOP_K):
        te = jnp.where(l8 == kk, idxs[kk], te)
        tg = jnp.where(l8 == kk, ex[kk] * inv, tg)
    te_ref[0] = te
    tg_ref[0] = tg


def _out_call(ya, yb, yc, yd, w_out, xa, xb, mod_l, nw1, nw2, wr, br, n_ctx_blk, blk0):
    b, _, d = xa.shape
    t = xa.shape[1] + (0 if xb is None else xb.shape[1])
    tm = ROW_TILE
    nblk = t // tm - blk0
    t_out = nblk * tm
    spec_a, spec_b, n_a_blk = _stream_specs(xa, xb, blk0)

    def mrow(bi, ti):
        return jnp.where(ti + blk0 < n_ctx_blk, b, bi)

    row3 = lambda bi, ti: (bi, ti + blk0, 0)
    out3 = lambda bi, ti: (bi, ti, 0)
    const = lambda bi, ti: (0, 0)
    modspec = lambda j: pl.BlockSpec((1, 1, d), lambda bi, ti: (mrow(bi, ti), 0, j))
    yspec = lambda y: pl.BlockSpec((1, tm, GROUP_W), row3 if y.shape[1] == t else out3)
    return pl.pallas_call(
        functools.partial(_out_kernel, n_a_blk=n_a_blk, blk0=blk0),
        grid=(b, nblk),
        in_specs=[yspec(ya), yspec(yb), yspec(yc), yspec(yd), pl.BlockSpec((d, d), const),
                  spec_a, spec_b, modspec(2), modspec(3), modspec(4),
                  pl.BlockSpec((1, d), const), pl.BlockSpec((1, d), const),
                  pl.BlockSpec((d, 128), const), pl.BlockSpec((1, 128), const)],
        out_specs=[pl.BlockSpec((1, tm, d), out3),
                   pl.BlockSpec((tm * d // 128, 128), lambda bi, ti: (bi * nblk + ti, 0)),
                   pl.BlockSpec((1, tm, 8), out3), pl.BlockSpec((1, tm, 8), out3)],
        out_shape=[jax.ShapeDtypeStruct((b, t_out, d), F32),
                   jax.ShapeDtypeStruct((b * t_out * d // 128, 128), F32),
                   jax.ShapeDtypeStruct((b, t_out, 8), I32),
                   jax.ShapeDtypeStruct((b, t_out, 8), F32)],
        compiler_params=_params(("parallel", "parallel"), VMEM_LIMIT),
        name="out_proj_router",
    )(ya, yb, yc, yd, w_out, xa, xa if xb is None else xb, mod_l, mod_l, mod_l, nw1, nw2, wr, br)


def _rank_kernel(te_ref, rank_ref, cnt_ref, carry_ref):
    i = pl.program_id(0)

    @pl.when(i == 0)
    def _():
        carry_ref[...] = jnp.zeros_like(carry_ref)

    te = te_ref[...]
    tb = te.shape[0]
    lane = lax.broadcasted_iota(I32, (tb, 128), 1)
    l8 = lax.broadcasted_iota(I32, (tb, 8), 1)
    below = (lax.broadcasted_iota(I32, (tb, tb), 0)
             > lax.broadcasted_iota(I32, (tb, tb), 1))
    lstrict = jnp.where(below, 1.0, 0.0).astype(BF16)
    base = carry_ref[...]
    out = jnp.zeros((tb, 8), I32)
    for k in range(TOP_K):
        oh = jnp.where(lane == te[:, k:k + 1], 1.0, 0.0)
        before = _dotf(lstrict, oh.astype(BF16)) + base
        rank_k = jnp.sum(oh * before, axis=-1, keepdims=True)
        out = jnp.where(l8 == k, rank_k.astype(I32), out)
        base = base + jnp.sum(oh, axis=0, keepdims=True)
    rank_ref[...] = out
    carry_ref[...] = base
    cnt_ref[...] = base.astype(I32)


def _rank_call(te):
    n_tok = te.shape[0]
    tb = RANK_TILE
    return pl.pallas_call(
        _rank_kernel,
        grid=(n_tok // tb,),
        in_specs=[pl.BlockSpec((tb, 8), lambda i: (i, 0))],
        out_specs=[pl.BlockSpec((tb, 8), lambda i: (i, 0)), pl.BlockSpec((1, 128), lambda i: (0, 0))],
        out_shape=[jax.ShapeDtypeStruct((n_tok, 8), I32), jax.ShapeDtypeStruct((1, 128), I32)],
        scratch_shapes=[pltpu.VMEM((1, 128), F32)],
        compiler_params=_params(("arbitrary",)),
        name="moe_rank",
    )(te)


def _slot_rows(tbl_ref, te_ref, rk_ref, r0, nt):
    idx = [(r0 + u) * 8 + k for u in range(DMA_UNROLL) for k in range(TOP_K)]
    experts = [te_ref[0, 0, j] for j in idx]
    ranks = [rk_ref[0, 0, j] for j in idx]
    return [pl.multiple_of((tbl_ref[e] + r) * nt, nt) for e, r in zip(experts, ranks)]


def _dispatch_kernel(tbl_ref, te_ref, rk_ref, f_ref, xs_hbm, zbuf, sem, zsem, *, tb, tm, nt):
    i = pl.program_id(0)

    @pl.when(i == 0)
    def _():
        zbuf[...] = jnp.zeros_like(zbuf)
        fills = [pltpu.make_async_copy(
            zbuf, xs_hbm.at[pl.ds(pl.multiple_of(tbl_ref[N_EXPERTS + e] * nt, nt), tm * nt)], zsem)
            for e in range(N_EXPERTS)]
        for fill in fills:
            fill.start()
        for fill in fills:
            fill.wait()

        def fill_unused(blk, carry):
            tail = pltpu.make_async_copy(
                zbuf, xs_hbm.at[pl.ds(pl.multiple_of(blk * (tm * nt), tm * nt), tm * nt)], zsem)
            tail.start()
            tail.wait()
            return carry
        lax.fori_loop(tbl_ref[2 * N_EXPERTS], xs_hbm.shape[0] // (tm * nt), fill_unused, 0)

    def body(g, carry):
        r0 = g * DMA_UNROLL
        dsts = _slot_rows(tbl_ref, te_ref, rk_ref, r0, nt)
        for u in range(DMA_UNROLL):
            src = f_ref.at[pl.ds(pl.multiple_of((r0 + u) * nt, nt), nt)]
            for k in range(TOP_K):
                pltpu.make_async_copy(src, xs_hbm.at[pl.ds(dsts[u * TOP_K + k], nt)],
                                      sem).start(priority=k % 2)
        return carry

    lax.fori_loop(0, tb // DMA_UNROLL, body, 0)
    for k in range(TOP_K):
        pltpu.make_async_copy(f_ref, xs_hbm.at[pl.ds(0, tb * nt)], sem).wait()


def _dispatch_call(tbl, te3, rk3, f_tiles, n_slots, tm):
    n_blk, _, per = te3.shape
    tb = per // 8
    nt = f_tiles.shape[0] // (n_blk * tb)
    grid_spec = pltpu.PrefetchScalarGridSpec(
        num_scalar_prefetch=1,
        grid=(n_blk,),
        in_specs=[pl.BlockSpec((1, 1, per), lambda i, tbl: (i, 0, 0), memory_space=pltpu.SMEM),
                  pl.BlockSpec((1, 1, per), lambda i, tbl: (i, 0, 0), memory_space=pltpu.SMEM),
                  pl.BlockSpec((tb * nt, 128), lambda i, tbl: (i, 0))],
        out_specs=pl.BlockSpec(memory_space=pl.ANY),
        scratch_shapes=[pltpu.VMEM((tm * nt, 128), F32), pltpu.SemaphoreType.DMA(()),
                        pltpu.SemaphoreType.DMA(())])
    return pl.pallas_call(
        functools.partial(_dispatch_kernel, tb=tb, tm=tm, nt=nt),
        grid_spec=grid_spec,
        out_shape=jax.ShapeDtypeStruct(((n_slots + tm) * nt, 128), F32),
        compiler_params=_params(("arbitrary",), VMEM_LIMIT),
        name="moe_dispatch",
    )(tbl, te3, rk3, f_tiles)


def _expert_kernel(be_ref, nu_ref, xs_ref, wgu_ref, bgu_ref, wd_ref, bd_ref, ys_ref,
                   wgu_s, wd_s, *, tm, d_ff):
    i = pl.program_id(0)
    nt = xs_ref.shape[0] // tm

    @pl.when(i < nu_ref[0])
    def _():
        @pl.when(jnp.logical_or(i == 0, be_ref[i] != be_ref[jnp.maximum(i - 1, 0)]))
        def _():
            for c in range(0, wgu_s.shape[0], 128):
                wgu_s[c:c + 128, :] = wgu_ref[c:c + 128, :].astype(BF16)
            for c in range(0, wd_s.shape[0], 128):
                wd_s[c:c + 128, :] = wd_ref[c:c + 128, :].astype(BF16)

        for r0 in range(0, tm, MOE_ROWS):
            x = jnp.concatenate(_load_token_tiles(xs_ref, MOE_ROWS, nt, base=r0 * nt),
                                axis=1).astype(BF16)
            gu = _dotf(x, wgu_s[...]) + bgu_ref[...]
            gate = jnp.minimum(gu[:, 0:d_ff], SWIGLU_LIMIT)
            up = jnp.clip(gu[:, d_ff:2 * d_ff], -SWIGLU_LIMIT, SWIGLU_LIMIT)
            act = (up + 1.0) * gate * _sigmoid(SWIGLU_ALPHA * gate)
            _store_token_tiles(ys_ref, _dotf(act.astype(BF16), wd_s[...]) + bd_ref[...],
                               base=r0 * nt)

    @pl.when(i >= nu_ref[0])
    def _():
        ys_ref[...] = jnp.zeros_like(ys_ref)


def _expert_call(block_expert, n_used, xs_tiles, wgu, bgu, wd, bd, layer, n_blocks, tm):
    d, two_ff = wgu.shape[2:]
    d_ff = two_ff // 2
    nt = d // 128
    ex = lambda i, be, nu: (layer, be[i], 0, 0)
    grid_spec = pltpu.PrefetchScalarGridSpec(
        num_scalar_prefetch=2,
        grid=(n_blocks,),
        in_specs=[pl.BlockSpec((tm * nt, 128), lambda i, be, nu: (jnp.minimum(i, nu[0] - 1), 0)),
                  pl.BlockSpec((None, None, d, two_ff), ex),
                  pl.BlockSpec((None, None, 1, two_ff), ex),
                  pl.BlockSpec((None, None, d_ff, d), ex),
                  pl.BlockSpec((None, None, 1, d), ex)],
        out_specs=pl.BlockSpec((tm * nt, 128), lambda i, be, nu: (i, 0)),
        scratch_shapes=[pltpu.VMEM((d, two_ff), BF16), pltpu.VMEM((d_ff, d), BF16)])
    return pl.pallas_call(
        functools.partial(_expert_kernel, tm=tm, d_ff=d_ff),
        grid_spec=grid_spec,
        out_shape=jax.ShapeDtypeStruct((n_blocks * tm * nt, 128), F32),
        compiler_params=_params(("arbitrary",), VMEM_LIMIT),
        name="moe_experts",
    )(block_expert, n_used, xs_tiles, wgu, bgu, wd, bd)


def _route(counts, n_tok, tm):
    n_blocks = -(-(n_tok * TOP_K + N_EXPERTS * (tm - 1)) // tm)
    padded = (counts + tm - 1) // tm * tm
    padded_end = jnp.cumsum(padded)
    group_start = padded_end - padded
    block_start = jnp.arange(n_blocks, dtype=I32) * tm
    block_expert = jnp.minimum(
        jnp.sum((padded_end[None, :] <= block_start[:, None]).astype(I32), axis=1), N_EXPERTS - 1)
    n_used = (padded_end[-1] // tm).astype(I32).reshape(1)
    table = jnp.concatenate([group_start, group_start + counts, n_used]).astype(I32)
    return table, block_expert, n_used, n_blocks


def _fin_kernel(tbl_ref, tec_ref, rkc_ref, ten_ref, rkn_ref, g_ref, x_ref, g2_ref, nw_ref, ys_hbm,
                out_ref, buf, sem, *, tb, nt):
    i = pl.program_id(0)
    s = i % 2

    def gather(te_ref, rk_ref, ss):
        def body(g, carry):
            r0 = g * DMA_UNROLL
            srcs = _slot_rows(tbl_ref, te_ref, rk_ref, r0, nt)
            for u in range(DMA_UNROLL):
                for k in range(TOP_K):
                    dst = pl.multiple_of((k * tb + r0 + u) * nt, nt)
                    pltpu.make_async_copy(ys_hbm.at[pl.ds(srcs[u * TOP_K + k], nt)],
                                          buf.at[ss, pl.ds(dst, nt)],
                                          sem.at[ss]).start(priority=k % 2)
            return carry
        lax.fori_loop(0, tb // DMA_UNROLL, body, 0)

    @pl.when(i == 0)
    def _():
        gather(tec_ref, rkc_ref, 0)

    @pl.when(i + 1 < pl.num_programs(0))
    def _():
        gather(ten_ref, rkn_ref, 1 - s)

    pltpu.make_async_copy(ys_hbm.at[pl.ds(0, TOP_K * tb * nt)], buf.at[s], sem.at[s]).wait()
    gates = g_ref[...]
    parts = []
    for j in range(nt):
        acc = None
        for k in range(TOP_K):
            v = buf[s, pl.ds(k * tb * nt + j, tb, stride=nt), :] * gates[:, k:k + 1]
            acc = v if acc is None else acc + v
        parts.append(acc)
    m = jnp.concatenate(parts, axis=1)
    out_ref[0] = x_ref[0] + g2_ref[0] * _rms(m, nw_ref[...])


def _fin_call(tbl, te3, rk3, gates, ys_tiles, xn, mod_l, nw, n_ctx_blk, blk0):
    b, t, d = xn.shape
    tb = ROW_TILE
    nblk = t // tb
    nt = d // 128
    n_blk = b * nblk

    def mrow(i):
        return jnp.where(i % nblk + blk0 < n_ctx_blk, b, i // nblk)

    smem = functools.partial(pl.BlockSpec, memory_space=pltpu.SMEM)
    cur = lambda i, tbl: (i, 0, 0)
    nxt = lambda i, tbl: (jnp.minimum(i + 1, n_blk - 1), 0, 0)
    grid_spec = pltpu.PrefetchScalarGridSpec(
        num_scalar_prefetch=1,
        grid=(n_blk,),
        in_specs=[smem((1, 1, tb * 8), cur), smem((1, 1, tb * 8), cur),
                  smem((1, 1, tb * 8), nxt), smem((1, 1, tb * 8), nxt),
                  pl.BlockSpec((tb, 8), lambda i, tbl: (i, 0)),
                  pl.BlockSpec((1, tb, d), lambda i, tbl: (i // nblk, i % nblk, 0)),
                  pl.BlockSpec((1, 1, d), lambda i, tbl: (mrow(i), 0, 5)),
                  pl.BlockSpec((1, d), lambda i, tbl: (0, 0)),
                  pl.BlockSpec(memory_space=pl.ANY)],
        out_specs=pl.BlockSpec((1, tb, d), lambda i, tbl: (i // nblk, i % nblk, 0)),
        scratch_shapes=[pltpu.VMEM((2, TOP_K * tb * nt, 128), F32), pltpu.SemaphoreType.DMA((2,))])
    return pl.pallas_call(
        functools.partial(_fin_kernel, tb=tb, nt=nt),
        grid_spec=grid_spec,
        out_shape=jax.ShapeDtypeStruct((b, t, d), F32),
        compiler_params=_params(("arbitrary",), VMEM_LIMIT),
        name="moe_combine",
    )(tbl, te3, rk3, te3, rk3, gates, xn, mod_l, nw, ys_tiles)


def _in_proj_columns():
    cols = np.full((IN_COLS,), -1, np.int64)
    for sec in range(2):
        for n in range(256):
            part, hm, j = n // 128, (n % 128) // 16, n % 16
            m, h = hm // 4, hm % 4
            cols[sec * 256 + n] = sec * 256 + h * 64 + m * 32 + part * 16 + j
    cols[512:768] = np.arange(512, 768)
    o, s = IN_DA, 768
    cols[o:o + 1024] = s + np.arange(1024)
    o, s = o + IN_ML, s + 1024
    cols[o:o + 16] = s + np.arange(16)
    o, s = o + IN_G, s + 16
    cols[o:o + MLA_Q_RANK] = s + np.arange(MLA_Q_RANK)
    cols[o + 256:o + 256 + MLA_KV_RANK] = s + MLA_Q_RANK + np.arange(MLA_KV_RANK)
    cols[o + 384 + 64:o + 384 + 96] = s + MLA_Q_RANK + MLA_KV_RANK + np.arange(MLA_ROPE)
    o, s = o + IN_MLA, s + MLA_Q_RANK + MLA_KV_RANK + MLA_ROPE
    cols[o:o + 256] = s + np.arange(256)
    return cols


def _take_cols(w, cols):
    valid = jnp.asarray(cols >= 0)
    return jnp.where(valid, jnp.take(w, jnp.asarray(np.maximum(cols, 0)), axis=-1), 0.0)


def _rope_tables(n_ctx, n_lat):
    pos = jnp.arange(n_lat)
    inv = ROPE_THETA ** (-jnp.arange(8, dtype=F32) / 8)
    ang = jnp.concatenate([(pos // GRID_W)[:, None] * inv, (pos % GRID_W)[:, None] * inv], axis=-1)
    cos = jnp.concatenate([jnp.ones((n_ctx, 16), F32), jnp.cos(ang)], axis=0)
    sin = jnp.concatenate([jnp.zeros((n_ctx, 16), F32), jnp.sin(ang)], axis=0)
    t = n_ctx + n_lat
    cos_da, sin_da = jnp.tile(cos, (1, 8)), jnp.tile(sin, (1, 8))
    one, zero = jnp.ones((t, 64), F32), jnp.zeros((t, 64), F32)
    z16, z32 = jnp.zeros((t, 16), F32), jnp.zeros((t, 32), F32)
    cm = jnp.concatenate([one, cos, cos, jnp.ones((t, 32), F32)], axis=1)
    sa = jnp.concatenate([zero, -sin, z16, z32], axis=1)
    sb = jnp.concatenate([zero, z16, sin, z32], axis=1)
    return cos_da, sin_da, cm, sa, sb


def _mla_weights(w_uq, w_ukv):
    hd = MLA_NOPE + MLA_ROPE
    wq = jnp.zeros((256, 512), F32)
    wkv = jnp.zeros((MLA_KV_RANK, 768), F32)
    for h in range(MLA_HEADS):
        wq = wq.at[:MLA_Q_RANK, 128 * h:128 * h + hd].set(w_uq[:, hd * h:hd * (h + 1)])
        wkv = wkv.at[:, 128 * h:128 * h + MLA_NOPE].set(w_ukv[:, 128 * h:128 * h + MLA_NOPE])
        wkv = wkv.at[:, 512 + 64 * h:512 + 64 * (h + 1)].set(w_ukv[:, 128 * h + MLA_NOPE:128 * (h + 1)])
    return wq.astype(BF16), wkv.astype(BF16)


def _s5_layout(a_re, a_im, log_step, b_re, b_im, c_re, c_im):
    ns = S5_NGROUPS * S5_STATE
    eye = jnp.eye(S5_NGROUPS, dtype=F32)
    are = a_re.reshape(2, 1, ns)
    aim = a_im.reshape(2, 1, ns)
    ls = jnp.repeat(log_step, S5_STATE, axis=-1).reshape(2, 1, ns)
    bd_b = lambda w: jnp.einsum("dgph,gk->dghkp", w, eye).reshape(2, GROUP_W, ns)
    bd_c = lambda w: jnp.einsum("dghp,gk->dgpkh", w, eye).reshape(2, ns, GROUP_W)
    return are, aim, ls, bd_b(b_re), bd_b(b_im), bd_c(c_re), bd_c(c_im)


def kernel(x, c, ctx, c_ctx, w_mod, b_mod, norm_w, w_in, w_out, da_lambda, da_subln, ml_conv_w,
           ml_conv_b, ml_gate_b, ml_norm, mla_q_norm, mla_w_uq, mla_kv_norm, mla_w_ukv, s5_a_re,
           s5_a_im, s5_log_step, s5_b_re, s5_b_im, s5_c_re, s5_c_im, s5_d, s5_w_glu, s5_b_glu,
           moe_w_router, moe_b_router, moe_w_gate_up, moe_b_gate_up, moe_w_down, moe_b_down):
    bsz, n_lat, d = x.shape
    n_ctx = ctx.shape[1]
    t = n_ctx + n_lat
    depth = w_mod.shape[0]
    tm = ROW_TILE
    assert n_ctx % tm == 0 and n_lat % tm == 0 and bsz % 8 == 0 and bsz < 16
    assert n_ctx % ML_CHUNK == 0 and n_ctx % S5_CHUNK == 0
    n_ctx_blk = n_ctx // tm

    cc = jnp.zeros((16, d), F32).at[:bsz].set(c).at[bsz].set(c_ctx)
    mod = _mod_call(cc, w_mod, b_mod)
    cos_da, sin_da, cm, sa, sb = _rope_tables(n_ctx, n_lat)
    in_cols = _in_proj_columns()
    xa, xb = ctx, x

    for l in range(depth):
        last = l == depth - 1
        lambda_init = 0.8 - 0.6 * math.exp(-0.3 * l)
        mod_l = mod[l].reshape(16, 1, 6 * d)
        w_in_p = _take_cols(w_in[l], in_cols).astype(BF16)
        p_da, p_ml, p_g, p_mla, u_t = _in_call(xa, xb, mod_l, norm_w[l, 0].reshape(1, d), w_in_p,
                                               cos_da, sin_da, n_ctx_blk)

        da_extra = [da_lambda[l], jnp.tile(da_subln[l], 4).reshape(1, GROUP_W)]
        da_kern = functools.partial(_da_kernel, lambda_init=lambda_init)
        da_kw = dict(q_col=0, k_col=1, v_col=2, q_w=GROUP_W)
        ya = _attn_call(da_kern, "diff_attn", p_da, p_da, p_da, da_extra, q_blk0=n_ctx_blk,
                        n_q_blk=n_lat // tm, n_keys=t, **da_kw)
        wq, wkv = _mla_weights(mla_w_uq[l], mla_w_ukv[l])
        qn = jnp.zeros((1, 256), F32).at[0, :MLA_Q_RANK].set(mla_q_norm[l])
        q_mla, k_mla, v_mla = _mla_prep_call(p_mla, qn, wq, mla_kv_norm[l].reshape(1, -1), wkv,
                                             cm, sa, sb)
        mla_kw = dict(q_col=0, k_col=0, v_col=0, q_w=512)
        yc = _attn_call(_mla_attn_kernel, "mla_attn", q_mla, k_mla, v_mla, [], q_blk0=n_ctx_blk,
                        n_q_blk=n_lat // tm, n_keys=t, **mla_kw)
        if not last:
            ya_c = _attn_call(da_kern, "diff_attn_ctx", p_da, p_da, p_da, da_extra, q_blk0=0,
                              n_q_blk=n_ctx_blk, n_keys=n_ctx, **da_kw)
            yc_c = _attn_call(_mla_attn_kernel, "mla_attn_ctx", q_mla, k_mla, v_mla, [], q_blk0=0,
                              n_q_blk=n_ctx_blk, n_keys=n_ctx, **mla_kw)
            ya = jnp.concatenate([ya_c, ya], axis=1)
            yc = jnp.concatenate([yc_c, yc], axis=1)

        gb = jnp.zeros((1, 128), F32).at[0, :16].set(ml_gate_b[l])
        yb = _ml_call(p_ml, p_g, ml_conv_w[l], ml_conv_b[l].reshape(1, -1), gb,
                      ml_norm[l].reshape(1, -1), n_ctx)

        s5p = _s5_layout(s5_a_re[l], s5_a_im[l], s5_log_step[l], s5_b_re[l], s5_b_im[l],
                         s5_c_re[l], s5_c_im[l])
        ys = _s5_call(u_t.reshape(t * bsz, GROUP_W), *s5p, n_ctx, bsz)
        yd = _glu_call(ys, u_t, s5_d[l].reshape(1, -1),
                       s5_w_glu[l].astype(BF16), s5_b_glu[l].reshape(1, -1), bsz)

        blk0 = n_ctx_blk if last else 0
        wr = jnp.zeros((d, 128), F32).at[:, :N_EXPERTS].set(moe_w_router[l])
        br = jnp.full((1, 128), -1e30, F32).at[0, :N_EXPERTS].set(moe_b_router[l])
        xn, f, te, tg = _out_call(ya, yb, yc, yd, w_out[l].astype(BF16), xa, xb, mod_l,
                                  norm_w[l, 1].reshape(1, d), norm_w[l, 2].reshape(1, d), wr, br,
                                  n_ctx_blk, blk0)

        t_moe = t - blk0 * tm
        n_tok = bsz * t_moe
        te = te.reshape(n_tok, 8)
        rank, cnt = _rank_call(te)
        tbl, be, nu, n_blocks = _route(cnt[0, :N_EXPERTS], n_tok, MOE_TILE)
        te3 = te.reshape(n_tok // tm, 1, tm * 8)
        rk3 = rank.reshape(n_tok // tm, 1, tm * 8)
        xs_tiles = _dispatch_call(tbl, te3, rk3, f, n_blocks * MOE_TILE, MOE_TILE)
        ys_tiles = _expert_call(be, nu, xs_tiles, moe_w_gate_up,
                                moe_b_gate_up.reshape(depth, N_EXPERTS, 1, -1), moe_w_down,
                                moe_b_down.reshape(depth, N_EXPERTS, 1, -1), l, n_blocks, MOE_TILE)
        xa, xb = _fin_call(tbl, te3, rk3, tg.reshape(n_tok, 8), ys_tiles, xn, mod_l,
                           norm_w[l, 3].reshape(1, d), n_ctx_blk, blk0), None
    return xa
```

```python
import functools
import math

import numpy as np
import jax
import jax.numpy as jnp
from jax import lax
from jax.experimental import pallas as pl
from jax.experimental.pallas import tpu as pltpu

F32, BF16, I32 = jnp.float32, jnp.bfloat16, jnp.int32
NORM_EPS = 1e-6
GRID_W = 64
ROPE_THETA = 10000.0
GROUP_W = 256
DA_QK = 32
ML_CHUNK = 128
MLA_HEADS, MLA_NOPE, MLA_ROPE, MLA_Q_RANK, MLA_KV_RANK = 4, 64, 32, 192, 128
S5_NGROUPS, S5_GROUP, S5_STATE = 16, 16, 64
N_EXPERTS, TOP_K = 32, 4
SWIGLU_ALPHA, SWIGLU_LIMIT = 1.702, 7.0
NEG_INF = float("-inf")

ROW_TILE = 256
MOE_TILE = 256
MOE_ROWS = 256
S5_CHUNK = 64
RANK_TILE = 512
DMA_UNROLL = 8
VMEM_LIMIT = 56 * 1024 * 1024

IN_DA, IN_ML, IN_G, IN_MLA, IN_S5 = 768, 1024, 128, 512, 256
IN_COLS = IN_DA + IN_ML + IN_G + IN_MLA + IN_S5


def _params(sem, vmem=None):
    return pltpu.CompilerParams(dimension_semantics=sem, vmem_limit_bytes=vmem)


def _dotf(a, b):
    return jnp.dot(a, b, preferred_element_type=F32)


def _dot_nt(a, b):
    return lax.dot_general(a, b, (((1,), (1,)), ((), ())), preferred_element_type=F32)


def _split2(a):
    hi = a.astype(BF16)
    lo = (a - hi.astype(F32)).astype(BF16)
    return hi, lo


def _dot3(a, b):
    ah, al = _split2(a)
    bh, bl = _split2(b)
    return _dotf(ah, bh) + _dotf(ah, bl) + _dotf(al, bh)


def _dot_exact_rhs(a, rhs_b):
    a1 = a.astype(BF16)
    r1 = a - a1.astype(F32)
    a2 = r1.astype(BF16)
    a3 = (r1 - a2.astype(F32)).astype(BF16)
    return _dotf(a1, rhs_b) + _dotf(a2, rhs_b) + _dotf(a3, rhs_b)


def _rms(x, w):
    ms = jnp.mean(x * x, axis=-1, keepdims=True)
    return x * lax.rsqrt(ms + NORM_EPS) * w


def _head_rms(a, width):
    n = a.shape[-1]
    sh = int(math.log2(width))
    r = lax.broadcasted_iota(I32, (n, n), 0) >> sh
    c = lax.broadcasted_iota(I32, (n, n), 1) >> sh
    g = jnp.where(r == c, 1.0 / width, 0.0).astype(BF16)
    hi, lo = _split2(a * a)
    ms = _dotf(hi, g) + _dotf(lo, g)
    return a * lax.rsqrt(ms + NORM_EPS)


def _store_token_tiles(ref, val, base=0):
    tm, d = val.shape
    nt = d // 128
    for j in range(nt):
        ref[pl.ds(base + j, tm, stride=nt), :] = val[:, 128 * j:128 * (j + 1)]


def _load_token_tiles(ref, tm, nt, base=0, lead=()):
    parts = [ref[lead + (pl.ds(base + j, tm, stride=nt), slice(None))] for j in range(nt)]
    return parts


def _sigmoid(x):
    return jax.nn.sigmoid(x)


def _log_sigmoid(x):
    return jnp.minimum(x, 0.0) - jnp.log(1.0 + jnp.exp(-jnp.abs(x)))


def _mod_kernel(c_ref, w_ref, b_ref, o_ref):
    c = c_ref[...]
    o_ref[0] = _dot3(c * _sigmoid(c), w_ref[0]) + b_ref[0]


def _mod_call(cc, w_mod, b_mod):
    n_layers, d, n = w_mod.shape
    tn = 1536
    return pl.pallas_call(
        _mod_kernel,
        grid=(n_layers, n // tn),
        in_specs=[pl.BlockSpec((16, d), lambda l, j: (0, 0)),
                  pl.BlockSpec((1, d, tn), lambda l, j: (l, 0, j)),
                  pl.BlockSpec((1, 1, tn), lambda l, j: (l, 0, j))],
        out_specs=pl.BlockSpec((1, 16, tn), lambda l, j: (l, 0, j)),
        out_shape=jax.ShapeDtypeStruct((n_layers, 16, n), F32),
        compiler_params=_params(("parallel", "parallel"), VMEM_LIMIT),
        name="mod_vectors",
    )(cc, w_mod, b_mod.reshape(n_layers, 1, n))


def _in_kernel(xa_ref, xb_ref, sh_ref, sc_ref, nw_ref, w_ref, c_ref, s_ref,
               da_ref, ml_ref, g_ref, mla_ref, s5_ref, *, qscale, n_a_blk):
    x = jnp.where(pl.program_id(1) < n_a_blk, xa_ref[0], xb_ref[0])
    h = _rms(x, nw_ref[...]) * (1.0 + sc_ref[0]) + sh_ref[0]
    hb = h.astype(BF16)
    da = _dotf(hb, w_ref[:, 0:IN_DA])
    c = c_ref[...]
    s = s_ref[...]
    q1, q2, k1, k2 = da[:, 0:128], da[:, 128:256], da[:, 256:384], da[:, 384:512]
    da_ref[0, :, 0:128] = ((q1 * c - q2 * s) * qscale).astype(BF16)
    da_ref[0, :, 128:256] = ((q2 * c + q1 * s) * qscale).astype(BF16)
    da_ref[0, :, 256:384] = (k1 * c - k2 * s).astype(BF16)
    da_ref[0, :, 384:512] = (k2 * c + k1 * s).astype(BF16)
    da_ref[0, :, 512:768] = da[:, 512:768].astype(BF16)
    o = IN_DA
    ml_ref[0] = _dotf(hb, w_ref[:, o:o + IN_ML])
    o += IN_ML
    g_ref[0] = _dotf(hb, w_ref[:, o:o + IN_G])
    o += IN_G
    mla_ref[0] = _dotf(hb, w_ref[:, o:o + IN_MLA])
    o += IN_MLA
    s5_ref[...] = _dotf(hb, w_ref[:, o:o + IN_S5])


def _stream_specs(xa, xb, blk0):
    tm = ROW_TILE
    d = xa.shape[2]
    n_a = xa.shape[1] // tm
    spec_a = pl.BlockSpec((1, tm, d), lambda bi, ti: (bi, jnp.minimum(ti + blk0, n_a - 1), 0))
    spec_b = pl.BlockSpec((1, tm, d), lambda bi, ti: (bi, jnp.maximum(ti + blk0 - n_a, 0), 0))
    return spec_a, spec_b, n_a


def _in_call(xa, xb, mod_l, nw, w_in_p, cos_da, sin_da, n_ctx_blk):
    b, _, d = xa.shape
    t = xa.shape[1] + (0 if xb is None else xb.shape[1])
    tm = ROW_TILE
    spec_a, spec_b, n_a_blk = _stream_specs(xa, xb, 0)

    def mrow(bi, ti):
        return jnp.where(ti < n_ctx_blk, b, bi)

    row3 = lambda bi, ti: (bi, ti, 0)
    return pl.pallas_call(
        functools.partial(_in_kernel, qscale=DA_QK ** -0.5 * LOG2_E, n_a_blk=n_a_blk),
        grid=(b, t // tm),
        in_specs=[spec_a, spec_b,
                  pl.BlockSpec((1, 1, d), lambda bi, ti: (mrow(bi, ti), 0, 0)),
                  pl.BlockSpec((1, 1, d), lambda bi, ti: (mrow(bi, ti), 0, 1)),
                  pl.BlockSpec((1, d), lambda bi, ti: (0, 0)),
                  pl.BlockSpec((d, IN_COLS), lambda bi, ti: (0, 0)),
                  pl.BlockSpec((tm, 128), lambda bi, ti: (ti, 0)),
                  pl.BlockSpec((tm, 128), lambda bi, ti: (ti, 0))],
        out_specs=[pl.BlockSpec((1, tm, IN_DA), row3),
                   pl.BlockSpec((1, tm, IN_ML), row3),
                   pl.BlockSpec((1, tm, IN_G), row3),
                   pl.BlockSpec((1, tm, IN_MLA), row3),
                   pl.BlockSpec((tm, IN_S5), lambda bi, ti: (ti, bi))],
        out_shape=[jax.ShapeDtypeStruct((b, t, IN_DA), BF16),
                   jax.ShapeDtypeStruct((b, t, IN_ML), F32),
                   jax.ShapeDtypeStruct((b, t, IN_G), F32),
                   jax.ShapeDtypeStruct((b, t, IN_MLA), F32),
                   jax.ShapeDtypeStruct((t, b * IN_S5), F32)],
        compiler_params=_params(("parallel", "parallel"), VMEM_LIMIT),
        name="in_proj",
    )(xa, xa if xb is None else xb, mod_l, mod_l, nw, w_in_p, cos_da, sin_da)


LOG2_E = 1.0 / math.log(2.0)


def _softmax_rows(s):
    mx = jnp.max(s, axis=-1, keepdims=True)
    p = jnp.exp2(s - mx)
    return p, jnp.sum(p, axis=-1, keepdims=True)


def _da_kernel(q_ref, k_ref, v_ref, lam_ref, sub_ref, o_ref, *, lambda_init):
    q = q_ref[0]
    k = k_ref[0]
    v = v_ref[0]
    lp = lam_ref[...]
    lam = (jnp.exp(jnp.sum(lp[0:1] * lp[1:2], axis=-1, keepdims=True))
           - jnp.exp(jnp.sum(lp[2:3] * lp[3:4], axis=-1, keepdims=True)) + lambda_init)
    lane = lax.broadcasted_iota(I32, (1, GROUP_W), 1)
    grp = (lane & 127) >> 4
    head = lane >> 6
    acc = jnp.zeros((q.shape[0], GROUP_W), F32)
    for h in range(4):
        ps, rs = [], []
        for m in range(2):
            qm = jnp.where(grp == m * 4 + h, q, jnp.zeros_like(q))
            p, l = _softmax_rows(_dot_nt(qm, k))
            ps.append(p.astype(BF16))
            rs.append(((1.0 if m == 0 else lam) / l).astype(BF16))
        w = ps[0] * rs[0] - ps[1] * rs[1]
        vm = jnp.where(head == h, v, jnp.zeros_like(v))
        acc = acc + _dotf(w, vm)
    y = _head_rms(acc, 64) * sub_ref[...] * (1.0 - lambda_init)
    o_ref[0] = y.astype(BF16)


def _mla_attn_kernel(q_ref, k_ref, v_ref, o_ref):
    q = q_ref[0]
    k = k_ref[0]
    v = v_ref[0]
    head = lax.broadcasted_iota(I32, (1, GROUP_W), 1) >> 6
    acc = jnp.zeros((q.shape[0], GROUP_W), F32)
    for h in range(MLA_HEADS):
        sl = slice(128 * h, 128 * (h + 1))
        p, l = _softmax_rows(_dot_nt(q[:, sl], k[:, sl]))
        vm = jnp.where(head == h, v, jnp.zeros_like(v))
        acc = acc + _dotf(p.astype(BF16), vm) * (1.0 / l)
    o_ref[0] = acc.astype(BF16)


def _attn_call(kernel, name, q_arr, k_arr, v_arr, extra, *, q_blk0, n_q_blk, n_keys,
               q_col, k_col, v_col, q_w):
    b = q_arr.shape[0]
    tq = ROW_TILE
    in_specs = [pl.BlockSpec((1, tq, q_w), lambda bi, qi: (bi, qi + q_blk0, q_col)),
                pl.BlockSpec((1, n_keys, q_w), lambda bi, qi: (bi, 0, k_col)),
                pl.BlockSpec((1, n_keys, GROUP_W), lambda bi, qi: (bi, 0, v_col))]
    in_specs += [pl.BlockSpec(e.shape, lambda bi, qi: (0, 0)) for e in extra]
    return pl.pallas_call(
        kernel,
        grid=(b, n_q_blk),
        in_specs=in_specs,
        out_specs=pl.BlockSpec((1, tq, GROUP_W), lambda bi, qi: (bi, qi, 0)),
        out_shape=jax.ShapeDtypeStruct((b, n_q_blk * tq, GROUP_W), BF16),
        compiler_params=_params(("parallel", "arbitrary"), VMEM_LIMIT),
        name=name,
    )(q_arr, k_arr, v_arr, *extra)


def _mla_prep_kernel(p_ref, qn_ref, wq_ref, kvn_ref, wkv_ref, c_ref, sa_ref, sb_ref,
                     q_ref, k_ref, v_ref, *, scale):
    p = p_ref[0]
    cq, ckv, kr = p[:, 0:256], p[:, 256:384], p[:, 384:512]
    msq = jnp.sum(cq * cq, axis=-1, keepdims=True) * (1.0 / MLA_Q_RANK)
    qn = (cq * lax.rsqrt(msq + NORM_EPS) * qn_ref[...]).astype(BF16)
    q = _dotf(qn, wq_ref[...])
    c = c_ref[...]
    sa = sa_ref[...]
    sb = sb_ref[...]

    def rope(a):
        return a * c + pltpu.roll(a, 112, 1) * sa + pltpu.roll(a, 16, 1) * sb

    for h in range(MLA_HEADS):
        sl = slice(128 * h, 128 * (h + 1))
        q_ref[0, :, sl] = (rope(q[:, sl]) * scale).astype(BF16)
    kvn = (_rms(ckv, kvn_ref[...])).astype(BF16)
    kv = _dotf(kvn, wkv_ref[...])
    krr = rope(kr)
    for h in range(MLA_HEADS):
        sl = slice(128 * h, 128 * (h + 1))
        k_ref[0, :, sl] = (kv[:, sl] + krr).astype(BF16)
    v_ref[0] = kv[:, 512:768].astype(BF16)


def _mla_prep_call(p_mla, qn, wq, kvn, wkv, cm, sa, sb):
    b, t, _ = p_mla.shape
    tm = ROW_TILE
    row3 = lambda bi, ti: (bi, ti, 0)
    const = lambda bi, ti: (0, 0)
    tab = lambda bi, ti: (ti, 0)
    return pl.pallas_call(
        functools.partial(_mla_prep_kernel, scale=(MLA_NOPE + MLA_ROPE) ** -0.5 * LOG2_E),
        grid=(b, t // tm),
        in_specs=[pl.BlockSpec((1, tm, IN_MLA), row3),
                  pl.BlockSpec(qn.shape, const), pl.BlockSpec(wq.shape, const),
                  pl.BlockSpec(kvn.shape, const), pl.BlockSpec(wkv.shape, const),
                  pl.BlockSpec((tm, 128), tab), pl.BlockSpec((tm, 128), tab),
                  pl.BlockSpec((tm, 128), tab)],
        out_specs=[pl.BlockSpec((1, tm, 512), row3), pl.BlockSpec((1, tm, 512), row3),
                   pl.BlockSpec((1, tm, GROUP_W), row3)],
        out_shape=[jax.ShapeDtypeStruct((b, t, 512), BF16),
                   jax.ShapeDtypeStruct((b, t, 512), BF16),
                   jax.ShapeDtypeStruct((b, t, GROUP_W), BF16)],
        compiler_params=_params(("parallel", "parallel"), VMEM_LIMIT),
        name="mla_prep",
    )(p_mla, qn, wq, kvn, wkv, cm, sa, sb)


def _ml_kernel(p_ref, g_ref, cw_ref, cb_ref, gb_ref, nw_ref, y_ref,
               qt_s, k_s, vt_s, ot_s, gt_s, hf_s, c_s, m_s, *, n_ctx_chunks, n_chunks):
    cl = ML_CHUNK
    t_total = n_chunks * cl
    row = lax.broadcasted_iota(I32, (cl, 1), 0)
    si = lax.broadcasted_iota(I32, (cl, cl), 0)
    ti = lax.broadcasted_iota(I32, (cl, cl), 1)
    lane128 = lax.broadcasted_iota(I32, (1, 128), 1)
    lane256 = lax.broadcasted_iota(I32, (1, 256), 1)
    row128 = lax.broadcasted_iota(I32, (128, 1), 0)
    hmask = [(lane256 >> 6) == h for h in range(4)]
    is_f = jnp.logical_and(((lane128 >> 2) & 1) == 1, lane128 < 16)
    r_nd = lax.broadcasted_iota(I32, (384, 512), 0)
    c_nd = lax.broadcasted_iota(I32, (384, 512), 1) >> 7
    nd_head = jnp.where(r_nd < 256, r_nd >> 6, r_nd - 256)
    nd_mask = nd_head == c_nd
    nd_ones = jnp.where(jnp.logical_and(nd_mask, r_nd >= 256), 1.0, 0.0).astype(BF16)
    r_st = lax.broadcasted_iota(I32, (384, 256), 0)
    st_head = jnp.where(r_st < 256, r_st >> 6, r_st - 256)
    st_mask = st_head == (lax.broadcasted_iota(I32, (384, 256), 1) >> 6)
    st_rowhead = jnp.where(lax.broadcasted_iota(I32, (384, 1), 0) < 256,
                           lax.broadcasted_iota(I32, (384, 1), 0) >> 6,
                           lax.broadcasted_iota(I32, (384, 1), 0) - 256)
    w0, w1, w2 = cw_ref[0:1], cw_ref[1:2], cw_ref[2:3]
    cb = cb_ref[...]
    gb = gb_ref[...]

    def prep_body(c, carry):
        s0 = pl.multiple_of(c * cl, cl)
        x = p_ref[0, pl.ds(s0, cl), 0:512]
        sp = pl.multiple_of(jnp.maximum(s0 - 8, 0), 8)
        sn = pl.multiple_of(jnp.minimum(s0 + cl, t_total - 8), 8)
        has_prev = jnp.logical_and(c != 0, c != n_ctx_chunks).astype(F32)
        has_next = jnp.logical_and(c != n_ctx_chunks - 1, c != n_chunks - 1).astype(F32)
        prev_row = p_ref[0, pl.ds(sp, 8), 0:512][7:8] * has_prev
        next_row = p_ref[0, pl.ds(sn, 8), 0:512][0:1] * has_next
        xp = jnp.where(row == 0, prev_row, pltpu.roll(x, 1, 0))
        xn = jnp.where(row == cl - 1, next_row, pltpu.roll(x, cl - 1, 0))
        z = xp * w0 + x * w1 + xn * w2 + cb
        qk = z * _sigmoid(z)
        qt_s[c] = qk[:, 0:256].T.astype(BF16)
        k_s[c] = (qk[:, 256:512] * (64 ** -0.5)).astype(BF16)
        vt_s[c] = p_ref[0, pl.ds(s0, cl), 512:768].T
        ot_s[c] = p_ref[0, pl.ds(s0, cl), 768:1024].T
        g = g_ref[0, pl.ds(s0, cl), :] + gb
        gt_s[c] = jnp.where(is_f, _log_sigmoid(g), g).T[0:16]
        return carry

    def rows_to_blocks(rows, n):
        return jnp.concatenate([jnp.broadcast_to(r, (n, 128)) for r in rows], axis=0)

    def head_rows(rows):
        out = jnp.zeros((128, 128), F32)
        for h in range(4):
            out = jnp.where(row128 == h, rows[h], out)
        return out

    def chunk(c, reverse):
        g_t = gt_s[c]
        tri = (si >= ti) if reverse else (si <= ti)
        bc_t = _dot_exact_rhs(g_t, jnp.where(tri, 1.0, 0.0).astype(BF16))
        qt = qt_s[c]
        kb = k_s[c]
        m_all = m_s[...]
        off = 8 if reverse else 0
        a_rows, r_rows, tots, m_old = [], [], [], []
        for h in range(4):
            il, fl = off + h, off + 4 + h
            a_rows.append(bc_t[fl:fl + 1, :])
            r_rows.append(g_t[il:il + 1, :] - bc_t[fl:fl + 1, :])
            tots.append(bc_t[fl:fl + 1, 0:1] if reverse else bc_t[fl:fl + 1, cl - 1:cl])
            m_old.append(m_all[:, h:h + 1])
        r_all = jnp.concatenate(r_rows, axis=1)
        d_t = jnp.broadcast_to(r_all, (cl, 4 * cl)).T + rows_to_blocks(a_rows, cl)
        lw = jnp.where(jnp.concatenate([tri] * 4, axis=0), d_t, NEG_INF)
        mt, wi, emt = [], [], []
        for h in range(4):
            linter = a_rows[h] + m_old[h]
            mt_h = jnp.maximum(linter, jnp.max(lw[cl * h:cl * (h + 1)], axis=0, keepdims=True))
            mt.append(mt_h)
            wi.append(jnp.exp(linter - mt_h))
            emt.append(jnp.exp(-mt_h))
        kstack = jnp.concatenate([jnp.where(hmask[h], kb, jnp.zeros_like(kb)) for h in range(4)],
                                 axis=0)
        s_t = _dotf(kstack, qt)
        w_t = (s_t * jnp.exp(lw - rows_to_blocks(mt, cl))).astype(BF16)
        vt = vt_s[c]
        vt4 = jnp.concatenate([vt.astype(BF16)] * 4, axis=1)
        lhs_nd = jnp.concatenate([vt4, jnp.zeros((128, 512), BF16)], axis=0)
        lhs_nd = jnp.where(nd_mask, lhs_nd, jnp.zeros_like(lhs_nd)) + nd_ones
        c_aug = c_s[...]
        nd = _dotf(lhs_nd, w_t) + jnp.concatenate(
            [rows_to_blocks(wi, 64), head_rows(wi)], axis=0) * _dotf(c_aug.astype(BF16), qt)
        den = [nd[256 + h:257 + h, :] for h in range(4)]
        h_t = nd[0:256] / jnp.maximum(jnp.abs(rows_to_blocks(den, 64)), rows_to_blocks(emt, 64))
        wupd, decs, m_new_all = [], [], m_all
        for h in range(4):
            lupd = tots[h] + r_rows[h]
            m_new = jnp.maximum(tots[h] + m_old[h], jnp.max(lupd, axis=-1, keepdims=True))
            wupd.append(jnp.exp(lupd - m_new))
            decs.append(jnp.exp(tots[h] + m_old[h] - m_new))
            m_new_all = jnp.where(lane128 == h, m_new, m_new_all)
        lhs_u = jnp.concatenate([vt * rows_to_blocks(wupd, 64), head_rows(wupd)], axis=0)
        upd = _dotf(lhs_u.astype(BF16), kb)
        dec_col = jnp.zeros((384, 1), F32)
        for h in range(4):
            dec_col = jnp.where(st_rowhead == h, decs[h], dec_col)
        c_s[...] = dec_col * c_aug + jnp.where(st_mask, upd, 0.0)
        m_s[...] = m_new_all
        return h_t

    def fwd_body(i, carry):
        hf_s[i] = chunk(i, False)
        return carry

    def bwd_body(i, carry):
        c = jnp.where(i < n_ctx_chunks, n_ctx_chunks - 1 - i, n_chunks - 1 - (i - n_ctx_chunks))
        gated = _sigmoid(ot_s[c]) * (hf_s[c] + chunk(c, True))
        ms = [jnp.mean(jnp.square(gated[64 * h:64 * (h + 1)]), axis=0, keepdims=True)
              for h in range(4)]
        y_t = gated * lax.rsqrt(rows_to_blocks(ms, 64) + NORM_EPS) * nw_ref[...]
        y_ref[0, pl.ds(pl.multiple_of(c * cl, cl), cl), :] = y_t.T.astype(BF16)
        return carry

    lax.fori_loop(0, n_chunks, prep_body, 0)
    c_s[...] = jnp.zeros_like(c_s)
    m_s[...] = jnp.zeros_like(m_s)
    lax.fori_loop(0, n_chunks, fwd_body, 0)
    c_s[...] = jnp.zeros_like(c_s)
    m_s[...] = jnp.zeros_like(m_s)
    lax.fori_loop(0, n_chunks, bwd_body, 0)


def _ml_call(p_ml, p_g, cw, cb, gb, nw, n_ctx):
    b, t, _ = p_ml.shape
    nc = t // ML_CHUNK
    const = lambda bi: (0, 0)
    nw_col = jnp.broadcast_to(nw.reshape(GROUP_W, 1), (GROUP_W, 128))
    return pl.pallas_call(
        functools.partial(_ml_kernel, n_ctx_chunks=n_ctx // ML_CHUNK, n_chunks=nc),
        grid=(b,),
        in_specs=[pl.BlockSpec((1, t, IN_ML), lambda bi: (bi, 0, 0)),
                  pl.BlockSpec((1, t, IN_G), lambda bi: (bi, 0, 0)),
                  pl.BlockSpec(cw.shape, const), pl.BlockSpec(cb.shape, const),
                  pl.BlockSpec(gb.shape, const), pl.BlockSpec(nw_col.shape, const)],
        out_specs=pl.BlockSpec((1, t, GROUP_W), lambda bi: (bi, 0, 0)),
        out_shape=jax.ShapeDtypeStruct((b, t, GROUP_W), BF16),
        scratch_shapes=[pltpu.VMEM((nc, GROUP_W, 128), BF16), pltpu.VMEM((nc, 128, GROUP_W), BF16),
                        pltpu.VMEM((nc, GROUP_W, 128), F32), pltpu.VMEM((nc, GROUP_W, 128), F32),
                        pltpu.VMEM((nc, 16, 128), F32), pltpu.VMEM((nc, GROUP_W, 128), F32),
                        pltpu.VMEM((384, GROUP_W), F32), pltpu.VMEM((1, 128), F32)],
        compiler_params=_params(("parallel",), VMEM_LIMIT),
        name="mlstm",
    )(p_ml, p_g, cw, cb, gb, nw_col)


def _s5_kernel(u_ref, are_ref, aim_ref, ls_ref, bre_ref, bim_ref, cre_ref, cim_ref, y_ref,
               ar_s, ai_s, bcat_s, ccat_s, st_s, bu_s, *, tc, nb):
    d = pl.program_id(0)
    i = pl.program_id(1)
    ns = S5_NGROUPS * S5_STATE

    @pl.when(i == 0)
    def _init():
        are = jnp.minimum(are_ref[0], -1e-4)
        aim = aim_ref[0]
        dt = jnp.exp(ls_ref[0])
        mag = jnp.exp(dt * are)
        abr = mag * jnp.cos(dt * aim)
        abi = mag * jnp.sin(dt * aim)
        inv = 1.0 / (are * are + aim * aim)
        fre = ((abr - 1.0) * are + abi * aim) * inv
        fim = (abi * are - (abr - 1.0) * aim) * inv
        bre = bre_ref[0]
        bim = bim_ref[0]
        bcat_s[:, 0:ns] = (bre * fre - bim * fim).astype(BF16)
        bcat_s[:, ns:2 * ns] = (bre * fim + bim * fre).astype(BF16)
        ccat_s[0:ns, :] = cre_ref[0].astype(BF16)
        ccat_s[ns:2 * ns, :] = (-cim_ref[0]).astype(BF16)
        ar_s[...] = jnp.broadcast_to(abr, (nb, ns))
        ai_s[...] = jnp.broadcast_to(abi, (nb, ns))
        st_s[...] = jnp.zeros_like(st_s)

    bu_s[...] = _dotf(u_ref[...].astype(BF16), bcat_s[...])
    ar = ar_s[...]
    ai = ai_s[...]

    def body(j, carry):
        xr, xi = carry
        t = j + d * (tc - 1 - 2 * j)
        r0 = pl.multiple_of(t * nb, nb)
        nr = ar * xr - ai * xi + bu_s[pl.ds(r0, nb), 0:ns]
        ni = ar * xi + ai * xr + bu_s[pl.ds(r0, nb), ns:2 * ns]
        bu_s[pl.ds(r0, nb), 0:ns] = nr
        bu_s[pl.ds(r0, nb), ns:2 * ns] = ni
        return nr, ni

    xr, xi = lax.fori_loop(0, tc, body, (st_s[0], st_s[1]), unroll=4)
    st_s[0] = xr
    st_s[1] = xi
    y = _dotf(bu_s[...].astype(BF16), ccat_s[...])
    for half in range(y.shape[1] // 128):
        y_ref[0, half] = y[:, 128 * half:128 * (half + 1)]


def _s5_call(u_tm, are, aim, ls, bre, bim, cre, cim, n_ctx, nb):
    rows, gw = u_tm.shape
    tc = S5_CHUNK
    n_chunks = rows // (tc * nb)
    n_ctx_chunks = n_ctx // tc
    ns = S5_NGROUPS * S5_STATE

    def chunk_of(d, i):
        rev = jnp.where(i < n_ctx_chunks, n_ctx_chunks - 1 - i, n_chunks - 1 - (i - n_ctx_chunks))
        return jnp.where(d == 0, i, rev)

    vec = pl.BlockSpec((1, 1, ns), lambda d, i: (d, 0, 0))
    return pl.pallas_call(
        functools.partial(_s5_kernel, tc=tc, nb=nb),
        grid=(2, n_chunks),
        in_specs=[pl.BlockSpec((tc * nb, gw), lambda d, i: (chunk_of(d, i), 0)),
                  vec, vec, vec,
                  pl.BlockSpec((1, gw, ns), lambda d, i: (d, 0, 0)),
                  pl.BlockSpec((1, gw, ns), lambda d, i: (d, 0, 0)),
                  pl.BlockSpec((1, ns, gw), lambda d, i: (d, 0, 0)),
                  pl.BlockSpec((1, ns, gw), lambda d, i: (d, 0, 0))],
        out_specs=pl.BlockSpec((1, gw // 128, tc * nb, 128),
                               lambda d, i: (d, 0, chunk_of(d, i), 0)),
        out_shape=jax.ShapeDtypeStruct((2, gw // 128, rows, 128), F32),
        scratch_shapes=[pltpu.VMEM((nb, ns), F32), pltpu.VMEM((nb, ns), F32),
                        pltpu.VMEM((gw, 2 * ns), BF16), pltpu.VMEM((2 * ns, gw), BF16),
                        pltpu.VMEM((2, nb, ns), F32), pltpu.VMEM((tc * nb, 2 * ns), F32)],
        compiler_params=_params(("arbitrary", "arbitrary"), VMEM_LIMIT),
        name="s5_scan",
    )(u_tm, are, aim, ls, bre, bim, cre, cim)


def _glu_kernel(ys_ref, u_ref, d_ref, w_ref, b_ref, o_ref, *, nb, tq):
    gw = GROUP_W
    rows = []
    for b in range(nb):
        y_b = jnp.concatenate(
            [ys_ref[0, half, pl.ds(b, tq, stride=nb), :] + ys_ref[1, half, pl.ds(b, tq, stride=nb), :]
             for half in range(gw // 128)], axis=1)
        rows.append(y_b + u_ref[:, gw * b:gw * (b + 1)] * d_ref[...])
    y = jnp.concatenate(rows, axis=0)
    g = y * (0.5 * (1.0 + jnp.tanh(math.sqrt(2.0 / math.pi) * (y + 0.044715 * (y * y * y)))))
    z = _dotf(g.astype(BF16), w_ref[...]) + b_ref[...]
    out = (g * _sigmoid(z)).astype(BF16)
    for b in range(nb):
        o_ref[b] = out[tq * b:tq * (b + 1)]


def _glu_call(ys, u_t, dsk, w, bias, nb):
    t = u_t.shape[0]
    tq = ROW_TILE // nb
    gw = GROUP_W
    const = lambda ti: (0, 0)
    return pl.pallas_call(
        functools.partial(_glu_kernel, nb=nb, tq=tq),
        grid=(t // tq,),
        in_specs=[pl.BlockSpec((2, gw // 128, tq * nb, 128), lambda ti: (0, 0, ti, 0)),
                  pl.BlockSpec((tq, nb * gw), lambda ti: (ti, 0)),
                  pl.BlockSpec(dsk.shape, const), pl.BlockSpec(w.shape, const),
                  pl.BlockSpec(bias.shape, const)],
        out_specs=pl.BlockSpec((nb, tq, gw), lambda ti: (0, ti, 0)),
        out_shape=jax.ShapeDtypeStruct((nb, t, gw), BF16),
        compiler_params=_params(("parallel",), VMEM_LIMIT),
        name="s5_glu",
    )(ys, u_t, dsk, w, bias)


def _out_kernel(ya_ref, yb_ref, yc_ref, yd_ref, w_ref, xa_ref, xb_ref, g1_ref, sh2_ref, sc2_ref,
                nw1_ref, nw2_ref, wr_ref, br_ref, xn_ref, f_ref, te_ref, tg_ref, *, n_a_blk, blk0):
    o = (_dotf(ya_ref[0], w_ref[0:256]) + _dotf(yb_ref[0], w_ref[256:512])
         + _dotf(yc_ref[0], w_ref[512:768]) + _dotf(yd_ref[0], w_ref[768:1024]))
    x = jnp.where(pl.program_id(1) + blk0 < n_a_blk, xa_ref[0], xb_ref[0])
    xn = x + g1_ref[0] * _rms(o, nw1_ref[...])
    xn_ref[0] = xn
    f = _rms(xn, nw2_ref[...]) * (1.0 + sc2_ref[0]) + sh2_ref[0]
    _store_token_tiles(f_ref, f)
    lg = _dot3(f, wr_ref[...]) + br_ref[...]
    tm = lg.shape[0]
    lane = lax.broadcasted_iota(I32, (tm, 128), 1).astype(F32)
    tops, idxs = [], []
    for _ in range(TOP_K):
        mx = jnp.max(lg, axis=-1, keepdims=True)
        idx = jnp.min(jnp.where(lg == mx, lane, 128.0), axis=-1, keepdims=True)
        tops.append(mx)
        idxs.append(idx.astype(I32))
        lg = jnp.where(lane == idx, NEG_INF, lg)
    ex = [jnp.exp(tv - tops[0]) for tv in tops]
    inv = 1.0 / (ex[0] + ex[1] + ex[2] + ex[3])
    l8 = lax.broadcasted_iota(I32, (tm, 8), 1)
    te = jnp.zeros((tm, 8), I32)
    tg = jnp.zeros((tm, 8), F32)
    for kk in range(TOP_K):
        te = jnp.where(l8 == kk, idxs[kk], te)
        tg = jnp.where(l8 == kk, ex[kk] * inv, tg)
    te_ref[0] = te
    tg_ref[0] = tg


def _out_call(ya, yb, yc, yd, w_out, xa, xb, mod_l, nw1, nw2, wr, br, n_ctx_blk, blk0):
    b, _, d = xa.shape
    t = xa.shape[1] + (0 if xb is None else xb.shape[1])
    tm = ROW_TILE
    nblk = t // tm - blk0
    t_out = nblk * tm
    spec_a, spec_b, n_a_blk = _stream_specs(xa, xb, blk0)

    def mrow(bi, ti):
        return jnp.where(ti + blk0 < n_ctx_blk, b, bi)

    row3 = lambda bi, ti: (bi, ti + blk0, 0)
    out3 = lambda bi, ti: (bi, ti, 0)
    const = lambda bi, ti: (0, 0)
    modspec = lambda j: pl.BlockSpec((1, 1, d), lambda bi, ti: (mrow(bi, ti), 0, j))
    yspec = lambda y: pl.BlockSpec((1, tm, GROUP_W), row3 if y.shape[1] == t else out3)
    return pl.pallas_call(
        functools.partial(_out_kernel, n_a_blk=n_a_blk, blk0=blk0),
        grid=(b, nblk),
        in_specs=[yspec(ya), yspec(yb), yspec(yc), yspec(yd), pl.BlockSpec((d, d), const),
                  spec_a, spec_b, modspec(2), modspec(3), modspec(4),
                  pl.BlockSpec((1, d), const), pl.BlockSpec((1, d), const),
                  pl.BlockSpec((d, 128), const), pl.BlockSpec((1, 128), const)],
        out_specs=[pl.BlockSpec((1, tm, d), out3),
                   pl.BlockSpec((tm * d // 128, 128), lambda bi, ti: (bi * nblk + ti, 0)),
                   pl.BlockSpec((1, tm, 8), out3), pl.BlockSpec((1, tm, 8), out3)],
        out_shape=[jax.ShapeDtypeStruct((b, t_out, d), F32),
                   jax.ShapeDtypeStruct((b * t_out * d // 128, 128), F32),
                   jax.ShapeDtypeStruct((b, t_out, 8), I32),
                   jax.ShapeDtypeStruct((b, t_out, 8), F32)],
        compiler_params=_params(("parallel", "parallel"), VMEM_LIMIT),
        name="out_proj_router",
    )(ya, yb, yc, yd, w_out, xa, xa if xb is None else xb, mod_l, mod_l, mod_l, nw1, nw2, wr, br)


def _rank_kernel(te_ref, rank_ref, cnt_ref, carry_ref):
    i = pl.program_id(0)

    @pl.when(i == 0)
    def _():
        carry_ref[...] = jnp.zeros_like(carry_ref)

    te = te_ref[...]
    tb = te.shape[0]
    lane = lax.broadcasted_iota(I32, (tb, 128), 1)
    l8 = lax.broadcasted_iota(I32, (tb, 8), 1)
    below = (lax.broadcasted_iota(I32, (tb, tb), 0)
             > lax.broadcasted_iota(I32, (tb, tb), 1))
    lstrict = jnp.where(below, 1.0, 0.0).astype(BF16)
    base = carry_ref[...]
    out = jnp.zeros((tb, 8), I32)
    for k in range(TOP_K):
        oh = jnp.where(lane == te[:, k:k + 1], 1.0, 0.0)
        before = _dotf(lstrict, oh.astype(BF16)) + base
        rank_k = jnp.sum(oh * before, axis=-1, keepdims=True)
        out = jnp.where(l8 == k, rank_k.astype(I32), out)
        base = base + jnp.sum(oh, axis=0, keepdims=True)
    rank_ref[...] = out
    carry_ref[...] = base
    cnt_ref[...] = base.astype(I32)


def _rank_call(te):
    n_tok = te.shape[0]
    tb = RANK_TILE
    return pl.pallas_call(
        _rank_kernel,
        grid=(n_tok // tb,),
        in_specs=[pl.BlockSpec((tb, 8), lambda i: (i, 0))],
        out_specs=[pl.BlockSpec((tb, 8), lambda i: (i, 0)), pl.BlockSpec((1, 128), lambda i: (0, 0))],
        out_shape=[jax.ShapeDtypeStruct((n_tok, 8), I32), jax.ShapeDtypeStruct((1, 128), I32)],
        scratch_shapes=[pltpu.VMEM((1, 128), F32)],
        compiler_params=_params(("arbitrary",)),
        name="moe_rank",
    )(te)


def _slot_kernel(tbl_ref, te_ref, rank_ref, slot_ref):
    te = te_ref[...]
    start = jnp.zeros(te.shape, I32)
    for e in range(N_EXPERTS):
        start = jnp.where(te == e, tbl_ref[e], start)
    slot_ref[...] = start + rank_ref[...]


def _slot_call(tbl, te, rank):
    n_tok = te.shape[0]
    tb = RANK_TILE
    spec = pl.BlockSpec((tb, 8), lambda i, tbl: (i, 0))
    return pl.pallas_call(
        _slot_kernel,
        grid_spec=pltpu.PrefetchScalarGridSpec(num_scalar_prefetch=1, grid=(n_tok // tb,),
                                               in_specs=[spec, spec], out_specs=spec),
        out_shape=jax.ShapeDtypeStruct((n_tok, 8), I32),
        compiler_params=_params(("parallel",)),
        name="moe_slots",
    )(tbl, te, rank)


def _slot_rows(slot_ref, r0, nt):
    return [pl.multiple_of(slot_ref[0, 0, (r0 + u) * 8 + k] * nt, nt)
            for u in range(DMA_UNROLL) for k in range(TOP_K)]


def _dispatch_kernel(tbl_ref, slot_ref, f_ref, xs_hbm, zbuf, sem, zsem, *, tb, tm, nt):
    i = pl.program_id(0)

    @pl.when(i == 0)
    def _():
        zbuf[...] = jnp.zeros_like(zbuf)
        fills = [pltpu.make_async_copy(
            zbuf, xs_hbm.at[pl.ds(pl.multiple_of(tbl_ref[N_EXPERTS + e] * nt, nt), tm * nt)], zsem)
            for e in range(N_EXPERTS)]
        for fill in fills:
            fill.start()
        for fill in fills:
            fill.wait()

        def fill_unused(blk, carry):
            tail = pltpu.make_async_copy(
                zbuf, xs_hbm.at[pl.ds(pl.multiple_of(blk * (tm * nt), tm * nt), tm * nt)], zsem)
            tail.start()
            tail.wait()
            return carry
        lax.fori_loop(tbl_ref[2 * N_EXPERTS], xs_hbm.shape[0] // (tm * nt), fill_unused, 0)

    def body(g, carry):
        r0 = g * DMA_UNROLL
        dsts = _slot_rows(slot_ref, r0, nt)
        for u in range(DMA_UNROLL):
            src = f_ref.at[pl.ds(pl.multiple_of((r0 + u) * nt, nt), nt)]
            for k in range(TOP_K):
                pltpu.make_async_copy(src, xs_hbm.at[pl.ds(dsts[u * TOP_K + k], nt)],
                                      sem).start(priority=k % 2)
        return carry

    lax.fori_loop(0, tb // DMA_UNROLL, body, 0)
    for k in range(TOP_K):
        pltpu.make_async_copy(f_ref, xs_hbm.at[pl.ds(0, tb * nt)], sem).wait()


def _dispatch_call(tbl, slot3, f_tiles, n_slots, tm):
    n_blk, _, per = slot3.shape
    tb = per // 8
    nt = f_tiles.shape[0] // (n_blk * tb)
    grid_spec = pltpu.PrefetchScalarGridSpec(
        num_scalar_prefetch=1,
        grid=(n_blk,),
        in_specs=[pl.BlockSpec((1, 1, per), lambda i, tbl: (i, 0, 0), memory_space=pltpu.SMEM),
                  pl.BlockSpec((tb * nt, 128), lambda i, tbl: (i, 0))],
        out_specs=pl.BlockSpec(memory_space=pl.ANY),
        scratch_shapes=[pltpu.VMEM((tm * nt, 128), F32), pltpu.SemaphoreType.DMA(()),
                        pltpu.SemaphoreType.DMA(())])
    return pl.pallas_call(
        functools.partial(_dispatch_kernel, tb=tb, tm=tm, nt=nt),
        grid_spec=grid_spec,
        out_shape=jax.ShapeDtypeStruct(((n_slots + tm) * nt, 128), F32),
        compiler_params=_params(("arbitrary",), VMEM_LIMIT),
        name="moe_dispatch",
    )(tbl, slot3, f_tiles)


def _expert_kernel(be_ref, nu_ref, xs_ref, wgu_ref, bgu_ref, wd_ref, bd_ref, ys_ref,
                   wgu_s, wd_s, *, tm, d_ff):
    i = pl.program_id(0)
    nt = xs_ref.shape[0] // tm

    @pl.when(i < nu_ref[0])
    def _():
        @pl.when(jnp.logical_or(i == 0, be_ref[i] != be_ref[jnp.maximum(i - 1, 0)]))
        def _():
            for c in range(0, wgu_s.shape[0], 128):
                wgu_s[c:c + 128, :] = wgu_ref[c:c + 128, :].astype(BF16)
            for c in range(0, wd_s.shape[0], 128):
                wd_s[c:c + 128, :] = wd_ref[c:c + 128, :].astype(BF16)

        for r0 in range(0, tm, MOE_ROWS):
            x = jnp.concatenate(_load_token_tiles(xs_ref, MOE_ROWS, nt, base=r0 * nt),
                                axis=1).astype(BF16)
            gu = _dotf(x, wgu_s[...]) + bgu_ref[...]
            gate = jnp.minimum(gu[:, 0:d_ff], SWIGLU_LIMIT)
            up = jnp.clip(gu[:, d_ff:2 * d_ff], -SWIGLU_LIMIT, SWIGLU_LIMIT)
            act = (up + 1.0) * gate * _sigmoid(SWIGLU_ALPHA * gate)
            _store_token_tiles(ys_ref, _dotf(act.astype(BF16), wd_s[...]) + bd_ref[...],
                               base=r0 * nt)

    @pl.when(i >= nu_ref[0])
    def _():
        ys_ref[...] = jnp.zeros_like(ys_ref)


def _expert_call(block_expert, n_used, xs_tiles, wgu, bgu, wd, bd, layer, n_blocks, tm):
    d, two_ff = wgu.shape[2:]
    d_ff = two_ff // 2
    nt = d // 128
    ex = lambda i, be, nu: (layer, be[i], 0, 0)
    grid_spec = pltpu.PrefetchScalarGridSpec(
        num_scalar_prefetch=2,
        grid=(n_blocks,),
        in_specs=[pl.BlockSpec((tm * nt, 128), lambda i, be, nu: (jnp.minimum(i, nu[0] - 1), 0)),
                  pl.BlockSpec((None, None, d, two_ff), ex),
                  pl.BlockSpec((None, None, 1, two_ff), ex),
                  pl.BlockSpec((None, None, d_ff, d), ex),
                  pl.BlockSpec((None, None, 1, d), ex)],
        out_specs=pl.BlockSpec((tm * nt, 128), lambda i, be, nu: (i, 0)),
        scratch_shapes=[pltpu.VMEM((d, two_ff), BF16), pltpu.VMEM((d_ff, d), BF16)])
    return pl.pallas_call(
        functools.partial(_expert_kernel, tm=tm, d_ff=d_ff),
        grid_spec=grid_spec,
        out_shape=jax.ShapeDtypeStruct((n_blocks * tm * nt, 128), F32),
        compiler_params=_params(("arbitrary",), VMEM_LIMIT),
        name="moe_experts",
    )(block_expert, n_used, xs_tiles, wgu, bgu, wd, bd)


def _route(counts, n_tok, tm):
    n_blocks = -(-(n_tok * TOP_K + N_EXPERTS * (tm - 1)) // tm)
    padded = (counts + tm - 1) // tm * tm
    padded_end = jnp.cumsum(padded)
    group_start = padded_end - padded
    block_start = jnp.arange(n_blocks, dtype=I32) * tm
    block_expert = jnp.minimum(
        jnp.sum((padded_end[None, :] <= block_start[:, None]).astype(I32), axis=1), N_EXPERTS - 1)
    n_used = (padded_end[-1] // tm).astype(I32).reshape(1)
    table = jnp.concatenate([group_start, group_start + counts, n_used]).astype(I32)
    return table, block_expert, n_used, n_blocks


def _fin_kernel(scur_ref, snxt_ref, g_ref, x_ref, g2_ref, nw_ref, ys_hbm, out_ref, buf, sem,
                *, tb, nt):
    i = pl.program_id(0)
    s = i % 2

    def gather(slot_ref, ss):
        def body(g, carry):
            r0 = g * DMA_UNROLL
            srcs = _slot_rows(slot_ref, r0, nt)
            for u in range(DMA_UNROLL):
                for k in range(TOP_K):
                    dst = pl.multiple_of((k * tb + r0 + u) * nt, nt)
                    pltpu.make_async_copy(ys_hbm.at[pl.ds(srcs[u * TOP_K + k], nt)],
                                          buf.at[ss, pl.ds(dst, nt)],
                                          sem.at[ss]).start(priority=k % 2)
            return carry
        lax.fori_loop(0, tb // DMA_UNROLL, body, 0)

    @pl.when(i == 0)
    def _():
        gather(scur_ref, 0)

    @pl.when(i + 1 < pl.num_programs(0))
    def _():
        gather(snxt_ref, 1 - s)

    pltpu.make_async_copy(ys_hbm.at[pl.ds(0, TOP_K * tb * nt)], buf.at[s], sem.at[s]).wait()
    gates = g_ref[...]
    parts = []
    for j in range(nt):
        acc = None
        for k in range(TOP_K):
            v = buf[s, pl.ds(k * tb * nt + j, tb, stride=nt), :] * gates[:, k:k + 1]
            acc = v if acc is None else acc + v
        parts.append(acc)
    m = jnp.concatenate(parts, axis=1)
    out_ref[0] = x_ref[0] + g2_ref[0] * _rms(m, nw_ref[...])


def _fin_call(slot3, gates, ys_tiles, xn, mod_l, nw, n_ctx_blk, blk0):
    b, t, d = xn.shape
    tb = ROW_TILE
    nblk = t // tb
    nt = d // 128
    n_blk = b * nblk

    def mrow(i):
        return jnp.where(i % nblk + blk0 < n_ctx_blk, b, i // nblk)

    smem = functools.partial(pl.BlockSpec, memory_space=pltpu.SMEM)
    return pl.pallas_call(
        functools.partial(_fin_kernel, tb=tb, nt=nt),
        grid=(n_blk,),
        in_specs=[smem((1, 1, tb * 8), lambda i: (i, 0, 0)),
                  smem((1, 1, tb * 8), lambda i: (jnp.minimum(i + 1, n_blk - 1), 0, 0)),
                  pl.BlockSpec((tb, 8), lambda i: (i, 0)),
                  pl.BlockSpec((1, tb, d), lambda i: (i // nblk, i % nblk, 0)),
                  pl.BlockSpec((1, 1, d), lambda i: (mrow(i), 0, 5)),
                  pl.BlockSpec((1, d), lambda i: (0, 0)),
                  pl.BlockSpec(memory_space=pl.ANY)],
        out_specs=pl.BlockSpec((1, tb, d), lambda i: (i // nblk, i % nblk, 0)),
        out_shape=jax.ShapeDtypeStruct((b, t, d), F32),
        scratch_shapes=[pltpu.VMEM((2, TOP_K * tb * nt, 128), F32), pltpu.SemaphoreType.DMA((2,))],
        compiler_params=_params(("arbitrary",), VMEM_LIMIT),
        name="moe_combine",
    )(slot3, slot3, gates, xn, mod_l, nw, ys_tiles)


def _in_proj_columns():
    cols = np.full((IN_COLS,), -1, np.int64)
    for sec in range(2):
        for n in range(256):
            part, hm, j = n // 128, (n % 128) // 16, n % 16
            m, h = hm // 4, hm % 4
            cols[sec * 256 + n] = sec * 256 + h * 64 + m * 32 + part * 16 + j
    cols[512:768] = np.arange(512, 768)
    o, s = IN_DA, 768
    cols[o:o + 1024] = s + np.arange(1024)
    o, s = o + IN_ML, s + 1024
    cols[o:o + 16] = s + np.arange(16)
    o, s = o + IN_G, s + 16
    cols[o:o + MLA_Q_RANK] = s + np.arange(MLA_Q_RANK)
    cols[o + 256:o + 256 + MLA_KV_RANK] = s + MLA_Q_RANK + np.arange(MLA_KV_RANK)
    cols[o + 384 + 64:o + 384 + 96] = s + MLA_Q_RANK + MLA_KV_RANK + np.arange(MLA_ROPE)
    o, s = o + IN_MLA, s + MLA_Q_RANK + MLA_KV_RANK + MLA_ROPE
    cols[o:o + 256] = s + np.arange(256)
    return cols


def _take_cols(w, cols):
    valid = jnp.asarray(cols >= 0)
    return jnp.where(valid, jnp.take(w, jnp.asarray(np.maximum(cols, 0)), axis=-1), 0.0)


def _rope_tables(n_ctx, n_lat):
    pos = jnp.arange(n_lat)
    inv = ROPE_THETA ** (-jnp.arange(8, dtype=F32) / 8)
    ang = jnp.concatenate([(pos // GRID_W)[:, None] * inv, (pos % GRID_W)[:, None] * inv], axis=-1)
    cos = jnp.concatenate([jnp.ones((n_ctx, 16), F32), jnp.cos(ang)], axis=0)
    sin = jnp.concatenate([jnp.zeros((n_ctx, 16), F32), jnp.sin(ang)], axis=0)
    t = n_ctx + n_lat
    cos_da, sin_da = jnp.tile(cos, (1, 8)), jnp.tile(sin, (1, 8))
    one, zero = jnp.ones((t, 64), F32), jnp.zeros((t, 64), F32)
    z16, z32 = jnp.zeros((t, 16), F32), jnp.zeros((t, 32), F32)
    cm = jnp.concatenate([one, cos, cos, jnp.ones((t, 32), F32)], axis=1)
    sa = jnp.concatenate([zero, -sin, z16, z32], axis=1)
    sb = jnp.concatenate([zero, z16, sin, z32], axis=1)
    return cos_da, sin_da, cm, sa, sb


def _mla_weights(w_uq, w_ukv):
    hd = MLA_NOPE + MLA_ROPE
    wq = jnp.zeros((256, 512), F32)
    wkv = jnp.zeros((MLA_KV_RANK, 768), F32)
    for h in range(MLA_HEADS):
        wq = wq.at[:MLA_Q_RANK, 128 * h:128 * h + hd].set(w_uq[:, hd * h:hd * (h + 1)])
        wkv = wkv.at[:, 128 * h:128 * h + MLA_NOPE].set(w_ukv[:, 128 * h:128 * h + MLA_NOPE])
        wkv = wkv.at[:, 512 + 64 * h:512 + 64 * (h + 1)].set(w_ukv[:, 128 * h + MLA_NOPE:128 * (h + 1)])
    return wq.astype(BF16), wkv.astype(BF16)


def _s5_layout(a_re, a_im, log_step, b_re, b_im, c_re, c_im):
    ns = S5_NGROUPS * S5_STATE
    eye = jnp.eye(S5_NGROUPS, dtype=F32)
    are = a_re.reshape(2, 1, ns)
    aim = a_im.reshape(2, 1, ns)
    ls = jnp.repeat(log_step, S5_STATE, axis=-1).reshape(2, 1, ns)
    bd_b = lambda w: jnp.einsum("dgph,gk->dghkp", w, eye).reshape(2, GROUP_W, ns)
    bd_c = lambda w: jnp.einsum("dghp,gk->dgpkh", w, eye).reshape(2, ns, GROUP_W)
    return are, aim, ls, bd_b(b_re), bd_b(b_im), bd_c(c_re), bd_c(c_im)


def kernel(x, c, ctx, c_ctx, w_mod, b_mod, norm_w, w_in, w_out, da_lambda, da_subln, ml_conv_w,
           ml_conv_b, ml_gate_b, ml_norm, mla_q_norm, mla_w_uq, mla_kv_norm, mla_w_ukv, s5_a_re,
           s5_a_im, s5_log_step, s5_b_re, s5_b_im, s5_c_re, s5_c_im, s5_d, s5_w_glu, s5_b_glu,
           moe_w_router, moe_b_router, moe_w_gate_up, moe_b_gate_up, moe_w_down, moe_b_down):
    bsz, n_lat, d = x.shape
    n_ctx = ctx.shape[1]
    t = n_ctx + n_lat
    depth = w_mod.shape[0]
    tm = ROW_TILE
    assert n_ctx % tm == 0 and n_lat % tm == 0 and bsz % 8 == 0 and bsz < 16
    assert n_ctx % ML_CHUNK == 0 and n_ctx % S5_CHUNK == 0
    n_ctx_blk = n_ctx // tm

    cc = jnp.zeros((16, d), F32).at[:bsz].set(c).at[bsz].set(c_ctx)
    mod = _mod_call(cc, w_mod, b_mod)
    cos_da, sin_da, cm, sa, sb = _rope_tables(n_ctx, n_lat)
    in_cols = _in_proj_columns()
    xa, xb = ctx, x

    for l in range(depth):
        last = l == depth - 1
        lambda_init = 0.8 - 0.6 * math.exp(-0.3 * l)
        mod_l = mod[l].reshape(16, 1, 6 * d)
        w_in_p = _take_cols(w_in[l], in_cols).astype(BF16)
        p_da, p_ml, p_g, p_mla, u_t = _in_call(xa, xb, mod_l, norm_w[l, 0].reshape(1, d), w_in_p,
                                               cos_da, sin_da, n_ctx_blk)

        da_extra = [da_lambda[l], jnp.tile(da_subln[l], 4).reshape(1, GROUP_W)]
        da_kern = functools.partial(_da_kernel, lambda_init=lambda_init)
        da_kw = dict(q_col=0, k_col=1, v_col=2, q_w=GROUP_W)
        ya = _attn_call(da_kern, "diff_attn", p_da, p_da, p_da, da_extra, q_blk0=n_ctx_blk,
                        n_q_blk=n_lat // tm, n_keys=t, **da_kw)
        wq, wkv = _mla_weights(mla_w_uq[l], mla_w_ukv[l])
        qn = jnp.zeros((1, 256), F32).at[0, :MLA_Q_RANK].set(mla_q_norm[l])
        q_mla, k_mla, v_mla = _mla_prep_call(p_mla, qn, wq, mla_kv_norm[l].reshape(1, -1), wkv,
                                             cm, sa, sb)
        mla_kw = dict(q_col=0, k_col=0, v_col=0, q_w=512)
        yc = _attn_call(_mla_attn_kernel, "mla_attn", q_mla, k_mla, v_mla, [], q_blk0=n_ctx_blk,
                        n_q_blk=n_lat // tm, n_keys=t, **mla_kw)
        if not last:
            ya_c = _attn_call(da_kern, "diff_attn_ctx", p_da, p_da, p_da, da_extra, q_blk0=0,
                              n_q_blk=n_ctx_blk, n_keys=n_ctx, **da_kw)
            yc_c = _attn_call(_mla_attn_kernel, "mla_attn_ctx", q_mla, k_mla, v_mla, [], q_blk0=0,
                              n_q_blk=n_ctx_blk, n_keys=n_ctx, **mla_kw)
            ya = jnp.concatenate([ya_c, ya], axis=1)
            yc = jnp.concatenate([yc_c, yc], axis=1)

        gb = jnp.zeros((1, 128), F32).at[0, :16].set(ml_gate_b[l])
        yb = _ml_call(p_ml, p_g, ml_conv_w[l], ml_conv_b[l].reshape(1, -1), gb,
                      ml_norm[l].reshape(1, -1), n_ctx)

        s5p = _s5_layout(s5_a_re[l], s5_a_im[l], s5_log_step[l], s5_b_re[l], s5_b_im[l],
                         s5_c_re[l], s5_c_im[l])
        ys = _s5_call(u_t.reshape(t * bsz, GROUP_W), *s5p, n_ctx, bsz)
        yd = _glu_call(ys, u_t, s5_d[l].reshape(1, -1),
                       s5_w_glu[l].astype(BF16), s5_b_glu[l].reshape(1, -1), bsz)

        blk0 = n_ctx_blk if last else 0
        wr = jnp.zeros((d, 128), F32).at[:, :N_EXPERTS].set(moe_w_router[l])
        br = jnp.full((1, 128), -1e30, F32).at[0, :N_EXPERTS].set(moe_b_router[l])
        xn, f, te, tg = _out_call(ya, yb, yc, yd, w_out[l].astype(BF16), xa, xb, mod_l,
                                  norm_w[l, 1].reshape(1, d), norm_w[l, 2].reshape(1, d), wr, br,
                                  n_ctx_blk, blk0)

        t_moe = t - blk0 * tm
        n_tok = bsz * t_moe
        te = te.reshape(n_tok, 8)
        rank, cnt = _rank_call(te)
        tbl, be, nu, n_blocks = _route(cnt[0, :N_EXPERTS], n_tok, MOE_TILE)
        slot3 = _slot_call(tbl, te, rank).reshape(n_tok // tm, 1, tm * 8)
        xs_tiles = _dispatch_call(tbl, slot3, f, n_blocks * MOE_TILE, MOE_TILE)
        ys_tiles = _expert_call(be, nu, xs_tiles, moe_w_gate_up,
                                moe_b_gate_up.reshape(depth, N_EXPERTS, 1, -1), moe_w_down,
                                moe_b_down.reshape(depth, N_EXPERTS, 1, -1), l, n_blocks, MOE_TILE)
        xa, xb = _fin_call(slot3, tg.reshape(n_tok, 8), ys_tiles, xn, mod_l,
                           norm_w[l, 3].reshape(1, d), n_ctx_blk, blk0), None
    return xa
```

```python
import functools
import math

import numpy as np
import jax
import jax.numpy as jnp
from jax import lax
from jax.experimental import pallas as pl
from jax.experimental.pallas import tpu as pltpu

F32, BF16, I32 = jnp.float32, jnp.bfloat16, jnp.int32
NORM_EPS = 1e-6
GRID_W = 64
ROPE_THETA = 10000.0
GROUP_W = 256
DA_QK = 32
ML_CHUNK = 128
MLA_HEADS, MLA_NOPE, MLA_ROPE, MLA_Q_RANK, MLA_KV_RANK = 4, 64, 32, 192, 128
S5_NGROUPS, S5_GROUP, S5_STATE = 16, 16, 64
N_EXPERTS, TOP_K = 32, 4
SWIGLU_ALPHA, SWIGLU_LIMIT = 1.702, 7.0
NEG_INF = float("-inf")

ROW_TILE = 256
MOE_TILE = 256
MOE_ROWS = 256
S5_CHUNK = 64
RANK_TILE = 512
DMA_UNROLL = 8
VMEM_LIMIT = 56 * 1024 * 1024

IN_DA, IN_ML, IN_G, IN_MLA, IN_S5 = 768, 1024, 128, 512, 256
IN_COLS = IN_DA + IN_ML + IN_G + IN_MLA + IN_S5


def _params(sem, vmem=None):
    return pltpu.CompilerParams(dimension_semantics=sem, vmem_limit_bytes=vmem)


def _dotf(a, b):
    return jnp.dot(a, b, preferred_element_type=F32)


def _dot_nt(a, b):
    return lax.dot_general(a, b, (((1,), (1,)), ((), ())), preferred_element_type=F32)


def _split2(a):
    hi = a.astype(BF16)
    lo = (a - hi.astype(F32)).astype(BF16)
    return hi, lo


def _dot3(a, b):
    ah, al = _split2(a)
    bh, bl = _split2(b)
    return _dotf(ah, bh) + _dotf(ah, bl) + _dotf(al, bh)


def _dot_exact_rhs(a, rhs_b):
    a1 = a.astype(BF16)
    r1 = a - a1.astype(F32)
    a2 = r1.astype(BF16)
    a3 = (r1 - a2.astype(F32)).astype(BF16)
    return _dotf(a1, rhs_b) + _dotf(a2, rhs_b) + _dotf(a3, rhs_b)


def _rms(x, w):
    ms = jnp.mean(x * x, axis=-1, keepdims=True)
    return x * lax.rsqrt(ms + NORM_EPS) * w


def _head_rms(a, width):
    n = a.shape[-1]
    sh = int(math.log2(width))
    r = lax.broadcasted_iota(I32, (n, n), 0) >> sh
    c = lax.broadcasted_iota(I32, (n, n), 1) >> sh
    g = jnp.where(r == c, 1.0 / width, 0.0).astype(BF16)
    hi, lo = _split2(a * a)
    ms = _dotf(hi, g) + _dotf(lo, g)
    return a * lax.rsqrt(ms + NORM_EPS)


def _store_token_tiles(ref, val, base=0):
    tm, d = val.shape
    nt = d // 128
    for j in range(nt):
        ref[pl.ds(base + j, tm, stride=nt), :] = val[:, 128 * j:128 * (j + 1)]


def _load_token_tiles(ref, tm, nt, base=0, lead=()):
    parts = [ref[lead + (pl.ds(base + j, tm, stride=nt), slice(None))] for j in range(nt)]
    return parts


def _sigmoid(x):
    return jax.nn.sigmoid(x)


def _log_sigmoid(x):
    return jnp.minimum(x, 0.0) - jnp.log(1.0 + jnp.exp(-jnp.abs(x)))


def _mod_kernel(c_ref, w_ref, b_ref, o_ref):
    c = c_ref[...]
    o_ref[0] = _dot3(c * _sigmoid(c), w_ref[0]) + b_ref[0]


def _mod_call(cc, w_mod, b_mod):
    n_layers, d, n = w_mod.shape
    tn = 1536
    return pl.pallas_call(
        _mod_kernel,
        grid=(n_layers, n // tn),
        in_specs=[pl.BlockSpec((16, d), lambda l, j: (0, 0)),
                  pl.BlockSpec((1, d, tn), lambda l, j: (l, 0, j)),
                  pl.BlockSpec((1, 1, tn), lambda l, j: (l, 0, j))],
        out_specs=pl.BlockSpec((1, 16, tn), lambda l, j: (l, 0, j)),
        out_shape=jax.ShapeDtypeStruct((n_layers, 16, n), F32),
        compiler_params=_params(("parallel", "parallel"), VMEM_LIMIT),
        name="mod_vectors",
    )(cc, w_mod, b_mod.reshape(n_layers, 1, n))


def _in_kernel(xa_ref, xb_ref, sh_ref, sc_ref, nw_ref, w_ref, c_ref, s_ref,
               da_ref, ml_ref, g_ref, mla_ref, s5_ref, *, qscale, n_a_blk):
    x = jnp.where(pl.program_id(1) < n_a_blk, xa_ref[0], xb_ref[0])
    h = _rms(x, nw_ref[...]) * (1.0 + sc_ref[0]) + sh_ref[0]
    hb = h.astype(BF16)
    da = _dotf(hb, w_ref[:, 0:IN_DA])
    c = c_ref[...]
    s = s_ref[...]
    q1, q2, k1, k2 = da[:, 0:128], da[:, 128:256], da[:, 256:384], da[:, 384:512]
    da_ref[0, :, 0:128] = ((q1 * c - q2 * s) * qscale).astype(BF16)
    da_ref[0, :, 128:256] = ((q2 * c + q1 * s) * qscale).astype(BF16)
    da_ref[0, :, 256:384] = (k1 * c - k2 * s).astype(BF16)
    da_ref[0, :, 384:512] = (k2 * c + k1 * s).astype(BF16)
    da_ref[0, :, 512:768] = da[:, 512:768].astype(BF16)
    o = IN_DA
    ml_ref[0] = _dotf(hb, w_ref[:, o:o + IN_ML])
    o += IN_ML
    g_ref[0] = _dotf(hb, w_ref[:, o:o + IN_G])
    o += IN_G
    mla_ref[0] = _dotf(hb, w_ref[:, o:o + IN_MLA])
    o += IN_MLA
    s5_ref[...] = _dotf(hb, w_ref[:, o:o + IN_S5])


def _stream_specs(xa, xb, blk0):
    tm = ROW_TILE
    d = xa.shape[2]
    n_a = xa.shape[1] // tm
    spec_a = pl.BlockSpec((1, tm, d), lambda bi, ti: (bi, jnp.minimum(ti + blk0, n_a - 1), 0))
    spec_b = pl.BlockSpec((1, tm, d), lambda bi, ti: (bi, jnp.maximum(ti + blk0 - n_a, 0), 0))
    return spec_a, spec_b, n_a


def _in_call(xa, xb, mod_l, nw, w_in_p, cos_da, sin_da, n_ctx_blk):
    b, _, d = xa.shape
    t = xa.shape[1] + (0 if xb is None else xb.shape[1])
    tm = ROW_TILE
    spec_a, spec_b, n_a_blk = _stream_specs(xa, xb, 0)

    def mrow(bi, ti):
        return jnp.where(ti < n_ctx_blk, b, bi)

    row3 = lambda bi, ti: (bi, ti, 0)
    return pl.pallas_call(
        functools.partial(_in_kernel, qscale=DA_QK ** -0.5 * LOG2_E, n_a_blk=n_a_blk),
        grid=(b, t // tm),
        in_specs=[spec_a, spec_b,
                  pl.BlockSpec((1, 1, d), lambda bi, ti: (mrow(bi, ti), 0, 0)),
                  pl.BlockSpec((1, 1, d), lambda bi, ti: (mrow(bi, ti), 0, 1)),
                  pl.BlockSpec((1, d), lambda bi, ti: (0, 0)),
                  pl.BlockSpec((d, IN_COLS), lambda bi, ti: (0, 0)),
                  pl.BlockSpec((tm, 128), lambda bi, ti: (ti, 0)),
                  pl.BlockSpec((tm, 128), lambda bi, ti: (ti, 0))],
        out_specs=[pl.BlockSpec((1, tm, IN_DA), row3),
                   pl.BlockSpec((1, tm, IN_ML), row3),
                   pl.BlockSpec((1, tm, IN_G), row3),
                   pl.BlockSpec((1, tm, IN_MLA), row3),
                   pl.BlockSpec((tm, IN_S5), lambda bi, ti: (ti, bi))],
        out_shape=[jax.ShapeDtypeStruct((b, t, IN_DA), BF16),
                   jax.ShapeDtypeStruct((b, t, IN_ML), F32),
                   jax.ShapeDtypeStruct((b, t, IN_G), F32),
                   jax.ShapeDtypeStruct((b, t, IN_MLA), F32),
                   jax.ShapeDtypeStruct((t, b * IN_S5), F32)],
        compiler_params=_params(("parallel", "parallel"), VMEM_LIMIT),
        name="in_proj",
    )(xa, xa if xb is None else xb, mod_l, mod_l, nw, w_in_p, cos_da, sin_da)


LOG2_E = 1.0 / math.log(2.0)


def _softmax_rows(s):
    mx = jnp.max(s, axis=-1, keepdims=True)
    p = jnp.exp2(s - mx)
    return p, jnp.sum(p, axis=-1, keepdims=True)


def _da_kernel(q_ref, k_ref, v_ref, lam_ref, sub_ref, o_ref, *, lambda_init):
    q = q_ref[0]
    k = k_ref[0]
    v = v_ref[0]
    lp = lam_ref[...]
    lam = (jnp.exp(jnp.sum(lp[0:1] * lp[1:2], axis=-1, keepdims=True))
           - jnp.exp(jnp.sum(lp[2:3] * lp[3:4], axis=-1, keepdims=True)) + lambda_init)
    lane = lax.broadcasted_iota(I32, (1, GROUP_W), 1)
    grp = (lane & 127) >> 4
    head = lane >> 6
    acc = jnp.zeros((q.shape[0], GROUP_W), F32)
    for h in range(4):
        ps, rs = [], []
        for m in range(2):
            qm = jnp.where(grp == m * 4 + h, q, jnp.zeros_like(q))
            p, l = _softmax_rows(_dot_nt(qm, k))
            ps.append(p.astype(BF16))
            rs.append(((1.0 if m == 0 else lam) / l).astype(BF16))
        w = ps[0] * rs[0] - ps[1] * rs[1]
        vm = jnp.where(head == h, v, jnp.zeros_like(v))
        acc = acc + _dotf(w, vm)
    y = _head_rms(acc, 64) * sub_ref[...] * (1.0 - lambda_init)
    o_ref[0] = y.astype(BF16)


def _mla_attn_kernel(q_ref, k_ref, v_ref, o_ref):
    q = q_ref[0]
    k = k_ref[0]
    v = v_ref[0]
    head = lax.broadcasted_iota(I32, (1, GROUP_W), 1) >> 6
    acc = jnp.zeros((q.shape[0], GROUP_W), F32)
    for h in range(MLA_HEADS):
        sl = slice(128 * h, 128 * (h + 1))
        p, l = _softmax_rows(_dot_nt(q[:, sl], k[:, sl]))
        vm = jnp.where(head == h, v, jnp.zeros_like(v))
        acc = acc + _dotf(p.astype(BF16), vm) * (1.0 / l)
    o_ref[0] = acc.astype(BF16)


def _attn_call(kernel, name, q_arr, k_arr, v_arr, extra, *, q_blk0, n_q_blk, n_keys,
               q_col, k_col, v_col, q_w):
    b = q_arr.shape[0]
    tq = ROW_TILE
    in_specs = [pl.BlockSpec((1, tq, q_w), lambda bi, qi: (bi, qi + q_blk0, q_col)),
                pl.BlockSpec((1, n_keys, q_w), lambda bi, qi: (bi, 0, k_col)),
                pl.BlockSpec((1, n_keys, GROUP_W), lambda bi, qi: (bi, 0, v_col))]
    in_specs += [pl.BlockSpec(e.shape, lambda bi, qi: (0, 0)) for e in extra]
    return pl.pallas_call(
        kernel,
        grid=(b, n_q_blk),
        in_specs=in_specs,
        out_specs=pl.BlockSpec((1, tq, GROUP_W), lambda bi, qi: (bi, qi, 0)),
        out_shape=jax.ShapeDtypeStruct((b, n_q_blk * tq, GROUP_W), BF16),
        compiler_params=_params(("parallel", "arbitrary"), VMEM_LIMIT),
        name=name,
    )(q_arr, k_arr, v_arr, *extra)


def _mla_prep_kernel(p_ref, qn_ref, wq_ref, kvn_ref, wkv_ref, c_ref, sa_ref, sb_ref,
                     q_ref, k_ref, v_ref, *, scale):
    p = p_ref[0]
    cq, ckv, kr = p[:, 0:256], p[:, 256:384], p[:, 384:512]
    msq = jnp.sum(cq * cq, axis=-1, keepdims=True) * (1.0 / MLA_Q_RANK)
    qn = (cq * lax.rsqrt(msq + NORM_EPS) * qn_ref[...]).astype(BF16)
    q = _dotf(qn, wq_ref[...])
    c = c_ref[...]
    sa = sa_ref[...]
    sb = sb_ref[...]

    def rope(a):
        return a * c + pltpu.roll(a, 112, 1) * sa + pltpu.roll(a, 16, 1) * sb

    for h in range(MLA_HEADS):
        sl = slice(128 * h, 128 * (h + 1))
        q_ref[0, :, sl] = (rope(q[:, sl]) * scale).astype(BF16)
    kvn = (_rms(ckv, kvn_ref[...])).astype(BF16)
    kv = _dotf(kvn, wkv_ref[...])
    krr = rope(kr)
    for h in range(MLA_HEADS):
        sl = slice(128 * h, 128 * (h + 1))
        k_ref[0, :, sl] = (kv[:, sl] + krr).astype(BF16)
    v_ref[0] = kv[:, 512:768].astype(BF16)


def _mla_prep_call(p_mla, qn, wq, kvn, wkv, cm, sa, sb):
    b, t, _ = p_mla.shape
    tm = ROW_TILE
    row3 = lambda bi, ti: (bi, ti, 0)
    const = lambda bi, ti: (0, 0)
    tab = lambda bi, ti: (ti, 0)
    return pl.pallas_call(
        functools.partial(_mla_prep_kernel, scale=(MLA_NOPE + MLA_ROPE) ** -0.5 * LOG2_E),
        grid=(b, t // tm),
        in_specs=[pl.BlockSpec((1, tm, IN_MLA), row3),
                  pl.BlockSpec(qn.shape, const), pl.BlockSpec(wq.shape, const),
                  pl.BlockSpec(kvn.shape, const), pl.BlockSpec(wkv.shape, const),
                  pl.BlockSpec((tm, 128), tab), pl.BlockSpec((tm, 128), tab),
                  pl.BlockSpec((tm, 128), tab)],
        out_specs=[pl.BlockSpec((1, tm, 512), row3), pl.BlockSpec((1, tm, 512), row3),
                   pl.BlockSpec((1, tm, GROUP_W), row3)],
        out_shape=[jax.ShapeDtypeStruct((b, t, 512), BF16),
                   jax.ShapeDtypeStruct((b, t, 512), BF16),
                   jax.ShapeDtypeStruct((b, t, GROUP_W), BF16)],
        compiler_params=_params(("parallel", "parallel"), VMEM_LIMIT),
        name="mla_prep",
    )(p_mla, qn, wq, kvn, wkv, cm, sa, sb)


def _ml_kernel(p_ref, g_ref, cw_ref, cb_ref, gb_ref, nw_ref, y_ref,
               qt_s, k_s, vt_s, ot_s, gt_s, hf_s, c_s, m_s, *, n_ctx_chunks, n_chunks):
    cl = ML_CHUNK
    t_total = n_chunks * cl
    row = lax.broadcasted_iota(I32, (cl, 1), 0)
    si = lax.broadcasted_iota(I32, (cl, cl), 0)
    ti = lax.broadcasted_iota(I32, (cl, cl), 1)
    lane128 = lax.broadcasted_iota(I32, (1, 128), 1)
    lane256 = lax.broadcasted_iota(I32, (1, 256), 1)
    row128 = lax.broadcasted_iota(I32, (128, 1), 0)
    hmask = [(lane256 >> 6) == h for h in range(4)]
    is_f = jnp.logical_and(((lane128 >> 2) & 1) == 1, lane128 < 16)
    r_nd = lax.broadcasted_iota(I32, (384, 512), 0)
    c_nd = lax.broadcasted_iota(I32, (384, 512), 1) >> 7
    nd_head = jnp.where(r_nd < 256, r_nd >> 6, r_nd - 256)
    nd_mask = nd_head == c_nd
    nd_ones = jnp.where(jnp.logical_and(nd_mask, r_nd >= 256), 1.0, 0.0).astype(BF16)
    r_st = lax.broadcasted_iota(I32, (384, 256), 0)
    st_head = jnp.where(r_st < 256, r_st >> 6, r_st - 256)
    st_mask = st_head == (lax.broadcasted_iota(I32, (384, 256), 1) >> 6)
    st_rowhead = jnp.where(lax.broadcasted_iota(I32, (384, 1), 0) < 256,
                           lax.broadcasted_iota(I32, (384, 1), 0) >> 6,
                           lax.broadcasted_iota(I32, (384, 1), 0) - 256)
    w0, w1, w2 = cw_ref[0:1], cw_ref[1:2], cw_ref[2:3]
    cb = cb_ref[...]
    gb = gb_ref[...]

    def prep_body(c, carry):
        s0 = pl.multiple_of(c * cl, cl)
        x = p_ref[0, pl.ds(s0, cl), 0:512]
        sp = pl.multiple_of(jnp.maximum(s0 - 8, 0), 8)
        sn = pl.multiple_of(jnp.minimum(s0 + cl, t_total - 8), 8)
        has_prev = jnp.logical_and(c != 0, c != n_ctx_chunks).astype(F32)
        has_next = jnp.logical_and(c != n_ctx_chunks - 1, c != n_chunks - 1).astype(F32)
        prev_row = p_ref[0, pl.ds(sp, 8), 0:512][7:8] * has_prev
        next_row = p_ref[0, pl.ds(sn, 8), 0:512][0:1] * has_next
        xp = jnp.where(row == 0, prev_row, pltpu.roll(x, 1, 0))
        xn = jnp.where(row == cl - 1, next_row, pltpu.roll(x, cl - 1, 0))
        z = xp * w0 + x * w1 + xn * w2 + cb
        qk = z * _sigmoid(z)
        qt_s[c] = qk[:, 0:256].T.astype(BF16)
        k_s[c] = (qk[:, 256:512] * (64 ** -0.5)).astype(BF16)
        vt_s[c] = p_ref[0, pl.ds(s0, cl), 512:768].T
        ot_s[c] = p_ref[0, pl.ds(s0, cl), 768:1024].T
        g = g_ref[0, pl.ds(s0, cl), :] + gb
        gt_s[c] = jnp.where(is_f, _log_sigmoid(g), g).T[0:16]
        return carry

    def rows_to_blocks(rows, n):
        return jnp.concatenate([jnp.broadcast_to(r, (n, 128)) for r in rows], axis=0)

    def head_rows(rows):
        out = jnp.zeros((128, 128), F32)
        for h in range(4):
            out = jnp.where(row128 == h, rows[h], out)
        return out

    def chunk(c, reverse):
        g_t = gt_s[c]
        tri = (si >= ti) if reverse else (si <= ti)
        bc_t = _dot_exact_rhs(g_t, jnp.where(tri, 1.0, 0.0).astype(BF16))
        qt = qt_s[c]
        kb = k_s[c]
        m_all = m_s[...]
        off = 8 if reverse else 0
        a_rows, r_rows, tots, m_old = [], [], [], []
        for h in range(4):
            il, fl = off + h, off + 4 + h
            a_rows.append(bc_t[fl:fl + 1, :])
            r_rows.append(g_t[il:il + 1, :] - bc_t[fl:fl + 1, :])
            tots.append(bc_t[fl:fl + 1, 0:1] if reverse else bc_t[fl:fl + 1, cl - 1:cl])
            m_old.append(m_all[:, h:h + 1])
        r_all = jnp.concatenate(r_rows, axis=1)
        d_t = jnp.broadcast_to(r_all, (cl, 4 * cl)).T + rows_to_blocks(a_rows, cl)
        lw = jnp.where(jnp.concatenate([tri] * 4, axis=0), d_t, NEG_INF)
        mt, wi, emt = [], [], []
        for h in range(4):
            linter = a_rows[h] + m_old[h]
            mt_h = jnp.maximum(linter, jnp.max(lw[cl * h:cl * (h + 1)], axis=0, keepdims=True))
            mt.append(mt_h)
            wi.append(jnp.exp(linter - mt_h))
            emt.append(jnp.exp(-mt_h))
        kstack = jnp.concatenate([jnp.where(hmask[h], kb, jnp.zeros_like(kb)) for h in range(4)],
                                 axis=0)
        s_t = _dotf(kstack, qt)
        w_t = (s_t * jnp.exp(lw - rows_to_blocks(mt, cl))).astype(BF16)
        vt = vt_s[c]
        vt4 = jnp.concatenate([vt.astype(BF16)] * 4, axis=1)
        lhs_nd = jnp.concatenate([vt4, jnp.zeros((128, 512), BF16)], axis=0)
        lhs_nd = jnp.where(nd_mask, lhs_nd, jnp.zeros_like(lhs_nd)) + nd_ones
        c_aug = c_s[...]
        nd = _dotf(lhs_nd, w_t) + jnp.concatenate(
            [rows_to_blocks(wi, 64), head_rows(wi)], axis=0) * _dotf(c_aug.astype(BF16), qt)
        den = [nd[256 + h:257 + h, :] for h in range(4)]
        h_t = nd[0:256] / jnp.maximum(jnp.abs(rows_to_blocks(den, 64)), rows_to_blocks(emt, 64))
        wupd, decs, m_new_all = [], [], m_all
        for h in range(4):
            lupd = tots[h] + r_rows[h]
            m_new = jnp.maximum(tots[h] + m_old[h], jnp.max(lupd, axis=-1, keepdims=True))
            wupd.append(jnp.exp(lupd - m_new))
            decs.append(jnp.exp(tots[h] + m_old[h] - m_new))
            m_new_all = jnp.where(lane128 == h, m_new, m_new_all)
        lhs_u = jnp.concatenate([vt * rows_to_blocks(wupd, 64), head_rows(wupd)], axis=0)
        upd = _dotf(lhs_u.astype(BF16), kb)
        dec_col = jnp.zeros((384, 1), F32)
        for h in range(4):
            dec_col = jnp.where(st_rowhead == h, decs[h], dec_col)
        c_s[...] = dec_col * c_aug + jnp.where(st_mask, upd, 0.0)
        m_s[...] = m_new_all
        return h_t

    def fwd_body(i, carry):
        hf_s[i] = chunk(i, False)
        return carry

    def bwd_body(i, carry):
        c = jnp.where(i < n_ctx_chunks, n_ctx_chunks - 1 - i, n_chunks - 1 - (i - n_ctx_chunks))
        gated = _sigmoid(ot_s[c]) * (hf_s[c] + chunk(c, True))
        ms = [jnp.mean(jnp.square(gated[64 * h:64 * (h + 1)]), axis=0, keepdims=True)
              for h in range(4)]
        y_t = gated * lax.rsqrt(rows_to_blocks(ms, 64) + NORM_EPS) * nw_ref[...]
        y_ref[0, pl.ds(pl.multiple_of(c * cl, cl), cl), :] = y_t.T.astype(BF16)
        return carry

    lax.fori_loop(0, n_chunks, prep_body, 0)
    c_s[...] = jnp.zeros_like(c_s)
    m_s[...] = jnp.zeros_like(m_s)
    lax.fori_loop(0, n_chunks, fwd_body, 0)
    c_s[...] = jnp.zeros_like(c_s)
    m_s[...] = jnp.zeros_like(m_s)
    lax.fori_loop(0, n_chunks, bwd_body, 0)


def _ml_call(p_ml, p_g, cw, cb, gb, nw, n_ctx):
    b, t, _ = p_ml.shape
    nc = t // ML_CHUNK
    const = lambda bi: (0, 0)
    nw_col = jnp.broadcast_to(nw.reshape(GROUP_W, 1), (GROUP_W, 128))
    return pl.pallas_call(
        functools.partial(_ml_kernel, n_ctx_chunks=n_ctx // ML_CHUNK, n_chunks=nc),
        grid=(b,),
        in_specs=[pl.BlockSpec((1, t, IN_ML), lambda bi: (bi, 0, 0)),
                  pl.BlockSpec((1, t, IN_G), lambda bi: (bi, 0, 0)),
                  pl.BlockSpec(cw.shape, const), pl.BlockSpec(cb.shape, const),
                  pl.BlockSpec(gb.shape, const), pl.BlockSpec(nw_col.shape, const)],
        out_specs=pl.BlockSpec((1, t, GROUP_W), lambda bi: (bi, 0, 0)),
        out_shape=jax.ShapeDtypeStruct((b, t, GROUP_W), BF16),
        scratch_shapes=[pltpu.VMEM((nc, GROUP_W, 128), BF16), pltpu.VMEM((nc, 128, GROUP_W), BF16),
                        pltpu.VMEM((nc, GROUP_W, 128), F32), pltpu.VMEM((nc, GROUP_W, 128), F32),
                        pltpu.VMEM((nc, 16, 128), F32), pltpu.VMEM((nc, GROUP_W, 128), F32),
                        pltpu.VMEM((384, GROUP_W), F32), pltpu.VMEM((1, 128), F32)],
        compiler_params=_params(("parallel",), VMEM_LIMIT),
        name="mlstm",
    )(p_ml, p_g, cw, cb, gb, nw_col)


def _s5_kernel(u_ref, are_ref, aim_ref, ls_ref, bre_ref, bim_ref, cre_ref, cim_ref, y_ref,
               ar_s, ai_s, bcat_s, ccat_s, st_s, bu_s, xh_s, *, tc, nb):
    d = pl.program_id(0)
    i = pl.program_id(1)
    ns = S5_NGROUPS * S5_STATE

    @pl.when(i == 0)
    def _init():
        are = jnp.minimum(are_ref[0], -1e-4)
        aim = aim_ref[0]
        dt = jnp.exp(ls_ref[0])
        mag = jnp.exp(dt * are)
        abr = mag * jnp.cos(dt * aim)
        abi = mag * jnp.sin(dt * aim)
        inv = 1.0 / (are * are + aim * aim)
        fre = ((abr - 1.0) * are + abi * aim) * inv
        fim = (abi * are - (abr - 1.0) * aim) * inv
        bre = bre_ref[0]
        bim = bim_ref[0]
        bcat_s[:, 0:ns] = (bre * fre - bim * fim).astype(BF16)
        bcat_s[:, ns:2 * ns] = (bre * fim + bim * fre).astype(BF16)
        ccat_s[0:ns, :] = cre_ref[0].astype(BF16)
        ccat_s[ns:2 * ns, :] = (-cim_ref[0]).astype(BF16)
        ar_s[...] = jnp.broadcast_to(abr, (nb, ns))
        ai_s[...] = jnp.broadcast_to(abi, (nb, ns))
        st_s[...] = jnp.zeros_like(st_s)

    bu_s[...] = _dotf(u_ref[...].astype(BF16), bcat_s[...])
    ar = ar_s[...]
    ai = ai_s[...]

    def step(xr, xi, t):
        r0 = pl.multiple_of(t * nb, nb)
        return (ar * xr - ai * xi + bu_s[pl.ds(r0, nb), 0:ns],
                ar * xi + ai * xr + bu_s[pl.ds(r0, nb), ns:2 * ns])

    def body(jj, carry):
        t_a = 2 * jj + d * (tc - 1 - 4 * jj)
        ra, ia = step(*carry, t_a)
        rb, ib = step(ra, ia, t_a + 1 - 2 * d)
        p0 = pl.multiple_of(jj * (2 * nb), 2 * nb)
        xh_s[pl.ds(p0, 2 * nb), 0:ns] = jnp.concatenate([ra, rb], axis=0).astype(BF16)
        xh_s[pl.ds(p0, 2 * nb), ns:2 * ns] = jnp.concatenate([ia, ib], axis=0).astype(BF16)
        return rb, ib

    xr, xi = lax.fori_loop(0, tc // 2, body, (st_s[0], st_s[1]), unroll=2)
    st_s[0] = xr
    st_s[1] = xi
    y = _dotf(xh_s[...], ccat_s[...])
    n_half = y.shape[1] // 128

    @pl.when(d == 0)
    def _():
        for half in range(n_half):
            y_ref[0, half] = y[:, 128 * half:128 * (half + 1)]

    @pl.when(d == 1)
    def _():
        for p in range(tc):
            for half in range(n_half):
                y_ref[0, half, (tc - 1 - p) * nb:(tc - p) * nb, :] = (
                    y[p * nb:(p + 1) * nb, 128 * half:128 * (half + 1)])


def _s5_call(u_tm, are, aim, ls, bre, bim, cre, cim, n_ctx, nb):
    rows, gw = u_tm.shape
    tc = S5_CHUNK
    n_chunks = rows // (tc * nb)
    n_ctx_chunks = n_ctx // tc
    ns = S5_NGROUPS * S5_STATE

    def chunk_of(d, i):
        rev = jnp.where(i < n_ctx_chunks, n_ctx_chunks - 1 - i, n_chunks - 1 - (i - n_ctx_chunks))
        return jnp.where(d == 0, i, rev)

    vec = pl.BlockSpec((1, 1, ns), lambda d, i: (d, 0, 0))
    return pl.pallas_call(
        functools.partial(_s5_kernel, tc=tc, nb=nb),
        grid=(2, n_chunks),
        in_specs=[pl.BlockSpec((tc * nb, gw), lambda d, i: (chunk_of(d, i), 0)),
                  vec, vec, vec,
                  pl.BlockSpec((1, gw, ns), lambda d, i: (d, 0, 0)),
                  pl.BlockSpec((1, gw, ns), lambda d, i: (d, 0, 0)),
                  pl.BlockSpec((1, ns, gw), lambda d, i: (d, 0, 0)),
                  pl.BlockSpec((1, ns, gw), lambda d, i: (d, 0, 0))],
        out_specs=pl.BlockSpec((1, gw // 128, tc * nb, 128),
                               lambda d, i: (d, 0, chunk_of(d, i), 0)),
        out_shape=jax.ShapeDtypeStruct((2, gw // 128, rows, 128), F32),
        scratch_shapes=[pltpu.VMEM((nb, ns), F32), pltpu.VMEM((nb, ns), F32),
                        pltpu.VMEM((gw, 2 * ns), BF16), pltpu.VMEM((2 * ns, gw), BF16),
                        pltpu.VMEM((2, nb, ns), F32), pltpu.VMEM((tc * nb, 2 * ns), F32),
                        pltpu.VMEM((tc * nb, 2 * ns), BF16)],
        compiler_params=_params(("arbitrary", "arbitrary"), VMEM_LIMIT),
        name="s5_scan",
    )(u_tm, are, aim, ls, bre, bim, cre, cim)


def _glu_kernel(ys_ref, u_ref, d_ref, w_ref, b_ref, o_ref, *, nb, tq):
    gw = GROUP_W
    rows = []
    for b in range(nb):
        y_b = jnp.concatenate(
            [ys_ref[0, half, pl.ds(b, tq, stride=nb), :] + ys_ref[1, half, pl.ds(b, tq, stride=nb), :]
             for half in range(gw // 128)], axis=1)
        rows.append(y_b + u_ref[:, gw * b:gw * (b + 1)] * d_ref[...])
    y = jnp.concatenate(rows, axis=0)
    g = y * (0.5 * (1.0 + jnp.tanh(math.sqrt(2.0 / math.pi) * (y + 0.044715 * (y * y * y)))))
    z = _dotf(g.astype(BF16), w_ref[...]) + b_ref[...]
    out = (g * _sigmoid(z)).astype(BF16)
    for b in range(nb):
        o_ref[b] = out[tq * b:tq * (b + 1)]


def _glu_call(ys, u_t, dsk, w, bias, nb):
    t = u_t.shape[0]
    tq = ROW_TILE // nb
    gw = GROUP_W
    const = lambda ti: (0, 0)
    return pl.pallas_call(
        functools.partial(_glu_kernel, nb=nb, tq=tq),
        grid=(t // tq,),
        in_specs=[pl.BlockSpec((2, gw // 128, tq * nb, 128), lambda ti: (0, 0, ti, 0)),
                  pl.BlockSpec((tq, nb * gw), lambda ti: (ti, 0)),
                  pl.BlockSpec(dsk.shape, const), pl.BlockSpec(w.shape, const),
                  pl.BlockSpec(bias.shape, const)],
        out_specs=pl.BlockSpec((nb, tq, gw), lambda ti: (0, ti, 0)),
        out_shape=jax.ShapeDtypeStruct((nb, t, gw), BF16),
        compiler_params=_params(("parallel",), VMEM_LIMIT),
        name="s5_glu",
    )(ys, u_t, dsk, w, bias)


def _out_kernel(ya_ref, yb_ref, yc_ref, yd_ref, w_ref, xa_ref, xb_ref, g1_ref, sh2_ref, sc2_ref,
                nw1_ref, nw2_ref, wr_ref, br_ref, xn_ref, f_ref, te_ref, tg_ref, *, n_a_blk, blk0):
    o = (_dotf(ya_ref[0], w_ref[0:256]) + _dotf(yb_ref[0], w_ref[256:512])
         + _dotf(yc_ref[0], w_ref[512:768]) + _dotf(yd_ref[0], w_ref[768:1024]))
    x = jnp.where(pl.program_id(1) + blk0 < n_a_blk, xa_ref[0], xb_ref[0])
    xn = x + g1_ref[0] * _rms(o, nw1_ref[...])
    xn_ref[0] = xn
    f = _rms(xn, nw2_ref[...]) * (1.0 + sc2_ref[0]) + sh2_ref[0]
    _store_token_tiles(f_ref, f)
    lg = _dot3(f, wr_ref[...]) + br_ref[...]
    tm = lg.shape[0]
    lane = lax.broadcasted_iota(I32, (tm, 128), 1).astype(F32)
    tops, idxs = [], []
    for _ in range(TOP_K):
        mx = jnp.max(lg, axis=-1, keepdims=True)
        idx = jnp.min(jnp.where(lg == mx, lane, 128.0), axis=-1, keepdims=True)
        tops.append(mx)
        idxs.append(idx.astype(I32))
        lg = jnp.where(lane == idx, NEG_INF, lg)
    ex = [jnp.exp(tv - tops[0]) for tv in tops]
    inv = 1.0 / (ex[0] + ex[1] + ex[2] + ex[3])
    l8 = lax.broadcasted_iota(I32, (tm, 8), 1)
    te = jnp.zeros((tm, 8), I32)
    tg = jnp.zeros((tm, 8), F32)
    for kk in range(TOP_K):
        te = jnp.where(l8 == kk, idxs[kk], te)
        tg = jnp.where(l8 == kk, ex[kk] * inv, tg)
    te_ref[0] = te
    tg_ref[0] = tg


def _out_call(ya, yb, yc, yd, w_out, xa, xb, mod_l, nw1, nw2, wr, br, n_ctx_blk, blk0):
    b, _, d = xa.shape
    t = xa.shape[1] + (0 if xb is None else xb.shape[1])
    tm = ROW_TILE
    nblk = t // tm - blk0
    t_out = nblk * tm
    spec_a, spec_b, n_a_blk = _stream_specs(xa, xb, blk0)

    def mrow(bi, ti):
        return jnp.where(ti + blk0 < n_ctx_blk, b, bi)

    row3 = lambda bi, ti: (bi, ti + blk0, 0)
    out3 = lambda bi, ti: (bi, ti, 0)
    const = lambda bi, ti: (0, 0)
    modspec = lambda j: pl.BlockSpec((1, 1, d), lambda bi, ti: (mrow(bi, ti), 0, j))
    yspec = lambda y: pl.BlockSpec((1, tm, GROUP_W), row3 if y.shape[1] == t else out3)
    return pl.pallas_call(
        functools.partial(_out_kernel, n_a_blk=n_a_blk, blk0=blk0),
        grid=(b, nblk),
        in_specs=[yspec(ya), yspec(yb), yspec(yc), yspec(yd), pl.BlockSpec((d, d), const),
                  spec_a, spec_b, modspec(2), modspec(3), modspec(4),
                  pl.BlockSpec((1, d), const), pl.BlockSpec((1, d), const),
                  pl.BlockSpec((d, 128), const), pl.BlockSpec((1, 128), const)],
        out_specs=[pl.BlockSpec((1, tm, d), out3),
                   pl.BlockSpec((tm * d // 128, 128), lambda bi, ti: (bi * nblk + ti, 0)),
                   pl.BlockSpec((1, tm, 8), out3), pl.BlockSpec((1, tm, 8), out3)],
        out_shape=[jax.ShapeDtypeStruct((b, t_out, d), F32),
                   jax.ShapeDtypeStruct((b * t_out * d // 128, 128), F32),
                   jax.ShapeDtypeStruct((b, t_out, 8), I32),
                   jax.ShapeDtypeStruct((b, t_out, 8), F32)],
        compiler_params=_params(("parallel", "parallel"), VMEM_LIMIT),
        name="out_proj_router",
    )(ya, yb, yc, yd, w_out, xa, xa if xb is None else xb, mod_l, mod_l, mod_l, nw1, nw2, wr, br)


def _rank_kernel(te_ref, rank_ref, cnt_ref, carry_ref):
    i = pl.program_id(0)

    @pl.when(i == 0)
    def _():
        carry_ref[...] = jnp.zeros_like(carry_ref)

    te = te_ref[...]
    tb = te.shape[0]
    lane = lax.broadcasted_iota(I32, (tb, 128), 1)
    l8 = lax.broadcasted_iota(I32, (tb, 8), 1)
    below = (lax.broadcasted_iota(I32, (tb, tb), 0)
             > lax.broadcasted_iota(I32, (tb, tb), 1))
    lstrict = jnp.where(below, 1.0, 0.0).astype(BF16)
    base = carry_ref[...]
    out = jnp.zeros((tb, 8), I32)
    for k in range(TOP_K):
        oh = jnp.where(lane == te[:, k:k + 1], 1.0, 0.0)
        before = _dotf(lstrict, oh.astype(BF16)) + base
        rank_k = jnp.sum(oh * before, axis=-1, keepdims=True)
        out = jnp.where(l8 == k, rank_k.astype(I32), out)
        base = base + jnp.sum(oh, axis=0, keepdims=True)
    rank_ref[...] = out
    carry_ref[...] = base
    cnt_ref[...] = base.astype(I32)


def _rank_call(te):
    n_tok = te.shape[0]
    tb = RANK_TILE
    return pl.pallas_call(
        _rank_kernel,
        grid=(n_tok // tb,),
        in_specs=[pl.BlockSpec((tb, 8), lambda i: (i, 0))],
        out_specs=[pl.BlockSpec((tb, 8), lambda i: (i, 0)), pl.BlockSpec((1, 128), lambda i: (0, 0))],
        out_shape=[jax.ShapeDtypeStruct((n_tok, 8), I32), jax.ShapeDtypeStruct((1, 128), I32)],
        scratch_shapes=[pltpu.VMEM((1, 128), F32)],
        compiler_params=_params(("arbitrary",)),
        name="moe_rank",
    )(te)


def _slot_kernel(tbl_ref, te_ref, rank_ref, slot_ref):
    te = te_ref[...]
    start = jnp.zeros(te.shape, I32)
    for e in range(N_EXPERTS):
        start = jnp.where(te == e, tbl_ref[e], start)
    slot_ref[...] = start + rank_ref[...]


def _slot_call(tbl, te, rank):
    n_tok = te.shape[0]
    tb = RANK_TILE
    spec = pl.BlockSpec((tb, 8), lambda i, tbl: (i, 0))
    return pl.pallas_call(
        _slot_kernel,
        grid_spec=pltpu.PrefetchScalarGridSpec(num_scalar_prefetch=1, grid=(n_tok // tb,),
                                               in_specs=[spec, spec], out_specs=spec),
        out_shape=jax.ShapeDtypeStruct((n_tok, 8), I32),
        compiler_params=_params(("parallel",)),
        name="moe_slots",
    )(tbl, te, rank)


def _slot_rows(slot_ref, r0, nt):
    return [pl.multiple_of(slot_ref[0, 0, (r0 + u) * 8 + k] * nt, nt)
            for u in range(DMA_UNROLL) for k in range(TOP_K)]


def _dispatch_kernel(tbl_ref, slot_ref, f_ref, xs_hbm, zbuf, sem, zsem, *, tb, tm, nt):
    i = pl.program_id(0)

    @pl.when(i == 0)
    def _():
        zbuf[...] = jnp.zeros_like(zbuf)
        fills = [pltpu.make_async_copy(
            zbuf, xs_hbm.at[pl.ds(pl.multiple_of(tbl_ref[N_EXPERTS + e] * nt, nt), tm * nt)], zsem)
            for e in range(N_EXPERTS)]
        for fill in fills:
            fill.start()
        for fill in fills:
            fill.wait()

        def fill_unused(blk, carry):
            tail = pltpu.make_async_copy(
                zbuf, xs_hbm.at[pl.ds(pl.multiple_of(blk * (tm * nt), tm * nt), tm * nt)], zsem)
            tail.start()
            tail.wait()
            return carry
        lax.fori_loop(tbl_ref[2 * N_EXPERTS], xs_hbm.shape[0] // (tm * nt), fill_unused, 0)

    def body(g, carry):
        r0 = g * DMA_UNROLL
        dsts = _slot_rows(slot_ref, r0, nt)
        for u in range(DMA_UNROLL):
            src = f_ref.at[pl.ds(pl.multiple_of((r0 + u) * nt, nt), nt)]
            for k in range(TOP_K):
                pltpu.make_async_copy(src, xs_hbm.at[pl.ds(dsts[u * TOP_K + k], nt)],
                                      sem).start(priority=k % 2)
        return carry

    lax.fori_loop(0, tb // DMA_UNROLL, body, 0)
    for k in range(TOP_K):
        pltpu.make_async_copy(f_ref, xs_hbm.at[pl.ds(0, tb * nt)], sem).wait()


def _dispatch_call(tbl, slot3, f_tiles, n_slots, tm):
    n_blk, _, per = slot3.shape
    tb = per // 8
    nt = f_tiles.shape[0] // (n_blk * tb)
    grid_spec = pltpu.PrefetchScalarGridSpec(
        num_scalar_prefetch=1,
        grid=(n_blk,),
        in_specs=[pl.BlockSpec((1, 1, per), lambda i, tbl: (i, 0, 0), memory_space=pltpu.SMEM),
                  pl.BlockSpec((tb * nt, 128), lambda i, tbl: (i, 0))],
        out_specs=pl.BlockSpec(memory_space=pl.ANY),
        scratch_shapes=[pltpu.VMEM((tm * nt, 128), F32), pltpu.SemaphoreType.DMA(()),
                        pltpu.SemaphoreType.DMA(())])
    return pl.pallas_call(
        functools.partial(_dispatch_kernel, tb=tb, tm=tm, nt=nt),
        grid_spec=grid_spec,
        out_shape=jax.ShapeDtypeStruct(((n_slots + tm) * nt, 128), F32),
        compiler_params=_params(("arbitrary",), VMEM_LIMIT),
        name="moe_dispatch",
    )(tbl, slot3, f_tiles)


def _expert_kernel(be_ref, nu_ref, xs_ref, wgu_ref, bgu_ref, wd_ref, bd_ref, ys_ref,
                   wgu_s, wd_s, *, tm, d_ff):
    i = pl.program_id(0)
    nt = xs_ref.shape[0] // tm

    @pl.when(i < nu_ref[0])
    def _():
        @pl.when(jnp.logical_or(i == 0, be_ref[i] != be_ref[jnp.maximum(i - 1, 0)]))
        def _():
            for c in range(0, wgu_s.shape[0], 128):
                wgu_s[c:c + 128, :] = wgu_ref[c:c + 128, :].astype(BF16)
            for c in range(0, wd_s.shape[0], 128):
                wd_s[c:c + 128, :] = wd_ref[c:c + 128, :].astype(BF16)

        for r0 in range(0, tm, MOE_ROWS):
            x = jnp.concatenate(_load_token_tiles(xs_ref, MOE_ROWS, nt, base=r0 * nt),
                                axis=1).astype(BF16)
            gu = _dotf(x, wgu_s[...]) + bgu_ref[...]
            gate = jnp.minimum(gu[:, 0:d_ff], SWIGLU_LIMIT)
            up = jnp.clip(gu[:, d_ff:2 * d_ff], -SWIGLU_LIMIT, SWIGLU_LIMIT)
            act = (up + 1.0) * gate * _sigmoid(SWIGLU_ALPHA * gate)
            _store_token_tiles(ys_ref, _dotf(act.astype(BF16), wd_s[...]) + bd_ref[...],
                               base=r0 * nt)

    @pl.when(i >= nu_ref[0])
    def _():
        ys_ref[...] = jnp.zeros_like(ys_ref)


def _expert_call(block_expert, n_used, xs_tiles, wgu, bgu, wd, bd, layer, n_blocks, tm):
    d, two_ff = wgu.shape[2:]
    d_ff = two_ff // 2
    nt = d // 128
    ex = lambda i, be, nu: (layer, be[i], 0, 0)
    grid_spec = pltpu.PrefetchScalarGridSpec(
        num_scalar_prefetch=2,
        grid=(n_blocks,),
        in_specs=[pl.BlockSpec((tm * nt, 128), lambda i, be, nu: (jnp.minimum(i, nu[0] - 1), 0)),
                  pl.BlockSpec((None, None, d, two_ff), ex),
                  pl.BlockSpec((None, None, 1, two_ff), ex),
                  pl.BlockSpec((None, None, d_ff, d), ex),
                  pl.BlockSpec((None, None, 1, d), ex)],
        out_specs=pl.BlockSpec((tm * nt, 128), lambda i, be, nu: (i, 0)),
        scratch_shapes=[pltpu.VMEM((d, two_ff), BF16), pltpu.VMEM((d_ff, d), BF16)])
    return pl.pallas_call(
        functools.partial(_expert_kernel, tm=tm, d_ff=d_ff),
        grid_spec=grid_spec,
        out_shape=jax.ShapeDtypeStruct((n_blocks * tm * nt, 128), F32),
        compiler_params=_params(("arbitrary",), VMEM_LIMIT),
        name="moe_experts",
    )(block_expert, n_used, xs_tiles, wgu, bgu, wd, bd)


def _route(counts, n_tok, tm):
    n_blocks = -(-(n_tok * TOP_K + N_EXPERTS * (tm - 1)) // tm)
    padded = (counts + tm - 1) // tm * tm
    padded_end = jnp.cumsum(padded)
    group_start = padded_end - padded
    block_start = jnp.arange(n_blocks, dtype=I32) * tm
    block_expert = jnp.minimum(
        jnp.sum((padded_end[None, :] <= block_start[:, None]).astype(I32), axis=1), N_EXPERTS - 1)
    n_used = (padded_end[-1] // tm).astype(I32).reshape(1)
    table = jnp.concatenate([group_start, group_start + counts, n_used]).astype(I32)
    return table, block_expert, n_used, n_blocks


def _fin_kernel(scur_ref, snxt_ref, g_ref, x_ref, g2_ref, nw_ref, ys_hbm, out_ref, buf, sem,
                *, tb, nt):
    i = pl.program_id(0)
    s = i % 2

    def gather(slot_ref, ss):
        def body(g, carry):
            r0 = g * DMA_UNROLL
            srcs = _slot_rows(slot_ref, r0, nt)
            for u in range(DMA_UNROLL):
                for k in range(TOP_K):
                    dst = pl.multiple_of((k * tb + r0 + u) * nt, nt)
                    pltpu.make_async_copy(ys_hbm.at[pl.ds(srcs[u * TOP_K + k], nt)],
                                          buf.at[ss, pl.ds(dst, nt)],
                                          sem.at[ss]).start(priority=k % 2)
            return carry
        lax.fori_loop(0, tb // DMA_UNROLL, body, 0)

    @pl.when(i == 0)
    def _():
        gather(scur_ref, 0)

    @pl.when(i + 1 < pl.num_programs(0))
    def _():
        gather(snxt_ref, 1 - s)

    pltpu.make_async_copy(ys_hbm.at[pl.ds(0, TOP_K * tb * nt)], buf.at[s], sem.at[s]).wait()
    gates = g_ref[...]
    parts = []
    for j in range(nt):
        acc = None
        for k in range(TOP_K):
            v = buf[s, pl.ds(k * tb * nt + j, tb, stride=nt), :] * gates[:, k:k + 1]
            acc = v if acc is None else acc + v
        parts.append(acc)
    m = jnp.concatenate(parts, axis=1)
    out_ref[0] = x_ref[0] + g2_ref[0] * _rms(m, nw_ref[...])


def _fin_call(slot3, gates, ys_tiles, xn, mod_l, nw, n_ctx_blk, blk0):
    b, t, d = xn.shape
    tb = ROW_TILE
    nblk = t // tb
    nt = d // 128
    n_blk = b * nblk

    def mrow(i):
        return jnp.where(i % nblk + blk0 < n_ctx_blk, b, i // nblk)

    smem = functools.partial(pl.BlockSpec, memory_space=pltpu.SMEM)
    return pl.pallas_call(
        functools.partial(_fin_kernel, tb=tb, nt=nt),
        grid=(n_blk,),
        in_specs=[smem((1, 1, tb * 8), lambda i: (i, 0, 0)),
                  smem((1, 1, tb * 8), lambda i: (jnp.minimum(i + 1, n_blk - 1), 0, 0)),
                  pl.BlockSpec((tb, 8), lambda i: (i, 0)),
                  pl.BlockSpec((1, tb, d), lambda i: (i // nblk, i % nblk, 0)),
                  pl.BlockSpec((1, 1, d), lambda i: (mrow(i), 0, 5)),
                  pl.BlockSpec((1, d), lambda i: (0, 0)),
                  pl.BlockSpec(memory_space=pl.ANY)],
        out_specs=pl.BlockSpec((1, tb, d), lambda i: (i // nblk, i % nblk, 0)),
        out_shape=jax.ShapeDtypeStruct((b, t, d), F32),
        scratch_shapes=[pltpu.VMEM((2, TOP_K * tb * nt, 128), F32), pltpu.SemaphoreType.DMA((2,))],
        compiler_params=_params(("arbitrary",), VMEM_LIMIT),
        name="moe_combine",
    )(slot3, slot3, gates, xn, mod_l, nw, ys_tiles)


def _in_proj_columns():
    cols = np.full((IN_COLS,), -1, np.int64)
    for sec in range(2):
        for n in range(256):
            part, hm, j = n // 128, (n % 128) // 16, n % 16
            m, h = hm // 4, hm % 4
            cols[sec * 256 + n] = sec * 256 + h * 64 + m * 32 + part * 16 + j
    cols[512:768] = np.arange(512, 768)
    o, s = IN_DA, 768
    cols[o:o + 1024] = s + np.arange(1024)
    o, s = o + IN_ML, s + 1024
    cols[o:o + 16] = s + np.arange(16)
    o, s = o + IN_G, s + 16
    cols[o:o + MLA_Q_RANK] = s + np.arange(MLA_Q_RANK)
    cols[o + 256:o + 256 + MLA_KV_RANK] = s + MLA_Q_RANK + np.arange(MLA_KV_RANK)
    cols[o + 384 + 64:o + 384 + 96] = s + MLA_Q_RANK + MLA_KV_RANK + np.arange(MLA_ROPE)
    o, s = o + IN_MLA, s + MLA_Q_RANK + MLA_KV_RANK + MLA_ROPE
    cols[o:o + 256] = s + np.arange(256)
    return cols


def _take_cols(w, cols):
    valid = jnp.asarray(cols >= 0)
    return jnp.where(valid, jnp.take(w, jnp.asarray(np.maximum(cols, 0)), axis=-1), 0.0)


def _rope_tables(n_ctx, n_lat):
    pos = jnp.arange(n_lat)
    inv = ROPE_THETA ** (-jnp.arange(8, dtype=F32) / 8)
    ang = jnp.concatenate([(pos // GRID_W)[:, None] * inv, (pos % GRID_W)[:, None] * inv], axis=-1)
    cos = jnp.concatenate([jnp.ones((n_ctx, 16), F32), jnp.cos(ang)], axis=0)
    sin = jnp.concatenate([jnp.zeros((n_ctx, 16), F32), jnp.sin(ang)], axis=0)
    t = n_ctx + n_lat
    cos_da, sin_da = jnp.tile(cos, (1, 8)), jnp.tile(sin, (1, 8))
    one, zero = jnp.ones((t, 64), F32), jnp.zeros((t, 64), F32)
    z16, z32 = jnp.zeros((t, 16), F32), jnp.zeros((t, 32), F32)
    cm = jnp.concatenate([one, cos, cos, jnp.ones((t, 32), F32)], axis=1)
    sa = jnp.concatenate([zero, -sin, z16, z32], axis=1)
    sb = jnp.concatenate([zero, z16, sin, z32], axis=1)
    return cos_da, sin_da, cm, sa, sb


def _mla_weights(w_uq, w_ukv):
    hd = MLA_NOPE + MLA_ROPE
    wq = jnp.pad(w_uq.reshape(MLA_Q_RANK, MLA_HEADS, hd),
                 ((0, 256 - MLA_Q_RANK), (0, 0), (0, 128 - hd))).reshape(256, 128 * MLA_HEADS)
    kv = w_ukv.reshape(MLA_KV_RANK, MLA_HEADS, -1)
    wk = jnp.pad(kv[:, :, :MLA_NOPE], ((0, 0), (0, 0), (0, 128 - MLA_NOPE)))
    wkv = jnp.concatenate([wk.reshape(MLA_KV_RANK, -1), kv[:, :, MLA_NOPE:].reshape(MLA_KV_RANK, -1)],
                          axis=1)
    return wq.astype(BF16), wkv.astype(BF16)


def _s5_layout(a_re, a_im, log_step, b_re, b_im, c_re, c_im):
    ns = S5_NGROUPS * S5_STATE
    eye = jnp.eye(S5_NGROUPS, dtype=F32)
    are = a_re.reshape(2, 1, ns)
    aim = a_im.reshape(2, 1, ns)
    ls = jnp.repeat(log_step, S5_STATE, axis=-1).reshape(2, 1, ns)
    bd_b = lambda w: jnp.einsum("dgph,gk->dghkp", w, eye).reshape(2, GROUP_W, ns)
    bd_c = lambda w: jnp.einsum("dghp,gk->dgpkh", w, eye).reshape(2, ns, GROUP_W)
    return are, aim, ls, bd_b(b_re), bd_b(b_im), bd_c(c_re), bd_c(c_im)


def kernel(x, c, ctx, c_ctx, w_mod, b_mod, norm_w, w_in, w_out, da_lambda, da_subln, ml_conv_w,
           ml_conv_b, ml_gate_b, ml_norm, mla_q_norm, mla_w_uq, mla_kv_norm, mla_w_ukv, s5_a_re,
           s5_a_im, s5_log_step, s5_b_re, s5_b_im, s5_c_re, s5_c_im, s5_d, s5_w_glu, s5_b_glu,
           moe_w_router, moe_b_router, moe_w_gate_up, moe_b_gate_up, moe_w_down, moe_b_down):
    bsz, n_lat, d = x.shape
    n_ctx = ctx.shape[1]
    t = n_ctx + n_lat
    depth = w_mod.shape[0]
    tm = ROW_TILE
    assert n_ctx % tm == 0 and n_lat % tm == 0 and bsz % 8 == 0 and bsz < 16
    assert n_ctx % ML_CHUNK == 0 and n_ctx % S5_CHUNK == 0
    n_ctx_blk = n_ctx // tm

    cc = jnp.pad(jnp.concatenate([c, c_ctx[None]], axis=0), ((0, 15 - bsz), (0, 0)))
    mod = _mod_call(cc, w_mod, b_mod)
    cos_da, sin_da, cm, sa, sb = _rope_tables(n_ctx, n_lat)
    in_cols = _in_proj_columns()
    xa, xb = ctx, x

    for l in range(depth):
        last = l == depth - 1
        lambda_init = 0.8 - 0.6 * math.exp(-0.3 * l)
        mod_l = mod[l].reshape(16, 1, 6 * d)
        w_in_p = _take_cols(w_in[l], in_cols).astype(BF16)
        p_da, p_ml, p_g, p_mla, u_t = _in_call(xa, xb, mod_l, norm_w[l, 0].reshape(1, d), w_in_p,
                                               cos_da, sin_da, n_ctx_blk)

        da_extra = [da_lambda[l], jnp.tile(da_subln[l], 4).reshape(1, GROUP_W)]
        da_kern = functools.partial(_da_kernel, lambda_init=lambda_init)
        da_kw = dict(q_col=0, k_col=1, v_col=2, q_w=GROUP_W)
        ya = _attn_call(da_kern, "diff_attn", p_da, p_da, p_da, da_extra, q_blk0=n_ctx_blk,
                        n_q_blk=n_lat // tm, n_keys=t, **da_kw)
        wq, wkv = _mla_weights(mla_w_uq[l], mla_w_ukv[l])
        qn = jnp.pad(mla_q_norm[l], (0, 256 - MLA_Q_RANK)).reshape(1, 256)
        q_mla, k_mla, v_mla = _mla_prep_call(p_mla, qn, wq, mla_kv_norm[l].reshape(1, -1), wkv,
                                             cm, sa, sb)
        mla_kw = dict(q_col=0, k_col=0, v_col=0, q_w=512)
        yc = _attn_call(_mla_attn_kernel, "mla_attn", q_mla, k_mla, v_mla, [], q_blk0=n_ctx_blk,
                        n_q_blk=n_lat // tm, n_keys=t, **mla_kw)
        if not last:
            ya_c = _attn_call(da_kern, "diff_attn_ctx", p_da, p_da, p_da, da_extra, q_blk0=0,
                              n_q_blk=n_ctx_blk, n_keys=n_ctx, **da_kw)
            yc_c = _attn_call(_mla_attn_kernel, "mla_attn_ctx", q_mla, k_mla, v_mla, [], q_blk0=0,
                              n_q_blk=n_ctx_blk, n_keys=n_ctx, **mla_kw)
            ya = jnp.concatenate([ya_c, ya], axis=1)
            yc = jnp.concatenate([yc_c, yc], axis=1)

        gb = jnp.pad(ml_gate_b[l], (0, 128 - 16)).reshape(1, 128)
        yb = _ml_call(p_ml, p_g, ml_conv_w[l], ml_conv_b[l].reshape(1, -1), gb,
                      ml_norm[l].reshape(1, -1), n_ctx)

        s5p = _s5_layout(s5_a_re[l], s5_a_im[l], s5_log_step[l], s5_b_re[l], s5_b_im[l],
                         s5_c_re[l], s5_c_im[l])
        ys = _s5_call(u_t.reshape(t * bsz, GROUP_W), *s5p, n_ctx, bsz)
        yd = _glu_call(ys, u_t, s5_d[l].reshape(1, -1),
                       s5_w_glu[l].astype(BF16), s5_b_glu[l].reshape(1, -1), bsz)

        blk0 = n_ctx_blk if last else 0
        wr = jnp.pad(moe_w_router[l], ((0, 0), (0, 128 - N_EXPERTS)))
        br = jnp.pad(moe_b_router[l], (0, 128 - N_EXPERTS), constant_values=-1e30).reshape(1, 128)
        xn, f, te, tg = _out_call(ya, yb, yc, yd, w_out[l].astype(BF16), xa, xb, mod_l,
                                  norm_w[l, 1].reshape(1, d), norm_w[l, 2].reshape(1, d), wr, br,
                                  n_ctx_blk, blk0)

        t_moe = t - blk0 * tm
        n_tok = bsz * t_moe
        te = te.reshape(n_tok, 8)
        rank, cnt = _rank_call(te)
        tbl, be, nu, n_blocks = _route(cnt[0, :N_EXPERTS], n_tok, MOE_TILE)
        slot3 = _slot_call(tbl, te, rank).reshape(n_tok // tm, 1, tm * 8)
        xs_tiles = _dispatch_call(tbl, slot3, f, n_blocks * MOE_TILE, MOE_TILE)
        ys_tiles = _expert_call(be, nu, xs_tiles, moe_w_gate_up,
                                moe_b_gate_up.reshape(depth, N_EXPERTS, 1, -1), moe_w_down,
                                moe_b_down.reshape(depth, N_EXPERTS, 1, -1), l, n_blocks, MOE_TILE)
        xa, xb = _fin_call(slot3, tg.reshape(n_tok, 8), ys_tiles, xn, mod_l,
                           norm_w[l, 3].reshape(1, d), n_ctx_blk, blk0), None
    return xa
```

```python
import functools
import math

import numpy as np
import jax
import jax.numpy as jnp
from jax import lax
from jax.experimental import pallas as pl
from jax.experimental.pallas import tpu as pltpu

F32, BF16, I32 = jnp.float32, jnp.bfloat16, jnp.int32
NORM_EPS = 1e-6
GRID_W = 64
ROPE_THETA = 10000.0
GROUP_W = 256
DA_QK = 32
ML_CHUNK = 128
ML_ROWS = 2
MLA_HEADS, MLA_NOPE, MLA_ROPE, MLA_Q_RANK, MLA_KV_RANK = 4, 64, 32, 192, 128
S5_NGROUPS, S5_GROUP, S5_STATE = 16, 16, 64
N_EXPERTS, TOP_K = 32, 4
SWIGLU_ALPHA, SWIGLU_LIMIT = 1.702, 7.0
NEG_INF = float("-inf")

ROW_TILE = 256
MOE_TILE = 256
MOE_ROWS = 256
S5_CHUNK = 64
RANK_TILE = 512
DMA_UNROLL = 8
VMEM_LIMIT = 56 * 1024 * 1024

IN_DA, IN_ML, IN_G, IN_MLA, IN_S5 = 768, 1024, 128, 512, 256
IN_COLS = IN_DA + IN_ML + IN_G + IN_MLA + IN_S5


def _params(sem, vmem=None):
    return pltpu.CompilerParams(dimension_semantics=sem, vmem_limit_bytes=vmem)


def _dotf(a, b):
    return jnp.dot(a, b, preferred_element_type=F32)


def _dot_nt(a, b):
    return lax.dot_general(a, b, (((1,), (1,)), ((), ())), preferred_element_type=F32)


def _split2(a):
    hi = a.astype(BF16)
    lo = (a - hi.astype(F32)).astype(BF16)
    return hi, lo


def _dot3(a, b):
    ah, al = _split2(a)
    bh, bl = _split2(b)
    return _dotf(ah, bh) + _dotf(ah, bl) + _dotf(al, bh)


def _dot_exact_rhs(a, rhs_b):
    a1 = a.astype(BF16)
    r1 = a - a1.astype(F32)
    a2 = r1.astype(BF16)
    a3 = (r1 - a2.astype(F32)).astype(BF16)
    return _dotf(a1, rhs_b) + _dotf(a2, rhs_b) + _dotf(a3, rhs_b)


def _rms(x, w):
    ms = jnp.mean(x * x, axis=-1, keepdims=True)
    return x * lax.rsqrt(ms + NORM_EPS) * w


def _head_rms(a, width):
    n = a.shape[-1]
    sh = int(math.log2(width))
    r = lax.broadcasted_iota(I32, (n, n), 0) >> sh
    c = lax.broadcasted_iota(I32, (n, n), 1) >> sh
    g = jnp.where(r == c, 1.0 / width, 0.0).astype(BF16)
    hi, lo = _split2(a * a)
    ms = _dotf(hi, g) + _dotf(lo, g)
    return a * lax.rsqrt(ms + NORM_EPS)


def _store_token_tiles(ref, val, base=0):
    tm, d = val.shape
    nt = d // 128
    for j in range(nt):
        ref[pl.ds(base + j, tm, stride=nt), :] = val[:, 128 * j:128 * (j + 1)]


def _load_token_tiles(ref, tm, nt, base=0, lead=()):
    parts = [ref[lead + (pl.ds(base + j, tm, stride=nt), slice(None))] for j in range(nt)]
    return parts


def _sigmoid(x):
    return jax.nn.sigmoid(x)


def _log_sigmoid(x):
    return jnp.minimum(x, 0.0) - jnp.log(1.0 + jnp.exp(-jnp.abs(x)))


def _mod_kernel(c_ref, w_ref, b_ref, o_ref):
    c = c_ref[...]
    o_ref[0] = _dot3(c * _sigmoid(c), w_ref[0]) + b_ref[0]


def _mod_call(cc, w_mod, b_mod):
    n_layers, d, n = w_mod.shape
    tn = 1536
    return pl.pallas_call(
        _mod_kernel,
        grid=(n_layers, n // tn),
        in_specs=[pl.BlockSpec((16, d), lambda l, j: (0, 0)),
                  pl.BlockSpec((1, d, tn), lambda l, j: (l, 0, j)),
                  pl.BlockSpec((1, 1, tn), lambda l, j: (l, 0, j))],
        out_specs=pl.BlockSpec((1, 16, tn), lambda l, j: (l, 0, j)),
        out_shape=jax.ShapeDtypeStruct((n_layers, 16, n), F32),
        compiler_params=_params(("parallel", "parallel"), VMEM_LIMIT),
        name="mod_vectors",
    )(cc, w_mod, b_mod.reshape(n_layers, 1, n))


def _in_kernel(xa_ref, xb_ref, sh_ref, sc_ref, nw_ref, w_ref, c_ref, s_ref,
               da_ref, ml_ref, g_ref, mla_ref, s5_ref, *, qscale, n_a_blk):
    x = jnp.where(pl.program_id(1) < n_a_blk, xa_ref[0], xb_ref[0])
    h = _rms(x, nw_ref[...]) * (1.0 + sc_ref[0]) + sh_ref[0]
    hb = h.astype(BF16)
    da = _dotf(hb, w_ref[:, 0:IN_DA])
    c = c_ref[...]
    s = s_ref[...]
    q1, q2, k1, k2 = da[:, 0:128], da[:, 128:256], da[:, 256:384], da[:, 384:512]
    da_ref[0, :, 0:128] = ((q1 * c - q2 * s) * qscale).astype(BF16)
    da_ref[0, :, 128:256] = ((q2 * c + q1 * s) * qscale).astype(BF16)
    da_ref[0, :, 256:384] = (k1 * c - k2 * s).astype(BF16)
    da_ref[0, :, 384:512] = (k2 * c + k1 * s).astype(BF16)
    da_ref[0, :, 512:768] = da[:, 512:768].astype(BF16)
    o = IN_DA
    ml_ref[0] = _dotf(hb, w_ref[:, o:o + IN_ML]).astype(BF16)
    o += IN_ML
    g_ref[0] = _dotf(hb, w_ref[:, o:o + IN_G])
    o += IN_G
    mla_ref[0] = _dotf(hb, w_ref[:, o:o + IN_MLA])
    o += IN_MLA
    s5_ref[...] = _dotf(hb, w_ref[:, o:o + IN_S5])


def _stream_specs(xa, xb, blk0):
    tm = ROW_TILE
    d = xa.shape[2]
    n_a = xa.shape[1] // tm
    spec_a = pl.BlockSpec((1, tm, d), lambda bi, ti: (bi, jnp.minimum(ti + blk0, n_a - 1), 0))
    spec_b = pl.BlockSpec((1, tm, d), lambda bi, ti: (bi, jnp.maximum(ti + blk0 - n_a, 0), 0))
    return spec_a, spec_b, n_a


def _in_call(xa, xb, mod_l, nw, w_in_p, cos_da, sin_da, n_ctx_blk):
    b, _, d = xa.shape
    t = xa.shape[1] + (0 if xb is None else xb.shape[1])
    tm = ROW_TILE
    spec_a, spec_b, n_a_blk = _stream_specs(xa, xb, 0)

    def mrow(bi, ti):
        return jnp.where(ti < n_ctx_blk, b, bi)

    row3 = lambda bi, ti: (bi, ti, 0)
    return pl.pallas_call(
        functools.partial(_in_kernel, qscale=DA_QK ** -0.5 * LOG2_E, n_a_blk=n_a_blk),
        grid=(b, t // tm),
        in_specs=[spec_a, spec_b,
                  pl.BlockSpec((1, 1, d), lambda bi, ti: (mrow(bi, ti), 0, 0)),
                  pl.BlockSpec((1, 1, d), lambda bi, ti: (mrow(bi, ti), 0, 1)),
                  pl.BlockSpec((1, d), lambda bi, ti: (0, 0)),
                  pl.BlockSpec((d, IN_COLS), lambda bi, ti: (0, 0)),
                  pl.BlockSpec((tm, 128), lambda bi, ti: (ti, 0)),
                  pl.BlockSpec((tm, 128), lambda bi, ti: (ti, 0))],
        out_specs=[pl.BlockSpec((1, tm, IN_DA), row3),
                   pl.BlockSpec((1, tm, IN_ML), row3),
                   pl.BlockSpec((1, tm, IN_G), row3),
                   pl.BlockSpec((1, tm, IN_MLA), row3),
                   pl.BlockSpec((tm, IN_S5), lambda bi, ti: (ti, bi))],
        out_shape=[jax.ShapeDtypeStruct((b, t, IN_DA), BF16),
                   jax.ShapeDtypeStruct((b, t, IN_ML), BF16),
                   jax.ShapeDtypeStruct((b, t, IN_G), F32),
                   jax.ShapeDtypeStruct((b, t, IN_MLA), F32),
                   jax.ShapeDtypeStruct((t, b * IN_S5), F32)],
        compiler_params=_params(("parallel", "parallel"), VMEM_LIMIT),
        name="in_proj",
    )(xa, xa if xb is None else xb, mod_l, mod_l, nw, w_in_p, cos_da, sin_da)


LOG2_E = 1.0 / math.log(2.0)


def _softmax_rows(s):
    mx = jnp.max(s, axis=-1, keepdims=True)
    p = jnp.exp2(s - mx)
    return p, jnp.sum(p, axis=-1, keepdims=True)


def _da_kernel(q_ref, k_ref, v_ref, lam_ref, sub_ref, o_ref, *, lambda_init):
    q = q_ref[0]
    k = k_ref[0]
    v = v_ref[0]
    lp = lam_ref[...]
    lam = (jnp.exp(jnp.sum(lp[0:1] * lp[1:2], axis=-1, keepdims=True))
           - jnp.exp(jnp.sum(lp[2:3] * lp[3:4], axis=-1, keepdims=True)) + lambda_init)
    lane = lax.broadcasted_iota(I32, (1, GROUP_W), 1)
    grp = (lane & 127) >> 4
    head = lane >> 6
    acc = jnp.zeros((q.shape[0], GROUP_W), F32)
    for h in range(4):
        ps, rs = [], []
        for m in range(2):
            qm = jnp.where(grp == m * 4 + h, q, jnp.zeros_like(q))
            p, l = _softmax_rows(_dot_nt(qm, k))
            ps.append(p.astype(BF16))
            rs.append(((1.0 if m == 0 else lam) / l).astype(BF16))
        w = ps[0] * rs[0] - ps[1] * rs[1]
        vm = jnp.where(head == h, v, jnp.zeros_like(v))
        acc = acc + _dotf(w, vm)
    y = _head_rms(acc, 64) * sub_ref[...] * (1.0 - lambda_init)
    o_ref[0] = y.astype(BF16)


def _mla_attn_kernel(q_ref, k_ref, v_ref, o_ref):
    q = q_ref[0]
    k = k_ref[0]
    v = v_ref[0]
    head = lax.broadcasted_iota(I32, (1, GROUP_W), 1) >> 6
    acc = jnp.zeros((q.shape[0], GROUP_W), F32)
    for h in range(MLA_HEADS):
        sl = slice(128 * h, 128 * (h + 1))
        p, l = _softmax_rows(_dot_nt(q[:, sl], k[:, sl]))
        vm = jnp.where(head == h, v, jnp.zeros_like(v))
        acc = acc + _dotf(p.astype(BF16), vm) * (1.0 / l)
    o_ref[0] = acc.astype(BF16)


def _attn_call(kernel, name, q_arr, k_arr, v_arr, extra, *, q_blk0, n_q_blk, n_keys,
               q_col, k_col, v_col, q_w):
    b = q_arr.shape[0]
    tq = ROW_TILE
    in_specs = [pl.BlockSpec((1, tq, q_w), lambda bi, qi: (bi, qi + q_blk0, q_col)),
                pl.BlockSpec((1, n_keys, q_w), lambda bi, qi: (bi, 0, k_col)),
                pl.BlockSpec((1, n_keys, GROUP_W), lambda bi, qi: (bi, 0, v_col))]
    in_specs += [pl.BlockSpec(e.shape, lambda bi, qi: (0, 0)) for e in extra]
    return pl.pallas_call(
        kernel,
        grid=(b, n_q_blk),
        in_specs=in_specs,
        out_specs=pl.BlockSpec((1, tq, GROUP_W), lambda bi, qi: (bi, qi, 0)),
        out_shape=jax.ShapeDtypeStruct((b, n_q_blk * tq, GROUP_W), BF16),
        compiler_params=_params(("parallel", "arbitrary"), VMEM_LIMIT),
        name=name,
    )(q_arr, k_arr, v_arr, *extra)


def _mla_prep_kernel(p_ref, qn_ref, wq_ref, kvn_ref, wkv_ref, c_ref, sa_ref, sb_ref,
                     q_ref, k_ref, v_ref, *, scale):
    p = p_ref[0]
    cq, ckv, kr = p[:, 0:256], p[:, 256:384], p[:, 384:512]
    msq = jnp.sum(cq * cq, axis=-1, keepdims=True) * (1.0 / MLA_Q_RANK)
    qn = (cq * lax.rsqrt(msq + NORM_EPS) * qn_ref[...]).astype(BF16)
    q = _dotf(qn, wq_ref[...])
    c = c_ref[...]
    sa = sa_ref[...]
    sb = sb_ref[...]

    def rope(a):
        return a * c + pltpu.roll(a, 112, 1) * sa + pltpu.roll(a, 16, 1) * sb

    for h in range(MLA_HEADS):
        sl = slice(128 * h, 128 * (h + 1))
        q_ref[0, :, sl] = (rope(q[:, sl]) * scale).astype(BF16)
    kvn = (_rms(ckv, kvn_ref[...])).astype(BF16)
    kv = _dotf(kvn, wkv_ref[...])
    krr = rope(kr)
    for h in range(MLA_HEADS):
        sl = slice(128 * h, 128 * (h + 1))
        k_ref[0, :, sl] = (kv[:, sl] + krr).astype(BF16)
    v_ref[0] = kv[:, 512:768].astype(BF16)


def _mla_prep_call(p_mla, qn, wq, kvn, wkv, cm, sa, sb):
    b, t, _ = p_mla.shape
    tm = ROW_TILE
    row3 = lambda bi, ti: (bi, ti, 0)
    const = lambda bi, ti: (0, 0)
    tab = lambda bi, ti: (ti, 0)
    return pl.pallas_call(
        functools.partial(_mla_prep_kernel, scale=(MLA_NOPE + MLA_ROPE) ** -0.5 * LOG2_E),
        grid=(b, t // tm),
        in_specs=[pl.BlockSpec((1, tm, IN_MLA), row3),
                  pl.BlockSpec(qn.shape, const), pl.BlockSpec(wq.shape, const),
                  pl.BlockSpec(kvn.shape, const), pl.BlockSpec(wkv.shape, const),
                  pl.BlockSpec((tm, 128), tab), pl.BlockSpec((tm, 128), tab),
                  pl.BlockSpec((tm, 128), tab)],
        out_specs=[pl.BlockSpec((1, tm, 512), row3), pl.BlockSpec((1, tm, 512), row3),
                   pl.BlockSpec((1, tm, GROUP_W), row3)],
        out_shape=[jax.ShapeDtypeStruct((b, t, 512), BF16),
                   jax.ShapeDtypeStruct((b, t, 512), BF16),
                   jax.ShapeDtypeStruct((b, t, GROUP_W), BF16)],
        compiler_params=_params(("parallel", "parallel"), VMEM_LIMIT),
        name="mla_prep",
    )(p_mla, qn, wq, kvn, wkv, cm, sa, sb)


def _ml_kernel(p_ref, g_ref, cw_ref, cb_ref, gb_ref, nw_ref, y_ref,
               qt_s, k_s, vt_s, ot_s, gt_s, hf_s, c_s, m_s, *, n_ctx_chunks, n_chunks):
    cl = ML_CHUNK
    t_total = n_chunks * cl
    nbr = p_ref.shape[0]
    row = lax.broadcasted_iota(I32, (cl, 1), 0)
    si = lax.broadcasted_iota(I32, (cl, cl), 0)
    ti = lax.broadcasted_iota(I32, (cl, cl), 1)
    lane128 = lax.broadcasted_iota(I32, (1, 128), 1)
    lane256 = lax.broadcasted_iota(I32, (1, 256), 1)
    row128 = lax.broadcasted_iota(I32, (128, 1), 0)
    hmask = [(lane256 >> 6) == h for h in range(4)]
    is_f = jnp.logical_and(((lane128 >> 2) & 1) == 1, lane128 < 16)
    r_nd = lax.broadcasted_iota(I32, (384, 512), 0)
    c_nd = lax.broadcasted_iota(I32, (384, 512), 1) >> 7
    nd_head = jnp.where(r_nd < 256, r_nd >> 6, r_nd - 256)
    nd_mask = nd_head == c_nd
    nd_ones = jnp.where(jnp.logical_and(nd_mask, r_nd >= 256), 1.0, 0.0).astype(BF16)
    r_st = lax.broadcasted_iota(I32, (384, 256), 0)
    st_head = jnp.where(r_st < 256, r_st >> 6, r_st - 256)
    st_mask = st_head == (lax.broadcasted_iota(I32, (384, 256), 1) >> 6)
    st_rowhead = jnp.where(lax.broadcasted_iota(I32, (384, 1), 0) < 256,
                           lax.broadcasted_iota(I32, (384, 1), 0) >> 6,
                           lax.broadcasted_iota(I32, (384, 1), 0) - 256)
    w0, w1, w2 = cw_ref[0:1], cw_ref[1:2], cw_ref[2:3]
    cb = cb_ref[...]
    gb = gb_ref[...]

    def prep_one(c, bb):
        s0 = pl.multiple_of(c * cl, cl)
        x = p_ref[bb, pl.ds(s0, cl), 0:512].astype(F32)
        sp = pl.multiple_of(jnp.maximum(s0 - 16, 0), 16)
        sn = pl.multiple_of(jnp.minimum(s0 + cl, t_total - 16), 16)
        has_prev = jnp.logical_and(c != 0, c != n_ctx_chunks).astype(F32)
        has_next = jnp.logical_and(c != n_ctx_chunks - 1, c != n_chunks - 1).astype(F32)
        prev_row = p_ref[bb, pl.ds(sp, 16), 0:512][15:16].astype(F32) * has_prev
        next_row = p_ref[bb, pl.ds(sn, 16), 0:512][0:1].astype(F32) * has_next
        xp = jnp.where(row == 0, prev_row, pltpu.roll(x, 1, 0))
        xn = jnp.where(row == cl - 1, next_row, pltpu.roll(x, cl - 1, 0))
        z = xp * w0 + x * w1 + xn * w2 + cb
        qk = z * _sigmoid(z)
        qt_s[bb, c] = qk[:, 0:256].T.astype(BF16)
        k_s[bb, c] = (qk[:, 256:512] * (64 ** -0.5)).astype(BF16)
        vt_s[bb, c] = p_ref[bb, pl.ds(s0, cl), 512:768].astype(F32).T
        ot_s[bb, c] = p_ref[bb, pl.ds(s0, cl), 768:1024].astype(F32).T
        g = g_ref[bb, pl.ds(s0, cl), :] + gb
        gt_s[bb, c] = jnp.where(is_f, _log_sigmoid(g), g).T[0:16]

    def prep_body(c, carry):
        for bb in range(nbr):
            prep_one(c, bb)
        return carry

    def rows_to_blocks(rows, n):
        return jnp.concatenate([jnp.broadcast_to(r, (n, 128)) for r in rows], axis=0)

    def head_rows(rows):
        out = jnp.zeros((128, 128), F32)
        for h in range(4):
            out = jnp.where(row128 == h, rows[h], out)
        return out

    def chunk(c, reverse, bb):
        g_t = gt_s[bb, c]
        tri = (si >= ti) if reverse else (si <= ti)
        bc_t = _dot_exact_rhs(g_t, jnp.where(tri, 1.0, 0.0).astype(BF16))
        qt = qt_s[bb, c]
        kb = k_s[bb, c]
        m_all = m_s[bb]
        off = 8 if reverse else 0
        a_rows, r_rows, tots, m_old = [], [], [], []
        for h in range(4):
            il, fl = off + h, off + 4 + h
            a_rows.append(bc_t[fl:fl + 1, :])
            r_rows.append(g_t[il:il + 1, :] - bc_t[fl:fl + 1, :])
            tots.append(bc_t[fl:fl + 1, 0:1] if reverse else bc_t[fl:fl + 1, cl - 1:cl])
            m_old.append(m_all[:, h:h + 1])
        r_all = jnp.concatenate(r_rows, axis=1)
        d_t = jnp.broadcast_to(r_all, (cl, 4 * cl)).T + rows_to_blocks(a_rows, cl)
        lw = jnp.where(jnp.concatenate([tri] * 4, axis=0), d_t, NEG_INF)
        mt, wi, emt = [], [], []
        for h in range(4):
            linter = a_rows[h] + m_old[h]
            mt_h = jnp.maximum(linter, jnp.max(lw[cl * h:cl * (h + 1)], axis=0, keepdims=True))
            mt.append(mt_h)
            wi.append(jnp.exp(linter - mt_h))
            emt.append(jnp.exp(-mt_h))
        kstack = jnp.concatenate([jnp.where(hmask[h], kb, jnp.zeros_like(kb)) for h in range(4)],
                                 axis=0)
        s_t = _dotf(kstack, qt)
        w_t = (s_t * jnp.exp(lw - rows_to_blocks(mt, cl))).astype(BF16)
        vt = vt_s[bb, c]
        vt4 = jnp.concatenate([vt.astype(BF16)] * 4, axis=1)
        lhs_nd = jnp.concatenate([vt4, jnp.zeros((128, 512), BF16)], axis=0)
        lhs_nd = jnp.where(nd_mask, lhs_nd, jnp.zeros_like(lhs_nd)) + nd_ones
        c_aug = c_s[bb]
        nd = _dotf(lhs_nd, w_t) + jnp.concatenate(
            [rows_to_blocks(wi, 64), head_rows(wi)], axis=0) * _dotf(c_aug.astype(BF16), qt)
        den = [nd[256 + h:257 + h, :] for h in range(4)]
        h_t = nd[0:256] / jnp.maximum(jnp.abs(rows_to_blocks(den, 64)), rows_to_blocks(emt, 64))
        wupd, decs, m_new_all = [], [], m_all
        for h in range(4):
            lupd = tots[h] + r_rows[h]
            m_new = jnp.maximum(tots[h] + m_old[h], jnp.max(lupd, axis=-1, keepdims=True))
            wupd.append(jnp.exp(lupd - m_new))
            decs.append(jnp.exp(tots[h] + m_old[h] - m_new))
            m_new_all = jnp.where(lane128 == h, m_new, m_new_all)
        lhs_u = jnp.concatenate([vt * rows_to_blocks(wupd, 64), head_rows(wupd)], axis=0)
        upd = _dotf(lhs_u.astype(BF16), kb)
        dec_col = jnp.zeros((384, 1), F32)
        for h in range(4):
            dec_col = jnp.where(st_rowhead == h, decs[h], dec_col)
        c_s[bb] = dec_col * c_aug + jnp.where(st_mask, upd, 0.0)
        m_s[bb] = m_new_all
        return h_t

    def fwd_body(i, carry):
        for bb in range(nbr):
            hf_s[bb, i] = chunk(i, False, bb)
        return carry

    def bwd_body(i, carry):
        c = jnp.where(i < n_ctx_chunks, n_ctx_chunks - 1 - i, n_chunks - 1 - (i - n_ctx_chunks))
        for bb in range(nbr):
            gated = _sigmoid(ot_s[bb, c]) * (hf_s[bb, c] + chunk(c, True, bb))
            ms = [jnp.mean(jnp.square(gated[64 * h:64 * (h + 1)]), axis=0, keepdims=True)
                  for h in range(4)]
            y_t = gated * lax.rsqrt(rows_to_blocks(ms, 64) + NORM_EPS) * nw_ref[...]
            y_ref[bb, pl.ds(pl.multiple_of(c * cl, cl), cl), :] = y_t.T.astype(BF16)
        return carry

    lax.fori_loop(0, n_chunks, prep_body, 0)
    c_s[...] = jnp.zeros_like(c_s)
    m_s[...] = jnp.zeros_like(m_s)
    lax.fori_loop(0, n_chunks, fwd_body, 0)
    c_s[...] = jnp.zeros_like(c_s)
    m_s[...] = jnp.zeros_like(m_s)
    lax.fori_loop(0, n_chunks, bwd_body, 0)


def _ml_call(p_ml, p_g, cw, cb, gb, nw, n_ctx):
    b, t, _ = p_ml.shape
    nc = t // ML_CHUNK
    nbr = ML_ROWS
    const = lambda bi: (0, 0)
    nw_col = jnp.broadcast_to(nw.reshape(GROUP_W, 1), (GROUP_W, 128))
    return pl.pallas_call(
        functools.partial(_ml_kernel, n_ctx_chunks=n_ctx // ML_CHUNK, n_chunks=nc),
        grid=(b // nbr,),
        in_specs=[pl.BlockSpec((nbr, t, IN_ML), lambda bi: (bi, 0, 0)),
                  pl.BlockSpec((nbr, t, IN_G), lambda bi: (bi, 0, 0)),
                  pl.BlockSpec(cw.shape, const), pl.BlockSpec(cb.shape, const),
                  pl.BlockSpec(gb.shape, const), pl.BlockSpec(nw_col.shape, const)],
        out_specs=pl.BlockSpec((nbr, t, GROUP_W), lambda bi: (bi, 0, 0)),
        out_shape=jax.ShapeDtypeStruct((b, t, GROUP_W), BF16),
        scratch_shapes=[pltpu.VMEM((nbr, nc, GROUP_W, 128), BF16),
                        pltpu.VMEM((nbr, nc, 128, GROUP_W), BF16),
                        pltpu.VMEM((nbr, nc, GROUP_W, 128), F32),
                        pltpu.VMEM((nbr, nc, GROUP_W, 128), F32),
                        pltpu.VMEM((nbr, nc, 16, 128), F32),
                        pltpu.VMEM((nbr, nc, GROUP_W, 128), F32),
                        pltpu.VMEM((nbr, 384, GROUP_W), F32), pltpu.VMEM((nbr, 1, 128), F32)],
        compiler_params=_params(("parallel",), VMEM_LIMIT),
        name="mlstm",
    )(p_ml, p_g, cw, cb, gb, nw_col)


def _s5_kernel(u_ref, are_ref, aim_ref, ls_ref, bre_ref, bim_ref, cre_ref, cim_ref, y_ref,
               ar_s, ai_s, bcat_s, ccat_s, st_s, bu_s, *, tc, nb):
    d = pl.program_id(0)
    i = pl.program_id(1)
    ns = S5_NGROUPS * S5_STATE

    @pl.when(i == 0)
    def _init():
        are = jnp.minimum(are_ref[0], -1e-4)
        aim = aim_ref[0]
        dt = jnp.exp(ls_ref[0])
        mag = jnp.exp(dt * are)
        abr = mag * jnp.cos(dt * aim)
        abi = mag * jnp.sin(dt * aim)
        inv = 1.0 / (are * are + aim * aim)
        fre = ((abr - 1.0) * are + abi * aim) * inv
        fim = (abi * are - (abr - 1.0) * aim) * inv
        bre = bre_ref[0]
        bim = bim_ref[0]
        bcat_s[:, 0:ns] = (bre * fre - bim * fim).astype(BF16)
        bcat_s[:, ns:2 * ns] = (bre * fim + bim * fre).astype(BF16)
        ccat_s[0:ns, :] = cre_ref[0].astype(BF16)
        ccat_s[ns:2 * ns, :] = (-cim_ref[0]).astype(BF16)
        ar_s[...] = jnp.broadcast_to(abr, (nb, ns))
        ai_s[...] = jnp.broadcast_to(abi, (nb, ns))
        st_s[...] = jnp.zeros_like(st_s)

    bu_s[...] = _dotf(u_ref[...].astype(BF16), bcat_s[...])
    ar = ar_s[...]
    ai = ai_s[...]

    def body(j, carry):
        xr, xi = carry
        t = j + d * (tc - 1 - 2 * j)
        r0 = pl.multiple_of(t * nb, nb)
        nr = ar * xr - ai * xi + bu_s[pl.ds(r0, nb), 0:ns]
        ni = ar * xi + ai * xr + bu_s[pl.ds(r0, nb), ns:2 * ns]
        bu_s[pl.ds(r0, nb), 0:ns] = nr
        bu_s[pl.ds(r0, nb), ns:2 * ns] = ni
        return nr, ni

    xr, xi = lax.fori_loop(0, tc, body, (st_s[0], st_s[1]), unroll=4)
    st_s[0] = xr
    st_s[1] = xi
    y = _dotf(bu_s[...].astype(BF16), ccat_s[...])
    for half in range(y.shape[1] // 128):
        y_ref[0, half] = y[:, 128 * half:128 * (half + 1)]


def _s5_call(u_tm, are, aim, ls, bre, bim, cre, cim, n_ctx, nb):
    rows, gw = u_tm.shape
    tc = S5_CHUNK
    n_chunks = rows // (tc * nb)
    n_ctx_chunks = n_ctx // tc
    ns = S5_NGROUPS * S5_STATE

    def chunk_of(d, i):
        rev = jnp.where(i < n_ctx_chunks, n_ctx_chunks - 1 - i, n_chunks - 1 - (i - n_ctx_chunks))
        return jnp.where(d == 0, i, rev)

    vec = pl.BlockSpec((1, 1, ns), lambda d, i: (d, 0, 0))
    return pl.pallas_call(
        functools.partial(_s5_kernel, tc=tc, nb=nb),
        grid=(2, n_chunks),
        in_specs=[pl.BlockSpec((tc * nb, gw), lambda d, i: (chunk_of(d, i), 0)),
                  vec, vec, vec,
                  pl.BlockSpec((1, gw, ns), lambda d, i: (d, 0, 0)),
                  pl.BlockSpec((1, gw, ns), lambda d, i: (d, 0, 0)),
                  pl.BlockSpec((1, ns, gw), lambda d, i: (d, 0, 0)),
                  pl.BlockSpec((1, ns, gw), lambda d, i: (d, 0, 0))],
        out_specs=pl.BlockSpec((1, gw // 128, tc * nb, 128),
                               lambda d, i: (d, 0, chunk_of(d, i), 0)),
        out_shape=jax.ShapeDtypeStruct((2, gw // 128, rows, 128), F32),
        scratch_shapes=[pltpu.VMEM((nb, ns), F32), pltpu.VMEM((nb, ns), F32),
                        pltpu.VMEM((gw, 2 * ns), BF16), pltpu.VMEM((2 * ns, gw), BF16),
                        pltpu.VMEM((2, nb, ns), F32), pltpu.VMEM((tc * nb, 2 * ns), F32)],
        compiler_params=_params(("arbitrary", "arbitrary"), VMEM_LIMIT),
        name="s5_scan",
    )(u_tm, are, aim, ls, bre, bim, cre, cim)


def _glu_kernel(ys_ref, u_ref, d_ref, w_ref, b_ref, o_ref, *, nb, tq):
    gw = GROUP_W
    rows = []
    for b in range(nb):
        y_b = jnp.concatenate(
            [ys_ref[0, half, pl.ds(b, tq, stride=nb), :] + ys_ref[1, half, pl.ds(b, tq, stride=nb), :]
             for half in range(gw // 128)], axis=1)
        rows.append(y_b + u_ref[:, gw * b:gw * (b + 1)] * d_ref[...])
    y = jnp.concatenate(rows, axis=0)
    g = y * (0.5 * (1.0 + jnp.tanh(math.sqrt(2.0 / math.pi) * (y + 0.044715 * (y * y * y)))))
    z = _dotf(g.astype(BF16), w_ref[...]) + b_ref[...]
    out = (g * _sigmoid(z)).astype(BF16)
    for b in range(nb):
        o_ref[b] = out[tq * b:tq * (b + 1)]


def _glu_call(ys, u_t, dsk, w, bias, nb):
    t = u_t.shape[0]
    tq = ROW_TILE // nb
    gw = GROUP_W
    const = lambda ti: (0, 0)
    return pl.pallas_call(
        functools.partial(_glu_kernel, nb=nb, tq=tq),
        grid=(t // tq,),
        in_specs=[pl.BlockSpec((2, gw // 128, tq * nb, 128), lambda ti: (0, 0, ti, 0)),
                  pl.BlockSpec((tq, nb * gw), lambda ti: (ti, 0)),
                  pl.BlockSpec(dsk.shape, const), pl.BlockSpec(w.shape, const),
                  pl.BlockSpec(bias.shape, const)],
        out_specs=pl.BlockSpec((nb, tq, gw), lambda ti: (0, ti, 0)),
        out_shape=jax.ShapeDtypeStruct((nb, t, gw), BF16),
        compiler_params=_params(("parallel",), VMEM_LIMIT),
        name="s5_glu",
    )(ys, u_t, dsk, w, bias)


def _out_kernel(ya_ref, yb_ref, yc_ref, yd_ref, w_ref, xa_ref, xb_ref, g1_ref, sh2_ref, sc2_ref,
                nw1_ref, nw2_ref, wr_ref, br_ref, xn_ref, f_ref, te_ref, tg_ref, *, n_a_blk, blk0):
    o = (_dotf(ya_ref[0], w_ref[0:256]) + _dotf(yb_ref[0], w_ref[256:512])
         + _dotf(yc_ref[0], w_ref[512:768]) + _dotf(yd_ref[0], w_ref[768:1024]))
    x = jnp.where(pl.program_id(1) + blk0 < n_a_blk, xa_ref[0], xb_ref[0])
    xn = x + g1_ref[0] * _rms(o, nw1_ref[...])
    xn_ref[0] = xn
    f = _rms(xn, nw2_ref[...]) * (1.0 + sc2_ref[0]) + sh2_ref[0]
    _store_token_tiles(f_ref, f)
    lg = _dot3(f, wr_ref[...]) + br_ref[...]
    tm = lg.shape[0]
    lane = lax.broadcasted_iota(I32, (tm, 128), 1).astype(F32)
    tops, idxs = [], []
    for _ in range(TOP_K):
        mx = jnp.max(lg, axis=-1, keepdims=True)
        idx = jnp.min(jnp.where(lg == mx, lane, 128.0), axis=-1, keepdims=True)
        tops.append(mx)
        idxs.append(idx.astype(I32))
        lg = jnp.where(lane == idx, NEG_INF, lg)
    ex = [jnp.exp(tv - tops[0]) for tv in tops]
    inv = 1.0 / (ex[0] + ex[1] + ex[2] + ex[3])
    l8 = lax.broadcasted_iota(I32, (tm, 8), 1)
    te = jnp.zeros((tm, 8), I32)
    tg = jnp.zeros((tm, 8), F32)
    for kk in range(TOP_K):
        te = jnp.where(l8 == kk, idxs[kk], te)
        tg = jnp.where(l8 == kk, ex[kk] * inv, tg)
    te_ref[0] = te
    tg_ref[0] = tg


def _out_call(ya, yb, yc, yd, w_out, xa, xb, mod_l, nw1, nw2, wr, br, n_ctx_blk, blk0):
    b, _, d = xa.shape
    t = xa.shape[1] + (0 if xb is None else xb.shape[1])
    tm = ROW_TILE
    nblk = t // tm - blk0
    t_out = nblk * tm
    spec_a, spec_b, n_a_blk = _stream_specs(xa, xb, blk0)

    def mrow(bi, ti):
        return jnp.where(ti + blk0 < n_ctx_blk, b, bi)

    row3 = lambda bi, ti: (bi, ti + blk0, 0)
    out3 = lambda bi, ti: (bi, ti, 0)
    const = lambda bi, ti: (0, 0)
    modspec = lambda j: pl.BlockSpec((1, 1, d), lambda bi, ti: (mrow(bi, ti), 0, j))
    yspec = lambda y: pl.BlockSpec((1, tm, GROUP_W), row3 if y.shape[1] == t else out3)
    return pl.pallas_call(
        functools.partial(_out_kernel, n_a_blk=n_a_blk, blk0=blk0),
        grid=(b, nblk),
        in_specs=[yspec(ya), yspec(yb), yspec(yc), yspec(yd), pl.BlockSpec((d, d), const),
                  spec_a, spec_b, modspec(2), modspec(3), modspec(4),
                  pl.BlockSpec((1, d), const), pl.BlockSpec((1, d), const),
                  pl.BlockSpec((d, 128), const), pl.BlockSpec((1, 128), const)],
        out_specs=[pl.BlockSpec((1, tm, d), out3),
                   pl.BlockSpec((tm * d // 128, 128), lambda bi, ti: (bi * nblk + ti, 0)),
                   pl.BlockSpec((1, tm, 8), out3), pl.BlockSpec((1, tm, 8), out3)],
        out_shape=[jax.ShapeDtypeStruct((b, t_out, d), F32),
                   jax.ShapeDtypeStruct((b * t_out * d // 128, 128), F32),
                   jax.ShapeDtypeStruct((b, t_out, 8), I32),
                   jax.ShapeDtypeStruct((b, t_out, 8), F32)],
        compiler_params=_params(("parallel", "parallel"), VMEM_LIMIT),
        name="out_proj_router",
    )(ya, yb, yc, yd, w_out, xa, xa if xb is None else xb, mod_l, mod_l, mod_l, nw1, nw2, wr, br)


def _rank_kernel(te_ref, rank_ref, cnt_ref, carry_ref):
    i = pl.program_id(0)

    @pl.when(i == 0)
    def _():
        carry_ref[...] = jnp.zeros_like(carry_ref)

    te = te_ref[...]
    tb = te.shape[0]
    lane = lax.broadcasted_iota(I32, (tb, 128), 1)
    l8 = lax.broadcasted_iota(I32, (tb, 8), 1)
    below = (lax.broadcasted_iota(I32, (tb, tb), 0)
             > lax.broadcasted_iota(I32, (tb, tb), 1))
    lstrict = jnp.where(below, 1.0, 0.0).astype(BF16)
    base = carry_ref[...]
    out = jnp.zeros((tb, 8), I32)
    for k in range(TOP_K):
        oh = jnp.where(lane == te[:, k:k + 1], 1.0, 0.0)
        before = _dotf(lstrict, oh.astype(BF16)) + base
        rank_k = jnp.sum(oh * before, axis=-1, keepdims=True)
        out = jnp.where(l8 == k, rank_k.astype(I32), out)
        base = base + jnp.sum(oh, axis=0, keepdims=True)
    rank_ref[...] = out
    carry_ref[...] = base
    cnt_ref[...] = base.astype(I32)


def _rank_call(te):
    n_tok = te.shape[0]
    tb = RANK_TILE
    return pl.pallas_call(
        _rank_kernel,
        grid=(n_tok // tb,),
        in_specs=[pl.BlockSpec((tb, 8), lambda i: (i, 0))],
        out_specs=[pl.BlockSpec((tb, 8), lambda i: (i, 0)), pl.BlockSpec((1, 128), lambda i: (0, 0))],
        out_shape=[jax.ShapeDtypeStruct((n_tok, 8), I32), jax.ShapeDtypeStruct((1, 128), I32)],
        scratch_shapes=[pltpu.VMEM((1, 128), F32)],
        compiler_params=_params(("arbitrary",)),
        name="moe_rank",
    )(te)


def _slot_kernel(tbl_ref, te_ref, rank_ref, slot_ref):
    te = te_ref[...]
    start = jnp.zeros(te.shape, I32)
    for e in range(N_EXPERTS):
        start = jnp.where(te == e, tbl_ref[e], start)
    slot_ref[...] = start + rank_ref[...]


def _slot_call(tbl, te, rank):
    n_tok = te.shape[0]
    tb = RANK_TILE
    spec = pl.BlockSpec((tb, 8), lambda i, tbl: (i, 0))
    return pl.pallas_call(
        _slot_kernel,
        grid_spec=pltpu.PrefetchScalarGridSpec(num_scalar_prefetch=1, grid=(n_tok // tb,),
                                               in_specs=[spec, spec], out_specs=spec),
        out_shape=jax.ShapeDtypeStruct((n_tok, 8), I32),
        compiler_params=_params(("parallel",)),
        name="moe_slots",
    )(tbl, te, rank)


def _slot_rows(slot_ref, r0, nt):
    return [pl.multiple_of(slot_ref[0, 0, (r0 + u) * 8 + k] * nt, nt)
            for u in range(DMA_UNROLL) for k in range(TOP_K)]


def _dispatch_kernel(tbl_ref, slot_ref, f_ref, xs_hbm, zbuf, sem, zsem, *, tb, tm, nt):
    i = pl.program_id(0)

    @pl.when(i == 0)
    def _():
        zbuf[...] = jnp.zeros_like(zbuf)
        fills = [pltpu.make_async_copy(
            zbuf, xs_hbm.at[pl.ds(pl.multiple_of(tbl_ref[N_EXPERTS + e] * nt, nt), tm * nt)], zsem)
            for e in range(N_EXPERTS)]
        for fill in fills:
            fill.start()
        for fill in fills:
            fill.wait()

        def fill_unused(blk, carry):
            tail = pltpu.make_async_copy(
                zbuf, xs_hbm.at[pl.ds(pl.multiple_of(blk * (tm * nt), tm * nt), tm * nt)], zsem)
            tail.start()
            tail.wait()
            return carry
        lax.fori_loop(tbl_ref[2 * N_EXPERTS], xs_hbm.shape[0] // (tm * nt), fill_unused, 0)

    def body(g, carry):
        r0 = g * DMA_UNROLL
        dsts = _slot_rows(slot_ref, r0, nt)
        for u in range(DMA_UNROLL):
            src = f_ref.at[pl.ds(pl.multiple_of((r0 + u) * nt, nt), nt)]
            for k in range(TOP_K):
                pltpu.make_async_copy(src, xs_hbm.at[pl.ds(dsts[u * TOP_K + k], nt)],
                                      sem).start(priority=k % 2)
        return carry

    lax.fori_loop(0, tb // DMA_UNROLL, body, 0)
    for k in range(TOP_K):
        pltpu.make_async_copy(f_ref, xs_hbm.at[pl.ds(0, tb * nt)], sem).wait()


def _dispatch_call(tbl, slot3, f_tiles, n_slots, tm):
    n_blk, _, per = slot3.shape
    tb = per // 8
    nt = f_tiles.shape[0] // (n_blk * tb)
    grid_spec = pltpu.PrefetchScalarGridSpec(
        num_scalar_prefetch=1,
        grid=(n_blk,),
        in_specs=[pl.BlockSpec((1, 1, per), lambda i, tbl: (i, 0, 0), memory_space=pltpu.SMEM),
                  pl.BlockSpec((tb * nt, 128), lambda i, tbl: (i, 0))],
        out_specs=pl.BlockSpec(memory_space=pl.ANY),
        scratch_shapes=[pltpu.VMEM((tm * nt, 128), F32), pltpu.SemaphoreType.DMA(()),
                        pltpu.SemaphoreType.DMA(())])
    return pl.pallas_call(
        functools.partial(_dispatch_kernel, tb=tb, tm=tm, nt=nt),
        grid_spec=grid_spec,
        out_shape=jax.ShapeDtypeStruct(((n_slots + tm) * nt, 128), F32),
        compiler_params=_params(("arbitrary",), VMEM_LIMIT),
        name="moe_dispatch",
    )(tbl, slot3, f_tiles)


def _expert_kernel(be_ref, nu_ref, xs_ref, wgu_ref, bgu_ref, wd_ref, bd_ref, ys_ref,
                   wgu_s, wd_s, *, tm, d_ff):
    i = pl.program_id(0)
    nt = xs_ref.shape[0] // tm

    @pl.when(i < nu_ref[0])
    def _():
        @pl.when(jnp.logical_or(i == 0, be_ref[i] != be_ref[jnp.maximum(i - 1, 0)]))
        def _():
            for c in range(0, wgu_s.shape[0], 128):
                wgu_s[c:c + 128, :] = wgu_ref[c:c + 128, :].astype(BF16)
            for c in range(0, wd_s.shape[0], 128):
                wd_s[c:c + 128, :] = wd_ref[c:c + 128, :].astype(BF16)

        for r0 in range(0, tm, MOE_ROWS):
            x = jnp.concatenate(_load_token_tiles(xs_ref, MOE_ROWS, nt, base=r0 * nt),
                                axis=1).astype(BF16)
            gu = _dotf(x, wgu_s[...]) + bgu_ref[...]
            gate = jnp.minimum(gu[:, 0:d_ff], SWIGLU_LIMIT)
            up = jnp.clip(gu[:, d_ff:2 * d_ff], -SWIGLU_LIMIT, SWIGLU_LIMIT)
            act = (up + 1.0) * gate * _sigmoid(SWIGLU_ALPHA * gate)
            _store_token_tiles(ys_ref, _dotf(act.astype(BF16), wd_s[...]) + bd_ref[...],
                               base=r0 * nt)

    @pl.when(i >= nu_ref[0])
    def _():
        ys_ref[...] = jnp.zeros_like(ys_ref)


def _expert_call(block_expert, n_used, xs_tiles, wgu, bgu, wd, bd, layer, n_blocks, tm):
    d, two_ff = wgu.shape[2:]
    d_ff = two_ff // 2
    nt = d // 128
    ex = lambda i, be, nu: (layer, be[i], 0, 0)
    grid_spec = pltpu.PrefetchScalarGridSpec(
        num_scalar_prefetch=2,
        grid=(n_blocks,),
        in_specs=[pl.BlockSpec((tm * nt, 128), lambda i, be, nu: (jnp.minimum(i, nu[0] - 1), 0)),
                  pl.BlockSpec((None, None, d, two_ff), ex),
                  pl.BlockSpec((None, None, 1, two_ff), ex),
                  pl.BlockSpec((None, None, d_ff, d), ex),
                  pl.BlockSpec((None, None, 1, d), ex)],
        out_specs=pl.BlockSpec((tm * nt, 128), lambda i, be, nu: (i, 0)),
        scratch_shapes=[pltpu.VMEM((d, two_ff), BF16), pltpu.VMEM((d_ff, d), BF16)])
    return pl.pallas_call(
        functools.partial(_expert_kernel, tm=tm, d_ff=d_ff),
        grid_spec=grid_spec,
        out_shape=jax.ShapeDtypeStruct((n_blocks * tm * nt, 128), F32),
        compiler_params=_params(("arbitrary",), VMEM_LIMIT),
        name="moe_experts",
    )(block_expert, n_used, xs_tiles, wgu, bgu, wd, bd)


def _route(counts, n_tok, tm):
    n_blocks = -(-(n_tok * TOP_K + N_EXPERTS * (tm - 1)) // tm)
    padded = (counts + tm - 1) // tm * tm
    padded_end = jnp.cumsum(padded)
    group_start = padded_end - padded
    block_start = jnp.arange(n_blocks, dtype=I32) * tm
    block_expert = jnp.minimum(
        jnp.sum((padded_end[None, :] <= block_start[:, None]).astype(I32), axis=1), N_EXPERTS - 1)
    n_used = (padded_end[-1] // tm).astype(I32).reshape(1)
    table = jnp.concatenate([group_start, group_start + counts, n_used]).astype(I32)
    return table, block_expert, n_used, n_blocks


def _fin_kernel(scur_ref, snxt_ref, g_ref, x_ref, g2_ref, nw_ref, ys_hbm, out_ref, buf, sem,
                *, tb, nt):
    i = pl.program_id(0)
    s = i % 2

    def gather(slot_ref, ss):
        def body(g, carry):
            r0 = g * DMA_UNROLL
            srcs = _slot_rows(slot_ref, r0, nt)
            for u in range(DMA_UNROLL):
                for k in range(TOP_K):
                    dst = pl.multiple_of((k * tb + r0 + u) * nt, nt)
                    pltpu.make_async_copy(ys_hbm.at[pl.ds(srcs[u * TOP_K + k], nt)],
                                          buf.at[ss, pl.ds(dst, nt)],
                                          sem.at[ss]).start(priority=k % 2)
            return carry
        lax.fori_loop(0, tb // DMA_UNROLL, body, 0)

    @pl.when(i == 0)
    def _():
        gather(scur_ref, 0)

    @pl.when(i + 1 < pl.num_programs(0))
    def _():
        gather(snxt_ref, 1 - s)

    pltpu.make_async_copy(ys_hbm.at[pl.ds(0, TOP_K * tb * nt)], buf.at[s], sem.at[s]).wait()
    gates = g_ref[...]
    parts = []
    for j in range(nt):
        acc = None
        for k in range(TOP_K):
            v = buf[s, pl.ds(k * tb * nt + j, tb, stride=nt), :] * gates[:, k:k + 1]
            acc = v if acc is None else acc + v
        parts.append(acc)
    m = jnp.concatenate(parts, axis=1)
    out_ref[0] = x_ref[0] + g2_ref[0] * _rms(m, nw_ref[...])


def _fin_call(slot3, gates, ys_tiles, xn, mod_l, nw, n_ctx_blk, blk0):
    b, t, d = xn.shape
    tb = ROW_TILE
    nblk = t // tb
    nt = d // 128
    n_blk = b * nblk

    def mrow(i):
        return jnp.where(i % nblk + blk0 < n_ctx_blk, b, i // nblk)

    smem = functools.partial(pl.BlockSpec, memory_space=pltpu.SMEM)
    return pl.pallas_call(
        functools.partial(_fin_kernel, tb=tb, nt=nt),
        grid=(n_blk,),
        in_specs=[smem((1, 1, tb * 8), lambda i: (i, 0, 0)),
                  smem((1, 1, tb * 8), lambda i: (jnp.minimum(i + 1, n_blk - 1), 0, 0)),
                  pl.BlockSpec((tb, 8), lambda i: (i, 0)),
                  pl.BlockSpec((1, tb, d), lambda i: (i // nblk, i % nblk, 0)),
                  pl.BlockSpec((1, 1, d), lambda i: (mrow(i), 0, 5)),
                  pl.BlockSpec((1, d), lambda i: (0, 0)),
                  pl.BlockSpec(memory_space=pl.ANY)],
        out_specs=pl.BlockSpec((1, tb, d), lambda i: (i // nblk, i % nblk, 0)),
        out_shape=jax.ShapeDtypeStruct((b, t, d), F32),
        scratch_shapes=[pltpu.VMEM((2, TOP_K * tb * nt, 128), F32), pltpu.SemaphoreType.DMA((2,))],
        compiler_params=_params(("arbitrary",), VMEM_LIMIT),
        name="moe_combine",
    )(slot3, slot3, gates, xn, mod_l, nw, ys_tiles)


def _in_proj_columns():
    cols = np.full((IN_COLS,), -1, np.int64)
    for sec in range(2):
        for n in range(256):
            part, hm, j = n // 128, (n % 128) // 16, n % 16
            m, h = hm // 4, hm % 4
            cols[sec * 256 + n] = sec * 256 + h * 64 + m * 32 + part * 16 + j
    cols[512:768] = np.arange(512, 768)
    o, s = IN_DA, 768
    cols[o:o + 1024] = s + np.arange(1024)
    o, s = o + IN_ML, s + 1024
    cols[o:o + 16] = s + np.arange(16)
    o, s = o + IN_G, s + 16
    cols[o:o + MLA_Q_RANK] = s + np.arange(MLA_Q_RANK)
    cols[o + 256:o + 256 + MLA_KV_RANK] = s + MLA_Q_RANK + np.arange(MLA_KV_RANK)
    cols[o + 384 + 64:o + 384 + 96] = s + MLA_Q_RANK + MLA_KV_RANK + np.arange(MLA_ROPE)
    o, s = o + IN_MLA, s + MLA_Q_RANK + MLA_KV_RANK + MLA_ROPE
    cols[o:o + 256] = s + np.arange(256)
    return cols


def _take_cols(w, cols):
    valid = jnp.asarray(cols >= 0)
    return jnp.where(valid, jnp.take(w, jnp.asarray(np.maximum(cols, 0)), axis=-1), 0.0)


def _rope_tables(n_ctx, n_lat):
    pos = jnp.arange(n_lat)
    inv = ROPE_THETA ** (-jnp.arange(8, dtype=F32) / 8)
    ang = jnp.concatenate([(pos // GRID_W)[:, None] * inv, (pos % GRID_W)[:, None] * inv], axis=-1)
    cos = jnp.concatenate([jnp.ones((n_ctx, 16), F32), jnp.cos(ang)], axis=0)
    sin = jnp.concatenate([jnp.zeros((n_ctx, 16), F32), jnp.sin(ang)], axis=0)
    t = n_ctx + n_lat
    cos_da, sin_da = jnp.tile(cos, (1, 8)), jnp.tile(sin, (1, 8))
    one, zero = jnp.ones((t, 64), F32), jnp.zeros((t, 64), F32)
    z16, z32 = jnp.zeros((t, 16), F32), jnp.zeros((t, 32), F32)
    cm = jnp.concatenate([one, cos, cos, jnp.ones((t, 32), F32)], axis=1)
    sa = jnp.concatenate([zero, -sin, z16, z32], axis=1)
    sb = jnp.concatenate([zero, z16, sin, z32], axis=1)
    return cos_da, sin_da, cm, sa, sb


def _mla_weights(w_uq, w_ukv):
    hd = MLA_NOPE + MLA_ROPE
    wq = jnp.pad(w_uq.reshape(MLA_Q_RANK, MLA_HEADS, hd),
                 ((0, 256 - MLA_Q_RANK), (0, 0), (0, 128 - hd))).reshape(256, 128 * MLA_HEADS)
    kv = w_ukv.reshape(MLA_KV_RANK, MLA_HEADS, -1)
    wk = jnp.pad(kv[:, :, :MLA_NOPE], ((0, 0), (0, 0), (0, 128 - MLA_NOPE)))
    wkv = jnp.concatenate([wk.reshape(MLA_KV_RANK, -1), kv[:, :, MLA_NOPE:].reshape(MLA_KV_RANK, -1)],
                          axis=1)
    return wq.astype(BF16), wkv.astype(BF16)


def _s5_layout(a_re, a_im, log_step, b_re, b_im, c_re, c_im):
    ns = S5_NGROUPS * S5_STATE
    eye = jnp.eye(S5_NGROUPS, dtype=F32)
    are = a_re.reshape(2, 1, ns)
    aim = a_im.reshape(2, 1, ns)
    ls = jnp.repeat(log_step, S5_STATE, axis=-1).reshape(2, 1, ns)
    bd_b = lambda w: jnp.einsum("dgph,gk->dghkp", w, eye).reshape(2, GROUP_W, ns)
    bd_c = lambda w: jnp.einsum("dghp,gk->dgpkh", w, eye).reshape(2, ns, GROUP_W)
    return are, aim, ls, bd_b(b_re), bd_b(b_im), bd_c(c_re), bd_c(c_im)


def kernel(x, c, ctx, c_ctx, w_mod, b_mod, norm_w, w_in, w_out, da_lambda, da_subln, ml_conv_w,
           ml_conv_b, ml_gate_b, ml_norm, mla_q_norm, mla_w_uq, mla_kv_norm, mla_w_ukv, s5_a_re,
           s5_a_im, s5_log_step, s5_b_re, s5_b_im, s5_c_re, s5_c_im, s5_d, s5_w_glu, s5_b_glu,
           moe_w_router, moe_b_router, moe_w_gate_up, moe_b_gate_up, moe_w_down, moe_b_down):
    bsz, n_lat, d = x.shape
    n_ctx = ctx.shape[1]
    t = n_ctx + n_lat
    depth = w_mod.shape[0]
    tm = ROW_TILE
    assert n_ctx % tm == 0 and n_lat % tm == 0 and bsz % 8 == 0 and bsz < 16
    assert n_ctx % ML_CHUNK == 0 and n_ctx % S5_CHUNK == 0
    n_ctx_blk = n_ctx // tm

    cc = jnp.pad(jnp.concatenate([c, c_ctx[None]], axis=0), ((0, 15 - bsz), (0, 0)))
    mod = _mod_call(cc, w_mod, b_mod)
    cos_da, sin_da, cm, sa, sb = _rope_tables(n_ctx, n_lat)
    in_cols = _in_proj_columns()
    xa, xb = ctx, x

    for l in range(depth):
        last = l == depth - 1
        lambda_init = 0.8 - 0.6 * math.exp(-0.3 * l)
        mod_l = mod[l].reshape(16, 1, 6 * d)
        w_in_p = _take_cols(w_in[l], in_cols).astype(BF16)
        p_da, p_ml, p_g, p_mla, u_t = _in_call(xa, xb, mod_l, norm_w[l, 0].reshape(1, d), w_in_p,
                                               cos_da, sin_da, n_ctx_blk)

        da_extra = [da_lambda[l], jnp.tile(da_subln[l], 4).reshape(1, GROUP_W)]
        da_kern = functools.partial(_da_kernel, lambda_init=lambda_init)
        da_kw = dict(q_col=0, k_col=1, v_col=2, q_w=GROUP_W)
        ya = _attn_call(da_kern, "diff_attn", p_da, p_da, p_da, da_extra, q_blk0=n_ctx_blk,
                        n_q_blk=n_lat // tm, n_keys=t, **da_kw)
        wq, wkv = _mla_weights(mla_w_uq[l], mla_w_ukv[l])
        qn = jnp.pad(mla_q_norm[l], (0, 256 - MLA_Q_RANK)).reshape(1, 256)
        q_mla, k_mla, v_mla = _mla_prep_call(p_mla, qn, wq, mla_kv_norm[l].reshape(1, -1), wkv,
                                             cm, sa, sb)
        mla_kw = dict(q_col=0, k_col=0, v_col=0, q_w=512)
        yc = _attn_call(_mla_attn_kernel, "mla_attn", q_mla, k_mla, v_mla, [], q_blk0=n_ctx_blk,
                        n_q_blk=n_lat // tm, n_keys=t, **mla_kw)
        if not last:
            ya_c = _attn_call(da_kern, "diff_attn_ctx", p_da, p_da, p_da, da_extra, q_blk0=0,
                              n_q_blk=n_ctx_blk, n_keys=n_ctx, **da_kw)
            yc_c = _attn_call(_mla_attn_kernel, "mla_attn_ctx", q_mla, k_mla, v_mla, [], q_blk0=0,
                              n_q_blk=n_ctx_blk, n_keys=n_ctx, **mla_kw)
            ya = jnp.concatenate([ya_c, ya], axis=1)
            yc = jnp.concatenate([yc_c, yc], axis=1)

        gb = jnp.pad(ml_gate_b[l], (0, 128 - 16)).reshape(1, 128)
        yb = _ml_call(p_ml, p_g, ml_conv_w[l], ml_conv_b[l].reshape(1, -1), gb,
                      ml_norm[l].reshape(1, -1), n_ctx)

        s5p = _s5_layout(s5_a_re[l], s5_a_im[l], s5_log_step[l], s5_b_re[l], s5_b_im[l],
                         s5_c_re[l], s5_c_im[l])
        ys = _s5_call(u_t.reshape(t * bsz, GROUP_W), *s5p, n_ctx, bsz)
        yd = _glu_call(ys, u_t, s5_d[l].reshape(1, -1),
                       s5_w_glu[l].astype(BF16), s5_b_glu[l].reshape(1, -1), bsz)

        blk0 = n_ctx_blk if last else 0
        wr = jnp.pad(moe_w_router[l], ((0, 0), (0, 128 - N_EXPERTS)))
        br = jnp.pad(moe_b_router[l], (0, 128 - N_EXPERTS), constant_values=-1e30).reshape(1, 128)
        xn, f, te, tg = _out_call(ya, yb, yc, yd, w_out[l].astype(BF16), xa, xb, mod_l,
                                  norm_w[l, 1].reshape(1, d), norm_w[l, 2].reshape(1, d), wr, br,
                                  n_ctx_blk, blk0)

        t_moe = t - blk0 * tm
        n_tok = bsz * t_moe
        te = te.reshape(n_tok, 8)
        rank, cnt = _rank_call(te)
        tbl, be, nu, n_blocks = _route(cnt[0, :N_EXPERTS], n_tok, MOE_TILE)
        slot3 = _slot_call(tbl, te, rank).reshape(n_tok // tm, 1, tm * 8)
        xs_tiles = _dispatch_call(tbl, slot3, f, n_blocks * MOE_TILE, MOE_TILE)
        ys_tiles = _expert_call(be, nu, xs_tiles, moe_w_gate_up,
                                moe_b_gate_up.reshape(depth, N_EXPERTS, 1, -1), moe_w_down,
                                moe_b_down.reshape(depth, N_EXPERTS, 1, -1), l, n_blocks, MOE_TILE)
        xa, xb = _fin_call(slot3, tg.reshape(n_tok, 8), ys_tiles, xn, mod_l,
                           norm_w[l, 3].reshape(1, d), n_ctx_blk, blk0), None
    return xa
```

```python
import functools
import math

import numpy as np
import jax
import jax.numpy as jnp
from jax import lax
from jax.experimental import pallas as pl
from jax.experimental.pallas import tpu as pltpu

F32, BF16, I32 = jnp.float32, jnp.bfloat16, jnp.int32
NORM_EPS = 1e-6
GRID_W = 64
ROPE_THETA = 10000.0
GROUP_W = 256
DA_QK = 32
ML_CHUNK = 128
ML_ROWS = 2
MLA_HEADS, MLA_NOPE, MLA_ROPE, MLA_Q_RANK, MLA_KV_RANK = 4, 64, 32, 192, 128
S5_NGROUPS, S5_GROUP, S5_STATE = 16, 16, 64
N_EXPERTS, TOP_K = 32, 4
SWIGLU_ALPHA, SWIGLU_LIMIT = 1.702, 7.0
NEG_INF = float("-inf")

ROW_TILE = 256
MOE_TILE = 256
MOE_ROWS = 256
S5_CHUNK = 64
RANK_TILE = 512
DISPATCH_TILE = 512
DMA_UNROLL = 8
VMEM_LIMIT = 56 * 1024 * 1024

IN_DA, IN_ML, IN_G, IN_MLA, IN_S5 = 768, 1024, 128, 512, 256
IN_COLS = IN_DA + IN_ML + IN_G + IN_MLA + IN_S5


def _params(sem, vmem=None):
    return pltpu.CompilerParams(dimension_semantics=sem, vmem_limit_bytes=vmem)


def _dotf(a, b):
    return jnp.dot(a, b, preferred_element_type=F32)


def _dot_nt(a, b):
    return lax.dot_general(a, b, (((1,), (1,)), ((), ())), preferred_element_type=F32)


def _split2(a):
    hi = a.astype(BF16)
    lo = (a - hi.astype(F32)).astype(BF16)
    return hi, lo


def _dot3(a, b):
    ah, al = _split2(a)
    bh, bl = _split2(b)
    return _dotf(ah, bh) + _dotf(ah, bl) + _dotf(al, bh)


def _dot_exact_rhs(a, rhs_b):
    a1 = a.astype(BF16)
    r1 = a - a1.astype(F32)
    a2 = r1.astype(BF16)
    a3 = (r1 - a2.astype(F32)).astype(BF16)
    return _dotf(a1, rhs_b) + _dotf(a2, rhs_b) + _dotf(a3, rhs_b)


def _rms(x, w):
    ms = jnp.mean(x * x, axis=-1, keepdims=True)
    return x * lax.rsqrt(ms + NORM_EPS) * w


def _head_rms(a, width):
    n = a.shape[-1]
    sh = int(math.log2(width))
    r = lax.broadcasted_iota(I32, (n, n), 0) >> sh
    c = lax.broadcasted_iota(I32, (n, n), 1) >> sh
    g = jnp.where(r == c, 1.0 / width, 0.0).astype(BF16)
    hi, lo = _split2(a * a)
    ms = _dotf(hi, g) + _dotf(lo, g)
    return a * lax.rsqrt(ms + NORM_EPS)


def _store_token_tiles(ref, val, base=0):
    tm, d = val.shape
    nt = d // 128
    for j in range(nt):
        ref[pl.ds(base + j, tm, stride=nt), :] = val[:, 128 * j:128 * (j + 1)]


def _load_token_tiles(ref, tm, nt, base=0, lead=()):
    parts = [ref[lead + (pl.ds(base + j, tm, stride=nt), slice(None))] for j in range(nt)]
    return parts


def _sigmoid(x):
    return jax.nn.sigmoid(x)


def _log_sigmoid(x):
    return jnp.minimum(x, 0.0) - jnp.log(1.0 + jnp.exp(-jnp.abs(x)))


def _mod_kernel(c_ref, w_ref, b_ref, o_ref):
    c = c_ref[...]
    o_ref[0] = _dot3(c * _sigmoid(c), w_ref[0]) + b_ref[0]


def _mod_call(cc, w_mod, b_mod):
    n_layers, d, n = w_mod.shape
    tn = 1536
    return pl.pallas_call(
        _mod_kernel,
        grid=(n_layers, n // tn),
        in_specs=[pl.BlockSpec((16, d), lambda l, j: (0, 0)),
                  pl.BlockSpec((1, d, tn), lambda l, j: (l, 0, j)),
                  pl.BlockSpec((1, 1, tn), lambda l, j: (l, 0, j))],
        out_specs=pl.BlockSpec((1, 16, tn), lambda l, j: (l, 0, j)),
        out_shape=jax.ShapeDtypeStruct((n_layers, 16, n), F32),
        compiler_params=_params(("parallel", "parallel"), VMEM_LIMIT),
        name="mod_vectors",
    )(cc, w_mod, b_mod.reshape(n_layers, 1, n))


def _in_kernel(xa_ref, xb_ref, sh_ref, sc_ref, nw_ref, w_ref, c_ref, s_ref,
               qn_ref, wq_ref, kvn_ref, wkv_ref, cm_ref, sa_ref, sb_ref,
               da_ref, ml_ref, g_ref, mq_ref, mk_ref, mv_ref, s5_ref, *, qscale, mla_scale, n_a_blk):
    x = jnp.where(pl.program_id(1) < n_a_blk, xa_ref[0], xb_ref[0])
    h = _rms(x, nw_ref[...]) * (1.0 + sc_ref[0]) + sh_ref[0]
    hb = h.astype(BF16)
    da = _dotf(hb, w_ref[:, 0:IN_DA])
    c = c_ref[...]
    s = s_ref[...]
    q1, q2, k1, k2 = da[:, 0:128], da[:, 128:256], da[:, 256:384], da[:, 384:512]
    da_ref[0, :, 0:128] = ((q1 * c - q2 * s) * qscale).astype(BF16)
    da_ref[0, :, 128:256] = ((q2 * c + q1 * s) * qscale).astype(BF16)
    da_ref[0, :, 256:384] = (k1 * c - k2 * s).astype(BF16)
    da_ref[0, :, 384:512] = (k2 * c + k1 * s).astype(BF16)
    da_ref[0, :, 512:768] = da[:, 512:768].astype(BF16)
    o = IN_DA
    ml_ref[0] = _dotf(hb, w_ref[:, o:o + IN_ML]).astype(BF16)
    o += IN_ML
    g_ref[0] = _dotf(hb, w_ref[:, o:o + IN_G])
    o += IN_G
    _mla_project(_dotf(hb, w_ref[:, o:o + IN_MLA]), qn_ref, wq_ref, kvn_ref, wkv_ref,
                 cm_ref, sa_ref, sb_ref, mq_ref, mk_ref, mv_ref, mla_scale)
    o += IN_MLA
    s5_ref[...] = _dotf(hb, w_ref[:, o:o + IN_S5])


def _stream_specs(xa, xb, blk0):
    tm = ROW_TILE
    d = xa.shape[2]
    n_a = xa.shape[1] // tm
    spec_a = pl.BlockSpec((1, tm, d), lambda bi, ti: (bi, jnp.minimum(ti + blk0, n_a - 1), 0))
    spec_b = pl.BlockSpec((1, tm, d), lambda bi, ti: (bi, jnp.maximum(ti + blk0 - n_a, 0), 0))
    return spec_a, spec_b, n_a


def _in_call(xa, xb, mod_l, nw, w_in_p, cos_da, sin_da, mla, n_ctx_blk):
    b, _, d = xa.shape
    t = xa.shape[1] + (0 if xb is None else xb.shape[1])
    tm = ROW_TILE
    spec_a, spec_b, n_a_blk = _stream_specs(xa, xb, 0)

    def mrow(bi, ti):
        return jnp.where(ti < n_ctx_blk, b, bi)

    row3 = lambda bi, ti: (bi, ti, 0)
    tab = pl.BlockSpec((tm, 128), lambda bi, ti: (ti, 0))
    return pl.pallas_call(
        functools.partial(_in_kernel, qscale=DA_QK ** -0.5 * LOG2_E,
                          mla_scale=(MLA_NOPE + MLA_ROPE) ** -0.5 * LOG2_E, n_a_blk=n_a_blk),
        grid=(b, t // tm),
        in_specs=[spec_a, spec_b,
                  pl.BlockSpec((1, 1, d), lambda bi, ti: (mrow(bi, ti), 0, 0)),
                  pl.BlockSpec((1, 1, d), lambda bi, ti: (mrow(bi, ti), 0, 1)),
                  pl.BlockSpec((1, d), lambda bi, ti: (0, 0)),
                  pl.BlockSpec((d, IN_COLS), lambda bi, ti: (0, 0)),
                  tab, tab] + [pl.BlockSpec(a.shape, lambda bi, ti: (0, 0)) for a in mla[:4]]
                 + [tab, tab, tab],
        out_specs=[pl.BlockSpec((1, tm, IN_DA), row3),
                   pl.BlockSpec((1, tm, IN_ML), row3),
                   pl.BlockSpec((1, tm, IN_G), row3),
                   pl.BlockSpec((1, tm, 512), row3), pl.BlockSpec((1, tm, 512), row3),
                   pl.BlockSpec((1, tm, GROUP_W), row3),
                   pl.BlockSpec((tm, IN_S5), lambda bi, ti: (ti, bi))],
        out_shape=[jax.ShapeDtypeStruct((b, t, IN_DA), BF16),
                   jax.ShapeDtypeStruct((b, t, IN_ML), BF16),
                   jax.ShapeDtypeStruct((b, t, IN_G), F32),
                   jax.ShapeDtypeStruct((b, t, 512), BF16), jax.ShapeDtypeStruct((b, t, 512), BF16),
                   jax.ShapeDtypeStruct((b, t, GROUP_W), BF16),
                   jax.ShapeDtypeStruct((t, b * IN_S5), F32)],
        compiler_params=_params(("parallel", "parallel"), VMEM_LIMIT),
        name="in_proj",
    )(xa, xa if xb is None else xb, mod_l, mod_l, nw, w_in_p, cos_da, sin_da, *mla)


LOG2_E = 1.0 / math.log(2.0)


def _softmax_rows(s):
    mx = jnp.max(s, axis=-1, keepdims=True)
    p = jnp.exp2(s - mx)
    return p, jnp.sum(p, axis=-1, keepdims=True)


def _da_kernel(q_ref, k_ref, v_ref, lam_ref, sub_ref, o_ref, *, lambda_init):
    q = q_ref[0]
    k = k_ref[0]
    v = v_ref[0]
    lp = lam_ref[...]
    lam = (jnp.exp(jnp.sum(lp[0:1] * lp[1:2], axis=-1, keepdims=True))
           - jnp.exp(jnp.sum(lp[2:3] * lp[3:4], axis=-1, keepdims=True)) + lambda_init)
    lane = lax.broadcasted_iota(I32, (1, GROUP_W), 1)
    grp = (lane & 127) >> 4
    head = lane >> 6
    acc = jnp.zeros((q.shape[0], GROUP_W), F32)
    for h in range(4):
        ps, rs = [], []
        for m in range(2):
            qm = jnp.where(grp == m * 4 + h, q, jnp.zeros_like(q))
            p, l = _softmax_rows(_dot_nt(qm, k))
            ps.append(p.astype(BF16))
            rs.append(((1.0 if m == 0 else lam) / l).astype(BF16))
        w = ps[0] * rs[0] - ps[1] * rs[1]
        vm = jnp.where(head == h, v, jnp.zeros_like(v))
        acc = acc + _dotf(w, vm)
    y = _head_rms(acc, 64) * sub_ref[...] * (1.0 - lambda_init)
    o_ref[0] = y.astype(BF16)


def _mla_attn_kernel(q_ref, k_ref, v_ref, o_ref):
    q = q_ref[0]
    k = k_ref[0]
    v = v_ref[0]
    head = lax.broadcasted_iota(I32, (1, GROUP_W), 1) >> 6
    acc = jnp.zeros((q.shape[0], GROUP_W), F32)
    for h in range(MLA_HEADS):
        sl = slice(128 * h, 128 * (h + 1))
        p, l = _softmax_rows(_dot_nt(q[:, sl], k[:, sl]))
        vm = jnp.where(head == h, v, jnp.zeros_like(v))
        acc = acc + _dotf(p.astype(BF16), vm) * (1.0 / l)
    o_ref[0] = acc.astype(BF16)


def _attn_call(kernel, name, q_arr, k_arr, v_arr, extra, *, q_blk0, n_q_blk, n_keys,
               q_col, k_col, v_col, q_w):
    b = q_arr.shape[0]
    tq = ROW_TILE
    in_specs = [pl.BlockSpec((1, tq, q_w), lambda bi, qi: (bi, qi + q_blk0, q_col)),
                pl.BlockSpec((1, n_keys, q_w), lambda bi, qi: (bi, 0, k_col)),
                pl.BlockSpec((1, n_keys, GROUP_W), lambda bi, qi: (bi, 0, v_col))]
    in_specs += [pl.BlockSpec(e.shape, lambda bi, qi: (0, 0)) for e in extra]
    return pl.pallas_call(
        kernel,
        grid=(b, n_q_blk),
        in_specs=in_specs,
        out_specs=pl.BlockSpec((1, tq, GROUP_W), lambda bi, qi: (bi, qi, 0)),
        out_shape=jax.ShapeDtypeStruct((b, n_q_blk * tq, GROUP_W), BF16),
        compiler_params=_params(("parallel", "arbitrary"), VMEM_LIMIT),
        name=name,
    )(q_arr, k_arr, v_arr, *extra)


def _mla_project(p, qn_ref, wq_ref, kvn_ref, wkv_ref, c_ref, sa_ref, sb_ref, q_ref, k_ref, v_ref,
                 scale):
    cq, ckv, kr = p[:, 0:256], p[:, 256:384], p[:, 384:512]
    msq = jnp.sum(cq * cq, axis=-1, keepdims=True) * (1.0 / MLA_Q_RANK)
    qn = (cq * lax.rsqrt(msq + NORM_EPS) * qn_ref[...]).astype(BF16)
    q = _dotf(qn, wq_ref[...])
    c = c_ref[...]
    sa = sa_ref[...]
    sb = sb_ref[...]

    def rope(a):
        return a * c + pltpu.roll(a, 112, 1) * sa + pltpu.roll(a, 16, 1) * sb

    for h in range(MLA_HEADS):
        sl = slice(128 * h, 128 * (h + 1))
        q_ref[0, :, sl] = (rope(q[:, sl]) * scale).astype(BF16)
    kvn = (_rms(ckv, kvn_ref[...])).astype(BF16)
    kv = _dotf(kvn, wkv_ref[...])
    krr = rope(kr)
    for h in range(MLA_HEADS):
        sl = slice(128 * h, 128 * (h + 1))
        k_ref[0, :, sl] = (kv[:, sl] + krr).astype(BF16)
    v_ref[0] = kv[:, 512:768].astype(BF16)


def _ml_kernel(p_ref, g_ref, cw_ref, cb_ref, gb_ref, nw_ref, y_ref,
               qt_s, k_s, vt_s, ot_s, gt_s, hf_s, c_s, m_s, *, n_ctx_chunks, n_chunks):
    cl = ML_CHUNK
    t_total = n_chunks * cl
    nbr = p_ref.shape[0]
    row = lax.broadcasted_iota(I32, (cl, 1), 0)
    si = lax.broadcasted_iota(I32, (cl, cl), 0)
    ti = lax.broadcasted_iota(I32, (cl, cl), 1)
    lane128 = lax.broadcasted_iota(I32, (1, 128), 1)
    lane256 = lax.broadcasted_iota(I32, (1, 256), 1)
    row128 = lax.broadcasted_iota(I32, (128, 1), 0)
    hmask = [(lane256 >> 6) == h for h in range(4)]
    is_f = jnp.logical_and(((lane128 >> 2) & 1) == 1, lane128 < 16)
    r_nd = lax.broadcasted_iota(I32, (384, 512), 0)
    c_nd = lax.broadcasted_iota(I32, (384, 512), 1) >> 7
    nd_head = jnp.where(r_nd < 256, r_nd >> 6, r_nd - 256)
    nd_mask = nd_head == c_nd
    nd_ones = jnp.where(jnp.logical_and(nd_mask, r_nd >= 256), 1.0, 0.0).astype(BF16)
    r_st = lax.broadcasted_iota(I32, (384, 256), 0)
    st_head = jnp.where(r_st < 256, r_st >> 6, r_st - 256)
    st_mask = st_head == (lax.broadcasted_iota(I32, (384, 256), 1) >> 6)
    st_rowhead = jnp.where(lax.broadcasted_iota(I32, (384, 1), 0) < 256,
                           lax.broadcasted_iota(I32, (384, 1), 0) >> 6,
                           lax.broadcasted_iota(I32, (384, 1), 0) - 256)
    w0, w1, w2 = cw_ref[0:1], cw_ref[1:2], cw_ref[2:3]
    cb = cb_ref[...]
    gb = gb_ref[...]

    def prep_one(c, bb):
        s0 = pl.multiple_of(c * cl, cl)
        x = p_ref[bb, pl.ds(s0, cl), 0:512].astype(F32)
        sp = pl.multiple_of(jnp.maximum(s0 - 16, 0), 16)
        sn = pl.multiple_of(jnp.minimum(s0 + cl, t_total - 16), 16)
        has_prev = jnp.logical_and(c != 0, c != n_ctx_chunks).astype(F32)
        has_next = jnp.logical_and(c != n_ctx_chunks - 1, c != n_chunks - 1).astype(F32)
        prev_row = p_ref[bb, pl.ds(sp, 16), 0:512][15:16].astype(F32) * has_prev
        next_row = p_ref[bb, pl.ds(sn, 16), 0:512][0:1].astype(F32) * has_next
        xp = jnp.where(row == 0, prev_row, pltpu.roll(x, 1, 0))
        xn = jnp.where(row == cl - 1, next_row, pltpu.roll(x, cl - 1, 0))
        z = xp * w0 + x * w1 + xn * w2 + cb
        qk = z * _sigmoid(z)
        qt_s[bb, c] = qk[:, 0:256].T.astype(BF16)
        k_s[bb, c] = (qk[:, 256:512] * (64 ** -0.5)).astype(BF16)
        vt_s[bb, c] = p_ref[bb, pl.ds(s0, cl), 512:768].astype(F32).T
        ot_s[bb, c] = p_ref[bb, pl.ds(s0, cl), 768:1024].astype(F32).T
        g = g_ref[bb, pl.ds(s0, cl), :] + gb
        gt_s[bb, c] = jnp.where(is_f, _log_sigmoid(g), g).T[0:16]

    def prep_body(c, carry):
        for bb in range(nbr):
            prep_one(c, bb)
        return carry

    def rows_to_blocks(rows, n):
        return jnp.concatenate([jnp.broadcast_to(r, (n, 128)) for r in rows], axis=0)

    def head_rows(rows):
        out = jnp.zeros((128, 128), F32)
        for h in range(4):
            out = jnp.where(row128 == h, rows[h], out)
        return out

    def chunk(c, reverse, bb):
        g_t = gt_s[bb, c]
        tri = (si >= ti) if reverse else (si <= ti)
        bc_t = _dot_exact_rhs(g_t, jnp.where(tri, 1.0, 0.0).astype(BF16))
        qt = qt_s[bb, c]
        kb = k_s[bb, c]
        m_all = m_s[bb]
        off = 8 if reverse else 0
        a_rows, r_rows, tots, m_old = [], [], [], []
        for h in range(4):
            il, fl = off + h, off + 4 + h
            a_rows.append(bc_t[fl:fl + 1, :])
            r_rows.append(g_t[il:il + 1, :] - bc_t[fl:fl + 1, :])
            tots.append(bc_t[fl:fl + 1, 0:1] if reverse else bc_t[fl:fl + 1, cl - 1:cl])
            m_old.append(m_all[:, h:h + 1])
        r_all = jnp.concatenate(r_rows, axis=1)
        d_t = jnp.broadcast_to(r_all, (cl, 4 * cl)).T + rows_to_blocks(a_rows, cl)
        lw = jnp.where(jnp.concatenate([tri] * 4, axis=0), d_t, NEG_INF)
        mt, wi, emt = [], [], []
        for h in range(4):
            linter = a_rows[h] + m_old[h]
            mt_h = jnp.maximum(linter, jnp.max(lw[cl * h:cl * (h + 1)], axis=0, keepdims=True))
            mt.append(mt_h)
            wi.append(jnp.exp(linter - mt_h))
            emt.append(jnp.exp(-mt_h))
        kstack = jnp.concatenate([jnp.where(hmask[h], kb, jnp.zeros_like(kb)) for h in range(4)],
                                 axis=0)
        s_t = _dotf(kstack, qt)
        w_t = (s_t * jnp.exp(lw - rows_to_blocks(mt, cl))).astype(BF16)
        vt = vt_s[bb, c]
        vt4 = jnp.concatenate([vt.astype(BF16)] * 4, axis=1)
        lhs_nd = jnp.concatenate([vt4, jnp.zeros((128, 512), BF16)], axis=0)
        lhs_nd = jnp.where(nd_mask, lhs_nd, jnp.zeros_like(lhs_nd)) + nd_ones
        c_aug = c_s[bb]
        nd = _dotf(lhs_nd, w_t) + jnp.concatenate(
            [rows_to_blocks(wi, 64), head_rows(wi)], axis=0) * _dotf(c_aug.astype(BF16), qt)
        den = [nd[256 + h:257 + h, :] for h in range(4)]
        h_t = nd[0:256] / jnp.maximum(jnp.abs(rows_to_blocks(den, 64)), rows_to_blocks(emt, 64))
        wupd, decs, m_new_all = [], [], m_all
        for h in range(4):
            lupd = tots[h] + r_rows[h]
            m_new = jnp.maximum(tots[h] + m_old[h], jnp.max(lupd, axis=-1, keepdims=True))
            wupd.append(jnp.exp(lupd - m_new))
            decs.append(jnp.exp(tots[h] + m_old[h] - m_new))
            m_new_all = jnp.where(lane128 == h, m_new, m_new_all)
        lhs_u = jnp.concatenate([vt * rows_to_blocks(wupd, 64), head_rows(wupd)], axis=0)
        upd = _dotf(lhs_u.astype(BF16), kb)
        dec_col = jnp.zeros((384, 1), F32)
        for h in range(4):
            dec_col = jnp.where(st_rowhead == h, decs[h], dec_col)
        c_s[bb] = dec_col * c_aug + jnp.where(st_mask, upd, 0.0)
        m_s[bb] = m_new_all
        return h_t

    def fwd_body(i, carry):
        for bb in range(nbr):
            hf_s[bb, i] = chunk(i, False, bb)
        return carry

    def bwd_body(i, carry):
        c = jnp.where(i < n_ctx_chunks, n_ctx_chunks - 1 - i, n_chunks - 1 - (i - n_ctx_chunks))
        for bb in range(nbr):
            gated = _sigmoid(ot_s[bb, c]) * (hf_s[bb, c] + chunk(c, True, bb))
            ms = [jnp.mean(jnp.square(gated[64 * h:64 * (h + 1)]), axis=0, keepdims=True)
                  for h in range(4)]
            y_t = gated * lax.rsqrt(rows_to_blocks(ms, 64) + NORM_EPS) * nw_ref[...]
            y_ref[bb, pl.ds(pl.multiple_of(c * cl, cl), cl), :] = y_t.T.astype(BF16)
        return carry

    lax.fori_loop(0, n_chunks, prep_body, 0)
    c_s[...] = jnp.zeros_like(c_s)
    m_s[...] = jnp.zeros_like(m_s)
    lax.fori_loop(0, n_chunks, fwd_body, 0)
    c_s[...] = jnp.zeros_like(c_s)
    m_s[...] = jnp.zeros_like(m_s)
    lax.fori_loop(0, n_chunks, bwd_body, 0)


def _ml_call(p_ml, p_g, cw, cb, gb, nw, n_ctx):
    b, t, _ = p_ml.shape
    nc = t // ML_CHUNK
    nbr = ML_ROWS
    const = lambda bi: (0, 0)
    nw_col = jnp.broadcast_to(nw.reshape(GROUP_W, 1), (GROUP_W, 128))
    return pl.pallas_call(
        functools.partial(_ml_kernel, n_ctx_chunks=n_ctx // ML_CHUNK, n_chunks=nc),
        grid=(b // nbr,),
        in_specs=[pl.BlockSpec((nbr, t, IN_ML), lambda bi: (bi, 0, 0)),
                  pl.BlockSpec((nbr, t, IN_G), lambda bi: (bi, 0, 0)),
                  pl.BlockSpec(cw.shape, const), pl.BlockSpec(cb.shape, const),
                  pl.BlockSpec(gb.shape, const), pl.BlockSpec(nw_col.shape, const)],
        out_specs=pl.BlockSpec((nbr, t, GROUP_W), lambda bi: (bi, 0, 0)),
        out_shape=jax.ShapeDtypeStruct((b, t, GROUP_W), BF16),
        scratch_shapes=[pltpu.VMEM((nbr, nc, GROUP_W, 128), BF16),
                        pltpu.VMEM((nbr, nc, 128, GROUP_W), BF16),
                        pltpu.VMEM((nbr, nc, GROUP_W, 128), F32),
                        pltpu.VMEM((nbr, nc, GROUP_W, 128), F32),
                        pltpu.VMEM((nbr, nc, 16, 128), F32),
                        pltpu.VMEM((nbr, nc, GROUP_W, 128), F32),
                        pltpu.VMEM((nbr, 384, GROUP_W), F32), pltpu.VMEM((nbr, 1, 128), F32)],
        compiler_params=_params(("parallel",), VMEM_LIMIT),
        name="mlstm",
    )(p_ml, p_g, cw, cb, gb, nw_col)


def _s5_kernel(u_ref, are_ref, aim_ref, ls_ref, bre_ref, bim_ref, cre_ref, cim_ref, y_ref,
               ar_s, ai_s, bcat_s, ccat_s, st_s, bu_s, *, tc, nb):
    d = pl.program_id(0)
    i = pl.program_id(1)
    ns = S5_NGROUPS * S5_STATE

    @pl.when(i == 0)
    def _init():
        are = jnp.minimum(are_ref[0], -1e-4)
        aim = aim_ref[0]
        dt = jnp.exp(ls_ref[0])
        mag = jnp.exp(dt * are)
        abr = mag * jnp.cos(dt * aim)
        abi = mag * jnp.sin(dt * aim)
        inv = 1.0 / (are * are + aim * aim)
        fre = ((abr - 1.0) * are + abi * aim) * inv
        fim = (abi * are - (abr - 1.0) * aim) * inv
        bre = bre_ref[0]
        bim = bim_ref[0]
        bcat_s[:, 0:ns] = (bre * fre - bim * fim).astype(BF16)
        bcat_s[:, ns:2 * ns] = (bre * fim + bim * fre).astype(BF16)
        ccat_s[0:ns, :] = cre_ref[0].astype(BF16)
        ccat_s[ns:2 * ns, :] = (-cim_ref[0]).astype(BF16)
        ar_s[...] = jnp.broadcast_to(abr, (nb, ns))
        ai_s[...] = jnp.broadcast_to(abi, (nb, ns))
        st_s[...] = jnp.zeros_like(st_s)

    bu_s[...] = _dotf(u_ref[...].astype(BF16), bcat_s[...])
    ar = ar_s[...]
    ai = ai_s[...]

    def body(j, carry):
        xr, xi = carry
        t = j + d * (tc - 1 - 2 * j)
        r0 = pl.multiple_of(t * nb, nb)
        nr = ar * xr - ai * xi + bu_s[pl.ds(r0, nb), 0:ns]
        ni = ar * xi + ai * xr + bu_s[pl.ds(r0, nb), ns:2 * ns]
        bu_s[pl.ds(r0, nb), 0:ns] = nr
        bu_s[pl.ds(r0, nb), ns:2 * ns] = ni
        return nr, ni

    xr, xi = lax.fori_loop(0, tc, body, (st_s[0], st_s[1]), unroll=4)
    st_s[0] = xr
    st_s[1] = xi
    y = _dotf(bu_s[...].astype(BF16), ccat_s[...])
    for half in range(y.shape[1] // 128):
        y_ref[0, half] = y[:, 128 * half:128 * (half + 1)]


def _s5_call(u_tm, are, aim, ls, bre, bim, cre, cim, n_ctx, nb):
    rows, gw = u_tm.shape
    tc = S5_CHUNK
    n_chunks = rows // (tc * nb)
    n_ctx_chunks = n_ctx // tc
    ns = S5_NGROUPS * S5_STATE

    def chunk_of(d, i):
        rev = jnp.where(i < n_ctx_chunks, n_ctx_chunks - 1 - i, n_chunks - 1 - (i - n_ctx_chunks))
        return jnp.where(d == 0, i, rev)

    vec = pl.BlockSpec((1, 1, ns), lambda d, i: (d, 0, 0))
    return pl.pallas_call(
        functools.partial(_s5_kernel, tc=tc, nb=nb),
        grid=(2, n_chunks),
        in_specs=[pl.BlockSpec((tc * nb, gw), lambda d, i: (chunk_of(d, i), 0)),
                  vec, vec, vec,
                  pl.BlockSpec((1, gw, ns), lambda d, i: (d, 0, 0)),
                  pl.BlockSpec((1, gw, ns), lambda d, i: (d, 0, 0)),
                  pl.BlockSpec((1, ns, gw), lambda d, i: (d, 0, 0)),
                  pl.BlockSpec((1, ns, gw), lambda d, i: (d, 0, 0))],
        out_specs=pl.BlockSpec((1, gw // 128, tc * nb, 128),
                               lambda d, i: (d, 0, chunk_of(d, i), 0)),
        out_shape=jax.ShapeDtypeStruct((2, gw // 128, rows, 128), F32),
        scratch_shapes=[pltpu.VMEM((nb, ns), F32), pltpu.VMEM((nb, ns), F32),
                        pltpu.VMEM((gw, 2 * ns), BF16), pltpu.VMEM((2 * ns, gw), BF16),
                        pltpu.VMEM((2, nb, ns), F32), pltpu.VMEM((tc * nb, 2 * ns), F32)],
        compiler_params=_params(("arbitrary", "arbitrary"), VMEM_LIMIT),
        name="s5_scan",
    )(u_tm, are, aim, ls, bre, bim, cre, cim)


def _glu_kernel(ys_ref, u_ref, d_ref, w_ref, b_ref, o_ref, *, nb, tq):
    gw = GROUP_W
    rows = []
    for b in range(nb):
        y_b = jnp.concatenate(
            [ys_ref[0, half, pl.ds(b, tq, stride=nb), :] + ys_ref[1, half, pl.ds(b, tq, stride=nb), :]
             for half in range(gw // 128)], axis=1)
        rows.append(y_b + u_ref[:, gw * b:gw * (b + 1)] * d_ref[...])
    y = jnp.concatenate(rows, axis=0)
    g = y * (0.5 * (1.0 + jnp.tanh(math.sqrt(2.0 / math.pi) * (y + 0.044715 * (y * y * y)))))
    z = _dotf(g.astype(BF16), w_ref[...]) + b_ref[...]
    out = (g * _sigmoid(z)).astype(BF16)
    for b in range(nb):
        o_ref[b] = out[tq * b:tq * (b + 1)]


def _glu_call(ys, u_t, dsk, w, bias, nb):
    t = u_t.shape[0]
    tq = ROW_TILE // nb
    gw = GROUP_W
    const = lambda ti: (0, 0)
    return pl.pallas_call(
        functools.partial(_glu_kernel, nb=nb, tq=tq),
        grid=(t // tq,),
        in_specs=[pl.BlockSpec((2, gw // 128, tq * nb, 128), lambda ti: (0, 0, ti, 0)),
                  pl.BlockSpec((tq, nb * gw), lambda ti: (ti, 0)),
                  pl.BlockSpec(dsk.shape, const), pl.BlockSpec(w.shape, const),
                  pl.BlockSpec(bias.shape, const)],
        out_specs=pl.BlockSpec((nb, tq, gw), lambda ti: (0, ti, 0)),
        out_shape=jax.ShapeDtypeStruct((nb, t, gw), BF16),
        compiler_params=_params(("parallel",), VMEM_LIMIT),
        name="s5_glu",
    )(ys, u_t, dsk, w, bias)


def _out_kernel(ya_ref, yb_ref, yc_ref, yd_ref, w_ref, xa_ref, xb_ref, g1_ref, sh2_ref, sc2_ref,
                nw1_ref, nw2_ref, wr_ref, br_ref, xn_ref, f_ref, te_ref, tg_ref, *, n_a_blk, blk0):
    o = (_dotf(ya_ref[0], w_ref[0:256]) + _dotf(yb_ref[0], w_ref[256:512])
         + _dotf(yc_ref[0], w_ref[512:768]) + _dotf(yd_ref[0], w_ref[768:1024]))
    x = jnp.where(pl.program_id(1) + blk0 < n_a_blk, xa_ref[0], xb_ref[0])
    xn = x + g1_ref[0] * _rms(o, nw1_ref[...])
    xn_ref[0] = xn
    f = _rms(xn, nw2_ref[...]) * (1.0 + sc2_ref[0]) + sh2_ref[0]
    _store_token_tiles(f_ref, f)
    lg = _dot3(f, wr_ref[...]) + br_ref[...]
    tm = lg.shape[0]
    lane = lax.broadcasted_iota(I32, (tm, 128), 1).astype(F32)
    tops, idxs = [], []
    for _ in range(TOP_K):
        mx = jnp.max(lg, axis=-1, keepdims=True)
        idx = jnp.min(jnp.where(lg == mx, lane, 128.0), axis=-1, keepdims=True)
        tops.append(mx)
        idxs.append(idx.astype(I32))
        lg = jnp.where(lane == idx, NEG_INF, lg)
    ex = [jnp.exp(tv - tops[0]) for tv in tops]
    inv = 1.0 / (ex[0] + ex[1] + ex[2] + ex[3])
    l8 = lax.broadcasted_iota(I32, (tm, 8), 1)
    te = jnp.zeros((tm, 8), I32)
    tg = jnp.zeros((tm, 8), F32)
    for kk in range(TOP_K):
        te = jnp.where(l8 == kk, idxs[kk], te)
        tg = jnp.where(l8 == kk, ex[kk] * inv, tg)
    te_ref[0] = te
    tg_ref[0] = tg


def _out_call(ya, yb, yc, yd, w_out, xa, xb, mod_l, nw1, nw2, wr, br, n_ctx_blk, blk0):
    b, _, d = xa.shape
    t = xa.shape[1] + (0 if xb is None else xb.shape[1])
    tm = ROW_TILE
    nblk = t // tm - blk0
    t_out = nblk * tm
    spec_a, spec_b, n_a_blk = _stream_specs(xa, xb, blk0)

    def mrow(bi, ti):
        return jnp.where(ti + blk0 < n_ctx_blk, b, bi)

    row3 = lambda bi, ti: (bi, ti + blk0, 0)
    out3 = lambda bi, ti: (bi, ti, 0)
    const = lambda bi, ti: (0, 0)
    modspec = lambda j: pl.BlockSpec((1, 1, d), lambda bi, ti: (mrow(bi, ti), 0, j))
    yspec = lambda y: pl.BlockSpec((1, tm, GROUP_W), row3 if y.shape[1] == t else out3)
    return pl.pallas_call(
        functools.partial(_out_kernel, n_a_blk=n_a_blk, blk0=blk0),
        grid=(b, nblk),
        in_specs=[yspec(ya), yspec(yb), yspec(yc), yspec(yd), pl.BlockSpec((d, d), const),
                  spec_a, spec_b, modspec(2), modspec(3), modspec(4),
                  pl.BlockSpec((1, d), const), pl.BlockSpec((1, d), const),
                  pl.BlockSpec((d, 128), const), pl.BlockSpec((1, 128), const)],
        out_specs=[pl.BlockSpec((1, tm, d), out3),
                   pl.BlockSpec((tm * d // 128, 128), lambda bi, ti: (bi * nblk + ti, 0)),
                   pl.BlockSpec((1, tm, 8), out3), pl.BlockSpec((1, tm, 8), out3)],
        out_shape=[jax.ShapeDtypeStruct((b, t_out, d), F32),
                   jax.ShapeDtypeStruct((b * t_out * d // 128, 128), F32),
                   jax.ShapeDtypeStruct((b, t_out, 8), I32),
                   jax.ShapeDtypeStruct((b, t_out, 8), F32)],
        compiler_params=_params(("parallel", "parallel"), VMEM_LIMIT),
        name="out_proj_router",
    )(ya, yb, yc, yd, w_out, xa, xa if xb is None else xb, mod_l, mod_l, mod_l, nw1, nw2, wr, br)


def _rank_kernel(te_ref, rank_ref, cnt_ref, carry_ref):
    i = pl.program_id(0)

    @pl.when(i == 0)
    def _():
        carry_ref[...] = jnp.zeros_like(carry_ref)

    te = te_ref[...]
    tb = te.shape[0]
    lane = lax.broadcasted_iota(I32, (tb, 128), 1)
    l8 = lax.broadcasted_iota(I32, (tb, 8), 1)
    below = (lax.broadcasted_iota(I32, (tb, tb), 0)
             > lax.broadcasted_iota(I32, (tb, tb), 1))
    lstrict = jnp.where(below, 1.0, 0.0).astype(BF16)
    base = carry_ref[...]
    out = jnp.zeros((tb, 8), I32)
    for k in range(TOP_K):
        oh = jnp.where(lane == te[:, k:k + 1], 1.0, 0.0)
        before = _dotf(lstrict, oh.astype(BF16)) + base
        rank_k = jnp.sum(oh * before, axis=-1, keepdims=True)
        out = jnp.where(l8 == k, rank_k.astype(I32), out)
        base = base + jnp.sum(oh, axis=0, keepdims=True)
    rank_ref[...] = out
    carry_ref[...] = base
    cnt_ref[...] = base.astype(I32)


def _rank_call(te):
    n_tok = te.shape[0]
    tb = RANK_TILE
    return pl.pallas_call(
        _rank_kernel,
        grid=(n_tok // tb,),
        in_specs=[pl.BlockSpec((tb, 8), lambda i: (i, 0))],
        out_specs=[pl.BlockSpec((tb, 8), lambda i: (i, 0)), pl.BlockSpec((1, 128), lambda i: (0, 0))],
        out_shape=[jax.ShapeDtypeStruct((n_tok, 8), I32), jax.ShapeDtypeStruct((1, 128), I32)],
        scratch_shapes=[pltpu.VMEM((1, 128), F32)],
        compiler_params=_params(("arbitrary",)),
        name="moe_rank",
    )(te)


def _slot_kernel(tbl_ref, te_ref, rank_ref, slot_ref):
    te = te_ref[...]
    start = jnp.zeros(te.shape, I32)
    for e in range(N_EXPERTS):
        start = jnp.where(te == e, tbl_ref[e], start)
    slot_ref[...] = start + rank_ref[...]


def _slot_call(tbl, te, rank):
    n_tok = te.shape[0]
    tb = RANK_TILE
    spec = pl.BlockSpec((tb, 8), lambda i, tbl: (i, 0))
    return pl.pallas_call(
        _slot_kernel,
        grid_spec=pltpu.PrefetchScalarGridSpec(num_scalar_prefetch=1, grid=(n_tok // tb,),
                                               in_specs=[spec, spec], out_specs=spec),
        out_shape=jax.ShapeDtypeStruct((n_tok, 8), I32),
        compiler_params=_params(("parallel",)),
        name="moe_slots",
    )(tbl, te, rank)


def _slot_rows(slot_ref, r0, nt):
    return [pl.multiple_of(slot_ref[0, 0, (r0 + u) * 8 + k] * nt, nt)
            for u in range(DMA_UNROLL) for k in range(TOP_K)]


def _dispatch_kernel(tbl_ref, slot_ref, f_ref, xs_hbm, zbuf, sem, zsem, *, tb, tm, nt):
    i = pl.program_id(0)

    @pl.when(i == 0)
    def _():
        zbuf[...] = jnp.zeros_like(zbuf)
        fills = [pltpu.make_async_copy(
            zbuf, xs_hbm.at[pl.ds(pl.multiple_of(tbl_ref[N_EXPERTS + e] * nt, nt), tm * nt)], zsem)
            for e in range(N_EXPERTS)]
        for fill in fills:
            fill.start()
        for fill in fills:
            fill.wait()

        def fill_unused(blk, carry):
            tail = pltpu.make_async_copy(
                zbuf, xs_hbm.at[pl.ds(pl.multiple_of(blk * (tm * nt), tm * nt), tm * nt)], zsem)
            tail.start()
            tail.wait()
            return carry
        lax.fori_loop(tbl_ref[2 * N_EXPERTS], xs_hbm.shape[0] // (tm * nt), fill_unused, 0)

    def body(g, carry):
        r0 = g * DMA_UNROLL
        dsts = _slot_rows(slot_ref, r0, nt)
        for u in range(DMA_UNROLL):
            src = f_ref.at[pl.ds(pl.multiple_of((r0 + u) * nt, nt), nt)]
            for k in range(TOP_K):
                pltpu.make_async_copy(src, xs_hbm.at[pl.ds(dsts[u * TOP_K + k], nt)],
                                      sem).start(priority=k % 2)
        return carry

    lax.fori_loop(0, tb // DMA_UNROLL, body, 0)
    for k in range(TOP_K):
        pltpu.make_async_copy(f_ref, xs_hbm.at[pl.ds(0, tb * nt)], sem).wait()


def _dispatch_call(tbl, slot3, f_tiles, n_slots, tm):
    n_blk, _, per = slot3.shape
    tb = per // 8
    nt = f_tiles.shape[0] // (n_blk * tb)
    grid_spec = pltpu.PrefetchScalarGridSpec(
        num_scalar_prefetch=1,
        grid=(n_blk,),
        in_specs=[pl.BlockSpec((1, 1, per), lambda i, tbl: (i, 0, 0), memory_space=pltpu.SMEM),
                  pl.BlockSpec((tb * nt, 128), lambda i, tbl: (i, 0))],
        out_specs=pl.BlockSpec(memory_space=pl.ANY),
        scratch_shapes=[pltpu.VMEM((tm * nt, 128), F32), pltpu.SemaphoreType.DMA(()),
                        pltpu.SemaphoreType.DMA(())])
    return pl.pallas_call(
        functools.partial(_dispatch_kernel, tb=tb, tm=tm, nt=nt),
        grid_spec=grid_spec,
        out_shape=jax.ShapeDtypeStruct(((n_slots + tm) * nt, 128), F32),
        compiler_params=_params(("arbitrary",), VMEM_LIMIT),
        name="moe_dispatch",
    )(tbl, slot3, f_tiles)


def _expert_kernel(be_ref, nu_ref, xs_ref, wgu_ref, bgu_ref, wd_ref, bd_ref, ys_ref,
                   wgu_s, wd_s, *, tm, d_ff):
    i = pl.program_id(0)
    nt = xs_ref.shape[0] // tm

    @pl.when(i < nu_ref[0])
    def _():
        @pl.when(jnp.logical_or(i == 0, be_ref[i] != be_ref[jnp.maximum(i - 1, 0)]))
        def _():
            for c in range(0, wgu_s.shape[0], 128):
                wgu_s[c:c + 128, :] = wgu_ref[c:c + 128, :].astype(BF16)
            for c in range(0, wd_s.shape[0], 128):
                wd_s[c:c + 128, :] = wd_ref[c:c + 128, :].astype(BF16)

        for r0 in range(0, tm, MOE_ROWS):
            x = jnp.concatenate(_load_token_tiles(xs_ref, MOE_ROWS, nt, base=r0 * nt),
                                axis=1).astype(BF16)
            gu = _dotf(x, wgu_s[...]) + bgu_ref[...]
            gate = jnp.minimum(gu[:, 0:d_ff], SWIGLU_LIMIT)
            up = jnp.clip(gu[:, d_ff:2 * d_ff], -SWIGLU_LIMIT, SWIGLU_LIMIT)
            act = (up + 1.0) * gate * _sigmoid(SWIGLU_ALPHA * gate)
            _store_token_tiles(ys_ref, _dotf(act.astype(BF16), wd_s[...]) + bd_ref[...],
                               base=r0 * nt)

    @pl.when(i >= nu_ref[0])
    def _():
        ys_ref[...] = jnp.zeros_like(ys_ref)


def _expert_call(block_expert, n_used, xs_tiles, wgu, bgu, wd, bd, layer, n_blocks, tm):
    d, two_ff = wgu.shape[2:]
    d_ff = two_ff // 2
    nt = d // 128
    ex = lambda i, be, nu: (layer, be[i], 0, 0)
    grid_spec = pltpu.PrefetchScalarGridSpec(
        num_scalar_prefetch=2,
        grid=(n_blocks,),
        in_specs=[pl.BlockSpec((tm * nt, 128), lambda i, be, nu: (jnp.minimum(i, nu[0] - 1), 0)),
                  pl.BlockSpec((None, None, d, two_ff), ex),
                  pl.BlockSpec((None, None, 1, two_ff), ex),
                  pl.BlockSpec((None, None, d_ff, d), ex),
                  pl.BlockSpec((None, None, 1, d), ex)],
        out_specs=pl.BlockSpec((tm * nt, 128), lambda i, be, nu: (i, 0)),
        scratch_shapes=[pltpu.VMEM((d, two_ff), BF16), pltpu.VMEM((d_ff, d), BF16)])
    return pl.pallas_call(
        functools.partial(_expert_kernel, tm=tm, d_ff=d_ff),
        grid_spec=grid_spec,
        out_shape=jax.ShapeDtypeStruct((n_blocks * tm * nt, 128), F32),
        compiler_params=_params(("arbitrary",), VMEM_LIMIT),
        name="moe_experts",
    )(block_expert, n_used, xs_tiles, wgu, bgu, wd, bd)


def _route(counts, n_tok, tm):
    n_blocks = -(-(n_tok * TOP_K + N_EXPERTS * (tm - 1)) // tm)
    padded = (counts + tm - 1) // tm * tm
    padded_end = jnp.cumsum(padded)
    group_start = padded_end - padded
    block_start = jnp.arange(n_blocks, dtype=I32) * tm
    block_expert = jnp.minimum(
        jnp.sum((padded_end[None, :] <= block_start[:, None]).astype(I32), axis=1), N_EXPERTS - 1)
    n_used = (padded_end[-1] // tm).astype(I32).reshape(1)
    table = jnp.concatenate([group_start, group_start + counts, n_used]).astype(I32)
    return table, block_expert, n_used, n_blocks


def _fin_kernel(scur_ref, snxt_ref, g_ref, x_ref, g2_ref, nw_ref, ys_hbm, out_ref, buf, sem,
                *, tb, nt):
    i = pl.program_id(0)
    s = i % 2

    def gather(slot_ref, ss):
        def body(g, carry):
            r0 = g * DMA_UNROLL
            srcs = _slot_rows(slot_ref, r0, nt)
            for u in range(DMA_UNROLL):
                for k in range(TOP_K):
                    dst = pl.multiple_of((k * tb + r0 + u) * nt, nt)
                    pltpu.make_async_copy(ys_hbm.at[pl.ds(srcs[u * TOP_K + k], nt)],
                                          buf.at[ss, pl.ds(dst, nt)],
                                          sem.at[ss]).start(priority=k % 2)
            return carry
        lax.fori_loop(0, tb // DMA_UNROLL, body, 0)

    @pl.when(i == 0)
    def _():
        gather(scur_ref, 0)

    @pl.when(i + 1 < pl.num_programs(0))
    def _():
        gather(snxt_ref, 1 - s)

    pltpu.make_async_copy(ys_hbm.at[pl.ds(0, TOP_K * tb * nt)], buf.at[s], sem.at[s]).wait()
    gates = g_ref[...]
    parts = []
    for j in range(nt):
        acc = None
        for k in range(TOP_K):
            v = buf[s, pl.ds(k * tb * nt + j, tb, stride=nt), :] * gates[:, k:k + 1]
            acc = v if acc is None else acc + v
        parts.append(acc)
    m = jnp.concatenate(parts, axis=1)
    out_ref[0] = x_ref[0] + g2_ref[0] * _rms(m, nw_ref[...])


def _fin_call(slot3, gates, ys_tiles, xn, mod_l, nw, n_ctx_blk, blk0):
    b, t, d = xn.shape
    tb = ROW_TILE
    nblk = t // tb
    nt = d // 128
    n_blk = b * nblk

    def mrow(i):
        return jnp.where(i % nblk + blk0 < n_ctx_blk, b, i // nblk)

    smem = functools.partial(pl.BlockSpec, memory_space=pltpu.SMEM)
    return pl.pallas_call(
        functools.partial(_fin_kernel, tb=tb, nt=nt),
        grid=(n_blk,),
        in_specs=[smem((1, 1, tb * 8), lambda i: (i, 0, 0)),
                  smem((1, 1, tb * 8), lambda i: (jnp.minimum(i + 1, n_blk - 1), 0, 0)),
                  pl.BlockSpec((tb, 8), lambda i: (i, 0)),
                  pl.BlockSpec((1, tb, d), lambda i: (i // nblk, i % nblk, 0)),
                  pl.BlockSpec((1, 1, d), lambda i: (mrow(i), 0, 5)),
                  pl.BlockSpec((1, d), lambda i: (0, 0)),
                  pl.BlockSpec(memory_space=pl.ANY)],
        out_specs=pl.BlockSpec((1, tb, d), lambda i: (i // nblk, i % nblk, 0)),
        out_shape=jax.ShapeDtypeStruct((b, t, d), F32),
        scratch_shapes=[pltpu.VMEM((2, TOP_K * tb * nt, 128), F32), pltpu.SemaphoreType.DMA((2,))],
        compiler_params=_params(("arbitrary",), VMEM_LIMIT),
        name="moe_combine",
    )(slot3, slot3, gates, xn, mod_l, nw, ys_tiles)


def _in_proj_columns():
    cols = np.full((IN_COLS,), -1, np.int64)
    for sec in range(2):
        for n in range(256):
            part, hm, j = n // 128, (n % 128) // 16, n % 16
            m, h = hm // 4, hm % 4
            cols[sec * 256 + n] = sec * 256 + h * 64 + m * 32 + part * 16 + j
    cols[512:768] = np.arange(512, 768)
    o, s = IN_DA, 768
    cols[o:o + 1024] = s + np.arange(1024)
    o, s = o + IN_ML, s + 1024
    cols[o:o + 16] = s + np.arange(16)
    o, s = o + IN_G, s + 16
    cols[o:o + MLA_Q_RANK] = s + np.arange(MLA_Q_RANK)
    cols[o + 256:o + 256 + MLA_KV_RANK] = s + MLA_Q_RANK + np.arange(MLA_KV_RANK)
    cols[o + 384 + 64:o + 384 + 96] = s + MLA_Q_RANK + MLA_KV_RANK + np.arange(MLA_ROPE)
    o, s = o + IN_MLA, s + MLA_Q_RANK + MLA_KV_RANK + MLA_ROPE
    cols[o:o + 256] = s + np.arange(256)
    return cols


def _take_cols(w, cols):
    valid = jnp.asarray(cols >= 0)
    return jnp.where(valid, jnp.take(w, jnp.asarray(np.maximum(cols, 0)), axis=-1), 0.0)


def _rope_tables(n_ctx, n_lat):
    pos = jnp.arange(n_lat)
    inv = ROPE_THETA ** (-jnp.arange(8, dtype=F32) / 8)
    ang = jnp.concatenate([(pos // GRID_W)[:, None] * inv, (pos % GRID_W)[:, None] * inv], axis=-1)
    cos = jnp.concatenate([jnp.ones((n_ctx, 16), F32), jnp.cos(ang)], axis=0)
    sin = jnp.concatenate([jnp.zeros((n_ctx, 16), F32), jnp.sin(ang)], axis=0)
    t = n_ctx + n_lat
    cos_da, sin_da = jnp.tile(cos, (1, 8)), jnp.tile(sin, (1, 8))
    one, zero = jnp.ones((t, 64), F32), jnp.zeros((t, 64), F32)
    z16, z32 = jnp.zeros((t, 16), F32), jnp.zeros((t, 32), F32)
    cm = jnp.concatenate([one, cos, cos, jnp.ones((t, 32), F32)], axis=1)
    sa = jnp.concatenate([zero, -sin, z16, z32], axis=1)
    sb = jnp.concatenate([zero, z16, sin, z32], axis=1)
    return cos_da, sin_da, cm, sa, sb


def _mla_weights(w_uq, w_ukv):
    hd = MLA_NOPE + MLA_ROPE
    wq = jnp.pad(w_uq.reshape(MLA_Q_RANK, MLA_HEADS, hd),
                 ((0, 256 - MLA_Q_RANK), (0, 0), (0, 128 - hd))).reshape(256, 128 * MLA_HEADS)
    kv = w_ukv.reshape(MLA_KV_RANK, MLA_HEADS, -1)
    wk = jnp.pad(kv[:, :, :MLA_NOPE], ((0, 0), (0, 0), (0, 128 - MLA_NOPE)))
    wkv = jnp.concatenate([wk.reshape(MLA_KV_RANK, -1), kv[:, :, MLA_NOPE:].reshape(MLA_KV_RANK, -1)],
                          axis=1)
    return wq.astype(BF16), wkv.astype(BF16)


def _s5_layout(a_re, a_im, log_step, b_re, b_im, c_re, c_im):
    ns = S5_NGROUPS * S5_STATE
    eye = jnp.eye(S5_NGROUPS, dtype=F32)
    are = a_re.reshape(2, 1, ns)
    aim = a_im.reshape(2, 1, ns)
    ls = jnp.repeat(log_step, S5_STATE, axis=-1).reshape(2, 1, ns)
    bd_b = lambda w: jnp.einsum("dgph,gk->dghkp", w, eye).reshape(2, GROUP_W, ns)
    bd_c = lambda w: jnp.einsum("dghp,gk->dgpkh", w, eye).reshape(2, ns, GROUP_W)
    return are, aim, ls, bd_b(b_re), bd_b(b_im), bd_c(c_re), bd_c(c_im)


def kernel(x, c, ctx, c_ctx, w_mod, b_mod, norm_w, w_in, w_out, da_lambda, da_subln, ml_conv_w,
           ml_conv_b, ml_gate_b, ml_norm, mla_q_norm, mla_w_uq, mla_kv_norm, mla_w_ukv, s5_a_re,
           s5_a_im, s5_log_step, s5_b_re, s5_b_im, s5_c_re, s5_c_im, s5_d, s5_w_glu, s5_b_glu,
           moe_w_router, moe_b_router, moe_w_gate_up, moe_b_gate_up, moe_w_down, moe_b_down):
    bsz, n_lat, d = x.shape
    n_ctx = ctx.shape[1]
    t = n_ctx + n_lat
    depth = w_mod.shape[0]
    tm = ROW_TILE
    assert n_ctx % tm == 0 and n_lat % tm == 0 and bsz % 8 == 0 and bsz < 16
    assert n_ctx % ML_CHUNK == 0 and n_ctx % S5_CHUNK == 0
    n_ctx_blk = n_ctx // tm

    cc = jnp.pad(jnp.concatenate([c, c_ctx[None]], axis=0), ((0, 15 - bsz), (0, 0)))
    mod = _mod_call(cc, w_mod, b_mod)
    cos_da, sin_da, cm, sa, sb = _rope_tables(n_ctx, n_lat)
    in_cols = _in_proj_columns()
    xa, xb = ctx, x

    for l in range(depth):
        last = l == depth - 1
        lambda_init = 0.8 - 0.6 * math.exp(-0.3 * l)
        mod_l = mod[l].reshape(16, 1, 6 * d)
        w_in_p = _take_cols(w_in[l], in_cols).astype(BF16)
        wq, wkv = _mla_weights(mla_w_uq[l], mla_w_ukv[l])
        qn = jnp.pad(mla_q_norm[l], (0, 256 - MLA_Q_RANK)).reshape(1, 256)
        mla = (qn, wq, mla_kv_norm[l].reshape(1, -1), wkv, cm, sa, sb)
        p_da, p_ml, p_g, q_mla, k_mla, v_mla, u_t = _in_call(
            xa, xb, mod_l, norm_w[l, 0].reshape(1, d), w_in_p, cos_da, sin_da, mla, n_ctx_blk)

        da_extra = [da_lambda[l], jnp.tile(da_subln[l], 4).reshape(1, GROUP_W)]
        da_kern = functools.partial(_da_kernel, lambda_init=lambda_init)
        da_kw = dict(q_col=0, k_col=1, v_col=2, q_w=GROUP_W)
        ya = _attn_call(da_kern, "diff_attn", p_da, p_da, p_da, da_extra, q_blk0=n_ctx_blk,
                        n_q_blk=n_lat // tm, n_keys=t, **da_kw)
        mla_kw = dict(q_col=0, k_col=0, v_col=0, q_w=512)
        yc = _attn_call(_mla_attn_kernel, "mla_attn", q_mla, k_mla, v_mla, [], q_blk0=n_ctx_blk,
                        n_q_blk=n_lat // tm, n_keys=t, **mla_kw)
        if not last:
            ya_c = _attn_call(da_kern, "diff_attn_ctx", p_da, p_da, p_da, da_extra, q_blk0=0,
                              n_q_blk=n_ctx_blk, n_keys=n_ctx, **da_kw)
            yc_c = _attn_call(_mla_attn_kernel, "mla_attn_ctx", q_mla, k_mla, v_mla, [], q_blk0=0,
                              n_q_blk=n_ctx_blk, n_keys=n_ctx, **mla_kw)
            ya = jnp.concatenate([ya_c, ya], axis=1)
            yc = jnp.concatenate([yc_c, yc], axis=1)

        gb = jnp.pad(ml_gate_b[l], (0, 128 - 16)).reshape(1, 128)
        yb = _ml_call(p_ml, p_g, ml_conv_w[l], ml_conv_b[l].reshape(1, -1), gb,
                      ml_norm[l].reshape(1, -1), n_ctx)

        s5p = _s5_layout(s5_a_re[l], s5_a_im[l], s5_log_step[l], s5_b_re[l], s5_b_im[l],
                         s5_c_re[l], s5_c_im[l])
        ys = _s5_call(u_t.reshape(t * bsz, GROUP_W), *s5p, n_ctx, bsz)
        yd = _glu_call(ys, u_t, s5_d[l].reshape(1, -1),
                       s5_w_glu[l].astype(BF16), s5_b_glu[l].reshape(1, -1), bsz)

        blk0 = n_ctx_blk if last else 0
        wr = jnp.pad(moe_w_router[l], ((0, 0), (0, 128 - N_EXPERTS)))
        br = jnp.pad(moe_b_router[l], (0, 128 - N_EXPERTS), constant_values=-1e30).reshape(1, 128)
        xn, f, te, tg = _out_call(ya, yb, yc, yd, w_out[l].astype(BF16), xa, xb, mod_l,
                                  norm_w[l, 1].reshape(1, d), norm_w[l, 2].reshape(1, d), wr, br,
                                  n_ctx_blk, blk0)

        t_moe = t - blk0 * tm
        n_tok = bsz * t_moe
        te = te.reshape(n_tok, 8)
        rank, cnt = _rank_call(te)
        tbl, be, nu, n_blocks = _route(cnt[0, :N_EXPERTS], n_tok, MOE_TILE)
        slots = _slot_call(tbl, te, rank)
        slot3 = slots.reshape(n_tok // tm, 1, tm * 8)
        xs_tiles = _dispatch_call(tbl, slots.reshape(n_tok // DISPATCH_TILE, 1, DISPATCH_TILE * 8), f,
                                  n_blocks * MOE_TILE, MOE_TILE)
        ys_tiles = _expert_call(be, nu, xs_tiles, moe_w_gate_up,
                                moe_b_gate_up.reshape(depth, N_EXPERTS, 1, -1), moe_w_down,
                                moe_b_down.reshape(depth, N_EXPERTS, 1, -1), l, n_blocks, MOE_TILE)
        xa, xb = _fin_call(slot3, tg.reshape(n_tok, 8), ys_tiles, xn, mod_l,
                           norm_w[l, 3].reshape(1, d), n_ctx_blk, blk0), None
    return xa
```

```python
import functools
import math

import numpy as np
import jax
import jax.numpy as jnp
from jax import lax
from jax.experimental import pallas as pl
from jax.experimental.pallas import tpu as pltpu

F32, BF16, I32 = jnp.float32, jnp.bfloat16, jnp.int32
NORM_EPS = 1e-6
GRID_W = 64
ROPE_THETA = 10000.0
GROUP_W = 256
DA_QK = 32
ML_CHUNK = 128
ML_ROWS = 2
MLA_HEADS, MLA_NOPE, MLA_ROPE, MLA_Q_RANK, MLA_KV_RANK = 4, 64, 32, 192, 128
S5_NGROUPS, S5_GROUP, S5_STATE = 16, 16, 64
N_EXPERTS, TOP_K = 32, 4
SWIGLU_ALPHA, SWIGLU_LIMIT = 1.702, 7.0
NEG_INF = float("-inf")

ROW_TILE = 256
MOE_TILE = 256
S5_CHUNK = 64
RANK_TILE = 512
DISPATCH_TILE = 1024
DMA_UNROLL = 8
VMEM_LIMIT = 56 * 1024 * 1024

IN_DA, IN_ML, IN_G, IN_MLA, IN_S5 = 768, 1024, 128, 512, 256
IN_COLS = IN_DA + IN_ML + IN_G + IN_MLA + IN_S5


def _params(sem, vmem=None):
    return pltpu.CompilerParams(dimension_semantics=sem, vmem_limit_bytes=vmem)


def _dotf(a, b):
    return jnp.dot(a, b, preferred_element_type=F32)


def _dot_nt(a, b):
    return lax.dot_general(a, b, (((1,), (1,)), ((), ())), preferred_element_type=F32)


def _split2(a):
    hi = a.astype(BF16)
    lo = (a - hi.astype(F32)).astype(BF16)
    return hi, lo


def _dot3(a, b):
    ah, al = _split2(a)
    bh, bl = _split2(b)
    return _dotf(ah, bh) + _dotf(ah, bl) + _dotf(al, bh)


def _dot_exact_rhs(a, rhs_b):
    a1 = a.astype(BF16)
    r1 = a - a1.astype(F32)
    a2 = r1.astype(BF16)
    a3 = (r1 - a2.astype(F32)).astype(BF16)
    return _dotf(a1, rhs_b) + _dotf(a2, rhs_b) + _dotf(a3, rhs_b)


def _rms(x, w):
    ms = jnp.mean(x * x, axis=-1, keepdims=True)
    return x * lax.rsqrt(ms + NORM_EPS) * w


def _head_rms(a, width):
    n = a.shape[-1]
    sh = int(math.log2(width))
    r = lax.broadcasted_iota(I32, (n, n), 0) >> sh
    c = lax.broadcasted_iota(I32, (n, n), 1) >> sh
    g = jnp.where(r == c, 1.0 / width, 0.0).astype(BF16)
    hi, lo = _split2(a * a)
    ms = _dotf(hi, g) + _dotf(lo, g)
    return a * lax.rsqrt(ms + NORM_EPS)


def _store_token_tiles(ref, val):
    tm, d = val.shape
    nt = d // 128
    for j in range(nt):
        ref[pl.ds(j, tm, stride=nt), :] = val[:, 128 * j:128 * (j + 1)]


def _load_token_tiles(ref, tm, nt):
    return jnp.concatenate([ref[pl.ds(j, tm, stride=nt), :] for j in range(nt)], axis=1)


def _sigmoid(x):
    return jax.nn.sigmoid(x)


def _log_sigmoid(x):
    return jnp.minimum(x, 0.0) - jnp.log(1.0 + jnp.exp(-jnp.abs(x)))


def _mod_kernel(c_ref, w_ref, b_ref, o_ref):
    c = c_ref[...]
    o_ref[0] = _dot3(c * _sigmoid(c), w_ref[0]) + b_ref[0]


def _mod_call(cc, w_mod, b_mod):
    n_layers, d, n = w_mod.shape
    tn = 1536
    return pl.pallas_call(
        _mod_kernel,
        grid=(n_layers, n // tn),
        in_specs=[pl.BlockSpec((16, d), lambda l, j: (0, 0)),
                  pl.BlockSpec((1, d, tn), lambda l, j: (l, 0, j)),
                  pl.BlockSpec((1, 1, tn), lambda l, j: (l, 0, j))],
        out_specs=pl.BlockSpec((1, 16, tn), lambda l, j: (l, 0, j)),
        out_shape=jax.ShapeDtypeStruct((n_layers, 16, n), F32),
        compiler_params=_params(("parallel", "parallel"), VMEM_LIMIT),
        name="mod_vectors",
    )(cc, w_mod, b_mod.reshape(n_layers, 1, n))


def _in_kernel(xa_ref, xb_ref, sh_ref, sc_ref, nw_ref, w_ref, c_ref, s_ref,
               qn_ref, wq_ref, kvn_ref, wkv_ref, cm_ref, sa_ref, sb_ref,
               da_ref, ml_ref, g_ref, mq_ref, mk_ref, mv_ref, s5_ref, *, qscale, mla_scale, n_a_blk):
    x = jnp.where(pl.program_id(1) < n_a_blk, xa_ref[0], xb_ref[0])
    h = _rms(x, nw_ref[...]) * (1.0 + sc_ref[0]) + sh_ref[0]
    hb = h.astype(BF16)
    da = _dotf(hb, w_ref[:, 0:IN_DA])
    c = c_ref[...]
    s = s_ref[...]
    q1, q2, k1, k2 = da[:, 0:128], da[:, 128:256], da[:, 256:384], da[:, 384:512]
    da_ref[0, :, 0:128] = ((q1 * c - q2 * s) * qscale).astype(BF16)
    da_ref[0, :, 128:256] = ((q2 * c + q1 * s) * qscale).astype(BF16)
    da_ref[0, :, 256:384] = (k1 * c - k2 * s).astype(BF16)
    da_ref[0, :, 384:512] = (k2 * c + k1 * s).astype(BF16)
    da_ref[0, :, 512:768] = da[:, 512:768].astype(BF16)
    o = IN_DA
    ml_ref[0] = _dotf(hb, w_ref[:, o:o + IN_ML]).astype(BF16)
    o += IN_ML
    g_ref[0] = _dotf(hb, w_ref[:, o:o + IN_G])
    o += IN_G
    _mla_project(_dotf(hb, w_ref[:, o:o + IN_MLA]), qn_ref, wq_ref, kvn_ref, wkv_ref,
                 cm_ref, sa_ref, sb_ref, mq_ref, mk_ref, mv_ref, mla_scale)
    o += IN_MLA
    s5_ref[...] = _dotf(hb, w_ref[:, o:o + IN_S5])


def _stream_specs(xa, xb, blk0):
    tm = ROW_TILE
    d = xa.shape[2]
    n_a = xa.shape[1] // tm
    spec_a = pl.BlockSpec((1, tm, d), lambda bi, ti: (bi, jnp.minimum(ti + blk0, n_a - 1), 0))
    spec_b = pl.BlockSpec((1, tm, d), lambda bi, ti: (bi, jnp.maximum(ti + blk0 - n_a, 0), 0))
    return spec_a, spec_b, n_a


def _in_call(xa, xb, mod_l, nw, w_in_p, cos_da, sin_da, mla, n_ctx_blk):
    b, _, d = xa.shape
    t = xa.shape[1] + (0 if xb is None else xb.shape[1])
    tm = ROW_TILE
    spec_a, spec_b, n_a_blk = _stream_specs(xa, xb, 0)

    def mrow(bi, ti):
        return jnp.where(ti < n_ctx_blk, b, bi)

    row3 = lambda bi, ti: (bi, ti, 0)
    tab = pl.BlockSpec((tm, 128), lambda bi, ti: (ti, 0))
    return pl.pallas_call(
        functools.partial(_in_kernel, qscale=DA_QK ** -0.5 * LOG2_E,
                          mla_scale=(MLA_NOPE + MLA_ROPE) ** -0.5 * LOG2_E, n_a_blk=n_a_blk),
        grid=(b, t // tm),
        in_specs=[spec_a, spec_b,
                  pl.BlockSpec((1, 1, d), lambda bi, ti: (mrow(bi, ti), 0, 0)),
                  pl.BlockSpec((1, 1, d), lambda bi, ti: (mrow(bi, ti), 0, 1)),
                  pl.BlockSpec((1, d), lambda bi, ti: (0, 0)),
                  pl.BlockSpec((d, IN_COLS), lambda bi, ti: (0, 0)),
                  tab, tab] + [pl.BlockSpec(a.shape, lambda bi, ti: (0, 0)) for a in mla[:4]]
                 + [tab, tab, tab],
        out_specs=[pl.BlockSpec((1, tm, IN_DA), row3),
                   pl.BlockSpec((1, tm, IN_ML), row3),
                   pl.BlockSpec((1, tm, IN_G), row3),
                   pl.BlockSpec((1, tm, 512), row3), pl.BlockSpec((1, tm, 512), row3),
                   pl.BlockSpec((1, tm, GROUP_W), row3),
                   pl.BlockSpec((tm, IN_S5), lambda bi, ti: (ti, bi))],
        out_shape=[jax.ShapeDtypeStruct((b, t, IN_DA), BF16),
                   jax.ShapeDtypeStruct((b, t, IN_ML), BF16),
                   jax.ShapeDtypeStruct((b, t, IN_G), F32),
                   jax.ShapeDtypeStruct((b, t, 512), BF16), jax.ShapeDtypeStruct((b, t, 512), BF16),
                   jax.ShapeDtypeStruct((b, t, GROUP_W), BF16),
                   jax.ShapeDtypeStruct((t, b * IN_S5), F32)],
        compiler_params=_params(("parallel", "parallel"), VMEM_LIMIT),
        name="in_proj",
    )(xa, xa if xb is None else xb, mod_l, mod_l, nw, w_in_p, cos_da, sin_da, *mla)


LOG2_E = 1.0 / math.log(2.0)


def _softmax_rows(s):
    mx = jnp.max(s, axis=-1, keepdims=True)
    p = jnp.exp2(s - mx)
    return p, jnp.sum(p, axis=-1, keepdims=True)


def _da_kernel(q_ref, k_ref, v_ref, lam_ref, sub_ref, o_ref, *, lambda_init):
    q = q_ref[0]
    k = k_ref[0]
    v = v_ref[0]
    lp = lam_ref[...]
    lam = (jnp.exp(jnp.sum(lp[0:1] * lp[1:2], axis=-1, keepdims=True))
           - jnp.exp(jnp.sum(lp[2:3] * lp[3:4], axis=-1, keepdims=True)) + lambda_init)
    lane = lax.broadcasted_iota(I32, (1, GROUP_W), 1)
    grp = (lane & 127) >> 4
    head = lane >> 6
    acc = jnp.zeros((q.shape[0], GROUP_W), F32)
    for h in range(4):
        ps, rs = [], []
        for m in range(2):
            qm = jnp.where(grp == m * 4 + h, q, jnp.zeros_like(q))
            p, l = _softmax_rows(_dot_nt(qm, k))
            ps.append(p.astype(BF16))
            rs.append(((1.0 if m == 0 else lam) / l).astype(BF16))
        w = ps[0] * rs[0] - ps[1] * rs[1]
        vm = jnp.where(head == h, v, jnp.zeros_like(v))
        acc = acc + _dotf(w, vm)
    y = _head_rms(acc, 64) * sub_ref[...] * (1.0 - lambda_init)
    o_ref[0] = y.astype(BF16)


def _mla_attn_kernel(q_ref, k_ref, v_ref, o_ref):
    q = q_ref[0]
    k = k_ref[0]
    v = v_ref[0]
    head = lax.broadcasted_iota(I32, (1, GROUP_W), 1) >> 6
    acc = jnp.zeros((q.shape[0], GROUP_W), F32)
    for h in range(MLA_HEADS):
        sl = slice(128 * h, 128 * (h + 1))
        p, l = _softmax_rows(_dot_nt(q[:, sl], k[:, sl]))
        vm = jnp.where(head == h, v, jnp.zeros_like(v))
        acc = acc + _dotf(p.astype(BF16), vm) * (1.0 / l)
    o_ref[0] = acc.astype(BF16)


def _attn_call(kernel, name, q_arr, k_arr, v_arr, extra, *, q_blk0, n_q_blk, n_keys,
               q_col, k_col, v_col, q_w):
    b = q_arr.shape[0]
    tq = ROW_TILE
    in_specs = [pl.BlockSpec((1, tq, q_w), lambda bi, qi: (bi, qi + q_blk0, q_col)),
                pl.BlockSpec((1, n_keys, q_w), lambda bi, qi: (bi, 0, k_col)),
                pl.BlockSpec((1, n_keys, GROUP_W), lambda bi, qi: (bi, 0, v_col))]
    in_specs += [pl.BlockSpec(e.shape, lambda bi, qi: (0, 0)) for e in extra]
    return pl.pallas_call(
        kernel,
        grid=(b, n_q_blk),
        in_specs=in_specs,
        out_specs=pl.BlockSpec((1, tq, GROUP_W), lambda bi, qi: (bi, qi, 0)),
        out_shape=jax.ShapeDtypeStruct((b, n_q_blk * tq, GROUP_W), BF16),
        compiler_params=_params(("parallel", "arbitrary"), VMEM_LIMIT),
        name=name,
    )(q_arr, k_arr, v_arr, *extra)


def _mla_project(p, qn_ref, wq_ref, kvn_ref, wkv_ref, c_ref, sa_ref, sb_ref, q_ref, k_ref, v_ref,
                 scale):
    cq, ckv, kr = p[:, 0:256], p[:, 256:384], p[:, 384:512]
    msq = jnp.sum(cq * cq, axis=-1, keepdims=True) * (1.0 / MLA_Q_RANK)
    qn = (cq * lax.rsqrt(msq + NORM_EPS) * qn_ref[...]).astype(BF16)
    q = _dotf(qn, wq_ref[...])
    c = c_ref[...]
    sa = sa_ref[...]
    sb = sb_ref[...]

    def rope(a):
        return a * c + pltpu.roll(a, 112, 1) * sa + pltpu.roll(a, 16, 1) * sb

    for h in range(MLA_HEADS):
        sl = slice(128 * h, 128 * (h + 1))
        q_ref[0, :, sl] = (rope(q[:, sl]) * scale).astype(BF16)
    kvn = (_rms(ckv, kvn_ref[...])).astype(BF16)
    kv = _dotf(kvn, wkv_ref[...])
    krr = rope(kr)
    for h in range(MLA_HEADS):
        sl = slice(128 * h, 128 * (h + 1))
        k_ref[0, :, sl] = (kv[:, sl] + krr).astype(BF16)
    v_ref[0] = kv[:, 512:768].astype(BF16)


def _ml_kernel(p_ref, g_ref, cw_ref, cb_ref, gb_ref, nw_ref, y_ref,
               qt_s, k_s, vt_s, ot_s, gt_s, hf_s, c_s, m_s, *, n_ctx_chunks, n_chunks):
    cl = ML_CHUNK
    t_total = n_chunks * cl
    nbr = p_ref.shape[0]
    row = lax.broadcasted_iota(I32, (cl, 1), 0)
    si = lax.broadcasted_iota(I32, (cl, cl), 0)
    ti = lax.broadcasted_iota(I32, (cl, cl), 1)
    lane128 = lax.broadcasted_iota(I32, (1, 128), 1)
    lane256 = lax.broadcasted_iota(I32, (1, 256), 1)
    row128 = lax.broadcasted_iota(I32, (128, 1), 0)
    hmask = [(lane256 >> 6) == h for h in range(4)]
    is_f = jnp.logical_and(((lane128 >> 2) & 1) == 1, lane128 < 16)
    r_nd = lax.broadcasted_iota(I32, (384, 512), 0)
    c_nd = lax.broadcasted_iota(I32, (384, 512), 1) >> 7
    nd_head = jnp.where(r_nd < 256, r_nd >> 6, r_nd - 256)
    nd_mask = nd_head == c_nd
    nd_ones = jnp.where(jnp.logical_and(nd_mask, r_nd >= 256), 1.0, 0.0).astype(BF16)
    r_st = lax.broadcasted_iota(I32, (384, 256), 0)
    st_head = jnp.where(r_st < 256, r_st >> 6, r_st - 256)
    st_mask = st_head == (lax.broadcasted_iota(I32, (384, 256), 1) >> 6)
    st_rowhead = jnp.where(lax.broadcasted_iota(I32, (384, 1), 0) < 256,
                           lax.broadcasted_iota(I32, (384, 1), 0) >> 6,
                           lax.broadcasted_iota(I32, (384, 1), 0) - 256)
    w0, w1, w2 = cw_ref[0:1], cw_ref[1:2], cw_ref[2:3]
    cb = cb_ref[...]
    gb = gb_ref[...]

    def prep_one(c, bb):
        s0 = pl.multiple_of(c * cl, cl)
        x = p_ref[bb, pl.ds(s0, cl), 0:512].astype(F32)
        sp = pl.multiple_of(jnp.maximum(s0 - 16, 0), 16)
        sn = pl.multiple_of(jnp.minimum(s0 + cl, t_total - 16), 16)
        has_prev = jnp.logical_and(c != 0, c != n_ctx_chunks).astype(F32)
        has_next = jnp.logical_and(c != n_ctx_chunks - 1, c != n_chunks - 1).astype(F32)
        prev_row = p_ref[bb, pl.ds(sp, 16), 0:512][15:16].astype(F32) * has_prev
        next_row = p_ref[bb, pl.ds(sn, 16), 0:512][0:1].astype(F32) * has_next
        xp = jnp.where(row == 0, prev_row, pltpu.roll(x, 1, 0))
        xn = jnp.where(row == cl - 1, next_row, pltpu.roll(x, cl - 1, 0))
        z = xp * w0 + x * w1 + xn * w2 + cb
        qk = z * _sigmoid(z)
        qt_s[bb, c] = qk[:, 0:256].T.astype(BF16)
        k_s[bb, c] = (qk[:, 256:512] * (64 ** -0.5)).astype(BF16)
        vt_s[bb, c] = p_ref[bb, pl.ds(s0, cl), 512:768].astype(F32).T
        ot_s[bb, c] = p_ref[bb, pl.ds(s0, cl), 768:1024].astype(F32).T
        g = g_ref[bb, pl.ds(s0, cl), :] + gb
        gt_s[bb, c] = jnp.where(is_f, _log_sigmoid(g), g).T[0:16]

    def prep_body(c, carry):
        for bb in range(nbr):
            prep_one(c, bb)
        return carry

    def rows_to_blocks(rows, n):
        return jnp.concatenate([jnp.broadcast_to(r, (n, 128)) for r in rows], axis=0)

    def head_rows(rows):
        out = jnp.zeros((128, 128), F32)
        for h in range(4):
            out = jnp.where(row128 == h, rows[h], out)
        return out

    def chunk(c, reverse, bb):
        g_t = gt_s[bb, c]
        tri = (si >= ti) if reverse else (si <= ti)
        bc_t = _dot_exact_rhs(g_t, jnp.where(tri, 1.0, 0.0).astype(BF16))
        qt = qt_s[bb, c]
        kb = k_s[bb, c]
        m_all = m_s[bb]
        off = 8 if reverse else 0
        a_rows, r_rows, tots, m_old = [], [], [], []
        for h in range(4):
            il, fl = off + h, off + 4 + h
            a_rows.append(bc_t[fl:fl + 1, :])
            r_rows.append(g_t[il:il + 1, :] - bc_t[fl:fl + 1, :])
            tots.append(bc_t[fl:fl + 1, 0:1] if reverse else bc_t[fl:fl + 1, cl - 1:cl])
            m_old.append(m_all[:, h:h + 1])
        r_all = jnp.concatenate(r_rows, axis=1)
        d_t = jnp.broadcast_to(r_all, (cl, 4 * cl)).T + rows_to_blocks(a_rows, cl)
        lw = jnp.where(jnp.concatenate([tri] * 4, axis=0), d_t, NEG_INF)
        mt, wi, emt = [], [], []
        for h in range(4):
            linter = a_rows[h] + m_old[h]
            mt_h = jnp.maximum(linter, jnp.max(lw[cl * h:cl * (h + 1)], axis=0, keepdims=True))
            mt.append(mt_h)
            wi.append(jnp.exp(linter - mt_h))
            emt.append(jnp.exp(-mt_h))
        kstack = jnp.concatenate([jnp.where(hmask[h], kb, jnp.zeros_like(kb)) for h in range(4)],
                                 axis=0)
        s_t = _dotf(kstack, qt)
        w_t = (s_t * jnp.exp(lw - rows_to_blocks(mt, cl))).astype(BF16)
        vt = vt_s[bb, c]
        vt4 = jnp.concatenate([vt.astype(BF16)] * 4, axis=1)
        lhs_nd = jnp.concatenate([vt4, jnp.zeros((128, 512), BF16)], axis=0)
        lhs_nd = jnp.where(nd_mask, lhs_nd, jnp.zeros_like(lhs_nd)) + nd_ones
        c_aug = c_s[bb]
        nd = _dotf(lhs_nd, w_t) + jnp.concatenate(
            [rows_to_blocks(wi, 64), head_rows(wi)], axis=0) * _dotf(c_aug.astype(BF16), qt)
        den = [nd[256 + h:257 + h, :] for h in range(4)]
        h_t = nd[0:256] / jnp.maximum(jnp.abs(rows_to_blocks(den, 64)), rows_to_blocks(emt, 64))
        wupd, decs, m_new_all = [], [], m_all
        for h in range(4):
            lupd = tots[h] + r_rows[h]
            m_new = jnp.maximum(tots[h] + m_old[h], jnp.max(lupd, axis=-1, keepdims=True))
            wupd.append(jnp.exp(lupd - m_new))
            decs.append(jnp.exp(tots[h] + m_old[h] - m_new))
            m_new_all = jnp.where(lane128 == h, m_new, m_new_all)
        lhs_u = jnp.concatenate([vt * rows_to_blocks(wupd, 64), head_rows(wupd)], axis=0)
        upd = _dotf(lhs_u.astype(BF16), kb)
        dec_col = jnp.zeros((384, 1), F32)
        for h in range(4):
            dec_col = jnp.where(st_rowhead == h, decs[h], dec_col)
        c_s[bb] = dec_col * c_aug + jnp.where(st_mask, upd, 0.0)
        m_s[bb] = m_new_all
        return h_t

    def fwd_body(i, carry):
        for bb in range(nbr):
            hf_s[bb, i] = chunk(i, False, bb)
        return carry

    def bwd_body(i, carry):
        c = jnp.where(i < n_ctx_chunks, n_ctx_chunks - 1 - i, n_chunks - 1 - (i - n_ctx_chunks))
        for bb in range(nbr):
            gated = _sigmoid(ot_s[bb, c]) * (hf_s[bb, c] + chunk(c, True, bb))
            ms = [jnp.mean(jnp.square(gated[64 * h:64 * (h + 1)]), axis=0, keepdims=True)
                  for h in range(4)]
            y_t = gated * lax.rsqrt(rows_to_blocks(ms, 64) + NORM_EPS) * nw_ref[...]
            y_ref[bb, pl.ds(pl.multiple_of(c * cl, cl), cl), :] = y_t.T.astype(BF16)
        return carry

    lax.fori_loop(0, n_chunks, prep_body, 0)
    c_s[...] = jnp.zeros_like(c_s)
    m_s[...] = jnp.zeros_like(m_s)
    lax.fori_loop(0, n_chunks, fwd_body, 0)
    c_s[...] = jnp.zeros_like(c_s)
    m_s[...] = jnp.zeros_like(m_s)
    lax.fori_loop(0, n_chunks, bwd_body, 0)


def _ml_call(p_ml, p_g, cw, cb, gb, nw, n_ctx):
    b, t, _ = p_ml.shape
    nc = t // ML_CHUNK
    nbr = ML_ROWS
    const = lambda bi: (0, 0)
    nw_col = jnp.broadcast_to(nw.reshape(GROUP_W, 1), (GROUP_W, 128))
    return pl.pallas_call(
        functools.partial(_ml_kernel, n_ctx_chunks=n_ctx // ML_CHUNK, n_chunks=nc),
        grid=(b // nbr,),
        in_specs=[pl.BlockSpec((nbr, t, IN_ML), lambda bi: (bi, 0, 0)),
                  pl.BlockSpec((nbr, t, IN_G), lambda bi: (bi, 0, 0)),
                  pl.BlockSpec(cw.shape, const), pl.BlockSpec(cb.shape, const),
                  pl.BlockSpec(gb.shape, const), pl.BlockSpec(nw_col.shape, const)],
        out_specs=pl.BlockSpec((nbr, t, GROUP_W), lambda bi: (bi, 0, 0)),
        out_shape=jax.ShapeDtypeStruct((b, t, GROUP_W), BF16),
        scratch_shapes=[pltpu.VMEM((nbr, nc, GROUP_W, 128), BF16),
                        pltpu.VMEM((nbr, nc, 128, GROUP_W), BF16),
                        pltpu.VMEM((nbr, nc, GROUP_W, 128), F32),
                        pltpu.VMEM((nbr, nc, GROUP_W, 128), F32),
                        pltpu.VMEM((nbr, nc, 16, 128), F32),
                        pltpu.VMEM((nbr, nc, GROUP_W, 128), F32),
                        pltpu.VMEM((nbr, 384, GROUP_W), F32), pltpu.VMEM((nbr, 1, 128), F32)],
        compiler_params=_params(("parallel",), VMEM_LIMIT),
        name="mlstm",
    )(p_ml, p_g, cw, cb, gb, nw_col)


def _s5_kernel(u_ref, are_ref, aim_ref, ls_ref, bre_ref, bim_ref, cre_ref, cim_ref, y_ref,
               ar_s, ai_s, bcat_s, ccat_s, st_s, bu_s, *, tc, nb):
    d = pl.program_id(0)
    i = pl.program_id(1)
    ns = S5_NGROUPS * S5_STATE

    @pl.when(i == 0)
    def _init():
        are = jnp.minimum(are_ref[0], -1e-4)
        aim = aim_ref[0]
        dt = jnp.exp(ls_ref[0])
        mag = jnp.exp(dt * are)
        abr = mag * jnp.cos(dt * aim)
        abi = mag * jnp.sin(dt * aim)
        inv = 1.0 / (are * are + aim * aim)
        fre = ((abr - 1.0) * are + abi * aim) * inv
        fim = (abi * are - (abr - 1.0) * aim) * inv
        bre = bre_ref[0]
        bim = bim_ref[0]
        bcat_s[:, 0:ns] = (bre * fre - bim * fim).astype(BF16)
        bcat_s[:, ns:2 * ns] = (bre * fim + bim * fre).astype(BF16)
        ccat_s[0:ns, :] = cre_ref[0].astype(BF16)
        ccat_s[ns:2 * ns, :] = (-cim_ref[0]).astype(BF16)
        ar_s[...] = jnp.broadcast_to(abr, (nb, ns))
        ai_s[...] = jnp.broadcast_to(abi, (nb, ns))
        st_s[...] = jnp.zeros_like(st_s)

    bu_s[...] = _dotf(u_ref[...].astype(BF16), bcat_s[...])
    ar = ar_s[...]
    ai = ai_s[...]

    def body(j, carry):
        xr, xi = carry
        t = j + d * (tc - 1 - 2 * j)
        r0 = pl.multiple_of(t * nb, nb)
        nr = ar * xr - ai * xi + bu_s[pl.ds(r0, nb), 0:ns]
        ni = ar * xi + ai * xr + bu_s[pl.ds(r0, nb), ns:2 * ns]
        bu_s[pl.ds(r0, nb), 0:ns] = nr
        bu_s[pl.ds(r0, nb), ns:2 * ns] = ni
        return nr, ni

    xr, xi = lax.fori_loop(0, tc, body, (st_s[0], st_s[1]), unroll=4)
    st_s[0] = xr
    st_s[1] = xi
    y = _dotf(bu_s[...].astype(BF16), ccat_s[...])
    for half in range(y.shape[1] // 128):
        y_ref[0, half] = y[:, 128 * half:128 * (half + 1)]


def _s5_call(u_tm, are, aim, ls, bre, bim, cre, cim, n_ctx, nb):
    rows, gw = u_tm.shape
    tc = S5_CHUNK
    n_chunks = rows // (tc * nb)
    n_ctx_chunks = n_ctx // tc
    ns = S5_NGROUPS * S5_STATE

    def chunk_of(d, i):
        rev = jnp.where(i < n_ctx_chunks, n_ctx_chunks - 1 - i, n_chunks - 1 - (i - n_ctx_chunks))
        return jnp.where(d == 0, i, rev)

    vec = pl.BlockSpec((1, 1, ns), lambda d, i: (d, 0, 0))
    return pl.pallas_call(
        functools.partial(_s5_kernel, tc=tc, nb=nb),
        grid=(2, n_chunks),
        in_specs=[pl.BlockSpec((tc * nb, gw), lambda d, i: (chunk_of(d, i), 0)),
                  vec, vec, vec,
                  pl.BlockSpec((1, gw, ns), lambda d, i: (d, 0, 0)),
                  pl.BlockSpec((1, gw, ns), lambda d, i: (d, 0, 0)),
                  pl.BlockSpec((1, ns, gw), lambda d, i: (d, 0, 0)),
                  pl.BlockSpec((1, ns, gw), lambda d, i: (d, 0, 0))],
        out_specs=pl.BlockSpec((1, gw // 128, tc * nb, 128),
                               lambda d, i: (d, 0, chunk_of(d, i), 0)),
        out_shape=jax.ShapeDtypeStruct((2, gw // 128, rows, 128), F32),
        scratch_shapes=[pltpu.VMEM((nb, ns), F32), pltpu.VMEM((nb, ns), F32),
                        pltpu.VMEM((gw, 2 * ns), BF16), pltpu.VMEM((2 * ns, gw), BF16),
                        pltpu.VMEM((2, nb, ns), F32), pltpu.VMEM((tc * nb, 2 * ns), F32)],
        compiler_params=_params(("arbitrary", "arbitrary"), VMEM_LIMIT),
        name="s5_scan",
    )(u_tm, are, aim, ls, bre, bim, cre, cim)


def _glu_kernel(ys_ref, u_ref, d_ref, w_ref, b_ref, o_ref, *, nb, tq):
    gw = GROUP_W
    rows = []
    for b in range(nb):
        y_b = jnp.concatenate(
            [ys_ref[0, half, pl.ds(b, tq, stride=nb), :] + ys_ref[1, half, pl.ds(b, tq, stride=nb), :]
             for half in range(gw // 128)], axis=1)
        rows.append(y_b + u_ref[:, gw * b:gw * (b + 1)] * d_ref[...])
    y = jnp.concatenate(rows, axis=0)
    g = y * (0.5 * (1.0 + jnp.tanh(math.sqrt(2.0 / math.pi) * (y + 0.044715 * (y * y * y)))))
    z = _dotf(g.astype(BF16), w_ref[...]) + b_ref[...]
    out = (g * _sigmoid(z)).astype(BF16)
    for b in range(nb):
        o_ref[b] = out[tq * b:tq * (b + 1)]


def _glu_call(ys, u_t, dsk, w, bias, nb):
    t = u_t.shape[0]
    tq = ROW_TILE // nb
    gw = GROUP_W
    const = lambda ti: (0, 0)
    return pl.pallas_call(
        functools.partial(_glu_kernel, nb=nb, tq=tq),
        grid=(t // tq,),
        in_specs=[pl.BlockSpec((2, gw // 128, tq * nb, 128), lambda ti: (0, 0, ti, 0)),
                  pl.BlockSpec((tq, nb * gw), lambda ti: (ti, 0)),
                  pl.BlockSpec(dsk.shape, const), pl.BlockSpec(w.shape, const),
                  pl.BlockSpec(bias.shape, const)],
        out_specs=pl.BlockSpec((nb, tq, gw), lambda ti: (0, ti, 0)),
        out_shape=jax.ShapeDtypeStruct((nb, t, gw), BF16),
        compiler_params=_params(("parallel",), VMEM_LIMIT),
        name="s5_glu",
    )(ys, u_t, dsk, w, bias)


def _out_kernel(ya_ref, yb_ref, yc_ref, yd_ref, w_ref, xa_ref, xb_ref, g1_ref, sh2_ref, sc2_ref,
                nw1_ref, nw2_ref, wr_ref, br_ref, xn_ref, f_ref, te_ref, tg_ref, *, n_a_blk, blk0):
    o = (_dotf(ya_ref[0], w_ref[0:256]) + _dotf(yb_ref[0], w_ref[256:512])
         + _dotf(yc_ref[0], w_ref[512:768]) + _dotf(yd_ref[0], w_ref[768:1024]))
    x = jnp.where(pl.program_id(1) + blk0 < n_a_blk, xa_ref[0], xb_ref[0])
    xn = x + g1_ref[0] * _rms(o, nw1_ref[...])
    xn_ref[0] = xn
    f = _rms(xn, nw2_ref[...]) * (1.0 + sc2_ref[0]) + sh2_ref[0]
    _store_token_tiles(f_ref, f)
    lg = _dot3(f, wr_ref[...]) + br_ref[...]
    tm = lg.shape[0]
    lane = lax.broadcasted_iota(I32, (tm, 128), 1).astype(F32)
    tops, idxs = [], []
    for _ in range(TOP_K):
        mx = jnp.max(lg, axis=-1, keepdims=True)
        idx = jnp.min(jnp.where(lg == mx, lane, 128.0), axis=-1, keepdims=True)
        tops.append(mx)
        idxs.append(idx.astype(I32))
        lg = jnp.where(lane == idx, NEG_INF, lg)
    ex = [jnp.exp(tv - tops[0]) for tv in tops]
    inv = 1.0 / (ex[0] + ex[1] + ex[2] + ex[3])
    l8 = lax.broadcasted_iota(I32, (tm, 8), 1)
    te = jnp.zeros((tm, 8), I32)
    tg = jnp.zeros((tm, 8), F32)
    for kk in range(TOP_K):
        te = jnp.where(l8 == kk, idxs[kk], te)
        tg = jnp.where(l8 == kk, ex[kk] * inv, tg)
    te_ref[0] = te
    tg_ref[0] = tg


def _out_call(ya, yb, yc, yd, w_out, xa, xb, mod_l, nw1, nw2, wr, br, n_ctx_blk, blk0):
    b, _, d = xa.shape
    t = xa.shape[1] + (0 if xb is None else xb.shape[1])
    tm = ROW_TILE
    nblk = t // tm - blk0
    t_out = nblk * tm
    spec_a, spec_b, n_a_blk = _stream_specs(xa, xb, blk0)

    def mrow(bi, ti):
        return jnp.where(ti + blk0 < n_ctx_blk, b, bi)

    row3 = lambda bi, ti: (bi, ti + blk0, 0)
    out3 = lambda bi, ti: (bi, ti, 0)
    const = lambda bi, ti: (0, 0)
    modspec = lambda j: pl.BlockSpec((1, 1, d), lambda bi, ti: (mrow(bi, ti), 0, j))
    yspec = lambda y: pl.BlockSpec((1, tm, GROUP_W), row3 if y.shape[1] == t else out3)
    return pl.pallas_call(
        functools.partial(_out_kernel, n_a_blk=n_a_blk, blk0=blk0),
        grid=(b, nblk),
        in_specs=[yspec(ya), yspec(yb), yspec(yc), yspec(yd), pl.BlockSpec((d, d), const),
                  spec_a, spec_b, modspec(2), modspec(3), modspec(4),
                  pl.BlockSpec((1, d), const), pl.BlockSpec((1, d), const),
                  pl.BlockSpec((d, 128), const), pl.BlockSpec((1, 128), const)],
        out_specs=[pl.BlockSpec((1, tm, d), out3),
                   pl.BlockSpec((tm * d // 128, 128), lambda bi, ti: (bi * nblk + ti, 0)),
                   pl.BlockSpec((1, tm, 8), out3), pl.BlockSpec((1, tm, 8), out3)],
        out_shape=[jax.ShapeDtypeStruct((b, t_out, d), F32),
                   jax.ShapeDtypeStruct((b * t_out * d // 128, 128), F32),
                   jax.ShapeDtypeStruct((b, t_out, 8), I32),
                   jax.ShapeDtypeStruct((b, t_out, 8), F32)],
        compiler_params=_params(("parallel", "parallel"), VMEM_LIMIT),
        name="out_proj_router",
    )(ya, yb, yc, yd, w_out, xa, xa if xb is None else xb, mod_l, mod_l, mod_l, nw1, nw2, wr, br)


def _rank_kernel(te_ref, rank_ref, cnt_ref, carry_ref):
    i = pl.program_id(0)

    @pl.when(i == 0)
    def _():
        carry_ref[...] = jnp.zeros_like(carry_ref)

    te = te_ref[...]
    tb = te.shape[0]
    lane = lax.broadcasted_iota(I32, (tb, 128), 1)
    l8 = lax.broadcasted_iota(I32, (tb, 8), 1)
    below = (lax.broadcasted_iota(I32, (tb, tb), 0)
             > lax.broadcasted_iota(I32, (tb, tb), 1))
    lstrict = jnp.where(below, 1.0, 0.0).astype(BF16)
    base = carry_ref[...]
    out = jnp.zeros((tb, 8), I32)
    for k in range(TOP_K):
        oh = jnp.where(lane == te[:, k:k + 1], 1.0, 0.0)
        before = _dotf(lstrict, oh.astype(BF16)) + base
        rank_k = jnp.sum(oh * before, axis=-1, keepdims=True)
        out = jnp.where(l8 == k, rank_k.astype(I32), out)
        base = base + jnp.sum(oh, axis=0, keepdims=True)
    rank_ref[...] = out
    carry_ref[...] = base
    cnt_ref[...] = base.astype(I32)


def _rank_call(te):
    n_tok = te.shape[0]
    tb = RANK_TILE
    return pl.pallas_call(
        _rank_kernel,
        grid=(n_tok // tb,),
        in_specs=[pl.BlockSpec((tb, 8), lambda i: (i, 0))],
        out_specs=[pl.BlockSpec((tb, 8), lambda i: (i, 0)), pl.BlockSpec((1, 128), lambda i: (0, 0))],
        out_shape=[jax.ShapeDtypeStruct((n_tok, 8), I32), jax.ShapeDtypeStruct((1, 128), I32)],
        scratch_shapes=[pltpu.VMEM((1, 128), F32)],
        compiler_params=_params(("arbitrary",)),
        name="moe_rank",
    )(te)


def _slot_kernel(tbl_ref, te_ref, rank_ref, slot_ref):
    te = te_ref[...]
    start = jnp.zeros(te.shape, I32)
    for e in range(N_EXPERTS):
        start = jnp.where(te == e, tbl_ref[e], start)
    slot_ref[...] = start + rank_ref[...]


def _slot_call(tbl, te, rank):
    n_tok = te.shape[0]
    tb = RANK_TILE
    spec = pl.BlockSpec((tb, 8), lambda i, tbl: (i, 0))
    return pl.pallas_call(
        _slot_kernel,
        grid_spec=pltpu.PrefetchScalarGridSpec(num_scalar_prefetch=1, grid=(n_tok // tb,),
                                               in_specs=[spec, spec], out_specs=spec),
        out_shape=jax.ShapeDtypeStruct((n_tok, 8), I32),
        compiler_params=_params(("parallel",)),
        name="moe_slots",
    )(tbl, te, rank)


def _slot_rows(slot_ref, r0, nt):
    return [pl.multiple_of(slot_ref[0, 0, (r0 + u) * 8 + k] * nt, nt)
            for u in range(DMA_UNROLL) for k in range(TOP_K)]


def _dispatch_kernel(tbl_ref, slot_ref, f_ref, xs_hbm, zbuf, sem, zsem, *, tb, tm, nt):
    i = pl.program_id(0)

    @pl.when(i == 0)
    def _():
        zbuf[...] = jnp.zeros_like(zbuf)
        fills = [pltpu.make_async_copy(
            zbuf, xs_hbm.at[pl.ds(pl.multiple_of(tbl_ref[N_EXPERTS + e] * nt, nt), tm * nt)], zsem)
            for e in range(N_EXPERTS)]
        for fill in fills:
            fill.start()
        for fill in fills:
            fill.wait()

        def fill_unused(blk, carry):
            tail = pltpu.make_async_copy(
                zbuf, xs_hbm.at[pl.ds(pl.multiple_of(blk * (tm * nt), tm * nt), tm * nt)], zsem)
            tail.start()
            tail.wait()
            return carry
        lax.fori_loop(tbl_ref[2 * N_EXPERTS], xs_hbm.shape[0] // (tm * nt), fill_unused, 0)

    def body(g, carry):
        r0 = g * DMA_UNROLL
        dsts = _slot_rows(slot_ref, r0, nt)
        for u in range(DMA_UNROLL):
            src = f_ref.at[pl.ds(pl.multiple_of((r0 + u) * nt, nt), nt)]
            for k in range(TOP_K):
                pltpu.make_async_copy(src, xs_hbm.at[pl.ds(dsts[u * TOP_K + k], nt)],
                                      sem).start(priority=k % 2)
        return carry

    lax.fori_loop(0, tb // DMA_UNROLL, body, 0)
    for k in range(TOP_K):
        pltpu.make_async_copy(f_ref, xs_hbm.at[pl.ds(0, tb * nt)], sem).wait()


def _dispatch_call(tbl, slot3, f_tiles, n_slots, tm):
    n_blk, _, per = slot3.shape
    tb = per // 8
    nt = f_tiles.shape[0] // (n_blk * tb)
    grid_spec = pltpu.PrefetchScalarGridSpec(
        num_scalar_prefetch=1,
        grid=(n_blk,),
        in_specs=[pl.BlockSpec((1, 1, per), lambda i, tbl: (i, 0, 0), memory_space=pltpu.SMEM),
                  pl.BlockSpec((tb * nt, 128), lambda i, tbl: (i, 0))],
        out_specs=pl.BlockSpec(memory_space=pl.ANY),
        scratch_shapes=[pltpu.VMEM((tm * nt, 128), F32), pltpu.SemaphoreType.DMA(()),
                        pltpu.SemaphoreType.DMA(())])
    return pl.pallas_call(
        functools.partial(_dispatch_kernel, tb=tb, tm=tm, nt=nt),
        grid_spec=grid_spec,
        out_shape=jax.ShapeDtypeStruct(((n_slots + tm) * nt, 128), F32),
        compiler_params=_params(("arbitrary",), VMEM_LIMIT),
        name="moe_dispatch",
    )(tbl, slot3, f_tiles)


def _expert_kernel(be_ref, nu_ref, xs_ref, wgu_ref, bgu_ref, wd_ref, bd_ref, ys_ref,
                   wgu_s, wd_s, *, tm, d_ff):
    i = pl.program_id(0)
    nt = xs_ref.shape[0] // tm

    @pl.when(i < nu_ref[0])
    def _():
        @pl.when(jnp.logical_or(i == 0, be_ref[i] != be_ref[jnp.maximum(i - 1, 0)]))
        def _():
            for c in range(0, wgu_s.shape[0], 128):
                wgu_s[c:c + 128, :] = wgu_ref[c:c + 128, :].astype(BF16)
            for c in range(0, wd_s.shape[0], 128):
                wd_s[c:c + 128, :] = wd_ref[c:c + 128, :].astype(BF16)

        x = _load_token_tiles(xs_ref, tm, nt).astype(BF16)
        gu = _dotf(x, wgu_s[...]) + bgu_ref[...]
        gate = jnp.minimum(gu[:, 0:d_ff], SWIGLU_LIMIT)
        up = jnp.clip(gu[:, d_ff:2 * d_ff], -SWIGLU_LIMIT, SWIGLU_LIMIT)
        act = (up + 1.0) * gate * _sigmoid(SWIGLU_ALPHA * gate)
        _store_token_tiles(ys_ref, _dotf(act.astype(BF16), wd_s[...]) + bd_ref[...])

    @pl.when(i >= nu_ref[0])
    def _():
        ys_ref[...] = jnp.zeros_like(ys_ref)


def _expert_call(block_expert, n_used, xs_tiles, wgu, bgu, wd, bd, layer, n_blocks, tm):
    d, two_ff = wgu.shape[2:]
    d_ff = two_ff // 2
    nt = d // 128
    ex = lambda i, be, nu: (layer, be[i], 0, 0)
    grid_spec = pltpu.PrefetchScalarGridSpec(
        num_scalar_prefetch=2,
        grid=(n_blocks,),
        in_specs=[pl.BlockSpec((tm * nt, 128), lambda i, be, nu: (jnp.minimum(i, nu[0] - 1), 0)),
                  pl.BlockSpec((None, None, d, two_ff), ex),
                  pl.BlockSpec((None, None, 1, two_ff), ex),
                  pl.BlockSpec((None, None, d_ff, d), ex),
                  pl.BlockSpec((None, None, 1, d), ex)],
        out_specs=pl.BlockSpec((tm * nt, 128), lambda i, be, nu: (i, 0)),
        scratch_shapes=[pltpu.VMEM((d, two_ff), BF16), pltpu.VMEM((d_ff, d), BF16)])
    return pl.pallas_call(
        functools.partial(_expert_kernel, tm=tm, d_ff=d_ff),
        grid_spec=grid_spec,
        out_shape=jax.ShapeDtypeStruct((n_blocks * tm * nt, 128), F32),
        compiler_params=_params(("arbitrary",), VMEM_LIMIT),
        name="moe_experts",
    )(block_expert, n_used, xs_tiles, wgu, bgu, wd, bd)


def _route(counts, n_tok, tm):
    n_blocks = -(-(n_tok * TOP_K + N_EXPERTS * (tm - 1)) // tm)
    padded = (counts + tm - 1) // tm * tm
    padded_end = jnp.cumsum(padded)
    group_start = padded_end - padded
    block_start = jnp.arange(n_blocks, dtype=I32) * tm
    block_expert = jnp.minimum(
        jnp.sum((padded_end[None, :] <= block_start[:, None]).astype(I32), axis=1), N_EXPERTS - 1)
    n_used = (padded_end[-1] // tm).astype(I32).reshape(1)
    table = jnp.concatenate([group_start, group_start + counts, n_used]).astype(I32)
    return table, block_expert, n_used, n_blocks


def _fin_kernel(scur_ref, snxt_ref, g_ref, x_ref, g2_ref, nw_ref, ys_hbm, out_ref, buf, sem,
                *, tb, nt):
    i = pl.program_id(0)
    s = i % 2

    def gather(slot_ref, ss):
        def body(g, carry):
            r0 = g * DMA_UNROLL
            srcs = _slot_rows(slot_ref, r0, nt)
            for u in range(DMA_UNROLL):
                for k in range(TOP_K):
                    dst = pl.multiple_of((k * tb + r0 + u) * nt, nt)
                    pltpu.make_async_copy(ys_hbm.at[pl.ds(srcs[u * TOP_K + k], nt)],
                                          buf.at[ss, pl.ds(dst, nt)],
                                          sem.at[ss]).start(priority=k % 2)
            return carry
        lax.fori_loop(0, tb // DMA_UNROLL, body, 0)

    @pl.when(i == 0)
    def _():
        gather(scur_ref, 0)

    @pl.when(i + 1 < pl.num_programs(0))
    def _():
        gather(snxt_ref, 1 - s)

    pltpu.make_async_copy(ys_hbm.at[pl.ds(0, TOP_K * tb * nt)], buf.at[s], sem.at[s]).wait()
    gates = g_ref[...]
    parts = []
    for j in range(nt):
        acc = None
        for k in range(TOP_K):
            v = buf[s, pl.ds(k * tb * nt + j, tb, stride=nt), :] * gates[:, k:k + 1]
            acc = v if acc is None else acc + v
        parts.append(acc)
    m = jnp.concatenate(parts, axis=1)
    out_ref[0] = x_ref[0] + g2_ref[0] * _rms(m, nw_ref[...])


def _fin_call(slot3, gates, ys_tiles, xn, mod_l, nw, n_ctx_blk, blk0):
    b, t, d = xn.shape
    tb = ROW_TILE
    nblk = t // tb
    nt = d // 128
    n_blk = b * nblk

    def mrow(i):
        return jnp.where(i % nblk + blk0 < n_ctx_blk, b, i // nblk)

    smem = functools.partial(pl.BlockSpec, memory_space=pltpu.SMEM)
    return pl.pallas_call(
        functools.partial(_fin_kernel, tb=tb, nt=nt),
        grid=(n_blk,),
        in_specs=[smem((1, 1, tb * 8), lambda i: (i, 0, 0)),
                  smem((1, 1, tb * 8), lambda i: (jnp.minimum(i + 1, n_blk - 1), 0, 0)),
                  pl.BlockSpec((tb, 8), lambda i: (i, 0)),
                  pl.BlockSpec((1, tb, d), lambda i: (i // nblk, i % nblk, 0)),
                  pl.BlockSpec((1, 1, d), lambda i: (mrow(i), 0, 5)),
                  pl.BlockSpec((1, d), lambda i: (0, 0)),
                  pl.BlockSpec(memory_space=pl.ANY)],
        out_specs=pl.BlockSpec((1, tb, d), lambda i: (i // nblk, i % nblk, 0)),
        out_shape=jax.ShapeDtypeStruct((b, t, d), F32),
        scratch_shapes=[pltpu.VMEM((2, TOP_K * tb * nt, 128), F32), pltpu.SemaphoreType.DMA((2,))],
        compiler_params=_params(("arbitrary",), VMEM_LIMIT),
        name="moe_combine",
    )(slot3, slot3, gates, xn, mod_l, nw, ys_tiles)


def _in_proj_columns():
    cols = np.full((IN_COLS,), -1, np.int64)
    for sec in range(2):
        for n in range(256):
            part, hm, j = n // 128, (n % 128) // 16, n % 16
            m, h = hm // 4, hm % 4
            cols[sec * 256 + n] = sec * 256 + h * 64 + m * 32 + part * 16 + j
    cols[512:768] = np.arange(512, 768)
    o, s = IN_DA, 768
    cols[o:o + 1024] = s + np.arange(1024)
    o, s = o + IN_ML, s + 1024
    cols[o:o + 16] = s + np.arange(16)
    o, s = o + IN_G, s + 16
    cols[o:o + MLA_Q_RANK] = s + np.arange(MLA_Q_RANK)
    cols[o + 256:o + 256 + MLA_KV_RANK] = s + MLA_Q_RANK + np.arange(MLA_KV_RANK)
    cols[o + 384 + 64:o + 384 + 96] = s + MLA_Q_RANK + MLA_KV_RANK + np.arange(MLA_ROPE)
    o, s = o + IN_MLA, s + MLA_Q_RANK + MLA_KV_RANK + MLA_ROPE
    cols[o:o + 256] = s + np.arange(256)
    return cols


def _take_cols(w, cols):
    valid = jnp.asarray(cols >= 0)
    return jnp.where(valid, jnp.take(w, jnp.asarray(np.maximum(cols, 0)), axis=-1), 0.0)


def _rope_tables(n_ctx, n_lat):
    pos = jnp.arange(n_lat)
    inv = ROPE_THETA ** (-jnp.arange(8, dtype=F32) / 8)
    ang = jnp.concatenate([(pos // GRID_W)[:, None] * inv, (pos % GRID_W)[:, None] * inv], axis=-1)
    cos = jnp.concatenate([jnp.ones((n_ctx, 16), F32), jnp.cos(ang)], axis=0)
    sin = jnp.concatenate([jnp.zeros((n_ctx, 16), F32), jnp.sin(ang)], axis=0)
    t = n_ctx + n_lat
    cos_da, sin_da = jnp.tile(cos, (1, 8)), jnp.tile(sin, (1, 8))
    one, zero = jnp.ones((t, 64), F32), jnp.zeros((t, 64), F32)
    z16, z32 = jnp.zeros((t, 16), F32), jnp.zeros((t, 32), F32)
    cm = jnp.concatenate([one, cos, cos, jnp.ones((t, 32), F32)], axis=1)
    sa = jnp.concatenate([zero, -sin, z16, z32], axis=1)
    sb = jnp.concatenate([zero, z16, sin, z32], axis=1)
    return cos_da, sin_da, cm, sa, sb


def _mla_weights(w_uq, w_ukv):
    hd = MLA_NOPE + MLA_ROPE
    wq = jnp.pad(w_uq.reshape(MLA_Q_RANK, MLA_HEADS, hd),
                 ((0, 256 - MLA_Q_RANK), (0, 0), (0, 128 - hd))).reshape(256, 128 * MLA_HEADS)
    kv = w_ukv.reshape(MLA_KV_RANK, MLA_HEADS, -1)
    wk = jnp.pad(kv[:, :, :MLA_NOPE], ((0, 0), (0, 0), (0, 128 - MLA_NOPE)))
    wkv = jnp.concatenate([wk.reshape(MLA_KV_RANK, -1), kv[:, :, MLA_NOPE:].reshape(MLA_KV_RANK, -1)],
                          axis=1)
    return wq.astype(BF16), wkv.astype(BF16)


def _s5_layout(a_re, a_im, log_step, b_re, b_im, c_re, c_im):
    ns = S5_NGROUPS * S5_STATE
    eye = jnp.eye(S5_NGROUPS, dtype=F32)
    are = a_re.reshape(2, 1, ns)
    aim = a_im.reshape(2, 1, ns)
    ls = jnp.repeat(log_step, S5_STATE, axis=-1).reshape(2, 1, ns)
    bd_b = lambda w: jnp.einsum("dgph,gk->dghkp", w, eye).reshape(2, GROUP_W, ns)
    bd_c = lambda w: jnp.einsum("dghp,gk->dgpkh", w, eye).reshape(2, ns, GROUP_W)
    return are, aim, ls, bd_b(b_re), bd_b(b_im), bd_c(c_re), bd_c(c_im)


def kernel(x, c, ctx, c_ctx, w_mod, b_mod, norm_w, w_in, w_out, da_lambda, da_subln, ml_conv_w,
           ml_conv_b, ml_gate_b, ml_norm, mla_q_norm, mla_w_uq, mla_kv_norm, mla_w_ukv, s5_a_re,
           s5_a_im, s5_log_step, s5_b_re, s5_b_im, s5_c_re, s5_c_im, s5_d, s5_w_glu, s5_b_glu,
           moe_w_router, moe_b_router, moe_w_gate_up, moe_b_gate_up, moe_w_down, moe_b_down):
    bsz, n_lat, d = x.shape
    n_ctx = ctx.shape[1]
    t = n_ctx + n_lat
    depth = w_mod.shape[0]
    tm = ROW_TILE
    assert n_ctx % tm == 0 and n_lat % tm == 0 and bsz % 8 == 0 and bsz < 16
    assert n_ctx % ML_CHUNK == 0 and n_ctx % S5_CHUNK == 0
    n_ctx_blk = n_ctx // tm

    cc = jnp.pad(jnp.concatenate([c, c_ctx[None]], axis=0), ((0, 15 - bsz), (0, 0)))
    mod = _mod_call(cc, w_mod, b_mod)
    cos_da, sin_da, cm, sa, sb = _rope_tables(n_ctx, n_lat)
    in_cols = _in_proj_columns()
    xa, xb = ctx, x

    for l in range(depth):
        last = l == depth - 1
        lambda_init = 0.8 - 0.6 * math.exp(-0.3 * l)
        mod_l = mod[l].reshape(16, 1, 6 * d)
        w_in_p = _take_cols(w_in[l], in_cols).astype(BF16)
        wq, wkv = _mla_weights(mla_w_uq[l], mla_w_ukv[l])
        qn = jnp.pad(mla_q_norm[l], (0, 256 - MLA_Q_RANK)).reshape(1, 256)
        mla = (qn, wq, mla_kv_norm[l].reshape(1, -1), wkv, cm, sa, sb)
        p_da, p_ml, p_g, q_mla, k_mla, v_mla, u_t = _in_call(
            xa, xb, mod_l, norm_w[l, 0].reshape(1, d), w_in_p, cos_da, sin_da, mla, n_ctx_blk)

        da_extra = [da_lambda[l], jnp.tile(da_subln[l], 4).reshape(1, GROUP_W)]
        da_kern = functools.partial(_da_kernel, lambda_init=lambda_init)
        da_kw = dict(q_col=0, k_col=1, v_col=2, q_w=GROUP_W)
        ya = _attn_call(da_kern, "diff_attn", p_da, p_da, p_da, da_extra, q_blk0=n_ctx_blk,
                        n_q_blk=n_lat // tm, n_keys=t, **da_kw)
        mla_kw = dict(q_col=0, k_col=0, v_col=0, q_w=512)
        yc = _attn_call(_mla_attn_kernel, "mla_attn", q_mla, k_mla, v_mla, [], q_blk0=n_ctx_blk,
                        n_q_blk=n_lat // tm, n_keys=t, **mla_kw)
        if not last:
            ya_c = _attn_call(da_kern, "diff_attn_ctx", p_da, p_da, p_da, da_extra, q_blk0=0,
                              n_q_blk=n_ctx_blk, n_keys=n_ctx, **da_kw)
            yc_c = _attn_call(_mla_attn_kernel, "mla_attn_ctx", q_mla, k_mla, v_mla, [], q_blk0=0,
                              n_q_blk=n_ctx_blk, n_keys=n_ctx, **mla_kw)
            ya = jnp.concatenate([ya_c, ya], axis=1)
            yc = jnp.concatenate([yc_c, yc], axis=1)

        gb = jnp.pad(ml_gate_b[l], (0, 128 - 16)).reshape(1, 128)
        yb = _ml_call(p_ml, p_g, ml_conv_w[l], ml_conv_b[l].reshape(1, -1), gb,
                      ml_norm[l].reshape(1, -1), n_ctx)

        s5p = _s5_layout(s5_a_re[l], s5_a_im[l], s5_log_step[l], s5_b_re[l], s5_b_im[l],
                         s5_c_re[l], s5_c_im[l])
        ys = _s5_call(u_t.reshape(t * bsz, GROUP_W), *s5p, n_ctx, bsz)
        yd = _glu_call(ys, u_t, s5_d[l].reshape(1, -1),
                       s5_w_glu[l].astype(BF16), s5_b_glu[l].reshape(1, -1), bsz)

        blk0 = n_ctx_blk if last else 0
        wr = jnp.pad(moe_w_router[l], ((0, 0), (0, 128 - N_EXPERTS)))
        br = jnp.pad(moe_b_router[l], (0, 128 - N_EXPERTS), constant_values=-1e30).reshape(1, 128)
        xn, f, te, tg = _out_call(ya, yb, yc, yd, w_out[l].astype(BF16), xa, xb, mod_l,
                                  norm_w[l, 1].reshape(1, d), norm_w[l, 2].reshape(1, d), wr, br,
                                  n_ctx_blk, blk0)

        t_moe = t - blk0 * tm
        n_tok = bsz * t_moe
        te = te.reshape(n_tok, 8)
        rank, cnt = _rank_call(te)
        tbl, be, nu, n_blocks = _route(cnt[0, :N_EXPERTS], n_tok, MOE_TILE)
        slots = _slot_call(tbl, te, rank)
        slot3 = slots.reshape(n_tok // tm, 1, tm * 8)
        xs_tiles = _dispatch_call(tbl, slots.reshape(n_tok // DISPATCH_TILE, 1, DISPATCH_TILE * 8), f,
                                  n_blocks * MOE_TILE, MOE_TILE)
        ys_tiles = _expert_call(be, nu, xs_tiles, moe_w_gate_up,
                                moe_b_gate_up.reshape(depth, N_EXPERTS, 1, -1), moe_w_down,
                                moe_b_down.reshape(depth, N_EXPERTS, 1, -1), l, n_blocks, MOE_TILE)
        xa, xb = _fin_call(slot3, tg.reshape(n_tok, 8), ys_tiles, xn, mod_l,
                           norm_w[l, 3].reshape(1, d), n_ctx_blk, blk0), None
    return xa
```

```python
import functools
import math

import numpy as np
import jax
import jax.numpy as jnp
from jax import lax
from jax.experimental import pallas as pl
from jax.experimental.pallas import tpu as pltpu

F32, BF16, I32 = jnp.float32, jnp.bfloat16, jnp.int32
NORM_EPS = 1e-6
GRID_W = 64
ROPE_THETA = 10000.0
GROUP_W = 256
DA_QK = 32
ML_CHUNK = 128
ML_ROWS = 2
MLA_HEADS, MLA_NOPE, MLA_ROPE, MLA_Q_RANK, MLA_KV_RANK = 4, 64, 32, 192, 128
S5_NGROUPS, S5_GROUP, S5_STATE = 16, 16, 64
N_EXPERTS, TOP_K = 32, 4
SWIGLU_ALPHA, SWIGLU_LIMIT = 1.702, 7.0
NEG_INF = float("-inf")

ROW_TILE = 256
MOE_TILE = 256
S5_CHUNK = 64
RANK_TILE = 512
DISPATCH_TILE = 1024
DMA_UNROLL = 8
VMEM_LIMIT = 56 * 1024 * 1024

IN_DA, IN_ML, IN_G, IN_MLA, IN_S5 = 768, 1024, 128, 512, 256
IN_COLS = IN_DA + IN_ML + IN_G + IN_MLA + IN_S5


def _params(sem, vmem=None):
    return pltpu.CompilerParams(dimension_semantics=sem, vmem_limit_bytes=vmem)


def _dotf(a, b):
    return jnp.dot(a, b, preferred_element_type=F32)


def _dot_nt(a, b):
    return lax.dot_general(a, b, (((1,), (1,)), ((), ())), preferred_element_type=F32)


def _split2(a):
    hi = a.astype(BF16)
    lo = (a - hi.astype(F32)).astype(BF16)
    return hi, lo


def _dot3(a, b):
    ah, al = _split2(a)
    bh, bl = _split2(b)
    return _dotf(ah, bh) + _dotf(ah, bl) + _dotf(al, bh)


def _dot_exact_rhs(a, rhs_b):
    a1 = a.astype(BF16)
    r1 = a - a1.astype(F32)
    a2 = r1.astype(BF16)
    a3 = (r1 - a2.astype(F32)).astype(BF16)
    return _dotf(a1, rhs_b) + _dotf(a2, rhs_b) + _dotf(a3, rhs_b)


def _rms(x, w):
    ms = jnp.mean(x * x, axis=-1, keepdims=True)
    return x * lax.rsqrt(ms + NORM_EPS) * w


def _head_rms(a, width):
    n = a.shape[-1]
    sh = int(math.log2(width))
    r = lax.broadcasted_iota(I32, (n, n), 0) >> sh
    c = lax.broadcasted_iota(I32, (n, n), 1) >> sh
    g = jnp.where(r == c, 1.0 / width, 0.0).astype(BF16)
    hi, lo = _split2(a * a)
    ms = _dotf(hi, g) + _dotf(lo, g)
    return a * lax.rsqrt(ms + NORM_EPS)


def _store_token_tiles(ref, val):
    tm, d = val.shape
    nt = d // 128
    for j in range(nt):
        ref[pl.ds(j, tm, stride=nt), :] = val[:, 128 * j:128 * (j + 1)]


def _load_token_tiles(ref, tm, nt):
    return jnp.concatenate([ref[pl.ds(j, tm, stride=nt), :] for j in range(nt)], axis=1)


def _sigmoid(x):
    return jax.nn.sigmoid(x)


def _log_sigmoid(x):
    return jnp.minimum(x, 0.0) - jnp.log(1.0 + jnp.exp(-jnp.abs(x)))


def _mod_kernel(c_ref, w_ref, b_ref, o_ref):
    c = c_ref[...]
    o_ref[0] = _dot3(c * _sigmoid(c), w_ref[0]) + b_ref[0]


def _mod_call(cc, w_mod, b_mod):
    n_layers, d, n = w_mod.shape
    tn = 1536
    return pl.pallas_call(
        _mod_kernel,
        grid=(n_layers, n // tn),
        in_specs=[pl.BlockSpec((16, d), lambda l, j: (0, 0)),
                  pl.BlockSpec((1, d, tn), lambda l, j: (l, 0, j)),
                  pl.BlockSpec((1, 1, tn), lambda l, j: (l, 0, j))],
        out_specs=pl.BlockSpec((1, 16, tn), lambda l, j: (l, 0, j)),
        out_shape=jax.ShapeDtypeStruct((n_layers, 16, n), F32),
        compiler_params=_params(("parallel", "parallel"), VMEM_LIMIT),
        name="mod_vectors",
    )(cc, w_mod, b_mod.reshape(n_layers, 1, n))


def _in_kernel(xa_ref, xb_ref, sh_ref, sc_ref, nw_ref, w_ref, c_ref, s_ref,
               qn_ref, wq_ref, kvn_ref, wkv_ref, cm_ref, sa_ref, sb_ref,
               da_ref, ml_ref, g_ref, mq_ref, mk_ref, mv_ref, s5_ref, *, qscale, mla_scale, n_a_blk):
    x = jnp.where(pl.program_id(1) < n_a_blk, xa_ref[0], xb_ref[0])
    h = _rms(x, nw_ref[...]) * (1.0 + sc_ref[0]) + sh_ref[0]
    hb = h.astype(BF16)
    da = _dotf(hb, w_ref[:, 0:IN_DA])
    c = c_ref[...]
    s = s_ref[...]
    q1, q2, k1, k2 = da[:, 0:128], da[:, 128:256], da[:, 256:384], da[:, 384:512]
    da_ref[0, :, 0:128] = ((q1 * c - q2 * s) * qscale).astype(BF16)
    da_ref[0, :, 128:256] = ((q2 * c + q1 * s) * qscale).astype(BF16)
    da_ref[0, :, 256:384] = (k1 * c - k2 * s).astype(BF16)
    da_ref[0, :, 384:512] = (k2 * c + k1 * s).astype(BF16)
    da_ref[0, :, 512:768] = da[:, 512:768].astype(BF16)
    o = IN_DA
    ml_ref[0] = _dotf(hb, w_ref[:, o:o + IN_ML]).astype(BF16)
    o += IN_ML
    g_ref[0] = _dotf(hb, w_ref[:, o:o + IN_G])
    o += IN_G
    _mla_project(_dotf(hb, w_ref[:, o:o + IN_MLA]), qn_ref, wq_ref, kvn_ref, wkv_ref,
                 cm_ref, sa_ref, sb_ref, mq_ref, mk_ref, mv_ref, mla_scale)
    o += IN_MLA
    s5_ref[...] = _dotf(hb, w_ref[:, o:o + IN_S5])


def _stream_specs(xa, xb, blk0):
    tm = ROW_TILE
    d = xa.shape[2]
    n_a = xa.shape[1] // tm
    spec_a = pl.BlockSpec((1, tm, d), lambda bi, ti: (bi, jnp.minimum(ti + blk0, n_a - 1), 0))
    spec_b = pl.BlockSpec((1, tm, d), lambda bi, ti: (bi, jnp.maximum(ti + blk0 - n_a, 0), 0))
    return spec_a, spec_b, n_a


def _in_call(xa, xb, mod_l, nw, w_in_p, cos_da, sin_da, mla, n_ctx_blk):
    b, _, d = xa.shape
    t = xa.shape[1] + (0 if xb is None else xb.shape[1])
    tm = ROW_TILE
    spec_a, spec_b, n_a_blk = _stream_specs(xa, xb, 0)

    def mrow(bi, ti):
        return jnp.where(ti < n_ctx_blk, b, bi)

    row3 = lambda bi, ti: (bi, ti, 0)
    tab = pl.BlockSpec((tm, 128), lambda bi, ti: (ti, 0))
    return pl.pallas_call(
        functools.partial(_in_kernel, qscale=DA_QK ** -0.5 * LOG2_E,
                          mla_scale=(MLA_NOPE + MLA_ROPE) ** -0.5 * LOG2_E, n_a_blk=n_a_blk),
        grid=(b, t // tm),
        in_specs=[spec_a, spec_b,
                  pl.BlockSpec((1, 1, d), lambda bi, ti: (mrow(bi, ti), 0, 0)),
                  pl.BlockSpec((1, 1, d), lambda bi, ti: (mrow(bi, ti), 0, 1)),
                  pl.BlockSpec((1, d), lambda bi, ti: (0, 0)),
                  pl.BlockSpec((d, IN_COLS), lambda bi, ti: (0, 0)),
                  tab, tab] + [pl.BlockSpec(a.shape, lambda bi, ti: (0, 0)) for a in mla[:4]]
                 + [tab, tab, tab],
        out_specs=[pl.BlockSpec((1, tm, IN_DA), row3),
                   pl.BlockSpec((1, tm, IN_ML), row3),
                   pl.BlockSpec((1, tm, IN_G), row3),
                   pl.BlockSpec((1, tm, 512), row3), pl.BlockSpec((1, tm, 512), row3),
                   pl.BlockSpec((1, tm, GROUP_W), row3),
                   pl.BlockSpec((tm, IN_S5), lambda bi, ti: (ti, bi))],
        out_shape=[jax.ShapeDtypeStruct((b, t, IN_DA), BF16),
                   jax.ShapeDtypeStruct((b, t, IN_ML), BF16),
                   jax.ShapeDtypeStruct((b, t, IN_G), F32),
                   jax.ShapeDtypeStruct((b, t, 512), BF16), jax.ShapeDtypeStruct((b, t, 512), BF16),
                   jax.ShapeDtypeStruct((b, t, GROUP_W), BF16),
                   jax.ShapeDtypeStruct((t, b * IN_S5), F32)],
        compiler_params=_params(("parallel", "parallel"), VMEM_LIMIT),
        name="in_proj",
    )(xa, xa if xb is None else xb, mod_l, mod_l, nw, w_in_p, cos_da, sin_da, *mla)


LOG2_E = 1.0 / math.log(2.0)


def _softmax_rows(s):
    mx = jnp.max(s, axis=-1, keepdims=True)
    p = jnp.exp2(s - mx)
    return p, jnp.sum(p, axis=-1, keepdims=True)


def _da_kernel(q_ref, k_ref, v_ref, lam_ref, sub_ref, o_ref, *, lambda_init):
    q = q_ref[0]
    k = k_ref[0]
    v = v_ref[0]
    lp = lam_ref[...]
    lam = (jnp.exp(jnp.sum(lp[0:1] * lp[1:2], axis=-1, keepdims=True))
           - jnp.exp(jnp.sum(lp[2:3] * lp[3:4], axis=-1, keepdims=True)) + lambda_init)
    lane = lax.broadcasted_iota(I32, (1, GROUP_W), 1)
    grp = (lane & 127) >> 4
    head = lane >> 6
    acc = jnp.zeros((q.shape[0], GROUP_W), F32)
    for h in range(4):
        ps, rs = [], []
        for m in range(2):
            qm = jnp.where(grp == m * 4 + h, q, jnp.zeros_like(q))
            p, l = _softmax_rows(_dot_nt(qm, k))
            ps.append(p.astype(BF16))
            rs.append(((1.0 if m == 0 else lam) / l).astype(BF16))
        w = ps[0] * rs[0] - ps[1] * rs[1]
        vm = jnp.where(head == h, v, jnp.zeros_like(v))
        acc = acc + _dotf(w, vm)
    y = _head_rms(acc, 64) * sub_ref[...] * (1.0 - lambda_init)
    o_ref[0] = y.astype(BF16)


def _mla_attn_kernel(q_ref, k_ref, v_ref, o_ref):
    q = q_ref[0]
    k = k_ref[0]
    v = v_ref[0]
    head = lax.broadcasted_iota(I32, (1, GROUP_W), 1) >> 6
    acc = jnp.zeros((q.shape[0], GROUP_W), F32)
    for h in range(MLA_HEADS):
        sl = slice(128 * h, 128 * (h + 1))
        p, l = _softmax_rows(_dot_nt(q[:, sl], k[:, sl]))
        vm = jnp.where(head == h, v, jnp.zeros_like(v))
        acc = acc + _dotf(p.astype(BF16), vm) * (1.0 / l)
    o_ref[0] = acc.astype(BF16)


def _attn_call(kernel, name, q_arr, k_arr, v_arr, extra, *, q_blk0, n_q_blk, n_keys,
               q_col, k_col, v_col, q_w):
    b = q_arr.shape[0]
    tq = ROW_TILE
    in_specs = [pl.BlockSpec((1, tq, q_w), lambda bi, qi: (bi, qi + q_blk0, q_col)),
                pl.BlockSpec((1, n_keys, q_w), lambda bi, qi: (bi, 0, k_col)),
                pl.BlockSpec((1, n_keys, GROUP_W), lambda bi, qi: (bi, 0, v_col))]
    in_specs += [pl.BlockSpec(e.shape, lambda bi, qi: (0, 0)) for e in extra]
    return pl.pallas_call(
        kernel,
        grid=(b, n_q_blk),
        in_specs=in_specs,
        out_specs=pl.BlockSpec((1, tq, GROUP_W), lambda bi, qi: (bi, qi, 0)),
        out_shape=jax.ShapeDtypeStruct((b, n_q_blk * tq, GROUP_W), BF16),
        compiler_params=_params(("parallel", "arbitrary"), VMEM_LIMIT),
        name=name,
    )(q_arr, k_arr, v_arr, *extra)


def _mla_project(p, qn_ref, wq_ref, kvn_ref, wkv_ref, c_ref, sa_ref, sb_ref, q_ref, k_ref, v_ref,
                 scale):
    cq, ckv, kr = p[:, 0:256], p[:, 256:384], p[:, 384:512]
    msq = jnp.sum(cq * cq, axis=-1, keepdims=True) * (1.0 / MLA_Q_RANK)
    qn = (cq * lax.rsqrt(msq + NORM_EPS) * qn_ref[...]).astype(BF16)
    q = _dotf(qn, wq_ref[...])
    c = c_ref[...]
    sa = sa_ref[...]
    sb = sb_ref[...]

    def rope(a):
        return a * c + pltpu.roll(a, 112, 1) * sa + pltpu.roll(a, 16, 1) * sb

    for h in range(MLA_HEADS):
        sl = slice(128 * h, 128 * (h + 1))
        q_ref[0, :, sl] = (rope(q[:, sl]) * scale).astype(BF16)
    kvn = (_rms(ckv, kvn_ref[...])).astype(BF16)
    kv = _dotf(kvn, wkv_ref[...])
    krr = rope(kr)
    for h in range(MLA_HEADS):
        sl = slice(128 * h, 128 * (h + 1))
        k_ref[0, :, sl] = (kv[:, sl] + krr).astype(BF16)
    v_ref[0] = kv[:, 512:768].astype(BF16)


def _ml_kernel(p_ref, g_ref, cw_ref, cb_ref, gb_ref, nw_ref, y_ref,
               qt_s, k_s, vt_s, ot_s, gt_s, hf_s, c_s, m_s, *, n_ctx_chunks, n_chunks):
    cl = ML_CHUNK
    t_total = n_chunks * cl
    nbr = p_ref.shape[0]
    row = lax.broadcasted_iota(I32, (cl, 1), 0)
    si = lax.broadcasted_iota(I32, (cl, cl), 0)
    ti = lax.broadcasted_iota(I32, (cl, cl), 1)
    lane128 = lax.broadcasted_iota(I32, (1, 128), 1)
    lane256 = lax.broadcasted_iota(I32, (1, 256), 1)
    row128 = lax.broadcasted_iota(I32, (128, 1), 0)
    hmask = [(lane256 >> 6) == h for h in range(4)]
    is_f = jnp.logical_and(((lane128 >> 2) & 1) == 1, lane128 < 16)
    r_nd = lax.broadcasted_iota(I32, (384, 512), 0)
    c_nd = lax.broadcasted_iota(I32, (384, 512), 1) >> 7
    nd_head = jnp.where(r_nd < 256, r_nd >> 6, r_nd - 256)
    nd_mask = nd_head == c_nd
    nd_ones = jnp.where(jnp.logical_and(nd_mask, r_nd >= 256), 1.0, 0.0).astype(BF16)
    r_st = lax.broadcasted_iota(I32, (384, 256), 0)
    st_head = jnp.where(r_st < 256, r_st >> 6, r_st - 256)
    st_mask = st_head == (lax.broadcasted_iota(I32, (384, 256), 1) >> 6)
    st_rowhead = jnp.where(lax.broadcasted_iota(I32, (384, 1), 0) < 256,
                           lax.broadcasted_iota(I32, (384, 1), 0) >> 6,
                           lax.broadcasted_iota(I32, (384, 1), 0) - 256)
    w0, w1, w2 = cw_ref[0:1], cw_ref[1:2], cw_ref[2:3]
    cb = cb_ref[...]
    gb = gb_ref[...]

    def prep_one(c, bb):
        s0 = pl.multiple_of(c * cl, cl)
        x = p_ref[bb, pl.ds(s0, cl), 0:512].astype(F32)
        sp = pl.multiple_of(jnp.maximum(s0 - 16, 0), 16)
        sn = pl.multiple_of(jnp.minimum(s0 + cl, t_total - 16), 16)
        has_prev = jnp.logical_and(c != 0, c != n_ctx_chunks).astype(F32)
        has_next = jnp.logical_and(c != n_ctx_chunks - 1, c != n_chunks - 1).astype(F32)
        prev_row = p_ref[bb, pl.ds(sp, 16), 0:512][15:16].astype(F32) * has_prev
        next_row = p_ref[bb, pl.ds(sn, 16), 0:512][0:1].astype(F32) * has_next
        xp = jnp.where(row == 0, prev_row, pltpu.roll(x, 1, 0))
        xn = jnp.where(row == cl - 1, next_row, pltpu.roll(x, cl - 1, 0))
        z = xp * w0 + x * w1 + xn * w2 + cb
        qk = z * _sigmoid(z)
        qt_s[bb, c] = qk[:, 0:256].T.astype(BF16)
        k_s[bb, c] = (qk[:, 256:512] * (64 ** -0.5)).astype(BF16)
        vt_s[bb, c] = p_ref[bb, pl.ds(s0, cl), 512:768].astype(F32).T
        ot_s[bb, c] = p_ref[bb, pl.ds(s0, cl), 768:1024].astype(F32).T
        g = g_ref[bb, pl.ds(s0, cl), :] + gb
        gt_s[bb, c] = jnp.where(is_f, _log_sigmoid(g), g).T[0:16]

    def prep_body(c, carry):
        for bb in range(nbr):
            prep_one(c, bb)
        return carry

    def rows_to_blocks(rows, n):
        return jnp.concatenate([jnp.broadcast_to(r, (n, 128)) for r in rows], axis=0)

    def head_rows(rows):
        out = jnp.zeros((128, 128), F32)
        for h in range(4):
            out = jnp.where(row128 == h, rows[h], out)
        return out

    def chunk(c, reverse, bb):
        g_t = gt_s[bb, c]
        tri = (si >= ti) if reverse else (si <= ti)
        bc_t = _dot_exact_rhs(g_t, jnp.where(tri, 1.0, 0.0).astype(BF16))
        qt = qt_s[bb, c]
        kb = k_s[bb, c]
        m_all = m_s[bb]
        off = 8 if reverse else 0
        a_rows, r_rows, tots, m_old = [], [], [], []
        for h in range(4):
            il, fl = off + h, off + 4 + h
            a_rows.append(bc_t[fl:fl + 1, :])
            r_rows.append(g_t[il:il + 1, :] - bc_t[fl:fl + 1, :])
            tots.append(bc_t[fl:fl + 1, 0:1] if reverse else bc_t[fl:fl + 1, cl - 1:cl])
            m_old.append(m_all[:, h:h + 1])
        r_all = jnp.concatenate(r_rows, axis=1)
        d_t = jnp.broadcast_to(r_all, (cl, 4 * cl)).T + rows_to_blocks(a_rows, cl)
        lw = jnp.where(jnp.concatenate([tri] * 4, axis=0), d_t, NEG_INF)
        mt, wi, emt = [], [], []
        for h in range(4):
            linter = a_rows[h] + m_old[h]
            mt_h = jnp.maximum(linter, jnp.max(lw[cl * h:cl * (h + 1)], axis=0, keepdims=True))
            mt.append(mt_h)
            wi.append(jnp.exp(linter - mt_h))
            emt.append(jnp.exp(-mt_h))
        kstack = jnp.concatenate([jnp.where(hmask[h], kb, jnp.zeros_like(kb)) for h in range(4)],
                                 axis=0)
        s_t = _dotf(kstack, qt)
        w_t = (s_t * jnp.exp(lw - rows_to_blocks(mt, cl))).astype(BF16)
        vt = vt_s[bb, c]
        vt4 = jnp.concatenate([vt.astype(BF16)] * 4, axis=1)
        lhs_nd = jnp.concatenate([vt4, jnp.zeros((128, 512), BF16)], axis=0)
        lhs_nd = jnp.where(nd_mask, lhs_nd, jnp.zeros_like(lhs_nd)) + nd_ones
        c_aug = c_s[bb]
        nd = _dotf(lhs_nd, w_t) + jnp.concatenate(
            [rows_to_blocks(wi, 64), head_rows(wi)], axis=0) * _dotf(c_aug.astype(BF16), qt)
        den = [nd[256 + h:257 + h, :] for h in range(4)]
        h_t = nd[0:256] / jnp.maximum(jnp.abs(rows_to_blocks(den, 64)), rows_to_blocks(emt, 64))
        wupd, decs, m_new_all = [], [], m_all
        for h in range(4):
            lupd = tots[h] + r_rows[h]
            m_new = jnp.maximum(tots[h] + m_old[h], jnp.max(lupd, axis=-1, keepdims=True))
            wupd.append(jnp.exp(lupd - m_new))
            decs.append(jnp.exp(tots[h] + m_old[h] - m_new))
            m_new_all = jnp.where(lane128 == h, m_new, m_new_all)
        lhs_u = jnp.concatenate([vt * rows_to_blocks(wupd, 64), head_rows(wupd)], axis=0)
        upd = _dotf(lhs_u.astype(BF16), kb)
        dec_col = jnp.zeros((384, 1), F32)
        for h in range(4):
            dec_col = jnp.where(st_rowhead == h, decs[h], dec_col)
        c_s[bb] = dec_col * c_aug + jnp.where(st_mask, upd, 0.0)
        m_s[bb] = m_new_all
        return h_t

    def fwd_body(i, carry):
        for bb in range(nbr):
            hf_s[bb, i] = chunk(i, False, bb)
        return carry

    def bwd_body(i, carry):
        c = jnp.where(i < n_ctx_chunks, n_ctx_chunks - 1 - i, n_chunks - 1 - (i - n_ctx_chunks))
        for bb in range(nbr):
            gated = _sigmoid(ot_s[bb, c]) * (hf_s[bb, c] + chunk(c, True, bb))
            ms = [jnp.mean(jnp.square(gated[64 * h:64 * (h + 1)]), axis=0, keepdims=True)
                  for h in range(4)]
            y_t = gated * lax.rsqrt(rows_to_blocks(ms, 64) + NORM_EPS) * nw_ref[...]
            y_ref[bb, pl.ds(pl.multiple_of(c * cl, cl), cl), :] = y_t.T.astype(BF16)
        return carry

    lax.fori_loop(0, n_chunks, prep_body, 0)
    c_s[...] = jnp.zeros_like(c_s)
    m_s[...] = jnp.zeros_like(m_s)
    lax.fori_loop(0, n_chunks, fwd_body, 0)
    c_s[...] = jnp.zeros_like(c_s)
    m_s[...] = jnp.zeros_like(m_s)
    lax.fori_loop(0, n_chunks, bwd_body, 0)


def _ml_call(p_ml, p_g, cw, cb, gb, nw, n_ctx):
    b, t, _ = p_ml.shape
    nc = t // ML_CHUNK
    nbr = ML_ROWS
    const = lambda bi: (0, 0)
    nw_col = jnp.broadcast_to(nw.reshape(GROUP_W, 1), (GROUP_W, 128))
    return pl.pallas_call(
        functools.partial(_ml_kernel, n_ctx_chunks=n_ctx // ML_CHUNK, n_chunks=nc),
        grid=(b // nbr,),
        in_specs=[pl.BlockSpec((nbr, t, IN_ML), lambda bi: (bi, 0, 0)),
                  pl.BlockSpec((nbr, t, IN_G), lambda bi: (bi, 0, 0)),
                  pl.BlockSpec(cw.shape, const), pl.BlockSpec(cb.shape, const),
                  pl.BlockSpec(gb.shape, const), pl.BlockSpec(nw_col.shape, const)],
        out_specs=pl.BlockSpec((nbr, t, GROUP_W), lambda bi: (bi, 0, 0)),
        out_shape=jax.ShapeDtypeStruct((b, t, GROUP_W), BF16),
        scratch_shapes=[pltpu.VMEM((nbr, nc, GROUP_W, 128), BF16),
                        pltpu.VMEM((nbr, nc, 128, GROUP_W), BF16),
                        pltpu.VMEM((nbr, nc, GROUP_W, 128), F32),
                        pltpu.VMEM((nbr, nc, GROUP_W, 128), F32),
                        pltpu.VMEM((nbr, nc, 16, 128), F32),
                        pltpu.VMEM((nbr, nc, GROUP_W, 128), F32),
                        pltpu.VMEM((nbr, 384, GROUP_W), F32), pltpu.VMEM((nbr, 1, 128), F32)],
        compiler_params=_params(("parallel",), VMEM_LIMIT),
        name="mlstm",
    )(p_ml, p_g, cw, cb, gb, nw_col)


def _s5_kernel(u_ref, are_ref, aim_ref, ls_ref, bre_ref, bim_ref, cre_ref, cim_ref, y_ref,
               ar_s, ai_s, bcat_s, ccat_s, st_s, bu_s, *, tc, nb):
    d = pl.program_id(0)
    i = pl.program_id(1)
    ns = S5_NGROUPS * S5_STATE

    @pl.when(i == 0)
    def _init():
        are = jnp.minimum(are_ref[0], -1e-4)
        aim = aim_ref[0]
        dt = jnp.exp(ls_ref[0])
        mag = jnp.exp(dt * are)
        abr = mag * jnp.cos(dt * aim)
        abi = mag * jnp.sin(dt * aim)
        inv = 1.0 / (are * are + aim * aim)
        fre = ((abr - 1.0) * are + abi * aim) * inv
        fim = (abi * are - (abr - 1.0) * aim) * inv
        bre = bre_ref[0]
        bim = bim_ref[0]
        bcat_s[:, 0:ns] = (bre * fre - bim * fim).astype(BF16)
        bcat_s[:, ns:2 * ns] = (bre * fim + bim * fre).astype(BF16)
        ccat_s[0:ns, :] = cre_ref[0].astype(BF16)
        ccat_s[ns:2 * ns, :] = (-cim_ref[0]).astype(BF16)
        ar_s[...] = jnp.broadcast_to(abr, (nb, ns))
        ai_s[...] = jnp.broadcast_to(abi, (nb, ns))
        st_s[...] = jnp.zeros_like(st_s)

    bu_s[...] = _dotf(u_ref[...].astype(BF16), bcat_s[...])
    ar = ar_s[...]
    ai = ai_s[...]

    def body(j, carry):
        xr, xi = carry
        t = j + d * (tc - 1 - 2 * j)
        r0 = pl.multiple_of(t * nb, nb)
        nr = ar * xr - ai * xi + bu_s[pl.ds(r0, nb), 0:ns]
        ni = ar * xi + ai * xr + bu_s[pl.ds(r0, nb), ns:2 * ns]
        bu_s[pl.ds(r0, nb), 0:ns] = nr
        bu_s[pl.ds(r0, nb), ns:2 * ns] = ni
        return nr, ni

    xr, xi = lax.fori_loop(0, tc, body, (st_s[0], st_s[1]), unroll=4)
    st_s[0] = xr
    st_s[1] = xi
    y = _dotf(bu_s[...].astype(BF16), ccat_s[...])
    for half in range(y.shape[1] // 128):
        y_ref[0, half] = y[:, 128 * half:128 * (half + 1)]


def _s5_call(u_tm, are, aim, ls, bre, bim, cre, cim, n_ctx, nb):
    rows, gw = u_tm.shape
    tc = S5_CHUNK
    n_chunks = rows // (tc * nb)
    n_ctx_chunks = n_ctx // tc
    ns = S5_NGROUPS * S5_STATE

    def chunk_of(d, i):
        rev = jnp.where(i < n_ctx_chunks, n_ctx_chunks - 1 - i, n_chunks - 1 - (i - n_ctx_chunks))
        return jnp.where(d == 0, i, rev)

    vec = pl.BlockSpec((1, 1, ns), lambda d, i: (d, 0, 0))
    return pl.pallas_call(
        functools.partial(_s5_kernel, tc=tc, nb=nb),
        grid=(2, n_chunks),
        in_specs=[pl.BlockSpec((tc * nb, gw), lambda d, i: (chunk_of(d, i), 0)),
                  vec, vec, vec,
                  pl.BlockSpec((1, gw, ns), lambda d, i: (d, 0, 0)),
                  pl.BlockSpec((1, gw, ns), lambda d, i: (d, 0, 0)),
                  pl.BlockSpec((1, ns, gw), lambda d, i: (d, 0, 0)),
                  pl.BlockSpec((1, ns, gw), lambda d, i: (d, 0, 0))],
        out_specs=pl.BlockSpec((1, gw // 128, tc * nb, 128),
                               lambda d, i: (d, 0, chunk_of(d, i), 0)),
        out_shape=jax.ShapeDtypeStruct((2, gw // 128, rows, 128), F32),
        scratch_shapes=[pltpu.VMEM((nb, ns), F32), pltpu.VMEM((nb, ns), F32),
                        pltpu.VMEM((gw, 2 * ns), BF16), pltpu.VMEM((2 * ns, gw), BF16),
                        pltpu.VMEM((2, nb, ns), F32), pltpu.VMEM((tc * nb, 2 * ns), F32)],
        compiler_params=_params(("arbitrary", "arbitrary"), VMEM_LIMIT),
        name="s5_scan",
    )(u_tm, are, aim, ls, bre, bim, cre, cim)


def _glu_kernel(ys_ref, u_ref, d_ref, w_ref, b_ref, o_ref, *, nb, tq):
    gw = GROUP_W
    rows = []
    for b in range(nb):
        y_b = jnp.concatenate(
            [ys_ref[0, half, pl.ds(b, tq, stride=nb), :] + ys_ref[1, half, pl.ds(b, tq, stride=nb), :]
             for half in range(gw // 128)], axis=1)
        rows.append(y_b + u_ref[:, gw * b:gw * (b + 1)] * d_ref[...])
    y = jnp.concatenate(rows, axis=0)
    g = y * (0.5 * (1.0 + jnp.tanh(math.sqrt(2.0 / math.pi) * (y + 0.044715 * (y * y * y)))))
    z = _dotf(g.astype(BF16), w_ref[...]) + b_ref[...]
    out = (g * _sigmoid(z)).astype(BF16)
    for b in range(nb):
        o_ref[b] = out[tq * b:tq * (b + 1)]


def _glu_call(ys, u_t, dsk, w, bias, nb):
    t = u_t.shape[0]
    tq = ROW_TILE // nb
    gw = GROUP_W
    const = lambda ti: (0, 0)
    return pl.pallas_call(
        functools.partial(_glu_kernel, nb=nb, tq=tq),
        grid=(t // tq,),
        in_specs=[pl.BlockSpec((2, gw // 128, tq * nb, 128), lambda ti: (0, 0, ti, 0)),
                  pl.BlockSpec((tq, nb * gw), lambda ti: (ti, 0)),
                  pl.BlockSpec(dsk.shape, const), pl.BlockSpec(w.shape, const),
                  pl.BlockSpec(bias.shape, const)],
        out_specs=pl.BlockSpec((nb, tq, gw), lambda ti: (0, ti, 0)),
        out_shape=jax.ShapeDtypeStruct((nb, t, gw), BF16),
        compiler_params=_params(("parallel",), VMEM_LIMIT),
        name="s5_glu",
    )(ys, u_t, dsk, w, bias)


def _out_kernel(ya_ref, yb_ref, yc_ref, yd_ref, w_ref, xa_ref, xb_ref, g1_ref, sh2_ref, sc2_ref,
                nw1_ref, nw2_ref, wr_ref, br_ref, xn_ref, f_ref, te_ref, tg_ref, *, n_a_blk, blk0):
    o = (_dotf(ya_ref[0], w_ref[0:256]) + _dotf(yb_ref[0], w_ref[256:512])
         + _dotf(yc_ref[0], w_ref[512:768]) + _dotf(yd_ref[0], w_ref[768:1024]))
    x = jnp.where(pl.program_id(1) + blk0 < n_a_blk, xa_ref[0], xb_ref[0])
    xn = x + g1_ref[0] * _rms(o, nw1_ref[...])
    xn_ref[0] = xn
    f = _rms(xn, nw2_ref[...]) * (1.0 + sc2_ref[0]) + sh2_ref[0]
    _store_token_tiles(f_ref, f)
    lg = _dot3(f, wr_ref[...]) + br_ref[...]
    tm = lg.shape[0]
    lane = lax.broadcasted_iota(I32, (tm, 128), 1).astype(F32)
    tops, idxs = [], []
    for _ in range(TOP_K):
        mx = jnp.max(lg, axis=-1, keepdims=True)
        idx = jnp.min(jnp.where(lg == mx, lane, 128.0), axis=-1, keepdims=True)
        tops.append(mx)
        idxs.append(idx.astype(I32))
        lg = jnp.where(lane == idx, NEG_INF, lg)
    ex = [jnp.exp(tv - tops[0]) for tv in tops]
    inv = 1.0 / (ex[0] + ex[1] + ex[2] + ex[3])
    l8 = lax.broadcasted_iota(I32, (tm, 8), 1)
    te = jnp.zeros((tm, 8), I32)
    tg = jnp.zeros((tm, 8), F32)
    for kk in range(TOP_K):
        te = jnp.where(l8 == kk, idxs[kk], te)
        tg = jnp.where(l8 == kk, ex[kk] * inv, tg)
    te_ref[0] = te
    tg_ref[0] = tg


def _out_call(ya, yb, yc, yd, w_out, xa, xb, mod_l, nw1, nw2, wr, br, n_ctx_blk, blk0):
    b, _, d = xa.shape
    t = xa.shape[1] + (0 if xb is None else xb.shape[1])
    tm = ROW_TILE
    nblk = t // tm - blk0
    t_out = nblk * tm
    spec_a, spec_b, n_a_blk = _stream_specs(xa, xb, blk0)

    def mrow(bi, ti):
        return jnp.where(ti + blk0 < n_ctx_blk, b, bi)

    row3 = lambda bi, ti: (bi, ti + blk0, 0)
    out3 = lambda bi, ti: (bi, ti, 0)
    const = lambda bi, ti: (0, 0)
    modspec = lambda j: pl.BlockSpec((1, 1, d), lambda bi, ti: (mrow(bi, ti), 0, j))
    yspec = lambda y: pl.BlockSpec((1, tm, GROUP_W), row3 if y.shape[1] == t else out3)
    return pl.pallas_call(
        functools.partial(_out_kernel, n_a_blk=n_a_blk, blk0=blk0),
        grid=(b, nblk),
        in_specs=[yspec(ya), yspec(yb), yspec(yc), yspec(yd), pl.BlockSpec((d, d), const),
                  spec_a, spec_b, modspec(2), modspec(3), modspec(4),
                  pl.BlockSpec((1, d), const), pl.BlockSpec((1, d), const),
                  pl.BlockSpec((d, 128), const), pl.BlockSpec((1, 128), const)],
        out_specs=[pl.BlockSpec((1, tm, d), out3),
                   pl.BlockSpec((tm * d // 128, 128), lambda bi, ti: (bi * nblk + ti, 0)),
                   pl.BlockSpec((1, tm, 8), out3), pl.BlockSpec((1, tm, 8), out3)],
        out_shape=[jax.ShapeDtypeStruct((b, t_out, d), F32),
                   jax.ShapeDtypeStruct((b * t_out * d // 128, 128), F32),
                   jax.ShapeDtypeStruct((b, t_out, 8), I32),
                   jax.ShapeDtypeStruct((b, t_out, 8), F32)],
        compiler_params=_params(("parallel", "parallel"), VMEM_LIMIT),
        name="out_proj_router",
    )(ya, yb, yc, yd, w_out, xa, xa if xb is None else xb, mod_l, mod_l, mod_l, nw1, nw2, wr, br)


def _rank_kernel(te_ref, rank_ref, cnt_ref, carry_ref):
    i = pl.program_id(0)

    @pl.when(i == 0)
    def _():
        carry_ref[...] = jnp.zeros_like(carry_ref)

    te = te_ref[...]
    tb = te.shape[0]
    lane = lax.broadcasted_iota(I32, (tb, 128), 1)
    l8 = lax.broadcasted_iota(I32, (tb, 8), 1)
    below = (lax.broadcasted_iota(I32, (tb, tb), 0)
             > lax.broadcasted_iota(I32, (tb, tb), 1))
    lstrict = jnp.where(below, 1.0, 0.0).astype(BF16)
    base = carry_ref[...]
    out = jnp.zeros((tb, 8), I32)
    for k in range(TOP_K):
        oh = jnp.where(lane == te[:, k:k + 1], 1.0, 0.0)
        before = _dotf(lstrict, oh.astype(BF16)) + base
        rank_k = jnp.sum(oh * before, axis=-1, keepdims=True)
        out = jnp.where(l8 == k, rank_k.astype(I32), out)
        base = base + jnp.sum(oh, axis=0, keepdims=True)
    rank_ref[...] = out
    carry_ref[...] = base
    cnt_ref[...] = base.astype(I32)


def _rank_call(te):
    n_tok = te.shape[0]
    tb = RANK_TILE
    return pl.pallas_call(
        _rank_kernel,
        grid=(n_tok // tb,),
        in_specs=[pl.BlockSpec((tb, 8), lambda i: (i, 0))],
        out_specs=[pl.BlockSpec((tb, 8), lambda i: (i, 0)), pl.BlockSpec((1, 128), lambda i: (0, 0))],
        out_shape=[jax.ShapeDtypeStruct((n_tok, 8), I32), jax.ShapeDtypeStruct((1, 128), I32)],
        scratch_shapes=[pltpu.VMEM((1, 128), F32)],
        compiler_params=_params(("arbitrary",)),
        name="moe_rank",
    )(te)


def _slot_kernel(tbl_ref, te_ref, rank_ref, slot_ref):
    te = te_ref[...]
    start = jnp.zeros(te.shape, I32)
    for e in range(N_EXPERTS):
        start = jnp.where(te == e, tbl_ref[e], start)
    slot_ref[...] = start + rank_ref[...]


def _slot_call(tbl, te, rank):
    n_tok = te.shape[0]
    tb = RANK_TILE
    spec = pl.BlockSpec((tb, 8), lambda i, tbl: (i, 0))
    return pl.pallas_call(
        _slot_kernel,
        grid_spec=pltpu.PrefetchScalarGridSpec(num_scalar_prefetch=1, grid=(n_tok // tb,),
                                               in_specs=[spec, spec], out_specs=spec),
        out_shape=jax.ShapeDtypeStruct((n_tok, 8), I32),
        compiler_params=_params(("parallel",)),
        name="moe_slots",
    )(tbl, te, rank)


def _slot_rows(slot_ref, r0, nt):
    return [pl.multiple_of(slot_ref[0, 0, (r0 + u) * 8 + k] * nt, nt)
            for u in range(DMA_UNROLL) for k in range(TOP_K)]


def _dispatch_kernel(tbl_ref, slot_ref, f_ref, xs_hbm, zbuf, sem, zsem, *, tb, tm, nt):
    i = pl.program_id(0)

    @pl.when(i == 0)
    def _():
        zbuf[...] = jnp.zeros_like(zbuf)
        fills = [pltpu.make_async_copy(
            zbuf, xs_hbm.at[pl.ds(pl.multiple_of(tbl_ref[N_EXPERTS + e] * nt, nt), tm * nt)], zsem)
            for e in range(N_EXPERTS)]
        for fill in fills:
            fill.start()
        for fill in fills:
            fill.wait()

        def fill_unused(blk, carry):
            tail = pltpu.make_async_copy(
                zbuf, xs_hbm.at[pl.ds(pl.multiple_of(blk * (tm * nt), tm * nt), tm * nt)], zsem)
            tail.start()
            tail.wait()
            return carry
        lax.fori_loop(tbl_ref[2 * N_EXPERTS], xs_hbm.shape[0] // (tm * nt), fill_unused, 0)

    def body(g, carry):
        r0 = g * DMA_UNROLL
        dsts = _slot_rows(slot_ref, r0, nt)
        for u in range(DMA_UNROLL):
            src = f_ref.at[pl.ds(pl.multiple_of((r0 + u) * nt, nt), nt)]
            for k in range(TOP_K):
                pltpu.make_async_copy(src, xs_hbm.at[pl.ds(dsts[u * TOP_K + k], nt)],
                                      sem).start(priority=k % 2)
        return carry

    lax.fori_loop(0, tb // DMA_UNROLL, body, 0)
    for k in range(TOP_K):
        pltpu.make_async_copy(f_ref, xs_hbm.at[pl.ds(0, tb * nt)], sem).wait()


def _dispatch_call(tbl, slot3, f_tiles, n_slots, tm):
    n_blk, _, per = slot3.shape
    tb = per // 8
    nt = f_tiles.shape[0] // (n_blk * tb)
    grid_spec = pltpu.PrefetchScalarGridSpec(
        num_scalar_prefetch=1,
        grid=(n_blk,),
        in_specs=[pl.BlockSpec((1, 1, per), lambda i, tbl: (i, 0, 0), memory_space=pltpu.SMEM),
                  pl.BlockSpec((tb * nt, 128), lambda i, tbl: (i, 0))],
        out_specs=pl.BlockSpec(memory_space=pl.ANY),
        scratch_shapes=[pltpu.VMEM((tm * nt, 128), F32), pltpu.SemaphoreType.DMA(()),
                        pltpu.SemaphoreType.DMA(())])
    return pl.pallas_call(
        functools.partial(_dispatch_kernel, tb=tb, tm=tm, nt=nt),
        grid_spec=grid_spec,
        out_shape=jax.ShapeDtypeStruct(((n_slots + tm) * nt, 128), F32),
        compiler_params=_params(("arbitrary",), VMEM_LIMIT),
        name="moe_dispatch",
    )(tbl, slot3, f_tiles)


def _expert_kernel(be_ref, nu_ref, first_ref, par_ref, nxt_ref, xs_ref, bgu_ref, bd_ref, wgu_hbm,
                   wd_hbm, ys_ref, wgu_f, wd_f, wgu_s, wd_s, sem, *, tm, d_ff, layer):
    i = pl.program_id(0)
    nt = xs_ref.shape[0] // tm

    def fetch(e, slot):
        return (pltpu.make_async_copy(wgu_hbm.at[layer, e], wgu_f.at[slot], sem.at[slot]),
                pltpu.make_async_copy(wd_hbm.at[layer, e], wd_f.at[slot], sem.at[slot]))

    @pl.when(i < nu_ref[0])
    def _():
        @pl.when(first_ref[i] == 1)
        def _():
            slot = par_ref[i]

            @pl.when(i == 0)
            def _():
                for cp in fetch(be_ref[0], 0):
                    cp.start()

            for cp in fetch(be_ref[i], slot):
                cp.wait()

            @pl.when(nxt_ref[i] >= 0)
            def _():
                for cp in fetch(nxt_ref[i], 1 - slot):
                    cp.start()

            for c in range(0, wgu_s.shape[0], 128):
                wgu_s[c:c + 128, :] = wgu_f[slot, c:c + 128, :].astype(BF16)
            for c in range(0, wd_s.shape[0], 128):
                wd_s[c:c + 128, :] = wd_f[slot, c:c + 128, :].astype(BF16)

        x = _load_token_tiles(xs_ref, tm, nt).astype(BF16)
        gu = _dotf(x, wgu_s[...]) + bgu_ref[...]
        gate = jnp.minimum(gu[:, 0:d_ff], SWIGLU_LIMIT)
        up = jnp.clip(gu[:, d_ff:2 * d_ff], -SWIGLU_LIMIT, SWIGLU_LIMIT)
        act = (up + 1.0) * gate * _sigmoid(SWIGLU_ALPHA * gate)
        _store_token_tiles(ys_ref, _dotf(act.astype(BF16), wd_s[...]) + bd_ref[...])

    @pl.when(i >= nu_ref[0])
    def _():
        ys_ref[...] = jnp.zeros_like(ys_ref)


def _expert_call(block_expert, n_used, xs_tiles, wgu, bgu, wd, bd, layer, n_blocks, tm):
    d, two_ff = wgu.shape[2:]
    d_ff = two_ff // 2
    nt = d // 128
    idx = jnp.arange(n_blocks, dtype=I32)
    active = idx < n_used[0]
    first = jnp.logical_and(active, jnp.logical_or(idx == 0, block_expert != jnp.roll(block_expert, 1)))
    parity = (jnp.cumsum(first.astype(I32)) - 1) % 2
    later = jnp.where(jnp.logical_and(block_expert[None, :] > block_expert[:, None], active[None, :]),
                      block_expert[None, :], N_EXPERTS)
    nxt = jnp.min(later, axis=1)
    nxt = jnp.where(nxt == N_EXPERTS, -1, nxt).astype(I32)
    ex = lambda i, be, nu, fi, pa, nx: (layer, be[i], 0, 0)
    grid_spec = pltpu.PrefetchScalarGridSpec(
        num_scalar_prefetch=5,
        grid=(n_blocks,),
        in_specs=[pl.BlockSpec((tm * nt, 128),
                               lambda i, be, nu, fi, pa, nx: (jnp.minimum(i, nu[0] - 1), 0)),
                  pl.BlockSpec((None, None, 1, two_ff), ex),
                  pl.BlockSpec((None, None, 1, d), ex),
                  pl.BlockSpec(memory_space=pl.ANY), pl.BlockSpec(memory_space=pl.ANY)],
        out_specs=pl.BlockSpec((tm * nt, 128), lambda i, be, nu, fi, pa, nx: (i, 0)),
        scratch_shapes=[pltpu.VMEM((2, d, two_ff), F32), pltpu.VMEM((2, d_ff, d), F32),
                        pltpu.VMEM((d, two_ff), BF16), pltpu.VMEM((d_ff, d), BF16),
                        pltpu.SemaphoreType.DMA((2,))])
    return pl.pallas_call(
        functools.partial(_expert_kernel, tm=tm, d_ff=d_ff, layer=layer),
        grid_spec=grid_spec,
        out_shape=jax.ShapeDtypeStruct((n_blocks * tm * nt, 128), F32),
        compiler_params=_params(("arbitrary",), VMEM_LIMIT),
        name="moe_experts",
    )(block_expert, n_used, first.astype(I32), parity.astype(I32), nxt, xs_tiles, bgu, bd, wgu, wd)


def _route(counts, n_tok, tm):
    n_blocks = -(-(n_tok * TOP_K + N_EXPERTS * (tm - 1)) // tm)
    padded = (counts + tm - 1) // tm * tm
    padded_end = jnp.cumsum(padded)
    group_start = padded_end - padded
    block_start = jnp.arange(n_blocks, dtype=I32) * tm
    block_expert = jnp.minimum(
        jnp.sum((padded_end[None, :] <= block_start[:, None]).astype(I32), axis=1), N_EXPERTS - 1)
    n_used = (padded_end[-1] // tm).astype(I32).reshape(1)
    table = jnp.concatenate([group_start, group_start + counts, n_used]).astype(I32)
    return table, block_expert, n_used, n_blocks


def _fin_kernel(scur_ref, snxt_ref, g_ref, x_ref, g2_ref, nw_ref, ys_hbm, out_ref, buf, sem,
                *, tb, nt):
    i = pl.program_id(0)
    s = i % 2

    def gather(slot_ref, ss):
        def body(g, carry):
            r0 = g * DMA_UNROLL
            srcs = _slot_rows(slot_ref, r0, nt)
            for u in range(DMA_UNROLL):
                for k in range(TOP_K):
                    dst = pl.multiple_of((k * tb + r0 + u) * nt, nt)
                    pltpu.make_async_copy(ys_hbm.at[pl.ds(srcs[u * TOP_K + k], nt)],
                                          buf.at[ss, pl.ds(dst, nt)],
                                          sem.at[ss]).start(priority=k % 2)
            return carry
        lax.fori_loop(0, tb // DMA_UNROLL, body, 0)

    @pl.when(i == 0)
    def _():
        gather(scur_ref, 0)

    @pl.when(i + 1 < pl.num_programs(0))
    def _():
        gather(snxt_ref, 1 - s)

    pltpu.make_async_copy(ys_hbm.at[pl.ds(0, TOP_K * tb * nt)], buf.at[s], sem.at[s]).wait()
    gates = g_ref[...]
    parts = []
    for j in range(nt):
        acc = None
        for k in range(TOP_K):
            v = buf[s, pl.ds(k * tb * nt + j, tb, stride=nt), :] * gates[:, k:k + 1]
            acc = v if acc is None else acc + v
        parts.append(acc)
    m = jnp.concatenate(parts, axis=1)
    out_ref[0] = x_ref[0] + g2_ref[0] * _rms(m, nw_ref[...])


def _fin_call(slot3, gates, ys_tiles, xn, mod_l, nw, n_ctx_blk, blk0):
    b, t, d = xn.shape
    tb = ROW_TILE
    nblk = t // tb
    nt = d // 128
    n_blk = b * nblk

    def mrow(i):
        return jnp.where(i % nblk + blk0 < n_ctx_blk, b, i // nblk)

    smem = functools.partial(pl.BlockSpec, memory_space=pltpu.SMEM)
    return pl.pallas_call(
        functools.partial(_fin_kernel, tb=tb, nt=nt),
        grid=(n_blk,),
        in_specs=[smem((1, 1, tb * 8), lambda i: (i, 0, 0)),
                  smem((1, 1, tb * 8), lambda i: (jnp.minimum(i + 1, n_blk - 1), 0, 0)),
                  pl.BlockSpec((tb, 8), lambda i: (i, 0)),
                  pl.BlockSpec((1, tb, d), lambda i: (i // nblk, i % nblk, 0)),
                  pl.BlockSpec((1, 1, d), lambda i: (mrow(i), 0, 5)),
                  pl.BlockSpec((1, d), lambda i: (0, 0)),
                  pl.BlockSpec(memory_space=pl.ANY)],
        out_specs=pl.BlockSpec((1, tb, d), lambda i: (i // nblk, i % nblk, 0)),
        out_shape=jax.ShapeDtypeStruct((b, t, d), F32),
        scratch_shapes=[pltpu.VMEM((2, TOP_K * tb * nt, 128), F32), pltpu.SemaphoreType.DMA((2,))],
        compiler_params=_params(("arbitrary",), VMEM_LIMIT),
        name="moe_combine",
    )(slot3, slot3, gates, xn, mod_l, nw, ys_tiles)


def _in_proj_columns():
    cols = np.full((IN_COLS,), -1, np.int64)
    for sec in range(2):
        for n in range(256):
            part, hm, j = n // 128, (n % 128) // 16, n % 16
            m, h = hm // 4, hm % 4
            cols[sec * 256 + n] = sec * 256 + h * 64 + m * 32 + part * 16 + j
    cols[512:768] = np.arange(512, 768)
    o, s = IN_DA, 768
    cols[o:o + 1024] = s + np.arange(1024)
    o, s = o + IN_ML, s + 1024
    cols[o:o + 16] = s + np.arange(16)
    o, s = o + IN_G, s + 16
    cols[o:o + MLA_Q_RANK] = s + np.arange(MLA_Q_RANK)
    cols[o + 256:o + 256 + MLA_KV_RANK] = s + MLA_Q_RANK + np.arange(MLA_KV_RANK)
    cols[o + 384 + 64:o + 384 + 96] = s + MLA_Q_RANK + MLA_KV_RANK + np.arange(MLA_ROPE)
    o, s = o + IN_MLA, s + MLA_Q_RANK + MLA_KV_RANK + MLA_ROPE
    cols[o:o + 256] = s + np.arange(256)
    return cols


def _take_cols(w, cols):
    valid = jnp.asarray(cols >= 0)
    return jnp.where(valid, jnp.take(w, jnp.asarray(np.maximum(cols, 0)), axis=-1), 0.0)


def _rope_tables(n_ctx, n_lat):
    pos = jnp.arange(n_lat)
    inv = ROPE_THETA ** (-jnp.arange(8, dtype=F32) / 8)
    ang = jnp.concatenate([(pos // GRID_W)[:, None] * inv, (pos % GRID_W)[:, None] * inv], axis=-1)
    cos = jnp.concatenate([jnp.ones((n_ctx, 16), F32), jnp.cos(ang)], axis=0)
    sin = jnp.concatenate([jnp.zeros((n_ctx, 16), F32), jnp.sin(ang)], axis=0)
    t = n_ctx + n_lat
    cos_da, sin_da = jnp.tile(cos, (1, 8)), jnp.tile(sin, (1, 8))
    one, zero = jnp.ones((t, 64), F32), jnp.zeros((t, 64), F32)
    z16, z32 = jnp.zeros((t, 16), F32), jnp.zeros((t, 32), F32)
    cm = jnp.concatenate([one, cos, cos, jnp.ones((t, 32), F32)], axis=1)
    sa = jnp.concatenate([zero, -sin, z16, z32], axis=1)
    sb = jnp.concatenate([zero, z16, sin, z32], axis=1)
    return cos_da, sin_da, cm, sa, sb


def _mla_weights(w_uq, w_ukv):
    hd = MLA_NOPE + MLA_ROPE
    wq = jnp.pad(w_uq.reshape(MLA_Q_RANK, MLA_HEADS, hd),
                 ((0, 256 - MLA_Q_RANK), (0, 0), (0, 128 - hd))).reshape(256, 128 * MLA_HEADS)
    kv = w_ukv.reshape(MLA_KV_RANK, MLA_HEADS, -1)
    wk = jnp.pad(kv[:, :, :MLA_NOPE], ((0, 0), (0, 0), (0, 128 - MLA_NOPE)))
    wkv = jnp.concatenate([wk.reshape(MLA_KV_RANK, -1), kv[:, :, MLA_NOPE:].reshape(MLA_KV_RANK, -1)],
                          axis=1)
    return wq.astype(BF16), wkv.astype(BF16)


def _s5_layout(a_re, a_im, log_step, b_re, b_im, c_re, c_im):
    ns = S5_NGROUPS * S5_STATE
    eye = jnp.eye(S5_NGROUPS, dtype=F32)
    are = a_re.reshape(2, 1, ns)
    aim = a_im.reshape(2, 1, ns)
    ls = jnp.repeat(log_step, S5_STATE, axis=-1).reshape(2, 1, ns)
    bd_b = lambda w: jnp.einsum("dgph,gk->dghkp", w, eye).reshape(2, GROUP_W, ns)
    bd_c = lambda w: jnp.einsum("dghp,gk->dgpkh", w, eye).reshape(2, ns, GROUP_W)
    return are, aim, ls, bd_b(b_re), bd_b(b_im), bd_c(c_re), bd_c(c_im)


def kernel(x, c, ctx, c_ctx, w_mod, b_mod, norm_w, w_in, w_out, da_lambda, da_subln, ml_conv_w,
           ml_conv_b, ml_gate_b, ml_norm, mla_q_norm, mla_w_uq, mla_kv_norm, mla_w_ukv, s5_a_re,
           s5_a_im, s5_log_step, s5_b_re, s5_b_im, s5_c_re, s5_c_im, s5_d, s5_w_glu, s5_b_glu,
           moe_w_router, moe_b_router, moe_w_gate_up, moe_b_gate_up, moe_w_down, moe_b_down):
    bsz, n_lat, d = x.shape
    n_ctx = ctx.shape[1]
    t = n_ctx + n_lat
    depth = w_mod.shape[0]
    tm = ROW_TILE
    assert n_ctx % tm == 0 and n_lat % tm == 0 and bsz % 8 == 0 and bsz < 16
    assert n_ctx % ML_CHUNK == 0 and n_ctx % S5_CHUNK == 0
    n_ctx_blk = n_ctx // tm

    cc = jnp.pad(jnp.concatenate([c, c_ctx[None]], axis=0), ((0, 15 - bsz), (0, 0)))
    mod = _mod_call(cc, w_mod, b_mod)
    cos_da, sin_da, cm, sa, sb = _rope_tables(n_ctx, n_lat)
    in_cols = _in_proj_columns()
    xa, xb = ctx, x

    for l in range(depth):
        last = l == depth - 1
        lambda_init = 0.8 - 0.6 * math.exp(-0.3 * l)
        mod_l = mod[l].reshape(16, 1, 6 * d)
        w_in_p = _take_cols(w_in[l], in_cols).astype(BF16)
        wq, wkv = _mla_weights(mla_w_uq[l], mla_w_ukv[l])
        qn = jnp.pad(mla_q_norm[l], (0, 256 - MLA_Q_RANK)).reshape(1, 256)
        mla = (qn, wq, mla_kv_norm[l].reshape(1, -1), wkv, cm, sa, sb)
        p_da, p_ml, p_g, q_mla, k_mla, v_mla, u_t = _in_call(
            xa, xb, mod_l, norm_w[l, 0].reshape(1, d), w_in_p, cos_da, sin_da, mla, n_ctx_blk)

        da_extra = [da_lambda[l], jnp.tile(da_subln[l], 4).reshape(1, GROUP_W)]
        da_kern = functools.partial(_da_kernel, lambda_init=lambda_init)
        da_kw = dict(q_col=0, k_col=1, v_col=2, q_w=GROUP_W)
        ya = _attn_call(da_kern, "diff_attn", p_da, p_da, p_da, da_extra, q_blk0=n_ctx_blk,
                        n_q_blk=n_lat // tm, n_keys=t, **da_kw)
        mla_kw = dict(q_col=0, k_col=0, v_col=0, q_w=512)
        yc = _attn_call(_mla_attn_kernel, "mla_attn", q_mla, k_mla, v_mla, [], q_blk0=n_ctx_blk,
                        n_q_blk=n_lat // tm, n_keys=t, **mla_kw)
        if not last:
            ya_c = _attn_call(da_kern, "diff_attn_ctx", p_da, p_da, p_da, da_extra, q_blk0=0,
                              n_q_blk=n_ctx_blk, n_keys=n_ctx, **da_kw)
            yc_c = _attn_call(_mla_attn_kernel, "mla_attn_ctx", q_mla, k_mla, v_mla, [], q_blk0=0,
                              n_q_blk=n_ctx_blk, n_keys=n_ctx, **mla_kw)
            ya = jnp.concatenate([ya_c, ya], axis=1)
            yc = jnp.concatenate([yc_c, yc], axis=1)

        gb = jnp.pad(ml_gate_b[l], (0, 128 - 16)).reshape(1, 128)
        yb = _ml_call(p_ml, p_g, ml_conv_w[l], ml_conv_b[l].reshape(1, -1), gb,
                      ml_norm[l].reshape(1, -1), n_ctx)

        s5p = _s5_layout(s5_a_re[l], s5_a_im[l], s5_log_step[l], s5_b_re[l], s5_b_im[l],
                         s5_c_re[l], s5_c_im[l])
        ys = _s5_call(u_t.reshape(t * bsz, GROUP_W), *s5p, n_ctx, bsz)
        yd = _glu_call(ys, u_t, s5_d[l].reshape(1, -1),
                       s5_w_glu[l].astype(BF16), s5_b_glu[l].reshape(1, -1), bsz)

        blk0 = n_ctx_blk if last else 0
        wr = jnp.pad(moe_w_router[l], ((0, 0), (0, 128 - N_EXPERTS)))
        br = jnp.pad(moe_b_router[l], (0, 128 - N_EXPERTS), constant_values=-1e30).reshape(1, 128)
        xn, f, te, tg = _out_call(ya, yb, yc, yd, w_out[l].astype(BF16), xa, xb, mod_l,
                                  norm_w[l, 1].reshape(1, d), norm_w[l, 2].reshape(1, d), wr, br,
                                  n_ctx_blk, blk0)

        t_moe = t - blk0 * tm
        n_tok = bsz * t_moe
        te = te.reshape(n_tok, 8)
        rank, cnt = _rank_call(te)
        tbl, be, nu, n_blocks = _route(cnt[0, :N_EXPERTS], n_tok, MOE_TILE)
        slots = _slot_call(tbl, te, rank)
        slot3 = slots.reshape(n_tok // tm, 1, tm * 8)
        xs_tiles = _dispatch_call(tbl, slots.reshape(n_tok // DISPATCH_TILE, 1, DISPATCH_TILE * 8), f,
                                  n_blocks * MOE_TILE, MOE_TILE)
        ys_tiles = _expert_call(be, nu, xs_tiles, moe_w_gate_up,
                                moe_b_gate_up.reshape(depth, N_EXPERTS, 1, -1), moe_w_down,
                                moe_b_down.reshape(depth, N_EXPERTS, 1, -1), l, n_blocks, MOE_TILE)
        xa, xb = _fin_call(slot3, tg.reshape(n_tok, 8), ys_tiles, xn, mod_l,
                           norm_w[l, 3].reshape(1, d), n_ctx_blk, blk0), None
    return xa
```

```python
import functools
import math

import numpy as np
import jax
import jax.numpy as jnp
from jax import lax
from jax.experimental import pallas as pl
from jax.experimental.pallas import tpu as pltpu

F32, BF16, I32 = jnp.float32, jnp.bfloat16, jnp.int32
NORM_EPS = 1e-6
GRID_W = 64
ROPE_THETA = 10000.0
GROUP_W = 256
DA_QK = 32
ML_CHUNK = 128
ML_ROWS = 2
MLA_HEADS, MLA_NOPE, MLA_ROPE, MLA_Q_RANK, MLA_KV_RANK = 4, 64, 32, 192, 128
S5_NGROUPS, S5_GROUP, S5_STATE = 16, 16, 64
N_EXPERTS, TOP_K = 32, 4
SWIGLU_ALPHA, SWIGLU_LIMIT = 1.702, 7.0
NEG_INF = float("-inf")

ROW_TILE = 256
MOE_TILE = 512
S5_CHUNK = 64
RANK_TILE = 512
DISPATCH_TILE = 1024
DMA_UNROLL = 8
VMEM_LIMIT = 56 * 1024 * 1024

IN_DA, IN_ML, IN_G, IN_MLA, IN_S5 = 768, 1024, 128, 512, 256
IN_COLS = IN_DA + IN_ML + IN_G + IN_MLA + IN_S5


def _params(sem, vmem=None):
    return pltpu.CompilerParams(dimension_semantics=sem, vmem_limit_bytes=vmem)


def _dotf(a, b):
    return jnp.dot(a, b, preferred_element_type=F32)


def _dot_nt(a, b):
    return lax.dot_general(a, b, (((1,), (1,)), ((), ())), preferred_element_type=F32)


def _split2(a):
    hi = a.astype(BF16)
    lo = (a - hi.astype(F32)).astype(BF16)
    return hi, lo


def _dot3(a, b):
    ah, al = _split2(a)
    bh, bl = _split2(b)
    return _dotf(ah, bh) + _dotf(ah, bl) + _dotf(al, bh)


def _dot_exact_rhs(a, rhs_b):
    a1 = a.astype(BF16)
    r1 = a - a1.astype(F32)
    a2 = r1.astype(BF16)
    a3 = (r1 - a2.astype(F32)).astype(BF16)
    return _dotf(a1, rhs_b) + _dotf(a2, rhs_b) + _dotf(a3, rhs_b)


def _rms(x, w):
    ms = jnp.mean(x * x, axis=-1, keepdims=True)
    return x * lax.rsqrt(ms + NORM_EPS) * w


def _head_rms(a, width):
    n = a.shape[-1]
    sh = int(math.log2(width))
    r = lax.broadcasted_iota(I32, (n, n), 0) >> sh
    c = lax.broadcasted_iota(I32, (n, n), 1) >> sh
    g = jnp.where(r == c, 1.0 / width, 0.0).astype(BF16)
    hi, lo = _split2(a * a)
    ms = _dotf(hi, g) + _dotf(lo, g)
    return a * lax.rsqrt(ms + NORM_EPS)


def _store_token_tiles(ref, val):
    tm, d = val.shape
    nt = d // 128
    for j in range(nt):
        ref[pl.ds(j, tm, stride=nt), :] = val[:, 128 * j:128 * (j + 1)]


def _load_token_tiles(ref, tm, nt):
    return jnp.concatenate([ref[pl.ds(j, tm, stride=nt), :] for j in range(nt)], axis=1)


def _sigmoid(x):
    return jax.nn.sigmoid(x)


def _log_sigmoid(x):
    return jnp.minimum(x, 0.0) - jnp.log(1.0 + jnp.exp(-jnp.abs(x)))


def _mod_kernel(c_ref, w_ref, b_ref, o_ref):
    c = c_ref[...]
    o_ref[0] = _dot3(c * _sigmoid(c), w_ref[0]) + b_ref[0]


def _mod_call(cc, w_mod, b_mod):
    n_layers, d, n = w_mod.shape
    tn = 1536
    return pl.pallas_call(
        _mod_kernel,
        grid=(n_layers, n // tn),
        in_specs=[pl.BlockSpec((16, d), lambda l, j: (0, 0)),
                  pl.BlockSpec((1, d, tn), lambda l, j: (l, 0, j)),
                  pl.BlockSpec((1, 1, tn), lambda l, j: (l, 0, j))],
        out_specs=pl.BlockSpec((1, 16, tn), lambda l, j: (l, 0, j)),
        out_shape=jax.ShapeDtypeStruct((n_layers, 16, n), F32),
        compiler_params=_params(("parallel", "parallel"), VMEM_LIMIT),
        name="mod_vectors",
    )(cc, w_mod, b_mod.reshape(n_layers, 1, n))


def _in_kernel(xa_ref, xb_ref, sh_ref, sc_ref, nw_ref, w_ref, c_ref, s_ref,
               qn_ref, wq_ref, kvn_ref, wkv_ref, cm_ref, sa_ref, sb_ref,
               da_ref, ml_ref, g_ref, mq_ref, mk_ref, mv_ref, s5_ref, *, qscale, mla_scale, n_a_blk):
    x = jnp.where(pl.program_id(1) < n_a_blk, xa_ref[0], xb_ref[0])
    h = _rms(x, nw_ref[...]) * (1.0 + sc_ref[0]) + sh_ref[0]
    hb = h.astype(BF16)
    da = _dotf(hb, w_ref[:, 0:IN_DA])
    c = c_ref[...]
    s = s_ref[...]
    q1, q2, k1, k2 = da[:, 0:128], da[:, 128:256], da[:, 256:384], da[:, 384:512]
    da_ref[0, :, 0:128] = ((q1 * c - q2 * s) * qscale).astype(BF16)
    da_ref[0, :, 128:256] = ((q2 * c + q1 * s) * qscale).astype(BF16)
    da_ref[0, :, 256:384] = (k1 * c - k2 * s).astype(BF16)
    da_ref[0, :, 384:512] = (k2 * c + k1 * s).astype(BF16)
    da_ref[0, :, 512:768] = da[:, 512:768].astype(BF16)
    o = IN_DA
    ml_ref[0] = _dotf(hb, w_ref[:, o:o + IN_ML]).astype(BF16)
    o += IN_ML
    g_ref[0] = _dotf(hb, w_ref[:, o:o + IN_G])
    o += IN_G
    _mla_project(_dotf(hb, w_ref[:, o:o + IN_MLA]), qn_ref, wq_ref, kvn_ref, wkv_ref,
                 cm_ref, sa_ref, sb_ref, mq_ref, mk_ref, mv_ref, mla_scale)
    o += IN_MLA
    s5_ref[...] = _dotf(hb, w_ref[:, o:o + IN_S5])


def _stream_specs(xa, xb, blk0):
    tm = ROW_TILE
    d = xa.shape[2]
    n_a = xa.shape[1] // tm
    spec_a = pl.BlockSpec((1, tm, d), lambda bi, ti: (bi, jnp.minimum(ti + blk0, n_a - 1), 0))
    spec_b = pl.BlockSpec((1, tm, d), lambda bi, ti: (bi, jnp.maximum(ti + blk0 - n_a, 0), 0))
    return spec_a, spec_b, n_a


def _in_call(xa, xb, mod_l, nw, w_in_p, cos_da, sin_da, mla, n_ctx_blk):
    b, _, d = xa.shape
    t = xa.shape[1] + (0 if xb is None else xb.shape[1])
    tm = ROW_TILE
    spec_a, spec_b, n_a_blk = _stream_specs(xa, xb, 0)

    def mrow(bi, ti):
        return jnp.where(ti < n_ctx_blk, b, bi)

    row3 = lambda bi, ti: (bi, ti, 0)
    tab = pl.BlockSpec((tm, 128), lambda bi, ti: (ti, 0))
    return pl.pallas_call(
        functools.partial(_in_kernel, qscale=DA_QK ** -0.5 * LOG2_E,
                          mla_scale=(MLA_NOPE + MLA_ROPE) ** -0.5 * LOG2_E, n_a_blk=n_a_blk),
        grid=(b, t // tm),
        in_specs=[spec_a, spec_b,
                  pl.BlockSpec((1, 1, d), lambda bi, ti: (mrow(bi, ti), 0, 0)),
                  pl.BlockSpec((1, 1, d), lambda bi, ti: (mrow(bi, ti), 0, 1)),
                  pl.BlockSpec((1, d), lambda bi, ti: (0, 0)),
                  pl.BlockSpec((d, IN_COLS), lambda bi, ti: (0, 0)),
                  tab, tab] + [pl.BlockSpec(a.shape, lambda bi, ti: (0, 0)) for a in mla[:4]]
                 + [tab, tab, tab],
        out_specs=[pl.BlockSpec((1, tm, IN_DA), row3),
                   pl.BlockSpec((1, tm, IN_ML), row3),
                   pl.BlockSpec((1, tm, IN_G), row3),
                   pl.BlockSpec((1, tm, 512), row3), pl.BlockSpec((1, tm, 512), row3),
                   pl.BlockSpec((1, tm, GROUP_W), row3),
                   pl.BlockSpec((tm, IN_S5), lambda bi, ti: (ti, bi))],
        out_shape=[jax.ShapeDtypeStruct((b, t, IN_DA), BF16),
                   jax.ShapeDtypeStruct((b, t, IN_ML), BF16),
                   jax.ShapeDtypeStruct((b, t, IN_G), F32),
                   jax.ShapeDtypeStruct((b, t, 512), BF16), jax.ShapeDtypeStruct((b, t, 512), BF16),
                   jax.ShapeDtypeStruct((b, t, GROUP_W), BF16),
                   jax.ShapeDtypeStruct((t, b * IN_S5), F32)],
        compiler_params=_params(("parallel", "parallel"), VMEM_LIMIT),
        name="in_proj",
    )(xa, xa if xb is None else xb, mod_l, mod_l, nw, w_in_p, cos_da, sin_da, *mla)


LOG2_E = 1.0 / math.log(2.0)


def _softmax_rows(s):
    mx = jnp.max(s, axis=-1, keepdims=True)
    p = jnp.exp2(s - mx)
    return p, jnp.sum(p, axis=-1, keepdims=True)


def _da_kernel(q_ref, k_ref, v_ref, lam_ref, sub_ref, o_ref, *, lambda_init):
    q = q_ref[0]
    k = k_ref[0]
    v = v_ref[0]
    lp = lam_ref[...]
    lam = (jnp.exp(jnp.sum(lp[0:1] * lp[1:2], axis=-1, keepdims=True))
           - jnp.exp(jnp.sum(lp[2:3] * lp[3:4], axis=-1, keepdims=True)) + lambda_init)
    lane = lax.broadcasted_iota(I32, (1, GROUP_W), 1)
    grp = (lane & 127) >> 4
    head = lane >> 6
    acc = jnp.zeros((q.shape[0], GROUP_W), F32)
    for h in range(4):
        ps, rs = [], []
        for m in range(2):
            qm = jnp.where(grp == m * 4 + h, q, jnp.zeros_like(q))
            p, l = _softmax_rows(_dot_nt(qm, k))
            ps.append(p.astype(BF16))
            rs.append(((1.0 if m == 0 else lam) / l).astype(BF16))
        w = ps[0] * rs[0] - ps[1] * rs[1]
        vm = jnp.where(head == h, v, jnp.zeros_like(v))
        acc = acc + _dotf(w, vm)
    y = _head_rms(acc, 64) * sub_ref[...] * (1.0 - lambda_init)
    o_ref[0] = y.astype(BF16)


def _mla_attn_kernel(q_ref, k_ref, v_ref, o_ref):
    q = q_ref[0]
    k = k_ref[0]
    v = v_ref[0]
    head = lax.broadcasted_iota(I32, (1, GROUP_W), 1) >> 6
    acc = jnp.zeros((q.shape[0], GROUP_W), F32)
    for h in range(MLA_HEADS):
        sl = slice(128 * h, 128 * (h + 1))
        p, l = _softmax_rows(_dot_nt(q[:, sl], k[:, sl]))
        vm = jnp.where(head == h, v, jnp.zeros_like(v))
        acc = acc + _dotf(p.astype(BF16), vm) * (1.0 / l)
    o_ref[0] = acc.astype(BF16)


def _attn_call(kernel, name, q_arr, k_arr, v_arr, extra, *, q_blk0, n_q_blk, n_keys,
               q_col, k_col, v_col, q_w):
    b = q_arr.shape[0]
    tq = ROW_TILE
    in_specs = [pl.BlockSpec((1, tq, q_w), lambda bi, qi: (bi, qi + q_blk0, q_col)),
                pl.BlockSpec((1, n_keys, q_w), lambda bi, qi: (bi, 0, k_col)),
                pl.BlockSpec((1, n_keys, GROUP_W), lambda bi, qi: (bi, 0, v_col))]
    in_specs += [pl.BlockSpec(e.shape, lambda bi, qi: (0, 0)) for e in extra]
    return pl.pallas_call(
        kernel,
        grid=(b, n_q_blk),
        in_specs=in_specs,
        out_specs=pl.BlockSpec((1, tq, GROUP_W), lambda bi, qi: (bi, qi, 0)),
        out_shape=jax.ShapeDtypeStruct((b, n_q_blk * tq, GROUP_W), BF16),
        compiler_params=_params(("parallel", "arbitrary"), VMEM_LIMIT),
        name=name,
    )(q_arr, k_arr, v_arr, *extra)


def _mla_project(p, qn_ref, wq_ref, kvn_ref, wkv_ref, c_ref, sa_ref, sb_ref, q_ref, k_ref, v_ref,
                 scale):
    cq, ckv, kr = p[:, 0:256], p[:, 256:384], p[:, 384:512]
    msq = jnp.sum(cq * cq, axis=-1, keepdims=True) * (1.0 / MLA_Q_RANK)
    qn = (cq * lax.rsqrt(msq + NORM_EPS) * qn_ref[...]).astype(BF16)
    q = _dotf(qn, wq_ref[...])
    c = c_ref[...]
    sa = sa_ref[...]
    sb = sb_ref[...]

    def rope(a):
        return a * c + pltpu.roll(a, 112, 1) * sa + pltpu.roll(a, 16, 1) * sb

    for h in range(MLA_HEADS):
        sl = slice(128 * h, 128 * (h + 1))
        q_ref[0, :, sl] = (rope(q[:, sl]) * scale).astype(BF16)
    kvn = (_rms(ckv, kvn_ref[...])).astype(BF16)
    kv = _dotf(kvn, wkv_ref[...])
    krr = rope(kr)
    for h in range(MLA_HEADS):
        sl = slice(128 * h, 128 * (h + 1))
        k_ref[0, :, sl] = (kv[:, sl] + krr).astype(BF16)
    v_ref[0] = kv[:, 512:768].astype(BF16)


def _ml_kernel(p_ref, g_ref, cw_ref, cb_ref, gb_ref, nw_ref, y_ref,
               qt_s, k_s, vt_s, ot_s, gt_s, hf_s, c_s, m_s, *, n_ctx_chunks, n_chunks):
    cl = ML_CHUNK
    t_total = n_chunks * cl
    nbr = p_ref.shape[0]
    row = lax.broadcasted_iota(I32, (cl, 1), 0)
    si = lax.broadcasted_iota(I32, (cl, cl), 0)
    ti = lax.broadcasted_iota(I32, (cl, cl), 1)
    lane128 = lax.broadcasted_iota(I32, (1, 128), 1)
    lane256 = lax.broadcasted_iota(I32, (1, 256), 1)
    row128 = lax.broadcasted_iota(I32, (128, 1), 0)
    hmask = [(lane256 >> 6) == h for h in range(4)]
    is_f = jnp.logical_and(((lane128 >> 2) & 1) == 1, lane128 < 16)
    r_nd = lax.broadcasted_iota(I32, (384, 512), 0)
    c_nd = lax.broadcasted_iota(I32, (384, 512), 1) >> 7
    nd_head = jnp.where(r_nd < 256, r_nd >> 6, r_nd - 256)
    nd_mask = nd_head == c_nd
    nd_ones = jnp.where(jnp.logical_and(nd_mask, r_nd >= 256), 1.0, 0.0).astype(BF16)
    r_st = lax.broadcasted_iota(I32, (384, 256), 0)
    st_head = jnp.where(r_st < 256, r_st >> 6, r_st - 256)
    st_mask = st_head == (lax.broadcasted_iota(I32, (384, 256), 1) >> 6)
    st_rowhead = jnp.where(lax.broadcasted_iota(I32, (384, 1), 0) < 256,
                           lax.broadcasted_iota(I32, (384, 1), 0) >> 6,
                           lax.broadcasted_iota(I32, (384, 1), 0) - 256)
    w0, w1, w2 = cw_ref[0:1], cw_ref[1:2], cw_ref[2:3]
    cb = cb_ref[...]
    gb = gb_ref[...]

    def prep_one(c, bb):
        s0 = pl.multiple_of(c * cl, cl)
        x = p_ref[bb, pl.ds(s0, cl), 0:512].astype(F32)
        sp = pl.multiple_of(jnp.maximum(s0 - 16, 0), 16)
        sn = pl.multiple_of(jnp.minimum(s0 + cl, t_total - 16), 16)
        has_prev = jnp.logical_and(c != 0, c != n_ctx_chunks).astype(F32)
        has_next = jnp.logical_and(c != n_ctx_chunks - 1, c != n_chunks - 1).astype(F32)
        prev_row = p_ref[bb, pl.ds(sp, 16), 0:512][15:16].astype(F32) * has_prev
        next_row = p_ref[bb, pl.ds(sn, 16), 0:512][0:1].astype(F32) * has_next
        xp = jnp.where(row == 0, prev_row, pltpu.roll(x, 1, 0))
        xn = jnp.where(row == cl - 1, next_row, pltpu.roll(x, cl - 1, 0))
        z = xp * w0 + x * w1 + xn * w2 + cb
        qk = z * _sigmoid(z)
        qt_s[bb, c] = qk[:, 0:256].T.astype(BF16)
        k_s[bb, c] = (qk[:, 256:512] * (64 ** -0.5)).astype(BF16)
        vt_s[bb, c] = p_ref[bb, pl.ds(s0, cl), 512:768].astype(F32).T
        ot_s[bb, c] = p_ref[bb, pl.ds(s0, cl), 768:1024].astype(F32).T
        g = g_ref[bb, pl.ds(s0, cl), :] + gb
        gt_s[bb, c] = jnp.where(is_f, _log_sigmoid(g), g).T[0:16]

    def prep_body(c, carry):
        for bb in range(nbr):
            prep_one(c, bb)
        return carry

    def rows_to_blocks(rows, n):
        return jnp.concatenate([jnp.broadcast_to(r, (n, 128)) for r in rows], axis=0)

    def head_rows(rows):
        out = jnp.zeros((128, 128), F32)
        for h in range(4):
            out = jnp.where(row128 == h, rows[h], out)
        return out

    def chunk(c, reverse, bb):
        g_t = gt_s[bb, c]
        tri = (si >= ti) if reverse else (si <= ti)
        bc_t = _dot_exact_rhs(g_t, jnp.where(tri, 1.0, 0.0).astype(BF16))
        qt = qt_s[bb, c]
        kb = k_s[bb, c]
        m_all = m_s[bb]
        off = 8 if reverse else 0
        a_rows, r_rows, tots, m_old = [], [], [], []
        for h in range(4):
            il, fl = off + h, off + 4 + h
            a_rows.append(bc_t[fl:fl + 1, :])
            r_rows.append(g_t[il:il + 1, :] - bc_t[fl:fl + 1, :])
            tots.append(bc_t[fl:fl + 1, 0:1] if reverse else bc_t[fl:fl + 1, cl - 1:cl])
            m_old.append(m_all[:, h:h + 1])
        r_all = jnp.concatenate(r_rows, axis=1)
        d_t = jnp.broadcast_to(r_all, (cl, 4 * cl)).T + rows_to_blocks(a_rows, cl)
        lw = jnp.where(jnp.concatenate([tri] * 4, axis=0), d_t, NEG_INF)
        mt, wi, emt = [], [], []
        for h in range(4):
            linter = a_rows[h] + m_old[h]
            mt_h = jnp.maximum(linter, jnp.max(lw[cl * h:cl * (h + 1)], axis=0, keepdims=True))
            mt.append(mt_h)
            wi.append(jnp.exp(linter - mt_h))
            emt.append(jnp.exp(-mt_h))
        kstack = jnp.concatenate([jnp.where(hmask[h], kb, jnp.zeros_like(kb)) for h in range(4)],
                                 axis=0)
        s_t = _dotf(kstack, qt)
        w_t = (s_t * jnp.exp(lw - rows_to_blocks(mt, cl))).astype(BF16)
        vt = vt_s[bb, c]
        vt4 = jnp.concatenate([vt.astype(BF16)] * 4, axis=1)
        lhs_nd = jnp.concatenate([vt4, jnp.zeros((128, 512), BF16)], axis=0)
        lhs_nd = jnp.where(nd_mask, lhs_nd, jnp.zeros_like(lhs_nd)) + nd_ones
        c_aug = c_s[bb]
        nd = _dotf(lhs_nd, w_t) + jnp.concatenate(
            [rows_to_blocks(wi, 64), head_rows(wi)], axis=0) * _dotf(c_aug.astype(BF16), qt)
        den = [nd[256 + h:257 + h, :] for h in range(4)]
        h_t = nd[0:256] / jnp.maximum(jnp.abs(rows_to_blocks(den, 64)), rows_to_blocks(emt, 64))
        wupd, decs, m_new_all = [], [], m_all
        for h in range(4):
            lupd = tots[h] + r_rows[h]
            m_new = jnp.maximum(tots[h] + m_old[h], jnp.max(lupd, axis=-1, keepdims=True))
            wupd.append(jnp.exp(lupd - m_new))
            decs.append(jnp.exp(tots[h] + m_old[h] - m_new))
            m_new_all = jnp.where(lane128 == h, m_new, m_new_all)
        lhs_u = jnp.concatenate([vt * rows_to_blocks(wupd, 64), head_rows(wupd)], axis=0)
        upd = _dotf(lhs_u.astype(BF16), kb)
        dec_col = jnp.zeros((384, 1), F32)
        for h in range(4):
            dec_col = jnp.where(st_rowhead == h, decs[h], dec_col)
        c_s[bb] = dec_col * c_aug + jnp.where(st_mask, upd, 0.0)
        m_s[bb] = m_new_all
        return h_t

    def fwd_body(i, carry):
        for bb in range(nbr):
            hf_s[bb, i] = chunk(i, False, bb)
        return carry

    def bwd_body(i, carry):
        c = jnp.where(i < n_ctx_chunks, n_ctx_chunks - 1 - i, n_chunks - 1 - (i - n_ctx_chunks))
        for bb in range(nbr):
            gated = _sigmoid(ot_s[bb, c]) * (hf_s[bb, c] + chunk(c, True, bb))
            ms = [jnp.mean(jnp.square(gated[64 * h:64 * (h + 1)]), axis=0, keepdims=True)
                  for h in range(4)]
            y_t = gated * lax.rsqrt(rows_to_blocks(ms, 64) + NORM_EPS) * nw_ref[...]
            y_ref[bb, pl.ds(pl.multiple_of(c * cl, cl), cl), :] = y_t.T.astype(BF16)
        return carry

    lax.fori_loop(0, n_chunks, prep_body, 0)
    c_s[...] = jnp.zeros_like(c_s)
    m_s[...] = jnp.zeros_like(m_s)
    lax.fori_loop(0, n_chunks, fwd_body, 0)
    c_s[...] = jnp.zeros_like(c_s)
    m_s[...] = jnp.zeros_like(m_s)
    lax.fori_loop(0, n_chunks, bwd_body, 0)


def _ml_call(p_ml, p_g, cw, cb, gb, nw, n_ctx):
    b, t, _ = p_ml.shape
    nc = t // ML_CHUNK
    nbr = ML_ROWS
    const = lambda bi: (0, 0)
    nw_col = jnp.broadcast_to(nw.reshape(GROUP_W, 1), (GROUP_W, 128))
    return pl.pallas_call(
        functools.partial(_ml_kernel, n_ctx_chunks=n_ctx // ML_CHUNK, n_chunks=nc),
        grid=(b // nbr,),
        in_specs=[pl.BlockSpec((nbr, t, IN_ML), lambda bi: (bi, 0, 0)),
                  pl.BlockSpec((nbr, t, IN_G), lambda bi: (bi, 0, 0)),
                  pl.BlockSpec(cw.shape, const), pl.BlockSpec(cb.shape, const),
                  pl.BlockSpec(gb.shape, const), pl.BlockSpec(nw_col.shape, const)],
        out_specs=pl.BlockSpec((nbr, t, GROUP_W), lambda bi: (bi, 0, 0)),
        out_shape=jax.ShapeDtypeStruct((b, t, GROUP_W), BF16),
        scratch_shapes=[pltpu.VMEM((nbr, nc, GROUP_W, 128), BF16),
                        pltpu.VMEM((nbr, nc, 128, GROUP_W), BF16),
                        pltpu.VMEM((nbr, nc, GROUP_W, 128), F32),
                        pltpu.VMEM((nbr, nc, GROUP_W, 128), F32),
                        pltpu.VMEM((nbr, nc, 16, 128), F32),
                        pltpu.VMEM((nbr, nc, GROUP_W, 128), F32),
                        pltpu.VMEM((nbr, 384, GROUP_W), F32), pltpu.VMEM((nbr, 1, 128), F32)],
        compiler_params=_params(("parallel",), VMEM_LIMIT),
        name="mlstm",
    )(p_ml, p_g, cw, cb, gb, nw_col)


def _s5_kernel(u_ref, are_ref, aim_ref, ls_ref, bre_ref, bim_ref, cre_ref, cim_ref, y_ref,
               ar_s, ai_s, bcat_s, ccat_s, st_s, bu_s, *, tc, nb):
    d = pl.program_id(0)
    i = pl.program_id(1)
    ns = S5_NGROUPS * S5_STATE

    @pl.when(i == 0)
    def _init():
        are = jnp.minimum(are_ref[0], -1e-4)
        aim = aim_ref[0]
        dt = jnp.exp(ls_ref[0])
        mag = jnp.exp(dt * are)
        abr = mag * jnp.cos(dt * aim)
        abi = mag * jnp.sin(dt * aim)
        inv = 1.0 / (are * are + aim * aim)
        fre = ((abr - 1.0) * are + abi * aim) * inv
        fim = (abi * are - (abr - 1.0) * aim) * inv
        bre = bre_ref[0]
        bim = bim_ref[0]
        bcat_s[:, 0:ns] = (bre * fre - bim * fim).astype(BF16)
        bcat_s[:, ns:2 * ns] = (bre * fim + bim * fre).astype(BF16)
        ccat_s[0:ns, :] = cre_ref[0].astype(BF16)
        ccat_s[ns:2 * ns, :] = (-cim_ref[0]).astype(BF16)
        ar_s[...] = jnp.broadcast_to(abr, (nb, ns))
        ai_s[...] = jnp.broadcast_to(abi, (nb, ns))
        st_s[...] = jnp.zeros_like(st_s)

    bu_s[...] = _dotf(u_ref[...].astype(BF16), bcat_s[...])
    ar = ar_s[...]
    ai = ai_s[...]

    def body(j, carry):
        xr, xi = carry
        t = j + d * (tc - 1 - 2 * j)
        r0 = pl.multiple_of(t * nb, nb)
        nr = ar * xr - ai * xi + bu_s[pl.ds(r0, nb), 0:ns]
        ni = ar * xi + ai * xr + bu_s[pl.ds(r0, nb), ns:2 * ns]
        bu_s[pl.ds(r0, nb), 0:ns] = nr
        bu_s[pl.ds(r0, nb), ns:2 * ns] = ni
        return nr, ni

    xr, xi = lax.fori_loop(0, tc, body, (st_s[0], st_s[1]), unroll=4)
    st_s[0] = xr
    st_s[1] = xi
    y = _dotf(bu_s[...].astype(BF16), ccat_s[...])
    for half in range(y.shape[1] // 128):
        y_ref[0, half] = y[:, 128 * half:128 * (half + 1)]


def _s5_call(u_tm, are, aim, ls, bre, bim, cre, cim, n_ctx, nb):
    rows, gw = u_tm.shape
    tc = S5_CHUNK
    n_chunks = rows // (tc * nb)
    n_ctx_chunks = n_ctx // tc
    ns = S5_NGROUPS * S5_STATE

    def chunk_of(d, i):
        rev = jnp.where(i < n_ctx_chunks, n_ctx_chunks - 1 - i, n_chunks - 1 - (i - n_ctx_chunks))
        return jnp.where(d == 0, i, rev)

    vec = pl.BlockSpec((1, 1, ns), lambda d, i: (d, 0, 0))
    return pl.pallas_call(
        functools.partial(_s5_kernel, tc=tc, nb=nb),
        grid=(2, n_chunks),
        in_specs=[pl.BlockSpec((tc * nb, gw), lambda d, i: (chunk_of(d, i), 0)),
                  vec, vec, vec,
                  pl.BlockSpec((1, gw, ns), lambda d, i: (d, 0, 0)),
                  pl.BlockSpec((1, gw, ns), lambda d, i: (d, 0, 0)),
                  pl.BlockSpec((1, ns, gw), lambda d, i: (d, 0, 0)),
                  pl.BlockSpec((1, ns, gw), lambda d, i: (d, 0, 0))],
        out_specs=pl.BlockSpec((1, gw // 128, tc * nb, 128),
                               lambda d, i: (d, 0, chunk_of(d, i), 0)),
        out_shape=jax.ShapeDtypeStruct((2, gw // 128, rows, 128), F32),
        scratch_shapes=[pltpu.VMEM((nb, ns), F32), pltpu.VMEM((nb, ns), F32),
                        pltpu.VMEM((gw, 2 * ns), BF16), pltpu.VMEM((2 * ns, gw), BF16),
                        pltpu.VMEM((2, nb, ns), F32), pltpu.VMEM((tc * nb, 2 * ns), F32)],
        compiler_params=_params(("arbitrary", "arbitrary"), VMEM_LIMIT),
        name="s5_scan",
    )(u_tm, are, aim, ls, bre, bim, cre, cim)


def _glu_kernel(ys_ref, u_ref, d_ref, w_ref, b_ref, o_ref, *, nb, tq):
    gw = GROUP_W
    rows = []
    for b in range(nb):
        y_b = jnp.concatenate(
            [ys_ref[0, half, pl.ds(b, tq, stride=nb), :] + ys_ref[1, half, pl.ds(b, tq, stride=nb), :]
             for half in range(gw // 128)], axis=1)
        rows.append(y_b + u_ref[:, gw * b:gw * (b + 1)] * d_ref[...])
    y = jnp.concatenate(rows, axis=0)
    g = y * (0.5 * (1.0 + jnp.tanh(math.sqrt(2.0 / math.pi) * (y + 0.044715 * (y * y * y)))))
    z = _dotf(g.astype(BF16), w_ref[...]) + b_ref[...]
    out = (g * _sigmoid(z)).astype(BF16)
    for b in range(nb):
        o_ref[b] = out[tq * b:tq * (b + 1)]


def _glu_call(ys, u_t, dsk, w, bias, nb):
    t = u_t.shape[0]
    tq = ROW_TILE // nb
    gw = GROUP_W
    const = lambda ti: (0, 0)
    return pl.pallas_call(
        functools.partial(_glu_kernel, nb=nb, tq=tq),
        grid=(t // tq,),
        in_specs=[pl.BlockSpec((2, gw // 128, tq * nb, 128), lambda ti: (0, 0, ti, 0)),
                  pl.BlockSpec((tq, nb * gw), lambda ti: (ti, 0)),
                  pl.BlockSpec(dsk.shape, const), pl.BlockSpec(w.shape, const),
                  pl.BlockSpec(bias.shape, const)],
        out_specs=pl.BlockSpec((nb, tq, gw), lambda ti: (0, ti, 0)),
        out_shape=jax.ShapeDtypeStruct((nb, t, gw), BF16),
        compiler_params=_params(("parallel",), VMEM_LIMIT),
        name="s5_glu",
    )(ys, u_t, dsk, w, bias)


def _out_kernel(ya_ref, yb_ref, yc_ref, yd_ref, w_ref, xa_ref, xb_ref, g1_ref, sh2_ref, sc2_ref,
                nw1_ref, nw2_ref, wr_ref, br_ref, xn_ref, f_ref, te_ref, tg_ref, *, n_a_blk, blk0):
    o = (_dotf(ya_ref[0], w_ref[0:256]) + _dotf(yb_ref[0], w_ref[256:512])
         + _dotf(yc_ref[0], w_ref[512:768]) + _dotf(yd_ref[0], w_ref[768:1024]))
    x = jnp.where(pl.program_id(1) + blk0 < n_a_blk, xa_ref[0], xb_ref[0])
    xn = x + g1_ref[0] * _rms(o, nw1_ref[...])
    xn_ref[0] = xn
    f = _rms(xn, nw2_ref[...]) * (1.0 + sc2_ref[0]) + sh2_ref[0]
    _store_token_tiles(f_ref, f)
    lg = _dot3(f, wr_ref[...]) + br_ref[...]
    tm = lg.shape[0]
    lane = lax.broadcasted_iota(I32, (tm, 128), 1).astype(F32)
    tops, idxs = [], []
    for _ in range(TOP_K):
        mx = jnp.max(lg, axis=-1, keepdims=True)
        idx = jnp.min(jnp.where(lg == mx, lane, 128.0), axis=-1, keepdims=True)
        tops.append(mx)
        idxs.append(idx.astype(I32))
        lg = jnp.where(lane == idx, NEG_INF, lg)
    ex = [jnp.exp(tv - tops[0]) for tv in tops]
    inv = 1.0 / (ex[0] + ex[1] + ex[2] + ex[3])
    l8 = lax.broadcasted_iota(I32, (tm, 8), 1)
    te = jnp.zeros((tm, 8), I32)
    tg = jnp.zeros((tm, 8), F32)
    for kk in range(TOP_K):
        te = jnp.where(l8 == kk, idxs[kk], te)
        tg = jnp.where(l8 == kk, ex[kk] * inv, tg)
    te_ref[0] = te
    tg_ref[0] = tg


def _out_call(ya, yb, yc, yd, w_out, xa, xb, mod_l, nw1, nw2, wr, br, n_ctx_blk, blk0):
    b, _, d = xa.shape
    t = xa.shape[1] + (0 if xb is None else xb.shape[1])
    tm = ROW_TILE
    nblk = t // tm - blk0
    t_out = nblk * tm
    spec_a, spec_b, n_a_blk = _stream_specs(xa, xb, blk0)

    def mrow(bi, ti):
        return jnp.where(ti + blk0 < n_ctx_blk, b, bi)

    row3 = lambda bi, ti: (bi, ti + blk0, 0)
    out3 = lambda bi, ti: (bi, ti, 0)
    const = lambda bi, ti: (0, 0)
    modspec = lambda j: pl.BlockSpec((1, 1, d), lambda bi, ti: (mrow(bi, ti), 0, j))
    yspec = lambda y: pl.BlockSpec((1, tm, GROUP_W), row3 if y.shape[1] == t else out3)
    return pl.pallas_call(
        functools.partial(_out_kernel, n_a_blk=n_a_blk, blk0=blk0),
        grid=(b, nblk),
        in_specs=[yspec(ya), yspec(yb), yspec(yc), yspec(yd), pl.BlockSpec((d, d), const),
                  spec_a, spec_b, modspec(2), modspec(3), modspec(4),
                  pl.BlockSpec((1, d), const), pl.BlockSpec((1, d), const),
                  pl.BlockSpec((d, 128), const), pl.BlockSpec((1, 128), const)],
        out_specs=[pl.BlockSpec((1, tm, d), out3),
                   pl.BlockSpec((tm * d // 128, 128), lambda bi, ti: (bi * nblk + ti, 0)),
                   pl.BlockSpec((1, tm, 8), out3), pl.BlockSpec((1, tm, 8), out3)],
        out_shape=[jax.ShapeDtypeStruct((b, t_out, d), F32),
                   jax.ShapeDtypeStruct((b * t_out * d // 128, 128), F32),
                   jax.ShapeDtypeStruct((b, t_out, 8), I32),
                   jax.ShapeDtypeStruct((b, t_out, 8), F32)],
        compiler_params=_params(("parallel", "parallel"), VMEM_LIMIT),
        name="out_proj_router",
    )(ya, yb, yc, yd, w_out, xa, xa if xb is None else xb, mod_l, mod_l, mod_l, nw1, nw2, wr, br)


def _rank_kernel(te_ref, rank_ref, cnt_ref, carry_ref):
    i = pl.program_id(0)

    @pl.when(i == 0)
    def _():
        carry_ref[...] = jnp.zeros_like(carry_ref)

    te = te_ref[...]
    tb = te.shape[0]
    lane = lax.broadcasted_iota(I32, (tb, 128), 1)
    l8 = lax.broadcasted_iota(I32, (tb, 8), 1)
    below = (lax.broadcasted_iota(I32, (tb, tb), 0)
             > lax.broadcasted_iota(I32, (tb, tb), 1))
    lstrict = jnp.where(below, 1.0, 0.0).astype(BF16)
    base = carry_ref[...]
    out = jnp.zeros((tb, 8), I32)
    for k in range(TOP_K):
        oh = jnp.where(lane == te[:, k:k + 1], 1.0, 0.0)
        before = _dotf(lstrict, oh.astype(BF16)) + base
        rank_k = jnp.sum(oh * before, axis=-1, keepdims=True)
        out = jnp.where(l8 == k, rank_k.astype(I32), out)
        base = base + jnp.sum(oh, axis=0, keepdims=True)
    rank_ref[...] = out
    carry_ref[...] = base
    cnt_ref[...] = base.astype(I32)


def _rank_call(te):
    n_tok = te.shape[0]
    tb = RANK_TILE
    return pl.pallas_call(
        _rank_kernel,
        grid=(n_tok // tb,),
        in_specs=[pl.BlockSpec((tb, 8), lambda i: (i, 0))],
        out_specs=[pl.BlockSpec((tb, 8), lambda i: (i, 0)), pl.BlockSpec((1, 128), lambda i: (0, 0))],
        out_shape=[jax.ShapeDtypeStruct((n_tok, 8), I32), jax.ShapeDtypeStruct((1, 128), I32)],
        scratch_shapes=[pltpu.VMEM((1, 128), F32)],
        compiler_params=_params(("arbitrary",)),
        name="moe_rank",
    )(te)


def _slot_kernel(tbl_ref, te_ref, rank_ref, slot_ref):
    te = te_ref[...]
    start = jnp.zeros(te.shape, I32)
    for e in range(N_EXPERTS):
        start = jnp.where(te == e, tbl_ref[e], start)
    slot_ref[...] = start + rank_ref[...]


def _slot_call(tbl, te, rank):
    n_tok = te.shape[0]
    tb = RANK_TILE
    spec = pl.BlockSpec((tb, 8), lambda i, tbl: (i, 0))
    return pl.pallas_call(
        _slot_kernel,
        grid_spec=pltpu.PrefetchScalarGridSpec(num_scalar_prefetch=1, grid=(n_tok // tb,),
                                               in_specs=[spec, spec], out_specs=spec),
        out_shape=jax.ShapeDtypeStruct((n_tok, 8), I32),
        compiler_params=_params(("parallel",)),
        name="moe_slots",
    )(tbl, te, rank)


def _slot_rows(slot_ref, r0, nt):
    return [pl.multiple_of(slot_ref[0, 0, (r0 + u) * 8 + k] * nt, nt)
            for u in range(DMA_UNROLL) for k in range(TOP_K)]


def _dispatch_kernel(tbl_ref, slot_ref, f_ref, xs_hbm, zbuf, sem, zsem, *, tb, tm, nt):
    i = pl.program_id(0)

    @pl.when(i == 0)
    def _():
        zbuf[...] = jnp.zeros_like(zbuf)
        fills = [pltpu.make_async_copy(
            zbuf, xs_hbm.at[pl.ds(pl.multiple_of(tbl_ref[N_EXPERTS + e] * nt, nt), tm * nt)], zsem)
            for e in range(N_EXPERTS)]
        for fill in fills:
            fill.start()
        for fill in fills:
            fill.wait()

        def fill_unused(blk, carry):
            tail = pltpu.make_async_copy(
                zbuf, xs_hbm.at[pl.ds(pl.multiple_of(blk * (tm * nt), tm * nt), tm * nt)], zsem)
            tail.start()
            tail.wait()
            return carry
        lax.fori_loop(tbl_ref[2 * N_EXPERTS], xs_hbm.shape[0] // (tm * nt), fill_unused, 0)

    def body(g, carry):
        r0 = g * DMA_UNROLL
        dsts = _slot_rows(slot_ref, r0, nt)
        for u in range(DMA_UNROLL):
            src = f_ref.at[pl.ds(pl.multiple_of((r0 + u) * nt, nt), nt)]
            for k in range(TOP_K):
                pltpu.make_async_copy(src, xs_hbm.at[pl.ds(dsts[u * TOP_K + k], nt)],
                                      sem).start(priority=k % 2)
        return carry

    lax.fori_loop(0, tb // DMA_UNROLL, body, 0)
    for k in range(TOP_K):
        pltpu.make_async_copy(f_ref, xs_hbm.at[pl.ds(0, tb * nt)], sem).wait()


def _dispatch_call(tbl, slot3, f_tiles, n_slots, tm):
    n_blk, _, per = slot3.shape
    tb = per // 8
    nt = f_tiles.shape[0] // (n_blk * tb)
    grid_spec = pltpu.PrefetchScalarGridSpec(
        num_scalar_prefetch=1,
        grid=(n_blk,),
        in_specs=[pl.BlockSpec((1, 1, per), lambda i, tbl: (i, 0, 0), memory_space=pltpu.SMEM),
                  pl.BlockSpec((tb * nt, 128), lambda i, tbl: (i, 0))],
        out_specs=pl.BlockSpec(memory_space=pl.ANY),
        scratch_shapes=[pltpu.VMEM((tm * nt, 128), F32), pltpu.SemaphoreType.DMA(()),
                        pltpu.SemaphoreType.DMA(())])
    return pl.pallas_call(
        functools.partial(_dispatch_kernel, tb=tb, tm=tm, nt=nt),
        grid_spec=grid_spec,
        out_shape=jax.ShapeDtypeStruct(((n_slots + tm) * nt, 128), F32),
        compiler_params=_params(("arbitrary",), VMEM_LIMIT),
        name="moe_dispatch",
    )(tbl, slot3, f_tiles)


def _expert_kernel(be_ref, nu_ref, first_ref, par_ref, nxt_ref, xs_ref, bgu_ref, bd_ref, wgu_hbm,
                   wd_hbm, ys_ref, wgu_f, wd_f, wgu_s, wd_s, sem, *, tm, d_ff, layer):
    i = pl.program_id(0)
    nt = xs_ref.shape[0] // tm

    def fetch(e, slot):
        return (pltpu.make_async_copy(wgu_hbm.at[layer, e], wgu_f.at[slot], sem.at[slot]),
                pltpu.make_async_copy(wd_hbm.at[layer, e], wd_f.at[slot], sem.at[slot]))

    @pl.when(i < nu_ref[0])
    def _():
        @pl.when(first_ref[i] == 1)
        def _():
            slot = par_ref[i]

            @pl.when(i == 0)
            def _():
                for cp in fetch(be_ref[0], 0):
                    cp.start()

            for cp in fetch(be_ref[i], slot):
                cp.wait()

            @pl.when(nxt_ref[i] >= 0)
            def _():
                for cp in fetch(nxt_ref[i], 1 - slot):
                    cp.start()

            for c in range(0, wgu_s.shape[0], 128):
                wgu_s[c:c + 128, :] = wgu_f[slot, c:c + 128, :].astype(BF16)
            for c in range(0, wd_s.shape[0], 128):
                wd_s[c:c + 128, :] = wd_f[slot, c:c + 128, :].astype(BF16)

        x = _load_token_tiles(xs_ref, tm, nt).astype(BF16)
        gu = _dotf(x, wgu_s[...]) + bgu_ref[...]
        gate = jnp.minimum(gu[:, 0:d_ff], SWIGLU_LIMIT)
        up = jnp.clip(gu[:, d_ff:2 * d_ff], -SWIGLU_LIMIT, SWIGLU_LIMIT)
        act = (up + 1.0) * gate * _sigmoid(SWIGLU_ALPHA * gate)
        _store_token_tiles(ys_ref, _dotf(act.astype(BF16), wd_s[...]) + bd_ref[...])

    @pl.when(i >= nu_ref[0])
    def _():
        ys_ref[...] = jnp.zeros_like(ys_ref)


def _expert_call(block_expert, n_used, xs_tiles, wgu, bgu, wd, bd, layer, n_blocks, tm):
    d, two_ff = wgu.shape[2:]
    d_ff = two_ff // 2
    nt = d // 128
    idx = jnp.arange(n_blocks, dtype=I32)
    active = idx < n_used[0]
    first = jnp.logical_and(active, jnp.logical_or(idx == 0, block_expert != jnp.roll(block_expert, 1)))
    parity = (jnp.cumsum(first.astype(I32)) - 1) % 2
    later = jnp.where(jnp.logical_and(block_expert[None, :] > block_expert[:, None], active[None, :]),
                      block_expert[None, :], N_EXPERTS)
    nxt = jnp.min(later, axis=1)
    nxt = jnp.where(nxt == N_EXPERTS, -1, nxt).astype(I32)
    ex = lambda i, be, nu, fi, pa, nx: (layer, be[i], 0, 0)
    grid_spec = pltpu.PrefetchScalarGridSpec(
        num_scalar_prefetch=5,
        grid=(n_blocks,),
        in_specs=[pl.BlockSpec((tm * nt, 128),
                               lambda i, be, nu, fi, pa, nx: (jnp.minimum(i, nu[0] - 1), 0)),
                  pl.BlockSpec((None, None, 1, two_ff), ex),
                  pl.BlockSpec((None, None, 1, d), ex),
                  pl.BlockSpec(memory_space=pl.ANY), pl.BlockSpec(memory_space=pl.ANY)],
        out_specs=pl.BlockSpec((tm * nt, 128), lambda i, be, nu, fi, pa, nx: (i, 0)),
        scratch_shapes=[pltpu.VMEM((2, d, two_ff), F32), pltpu.VMEM((2, d_ff, d), F32),
                        pltpu.VMEM((d, two_ff), BF16), pltpu.VMEM((d_ff, d), BF16),
                        pltpu.SemaphoreType.DMA((2,))])
    return pl.pallas_call(
        functools.partial(_expert_kernel, tm=tm, d_ff=d_ff, layer=layer),
        grid_spec=grid_spec,
        out_shape=jax.ShapeDtypeStruct((n_blocks * tm * nt, 128), F32),
        compiler_params=_params(("arbitrary",), VMEM_LIMIT),
        name="moe_experts",
    )(block_expert, n_used, first.astype(I32), parity.astype(I32), nxt, xs_tiles, bgu, bd, wgu, wd)


def _route(counts, n_tok, tm):
    n_blocks = -(-(n_tok * TOP_K + N_EXPERTS * (tm - 1)) // tm)
    padded = (counts + tm - 1) // tm * tm
    padded_end = jnp.cumsum(padded)
    group_start = padded_end - padded
    block_start = jnp.arange(n_blocks, dtype=I32) * tm
    block_expert = jnp.minimum(
        jnp.sum((padded_end[None, :] <= block_start[:, None]).astype(I32), axis=1), N_EXPERTS - 1)
    n_used = (padded_end[-1] // tm).astype(I32).reshape(1)
    table = jnp.concatenate([group_start, group_start + counts, n_used]).astype(I32)
    return table, block_expert, n_used, n_blocks


def _fin_kernel(scur_ref, snxt_ref, g_ref, x_ref, g2_ref, nw_ref, ys_hbm, out_ref, buf, sem,
                *, tb, nt):
    i = pl.program_id(0)
    s = i % 2

    def gather(slot_ref, ss):
        def body(g, carry):
            r0 = g * DMA_UNROLL
            srcs = _slot_rows(slot_ref, r0, nt)
            for u in range(DMA_UNROLL):
                for k in range(TOP_K):
                    dst = pl.multiple_of((k * tb + r0 + u) * nt, nt)
                    pltpu.make_async_copy(ys_hbm.at[pl.ds(srcs[u * TOP_K + k], nt)],
                                          buf.at[ss, pl.ds(dst, nt)],
                                          sem.at[ss]).start(priority=k % 2)
            return carry
        lax.fori_loop(0, tb // DMA_UNROLL, body, 0)

    @pl.when(i == 0)
    def _():
        gather(scur_ref, 0)

    @pl.when(i + 1 < pl.num_programs(0))
    def _():
        gather(snxt_ref, 1 - s)

    pltpu.make_async_copy(ys_hbm.at[pl.ds(0, TOP_K * tb * nt)], buf.at[s], sem.at[s]).wait()
    gates = g_ref[...]
    parts = []
    for j in range(nt):
        acc = None
        for k in range(TOP_K):
            v = buf[s, pl.ds(k * tb * nt + j, tb, stride=nt), :] * gates[:, k:k + 1]
            acc = v if acc is None else acc + v
        parts.append(acc)
    m = jnp.concatenate(parts, axis=1)
    out_ref[0] = x_ref[0] + g2_ref[0] * _rms(m, nw_ref[...])


def _fin_call(slot3, gates, ys_tiles, xn, mod_l, nw, n_ctx_blk, blk0):
    b, t, d = xn.shape
    tb = ROW_TILE
    nblk = t // tb
    nt = d // 128
    n_blk = b * nblk

    def mrow(i):
        return jnp.where(i % nblk + blk0 < n_ctx_blk, b, i // nblk)

    smem = functools.partial(pl.BlockSpec, memory_space=pltpu.SMEM)
    return pl.pallas_call(
        functools.partial(_fin_kernel, tb=tb, nt=nt),
        grid=(n_blk,),
        in_specs=[smem((1, 1, tb * 8), lambda i: (i, 0, 0)),
                  smem((1, 1, tb * 8), lambda i: (jnp.minimum(i + 1, n_blk - 1), 0, 0)),
                  pl.BlockSpec((tb, 8), lambda i: (i, 0)),
                  pl.BlockSpec((1, tb, d), lambda i: (i // nblk, i % nblk, 0)),
                  pl.BlockSpec((1, 1, d), lambda i: (mrow(i), 0, 5)),
                  pl.BlockSpec((1, d), lambda i: (0, 0)),
                  pl.BlockSpec(memory_space=pl.ANY)],
        out_specs=pl.BlockSpec((1, tb, d), lambda i: (i // nblk, i % nblk, 0)),
        out_shape=jax.ShapeDtypeStruct((b, t, d), F32),
        scratch_shapes=[pltpu.VMEM((2, TOP_K * tb * nt, 128), F32), pltpu.SemaphoreType.DMA((2,))],
        compiler_params=_params(("arbitrary",), VMEM_LIMIT),
        name="moe_combine",
    )(slot3, slot3, gates, xn, mod_l, nw, ys_tiles)


def _in_proj_columns():
    cols = np.full((IN_COLS,), -1, np.int64)
    for sec in range(2):
        for n in range(256):
            part, hm, j = n // 128, (n % 128) // 16, n % 16
            m, h = hm // 4, hm % 4
            cols[sec * 256 + n] = sec * 256 + h * 64 + m * 32 + part * 16 + j
    cols[512:768] = np.arange(512, 768)
    o, s = IN_DA, 768
    cols[o:o + 1024] = s + np.arange(1024)
    o, s = o + IN_ML, s + 1024
    cols[o:o + 16] = s + np.arange(16)
    o, s = o + IN_G, s + 16
    cols[o:o + MLA_Q_RANK] = s + np.arange(MLA_Q_RANK)
    cols[o + 256:o + 256 + MLA_KV_RANK] = s + MLA_Q_RANK + np.arange(MLA_KV_RANK)
    cols[o + 384 + 64:o + 384 + 96] = s + MLA_Q_RANK + MLA_KV_RANK + np.arange(MLA_ROPE)
    o, s = o + IN_MLA, s + MLA_Q_RANK + MLA_KV_RANK + MLA_ROPE
    cols[o:o + 256] = s + np.arange(256)
    return cols


def _take_cols(w, cols):
    valid = jnp.asarray(cols >= 0)
    return jnp.where(valid, jnp.take(w, jnp.asarray(np.maximum(cols, 0)), axis=-1), 0.0)


def _rope_tables(n_ctx, n_lat):
    pos = jnp.arange(n_lat)
    inv = ROPE_THETA ** (-jnp.arange(8, dtype=F32) / 8)
    ang = jnp.concatenate([(pos // GRID_W)[:, None] * inv, (pos % GRID_W)[:, None] * inv], axis=-1)
    cos = jnp.concatenate([jnp.ones((n_ctx, 16), F32), jnp.cos(ang)], axis=0)
    sin = jnp.concatenate([jnp.zeros((n_ctx, 16), F32), jnp.sin(ang)], axis=0)
    t = n_ctx + n_lat
    cos_da, sin_da = jnp.tile(cos, (1, 8)), jnp.tile(sin, (1, 8))
    one, zero = jnp.ones((t, 64), F32), jnp.zeros((t, 64), F32)
    z16, z32 = jnp.zeros((t, 16), F32), jnp.zeros((t, 32), F32)
    cm = jnp.concatenate([one, cos, cos, jnp.ones((t, 32), F32)], axis=1)
    sa = jnp.concatenate([zero, -sin, z16, z32], axis=1)
    sb = jnp.concatenate([zero, z16, sin, z32], axis=1)
    return cos_da, sin_da, cm, sa, sb


def _mla_weights(w_uq, w_ukv):
    hd = MLA_NOPE + MLA_ROPE
    wq = jnp.pad(w_uq.reshape(MLA_Q_RANK, MLA_HEADS, hd),
                 ((0, 256 - MLA_Q_RANK), (0, 0), (0, 128 - hd))).reshape(256, 128 * MLA_HEADS)
    kv = w_ukv.reshape(MLA_KV_RANK, MLA_HEADS, -1)
    wk = jnp.pad(kv[:, :, :MLA_NOPE], ((0, 0), (0, 0), (0, 128 - MLA_NOPE)))
    wkv = jnp.concatenate([wk.reshape(MLA_KV_RANK, -1), kv[:, :, MLA_NOPE:].reshape(MLA_KV_RANK, -1)],
                          axis=1)
    return wq.astype(BF16), wkv.astype(BF16)


def _s5_layout(a_re, a_im, log_step, b_re, b_im, c_re, c_im):
    ns = S5_NGROUPS * S5_STATE
    eye = jnp.eye(S5_NGROUPS, dtype=F32)
    are = a_re.reshape(2, 1, ns)
    aim = a_im.reshape(2, 1, ns)
    ls = jnp.repeat(log_step, S5_STATE, axis=-1).reshape(2, 1, ns)
    bd_b = lambda w: jnp.einsum("dgph,gk->dghkp", w, eye).reshape(2, GROUP_W, ns)
    bd_c = lambda w: jnp.einsum("dghp,gk->dgpkh", w, eye).reshape(2, ns, GROUP_W)
    return are, aim, ls, bd_b(b_re), bd_b(b_im), bd_c(c_re), bd_c(c_im)


def kernel(x, c, ctx, c_ctx, w_mod, b_mod, norm_w, w_in, w_out, da_lambda, da_subln, ml_conv_w,
           ml_conv_b, ml_gate_b, ml_norm, mla_q_norm, mla_w_uq, mla_kv_norm, mla_w_ukv, s5_a_re,
           s5_a_im, s5_log_step, s5_b_re, s5_b_im, s5_c_re, s5_c_im, s5_d, s5_w_glu, s5_b_glu,
           moe_w_router, moe_b_router, moe_w_gate_up, moe_b_gate_up, moe_w_down, moe_b_down):
    bsz, n_lat, d = x.shape
    n_ctx = ctx.shape[1]
    t = n_ctx + n_lat
    depth = w_mod.shape[0]
    tm = ROW_TILE
    assert n_ctx % tm == 0 and n_lat % tm == 0 and bsz % 8 == 0 and bsz < 16
    assert n_ctx % ML_CHUNK == 0 and n_ctx % S5_CHUNK == 0
    n_ctx_blk = n_ctx // tm

    cc = jnp.pad(jnp.concatenate([c, c_ctx[None]], axis=0), ((0, 15 - bsz), (0, 0)))
    mod = _mod_call(cc, w_mod, b_mod)
    cos_da, sin_da, cm, sa, sb = _rope_tables(n_ctx, n_lat)
    in_cols = _in_proj_columns()
    xa, xb = ctx, x

    for l in range(depth):
        last = l == depth - 1
        lambda_init = 0.8 - 0.6 * math.exp(-0.3 * l)
        mod_l = mod[l].reshape(16, 1, 6 * d)
        w_in_p = _take_cols(w_in[l], in_cols).astype(BF16)
        wq, wkv = _mla_weights(mla_w_uq[l], mla_w_ukv[l])
        qn = jnp.pad(mla_q_norm[l], (0, 256 - MLA_Q_RANK)).reshape(1, 256)
        mla = (qn, wq, mla_kv_norm[l].reshape(1, -1), wkv, cm, sa, sb)
        p_da, p_ml, p_g, q_mla, k_mla, v_mla, u_t = _in_call(
            xa, xb, mod_l, norm_w[l, 0].reshape(1, d), w_in_p, cos_da, sin_da, mla, n_ctx_blk)

        da_extra = [da_lambda[l], jnp.tile(da_subln[l], 4).reshape(1, GROUP_W)]
        da_kern = functools.partial(_da_kernel, lambda_init=lambda_init)
        da_kw = dict(q_col=0, k_col=1, v_col=2, q_w=GROUP_W)
        ya = _attn_call(da_kern, "diff_attn", p_da, p_da, p_da, da_extra, q_blk0=n_ctx_blk,
                        n_q_blk=n_lat // tm, n_keys=t, **da_kw)
        mla_kw = dict(q_col=0, k_col=0, v_col=0, q_w=512)
        yc = _attn_call(_mla_attn_kernel, "mla_attn", q_mla, k_mla, v_mla, [], q_blk0=n_ctx_blk,
                        n_q_blk=n_lat // tm, n_keys=t, **mla_kw)
        if not last:
            ya_c = _attn_call(da_kern, "diff_attn_ctx", p_da, p_da, p_da, da_extra, q_blk0=0,
                              n_q_blk=n_ctx_blk, n_keys=n_ctx, **da_kw)
            yc_c = _attn_call(_mla_attn_kernel, "mla_attn_ctx", q_mla, k_mla, v_mla, [], q_blk0=0,
                              n_q_blk=n_ctx_blk, n_keys=n_ctx, **mla_kw)
            ya = jnp.concatenate([ya_c, ya], axis=1)
            yc = jnp.concatenate([yc_c, yc], axis=1)

        gb = jnp.pad(ml_gate_b[l], (0, 128 - 16)).reshape(1, 128)
        yb = _ml_call(p_ml, p_g, ml_conv_w[l], ml_conv_b[l].reshape(1, -1), gb,
                      ml_norm[l].reshape(1, -1), n_ctx)

        s5p = _s5_layout(s5_a_re[l], s5_a_im[l], s5_log_step[l], s5_b_re[l], s5_b_im[l],
                         s5_c_re[l], s5_c_im[l])
        ys = _s5_call(u_t.reshape(t * bsz, GROUP_W), *s5p, n_ctx, bsz)
        yd = _glu_call(ys, u_t, s5_d[l].reshape(1, -1),
                       s5_w_glu[l].astype(BF16), s5_b_glu[l].reshape(1, -1), bsz)

        blk0 = n_ctx_blk if last else 0
        wr = jnp.pad(moe_w_router[l], ((0, 0), (0, 128 - N_EXPERTS)))
        br = jnp.pad(moe_b_router[l], (0, 128 - N_EXPERTS), constant_values=-1e30).reshape(1, 128)
        xn, f, te, tg = _out_call(ya, yb, yc, yd, w_out[l].astype(BF16), xa, xb, mod_l,
                                  norm_w[l, 1].reshape(1, d), norm_w[l, 2].reshape(1, d), wr, br,
                                  n_ctx_blk, blk0)

        t_moe = t - blk0 * tm
        n_tok = bsz * t_moe
        te = te.reshape(n_tok, 8)
        rank, cnt = _rank_call(te)
        tbl, be, nu, n_blocks = _route(cnt[0, :N_EXPERTS], n_tok, MOE_TILE)
        slots = _slot_call(tbl, te, rank)
        slot3 = slots.reshape(n_tok // tm, 1, tm * 8)
        xs_tiles = _dispatch_call(tbl, slots.reshape(n_tok // DISPATCH_TILE, 1, DISPATCH_TILE * 8), f,
                                  n_blocks * MOE_TILE, MOE_TILE)
        ys_tiles = _expert_call(be, nu, xs_tiles, moe_w_gate_up,
                                moe_b_gate_up.reshape(depth, N_EXPERTS, 1, -1), moe_w_down,
                                moe_b_down.reshape(depth, N_EXPERTS, 1, -1), l, n_blocks, MOE_TILE)
        xa, xb = _fin_call(slot3, tg.reshape(n_tok, 8), ys_tiles, xn, mod_l,
                           norm_w[l, 3].reshape(1, d), n_ctx_blk, blk0), None
    return xa
```
